```python
import math
import jax
import jax.numpy as jnp
from jax import lax
import numpy as np


D_MODEL = 1024
BATCH = 4
SEQ = 8192
DEPTH = 2

HEAD_DIM = 64
GROUP_HEADS = 4
GROUP_W = GROUP_HEADS * HEAD_DIM
N_GROUPS = 4
D_MIX = N_GROUPS * GROUP_W
NORM_EPS = 1e-6
NEG = -1e30
Q_BLOCK = 128

CMP_LEN = 32
CMP_STRIDE = 16
SEL_LEN = 64
SEL_TOPN = 16
WINDOW = 512
N_CMP_IN_SEL = (SEL_LEN - CMP_LEN) // CMP_STRIDE + 1
SEL_FORCE = 1e3
N_BUCKETS = 32
MAX_DISTANCE = 128

MLSTM_CHUNK = 64
GDN_CHUNK = 64
CONV_W = 4

Q_LORA = 256
KV_LORA = 128
QK_NOPE = 64
QK_ROPE = 32
V_HEAD = 64
ROPE_THETA = 10000.0

PEER_HEADS = 8
PEER_DK = 256
N_KEYS = 128
N_EXPERTS = N_KEYS * N_KEYS
PEER_TOPK = 16
PEER_CHUNK = 128

NSA_IN = GROUP_W + 6 * HEAD_DIM + 3 * GROUP_HEADS
MLSTM_IN = 4 * GROUP_W + 2 * GROUP_HEADS
GDN_IN = 4 * GROUP_W + 2 * GROUP_HEADS
MLA_IN = Q_LORA + KV_LORA + QK_ROPE
D_IN = NSA_IN + MLSTM_IN + GDN_IN + MLA_IN

kernel_name = 'hybrid_nsa_mlstm_gdn_mla_peer'


def _split(t, widths):
    outs = []
    start = 0
    for w in widths:
        outs.append(t[..., start:start + w])
        start += w
    return outs


def rms_norm(x, g):
    xf = x.astype(jnp.float32)
    y = xf * lax.rsqrt(jnp.mean(xf * xf, axis=-1, keepdims=True) + NORM_EPS)
    return (y * g.astype(jnp.float32)).astype(x.dtype)


def _heads(t):
    b, s, _ = t.shape
    return t.reshape(b, s, -1, HEAD_DIM).transpose(0, 2, 1, 3)


def _unheads(t):
    b, h, s, d = t.shape
    return t.transpose(0, 2, 1, 3).reshape(b, s, h * d)


def _chunk(t, size):
    sh = t.shape
    return t.reshape(sh[0], sh[1], sh[2] // size, size, *sh[3:])


def masked_softmax(s, mask):
    s = jnp.where(mask, s.astype(jnp.float32), NEG)
    p = jax.nn.softmax(s, axis=-1)
    return jnp.where(mask, p, 0.0)


def t5_bucket(dist):
    dist = jnp.maximum(dist, 0)
    max_exact = N_BUCKETS // 2
    d = jnp.maximum(dist, 1).astype(jnp.float32)
    large = max_exact + (jnp.log(d / max_exact) / math.log(MAX_DISTANCE / max_exact)
                         * (N_BUCKETS - max_exact)).astype(jnp.int32)
    large = jnp.minimum(large, N_BUCKETS - 1)
    return jnp.where(dist < max_exact, dist, large)


def t5_bias(dist, table, head_axis):
    return jnp.moveaxis(table[t5_bucket(dist)].astype(jnp.float32), -1, head_axis)


def apply_rope(t, cos, sin):
    t1, t2 = t[..., :QK_ROPE // 2], t[..., QK_ROPE // 2:]
    return jnp.concatenate([t1 * cos - t2 * sin, t2 * cos + t1 * sin], axis=-1)


def causal_conv(t, w):
    c = t.shape[-1]
    return lax.conv_general_dilated(t, w[:, None, :].astype(t.dtype), window_strides=(1,),
                                    padding=[(CONV_W - 1, 0)],
                                    dimension_numbers=('NWC', 'WIO', 'NWC'),
                                    feature_group_count=c)


def nsa_group(h, w_cmp_k, w_cmp_v, cmp_pos, t5_table):
    f32 = jnp.float32
    b, s, _ = h.shape
    q, kc, vc, ks, vs, kw, vw, gates = _split(h, (GROUP_W,) + (HEAD_DIM,) * 6 + (3 * GROUP_HEADS,))
    qh = _heads(q).astype(f32) * HEAD_DIM ** -0.5

    n_cmp = (s - CMP_LEN) // CMP_STRIDE + 1

    def compress(t, w):
        tr = t.reshape(b, s // CMP_STRIDE, CMP_STRIDE, HEAD_DIM)
        blocks = jnp.concatenate([tr[:, :-1], tr[:, 1:]], axis=2) + cmp_pos
        return (blocks.reshape(b, n_cmp, CMP_LEN * HEAD_DIM) @ w).astype(f32)

    k_cmp = compress(kc, w_cmp_k)
    v_cmp = compress(vc, w_cmp_v)
    cmp_end = jnp.arange(n_cmp) * CMP_STRIDE + CMP_LEN - 1

    n_sel = s // SEL_LEN
    top_n = min(SEL_TOPN, n_sel)
    ks_blk = ks.astype(f32).reshape(b, n_sel, SEL_LEN, HEAD_DIM)
    vs_blk = vs.astype(f32).reshape(b, n_sel, SEL_LEN, HEAD_DIM)
    sel_blk = jnp.arange(n_sel)
    n_pad = n_sel * (SEL_LEN // CMP_STRIDE) - n_cmp

    kw_pad = jnp.pad(kw.astype(f32), ((0, 0), (WINDOW, 0), (0, 0)))
    vw_pad = jnp.pad(vw.astype(f32), ((0, 0), (WINDOW, 0), (0, 0)))

    nb = s // Q_BLOCK
    q_blocks = jnp.moveaxis(qh.reshape(b, GROUP_HEADS, nb, Q_BLOCK, HEAD_DIM), 2, 0)

    def block(args):
        qb, bi = args
        t = bi * Q_BLOCK + jnp.arange(Q_BLOCK)
        dist_c = t[:, None] - cmp_end[None, :]
        s_c = jnp.einsum('bhqd,bnd->bhqn', qb, k_cmp) + t5_bias(dist_c, t5_table, 0)
        p_c = masked_softmax(s_c, dist_c >= 0)
        o_c = jnp.einsum('bhqn,bnd->bhqd', p_c, v_cmp)
        imp = jnp.pad(p_c.sum(axis=1), ((0, 0), (0, 0), (0, n_pad)))
        imp = imp.reshape(b, Q_BLOCK, n_sel, SEL_LEN // CMP_STRIDE)[..., :N_CMP_IN_SEL].sum(-1)
        cur = t // SEL_LEN
        forced = ((sel_blk[None, :] == 0) | (sel_blk[None, :] == cur[:, None])
                  | (sel_blk[None, :] == cur[:, None] - 1))
        score = jnp.where(sel_blk[None, :] <= cur[:, None],
                          imp + jnp.where(forced, SEL_FORCE, 0.0), NEG)
        top_s, idx = lax.top_k(score, top_n)
        kg = jax.vmap(lambda kb_, ib_: kb_[ib_])(ks_blk, idx)
        vg = jax.vmap(lambda vb_, ib_: vb_[ib_])(vs_blk, idx)
        tok = idx[..., None] * SEL_LEN + jnp.arange(SEL_LEN)
        dist_s = t[None, :, None, None] - tok
        m_s = (top_s > 0.5 * NEG)[..., None] & (dist_s >= 0)
        s_s = jnp.einsum('bhqd,bqnld->bhqnl', qb, kg) + t5_bias(dist_s, t5_table, 1)
        p_s = masked_softmax(s_s.reshape(b, GROUP_HEADS, Q_BLOCK, top_n * SEL_LEN),
                             m_s.reshape(b, 1, Q_BLOCK, top_n * SEL_LEN))
        o_s = jnp.einsum('bhqnl,bqnld->bhqd', p_s.reshape(s_s.shape), vg)
        kwb = lax.dynamic_slice_in_dim(kw_pad, bi * Q_BLOCK, Q_BLOCK + WINDOW, axis=1)
        vwb = lax.dynamic_slice_in_dim(vw_pad, bi * Q_BLOCK, Q_BLOCK + WINDOW, axis=1)
        src = bi * Q_BLOCK - WINDOW + jnp.arange(Q_BLOCK + WINDOW)
        dist_w = t[:, None] - src[None, :]
        m_w = (dist_w >= 0) & (dist_w < WINDOW) & (src[None, :] >= 0)
        s_w = jnp.einsum('bhqd,bkd->bhqk', qb, kwb) + t5_bias(dist_w, t5_table, 0)
        o_w = jnp.einsum('bhqk,bkd->bhqd', masked_softmax(s_w, m_w), vwb)
        return jnp.stack([o_c, o_s, o_w], axis=0)

    outs = lax.map(block, (q_blocks, jnp.arange(nb)))
    outs = outs.transpose(1, 2, 3, 0, 4, 5).reshape(3, b, GROUP_HEADS, s, HEAD_DIM)
    g = jax.nn.sigmoid(gates.astype(f32)).reshape(b, s, 3, GROUP_HEADS).transpose(2, 0, 3, 1)
    return _unheads((g[..., None] * outs).sum(0))


def mlstm_group(h, i_bias, f_bias, norm_g):
    f32 = jnp.float32
    b, s, _ = h.shape
    q, k, v, ig, fg, og = _split(h, (GROUP_W,) * 3 + (GROUP_HEADS, GROUP_HEADS, GROUP_W))
    qh = _heads(q).astype(f32)
    kh = _heads(k).astype(f32) * HEAD_DIM ** -0.5
    vh = _heads(v).astype(f32)
    i_pre = (ig + i_bias).astype(f32).transpose(0, 2, 1)
    log_f = jax.nn.log_sigmoid((fg + f_bias).astype(f32)).transpose(0, 2, 1)
    L = MLSTM_CHUNK
    tri = jnp.tril(jnp.ones((L, L), dtype=bool))

    def step(carry, xs):
        c_st, n_st, m_st = carry
        qc, kc, vc, ic, lf = xs
        bcum = jnp.cumsum(lf, axis=-1)
        log_d = jnp.where(tri, bcum[..., :, None] - bcum[..., None, :] + ic[..., None, :], -jnp.inf)
        m_inter = bcum + m_st[..., None]
        m_t = jnp.maximum(log_d.max(-1), m_inter)
        dmat = jnp.exp(log_d - m_t[..., None])
        inter = jnp.exp(m_inter - m_t)
        sqk = jnp.einsum('bhtd,bhsd->bhts', qc, kc) * dmat
        num = (jnp.einsum('bhts,bhse->bhte', sqk, vc)
               + inter[..., None] * jnp.einsum('bhek,bhtk->bhte', c_st, qc))
        den = sqk.sum(-1) + inter * jnp.einsum('bhk,bhtk->bht', n_st, qc)
        h_t = num / jnp.maximum(jnp.abs(den), jnp.exp(-m_t))[..., None]
        b_end = bcum[..., -1]
        log_w = b_end[..., None] - bcum + ic
        m_new = jnp.maximum(b_end + m_st, log_w.max(-1))
        w = jnp.exp(log_w - m_new[..., None])
        decay = jnp.exp(b_end + m_st - m_new)
        c_st = decay[..., None, None] * c_st + jnp.einsum('bhs,bhse,bhsk->bhek', w, vc, kc)
        n_st = decay[..., None] * n_st + jnp.einsum('bhs,bhsk->bhk', w, kc)
        return (c_st, n_st, m_new), h_t

    xs = (jnp.moveaxis(_chunk(qh, L), 2, 0), jnp.moveaxis(_chunk(kh, L), 2, 0),
          jnp.moveaxis(_chunk(vh, L), 2, 0), jnp.moveaxis(_chunk(i_pre, L), 2, 0),
          jnp.moveaxis(_chunk(log_f, L), 2, 0))
    init = (jnp.zeros((b, GROUP_HEADS, HEAD_DIM, HEAD_DIM), f32),
            jnp.zeros((b, GROUP_HEADS, HEAD_DIM), f32),
            jnp.zeros((b, GROUP_HEADS), f32))
    _, hs = lax.scan(step, init, xs)
    hs = jnp.moveaxis(hs, 0, 2).reshape(b, GROUP_HEADS, s, HEAD_DIM).transpose(0, 2, 1, 3)
    o = jax.nn.sigmoid(og.astype(f32)).reshape(b, s, GROUP_HEADS, HEAD_DIM)
    y = rms_norm(o * hs, norm_g.reshape(GROUP_HEADS, HEAD_DIM))
    return y.reshape(b, s, GROUP_W)


def gdn_group(h, conv_w, a_log, dt_bias, norm_g):
    f32 = jnp.float32
    b, s, _ = h.shape
    qkv, beta_pre, a_pre, z = _split(h, (3 * GROUP_W, GROUP_HEADS, GROUP_HEADS, GROUP_W))
    qkv = jax.nn.silu(causal_conv(qkv, conv_w))
    q, k, v = _split(qkv, (GROUP_W,) * 3)

    def l2n(t):
        return t * lax.rsqrt(jnp.sum(t * t, axis=-1, keepdims=True) + NORM_EPS)

    qh = l2n(_heads(q).astype(f32)) * HEAD_DIM ** -0.5
    kh = l2n(_heads(k).astype(f32))
    vh = _heads(v).astype(f32)
    beta = jax.nn.sigmoid(beta_pre.astype(f32)).transpose(0, 2, 1)
    g = (-jnp.exp(a_log.astype(f32)) * jax.nn.softplus((a_pre + dt_bias).astype(f32))).transpose(0, 2, 1)
    L = GDN_CHUNK
    qc, kc, vc = _chunk(qh, L), _chunk(kh, L), _chunk(vh, L)
    gc = jnp.cumsum(_chunk(g, L), axis=-1)
    bc = _chunk(beta, L)
    tri = jnp.tril(jnp.ones((L, L), dtype=bool))
    strict = jnp.tril(jnp.ones((L, L), dtype=bool), k=-1)
    decay = jnp.exp(jnp.where(tri, gc[..., :, None] - gc[..., None, :], -jnp.inf))
    kb = kc * bc[..., None]
    vb = vc * bc[..., None]
    a_mat = jnp.where(strict, jnp.einsum('bhnid,bhnjd->bhnij', kb, kc) * decay, 0.0)
    m_mat = a_mat + jnp.eye(L, dtype=f32)
    u = lax.linalg.triangular_solve(m_mat, vb, left_side=True, lower=True)
    w = lax.linalg.triangular_solve(m_mat, kb * jnp.exp(gc)[..., None], left_side=True, lower=True)
    attn_qk = jnp.einsum('bhnid,bhnjd->bhnij', qc, kc) * decay
    q_dec = qc * jnp.exp(gc)[..., None]
    k_dec = kc * jnp.exp(gc[..., -1:] - gc)[..., None]
    g_end = jnp.exp(gc[..., -1])

    def step(state, xs):
        qd, aqk, uc, wc, kd, ge = xs
        v_new = uc - jnp.einsum('bhlk,bhkv->bhlv', wc, state)
        o = jnp.einsum('bhlk,bhkv->bhlv', qd, state) + jnp.einsum('bhij,bhjv->bhiv', aqk, v_new)
        state = state * ge[..., None, None] + jnp.einsum('bhlk,bhlv->bhkv', kd, v_new)
        return state, o

    xs = (jnp.moveaxis(q_dec, 2, 0), jnp.moveaxis(attn_qk, 2, 0), jnp.moveaxis(u, 2, 0),
          jnp.moveaxis(w, 2, 0), jnp.moveaxis(k_dec, 2, 0), jnp.moveaxis(g_end, 2, 0))
    init = jnp.zeros((b, GROUP_HEADS, HEAD_DIM, HEAD_DIM), f32)
    _, os_ = lax.scan(step, init, xs)
    os_ = jnp.moveaxis(os_, 0, 2).reshape(b, GROUP_HEADS, s, HEAD_DIM).transpose(0, 2, 1, 3)
    zg = jax.nn.silu(z.astype(f32)).reshape(b, s, GROUP_HEADS, HEAD_DIM)
    return (rms_norm(os_, norm_g) * zg).reshape(b, s, GROUP_W)


def mla_group(h, positions, q_norm_g, kv_norm_g, w_uq, w_ukv):
    f32 = jnp.float32
    b, s, _ = h.shape
    c_q, c_kv, k_r = _split(h, (Q_LORA, KV_LORA, QK_ROPE))
    qf = (rms_norm(c_q, q_norm_g) @ w_uq).astype(f32).reshape(b, s, GROUP_HEADS, QK_NOPE + QK_ROPE)
    kvf = (rms_norm(c_kv, kv_norm_g) @ w_ukv).astype(f32).reshape(b, s, GROUP_HEADS, QK_NOPE + V_HEAD)
    inv_freq = ROPE_THETA ** (-jnp.arange(0, QK_ROPE, 2, dtype=f32) / QK_ROPE)
    ang = positions.astype(f32)[..., None] * inv_freq
    cos, sin = jnp.cos(ang), jnp.sin(ang)
    scale = (QK_NOPE + QK_ROPE) ** -0.5
    q_nope = qf[..., :QK_NOPE].transpose(0, 2, 1, 3) * scale
    q_rope = apply_rope(qf[..., QK_NOPE:], cos[:, :, None], sin[:, :, None]).transpose(0, 2, 1, 3) * scale
    k_nope = kvf[..., :QK_NOPE].transpose(0, 2, 1, 3)
    v = kvf[..., QK_NOPE:].transpose(0, 2, 1, 3)
    k_rope = apply_rope(k_r.astype(f32), cos, sin)
    nb = s // Q_BLOCK
    qn_b = jnp.moveaxis(q_nope.reshape(b, GROUP_HEADS, nb, Q_BLOCK, QK_NOPE), 2, 0)
    qr_b = jnp.moveaxis(q_rope.reshape(b, GROUP_HEADS, nb, Q_BLOCK, QK_ROPE), 2, 0)
    key_pos = jnp.arange(s)

    def block(args):
        qn, qr, bi = args
        t = bi * Q_BLOCK + jnp.arange(Q_BLOCK)
        sc = jnp.einsum('bhqd,bhkd->bhqk', qn, k_nope) + jnp.einsum('bhqr,bkr->bhqk', qr, k_rope)
        p = masked_softmax(sc, key_pos[None, :] <= t[:, None])
        return jnp.einsum('bhqk,bhkd->bhqd', p, v)

    o = lax.map(block, (qn_b, qr_b, jnp.arange(nb)))
    o = jnp.moveaxis(o, 0, 2).reshape(b, GROUP_HEADS, s, V_HEAD)
    return _unheads(o)


def hybrid_mixer(xn, positions, t5_table, w_in, w_out, nsa_w_cmp_k, nsa_w_cmp_v, nsa_cmp_pos,
                 mlstm_i_bias, mlstm_f_bias, mlstm_norm_g, gdn_conv_w, gdn_a_log, gdn_dt_bias,
                 gdn_norm_g, mla_q_norm_g, mla_kv_norm_g, mla_w_uq, mla_w_ukv):
    h = xn @ w_in
    h_nsa, h_ml, h_gdn, h_mla = _split(h, (NSA_IN, MLSTM_IN, GDN_IN, MLA_IN))
    o = jnp.concatenate([
        nsa_group(h_nsa, nsa_w_cmp_k, nsa_w_cmp_v, nsa_cmp_pos, t5_table).astype(xn.dtype),
        mlstm_group(h_ml, mlstm_i_bias, mlstm_f_bias, mlstm_norm_g).astype(xn.dtype),
        gdn_group(h_gdn, gdn_conv_w, gdn_a_log, gdn_dt_bias, gdn_norm_g).astype(xn.dtype),
        mla_group(h_mla, positions, mla_q_norm_g, mla_kv_norm_g, mla_w_uq, mla_w_ukv).astype(xn.dtype),
    ], axis=-1)
    return o @ w_out


def peer_ffn(xn, w_q, sub_keys, u_tab, v_tab):
    b, s, d = xn.shape
    half = PEER_DK // 2
    xt = xn.reshape(b * s // PEER_CHUNK, PEER_CHUNK, d)

    def chunk(xc):
        qh = (xc @ w_q).reshape(PEER_CHUNK, PEER_HEADS, 2, half)
        sc = jnp.einsum('tpcd,pcnd->tpcn', qh, sub_keys).astype(jnp.float32)
        s1, i1 = lax.top_k(sc[:, :, 0], PEER_TOPK)
        s2, i2 = lax.top_k(sc[:, :, 1], PEER_TOPK)
        cand = (s1[..., :, None] + s2[..., None, :]).reshape(PEER_CHUNK, PEER_HEADS, PEER_TOPK * PEER_TOPK)
        cidx = (i1[..., :, None] * N_KEYS + i2[..., None, :]).reshape(PEER_CHUNK, PEER_HEADS, PEER_TOPK * PEER_TOPK)
        top, pos = lax.top_k(cand, PEER_TOPK)
        eidx = jnp.take_along_axis(cidx, pos, axis=-1)
        gate = jax.nn.softmax(top, axis=-1)
        u = u_tab[eidx]
        v = v_tab[eidx]
        act = jax.nn.gelu(jnp.einsum('td,tpkd->tpk', xc, u).astype(jnp.float32), approximate=False)
        return jnp.einsum('tpk,tpkd->td', (gate * act).astype(xc.dtype), v)

    return lax.map(chunk, xt).reshape(b, s, d)


def setup_inputs(seed: int = 0) -> dict:
    key = jax.random.key(seed)
    ks = jax.random.split(key, 32)
    f32 = jnp.float32

    def nrm(k, shape, scale):
        return jax.random.normal(k, shape, f32) * scale

    dt = jnp.exp(jax.random.uniform(ks[16], (DEPTH, GROUP_HEADS), f32)
                 * (math.log(0.1) - math.log(0.001)) + math.log(0.001))
    return {
        'x': nrm(ks[0], (BATCH, SEQ, D_MODEL), 1.0),
        'positions': jnp.tile(jnp.arange(SEQ, dtype=jnp.int32)[None, :], (BATCH, 1)),
        't5_table': nrm(ks[1], (N_BUCKETS, GROUP_HEADS), 0.5),
        'norm1_g': 1.0 + nrm(ks[2], (DEPTH, D_MODEL), 0.02),
        'norm2_g': 1.0 + nrm(ks[3], (DEPTH, D_MODEL), 0.02),
        'final_norm_g': 1.0 + nrm(ks[4], (D_MODEL,), 0.02),
        'w_in': nrm(ks[5], (DEPTH, D_MODEL, D_IN), D_MODEL ** -0.5),
        'w_out': nrm(ks[6], (DEPTH, D_MIX, D_MODEL), D_MIX ** -0.5),
        'nsa_w_cmp_k': nrm(ks[7], (DEPTH, CMP_LEN * HEAD_DIM, HEAD_DIM), (CMP_LEN * HEAD_DIM) ** -0.5),
        'nsa_w_cmp_v': nrm(ks[8], (DEPTH, CMP_LEN * HEAD_DIM, HEAD_DIM), (CMP_LEN * HEAD_DIM) ** -0.5),
        'nsa_cmp_pos': nrm(ks[9], (DEPTH, CMP_LEN, HEAD_DIM), 0.1),
        'mlstm_i_bias': nrm(ks[10], (DEPTH, GROUP_HEADS), 0.1),
        'mlstm_f_bias': jnp.linspace(3.0, 6.0, GROUP_HEADS, dtype=f32)[None, :] + nrm(ks[11], (DEPTH, GROUP_HEADS), 0.1),
        'mlstm_norm_g': 1.0 + nrm(ks[12], (DEPTH, GROUP_W), 0.02),
        'gdn_conv_w': nrm(ks[13], (DEPTH, CONV_W, 3 * GROUP_W), CONV_W ** -0.5),
        'gdn_a_log': jnp.log(jax.random.uniform(ks[14], (DEPTH, GROUP_HEADS), f32, 1.0, 16.0)),
        'gdn_dt_bias': dt + jnp.log(-jnp.expm1(-dt)),
        'gdn_norm_g': 1.0 + nrm(ks[15], (DEPTH, HEAD_DIM), 0.02),
        'mla_q_norm_g': 1.0 + nrm(ks[17], (DEPTH, Q_LORA), 0.02),
        'mla_kv_norm_g': 1.0 + nrm(ks[18], (DEPTH, KV_LORA), 0.02),
        'mla_w_uq': nrm(ks[19], (DEPTH, Q_LORA, GROUP_HEADS * (QK_NOPE + QK_ROPE)), Q_LORA ** -0.5),
        'mla_w_ukv': nrm(ks[20], (DEPTH, KV_LORA, GROUP_HEADS * (QK_NOPE + V_HEAD)), KV_LORA ** -0.5),
        'peer_w_q': nrm(ks[21], (DEPTH, D_MODEL, PEER_HEADS * PEER_DK), D_MODEL ** -0.5),
        'peer_sub_keys': nrm(ks[22], (DEPTH, PEER_HEADS, 2, N_KEYS, PEER_DK // 2), (PEER_DK // 2) ** -0.5),
        'peer_u': nrm(ks[23], (DEPTH, N_EXPERTS, D_MODEL), D_MODEL ** -0.5),
        'peer_v': nrm(ks[24], (DEPTH, N_EXPERTS, D_MODEL), 0.2),
    }


def reference(x, positions, t5_table, norm1_g, norm2_g, final_norm_g, w_in, w_out,
              nsa_w_cmp_k, nsa_w_cmp_v, nsa_cmp_pos, mlstm_i_bias, mlstm_f_bias, mlstm_norm_g,
              gdn_conv_w, gdn_a_log, gdn_dt_bias, gdn_norm_g, mla_q_norm_g, mla_kv_norm_g,
              mla_w_uq, mla_w_ukv, peer_w_q, peer_sub_keys, peer_u, peer_v):
    h = x
    for l in range(DEPTH):
        xn = rms_norm(h, norm1_g[l])
        h = h + hybrid_mixer(xn, positions, t5_table, w_in[l], w_out[l], nsa_w_cmp_k[l],
                             nsa_w_cmp_v[l], nsa_cmp_pos[l], mlstm_i_bias[l], mlstm_f_bias[l],
                             mlstm_norm_g[l], gdn_conv_w[l], gdn_a_log[l], gdn_dt_bias[l],
                             gdn_norm_g[l], mla_q_norm_g[l], mla_kv_norm_g[l], mla_w_uq[l],
                             mla_w_ukv[l])
        h = h + peer_ffn(rms_norm(h, norm2_g[l]), peer_w_q[l], peer_sub_keys[l], peer_u[l], peer_v[l])
    return rms_norm(h, final_norm_g)
```

```python
import functools
import math

import jax
import jax.numpy as jnp
import numpy as np
from jax import lax
from jax.experimental import pallas as pl
from jax.experimental.pallas import tpu as pltpu

F32 = jnp.float32
BF16 = jnp.bfloat16
NORM_EPS = 1e-6
NEG = -1e30
LANE = 128
ROW_TILE = 512
VMEM_LIMIT = 56 * 1024 * 1024

HEAD_DIM = 64
GROUP_HEADS = 4
GROUP_W = GROUP_HEADS * HEAD_DIM
Q_BLOCK = 128
CMP_LEN = 32
CMP_STRIDE = 16
SEL_LEN = 64
SEL_TOPN = 16
WINDOW = 512
SEL_FORCE = 1e3
N_BUCKETS = 32
MAX_DISTANCE = 128


def _round_up(n, m):
    return (n + m - 1) // m * m


def _dot_nt(a, b):
    return lax.dot_general(a, b, (((1,), (1,)), ((), ())), preferred_element_type=F32)


def _dot(a, b):
    return jnp.dot(a, b, preferred_element_type=F32)


def _norm_matmul_kernel(x_ref, g_ref, w_ref, o_ref):
    x = x_ref[...]
    y = x * lax.rsqrt(jnp.mean(x * x, axis=-1, keepdims=True) + NORM_EPS) * g_ref[...]
    o_ref[...] = _dot(y.astype(BF16), w_ref[...])


def norm_matmul(x, g, w):
    t, d = x.shape
    n = w.shape[1]
    return pl.pallas_call(
        _norm_matmul_kernel,
        grid=(t // ROW_TILE,),
        in_specs=[pl.BlockSpec((ROW_TILE, d), lambda i: (i, 0)),
                  pl.BlockSpec((1, d), lambda i: (0, 0)),
                  pl.BlockSpec((d, n), lambda i: (0, 0))],
        out_specs=pl.BlockSpec((ROW_TILE, n), lambda i: (i, 0)),
        out_shape=jax.ShapeDtypeStruct((t, n), F32),
        compiler_params=pltpu.CompilerParams(dimension_semantics=("arbitrary",),
                                             vmem_limit_bytes=VMEM_LIMIT),
        name="norm_matmul",
    )(x, g.reshape(1, d), w)


def _matmul_residual_kernel(o_ref, w_ref, r_ref, out_ref):
    out_ref[...] = r_ref[...] + _dot(o_ref[...].astype(BF16), w_ref[...])


def matmul_residual(o, w, res):
    t, d = o.shape
    n = w.shape[1]
    return pl.pallas_call(
        _matmul_residual_kernel,
        grid=(t // ROW_TILE,),
        in_specs=[pl.BlockSpec((ROW_TILE, d), lambda i: (i, 0)),
                  pl.BlockSpec((d, n), lambda i: (0, 0)),
                  pl.BlockSpec((ROW_TILE, n), lambda i: (i, 0))],
        out_specs=pl.BlockSpec((ROW_TILE, n), lambda i: (i, 0)),
        out_shape=jax.ShapeDtypeStruct((t, n), F32),
        compiler_params=pltpu.CompilerParams(dimension_semantics=("arbitrary",),
                                             vmem_limit_bytes=VMEM_LIMIT),
        name="matmul_residual",
    )(o, w, res)


def _rmsnorm_kernel(x_ref, g_ref, o_ref):
    x = x_ref[...]
    o_ref[...] = x * lax.rsqrt(jnp.mean(x * x, axis=-1, keepdims=True) + NORM_EPS) * g_ref[...]


def rmsnorm(x, g):
    t, d = x.shape
    return pl.pallas_call(
        _rmsnorm_kernel,
        grid=(t // ROW_TILE,),
        in_specs=[pl.BlockSpec((ROW_TILE, d), lambda i: (i, 0)),
                  pl.BlockSpec((1, d), lambda i: (0, 0))],
        out_specs=pl.BlockSpec((ROW_TILE, d), lambda i: (i, 0)),
        out_shape=jax.ShapeDtypeStruct((t, d), F32),
        compiler_params=pltpu.CompilerParams(dimension_semantics=("arbitrary",)),
        name="rmsnorm",
    )(x, g.reshape(1, d))


def _t5_bucket_starts():
    d = np.arange(0, 2 * MAX_DISTANCE)
    max_exact = N_BUCKETS // 2
    val = (np.log(np.maximum(d, 1).astype(np.float32) / np.float32(max_exact))
           / np.float32(math.log(MAX_DISTANCE / max_exact)) * np.float32(N_BUCKETS - max_exact))
    large = np.minimum(max_exact + val.astype(np.int32), N_BUCKETS - 1)
    bucket = np.where(d < max_exact, d, large)
    assert np.all(np.diff(bucket) >= 0)
    return [(int(k), int(d[bucket == k].min())) for k in range(N_BUCKETS) if np.any(bucket == k)]


T5_STARTS = _t5_bucket_starts()


def _t5_bias_heads(dist, tab_ref):
    accs = [jnp.full(dist.shape, tab_ref[0, h], F32) for h in range(GROUP_HEADS)]
    for bucket, start in T5_STARTS[1:]:
        m = dist >= start
        accs = [jnp.where(m, tab_ref[bucket, h], accs[h]) for h in range(GROUP_HEADS)]
    return jnp.stack(accs)


def _nsa_compress_kernel(x_ref, pos_ref, w_ref, o_ref):
    x = x_ref[0, 0]
    n = x.shape[0]
    lo = _dot((x + pos_ref[0, 0:1, :]).astype(BF16), w_ref[0, 0])
    hi = _dot((x + pos_ref[0, 1:2, :]).astype(BF16), w_ref[0, 1])
    o_ref[0, 0] = (lo + pltpu.roll(hi, n - 1, axis=0)).astype(o_ref.dtype)


def nsa_compress(x, pos, w):
    two, b, n, c = x.shape
    return pl.pallas_call(
        _nsa_compress_kernel,
        grid=(two, b),
        in_specs=[pl.BlockSpec((1, 1, n, c), lambda k, i: (k, i, 0, 0)),
                  pl.BlockSpec((1, 2, c), lambda k, i: (0, 0, 0)),
                  pl.BlockSpec((1, 2, c, HEAD_DIM), lambda k, i: (k, 0, 0, 0))],
        out_specs=pl.BlockSpec((1, 1, n, HEAD_DIM), lambda k, i: (k, i, 0, 0)),
        out_shape=jax.ShapeDtypeStruct((two, b, n, HEAD_DIM), BF16),
        compiler_params=pltpu.CompilerParams(dimension_semantics=("arbitrary", "arbitrary")),
        name="nsa_compress",
    )(x, pos, w)


FAR_TILE = 256
NEAR_KEYS = 384
WIN_KEYS = WINDOW + Q_BLOCK


def _nsa_attn_kernel(tab_ref, q_ref, g_ref, kc_ref, vc_ref, ks_ref, vs_ref, kw_ref, vw_ref,
                     o_ref, m_scr, acc_scr):
    H, Q, D = GROUP_HEADS, Q_BLOCK, HEAD_DIM
    seq = ks_ref.shape[1]
    n_cmp = kc_ref.shape[1]
    n_sel = seq // SEL_LEN
    i = pl.program_id(1)
    t0 = i * Q

    q = q_ref[0] * (D ** -0.5)
    q4 = jnp.concatenate([q[:, h * D:(h + 1) * D] for h in range(H)], axis=0).astype(BF16)

    s = _dot_nt(q4, kc_ref[0]).reshape(H, Q, n_cmp)
    t_c = t0 + lax.broadcasted_iota(jnp.int32, (Q, n_cmp), 0)
    dist_c = t_c - (lax.broadcasted_iota(jnp.int32, (Q, n_cmp), 1) * CMP_STRIDE + CMP_LEN - 1)
    valid_c = dist_c >= 0
    s = jnp.where(valid_c, s + _t5_bias_heads(dist_c, tab_ref), NEG)
    p = jnp.where(valid_c, jnp.exp(s - jnp.max(s, axis=-1, keepdims=True)), 0.0)
    l = jnp.sum(p, axis=-1, keepdims=True)
    p = p * (1.0 / jnp.where(l > 0.0, l, 1.0))
    o_c = _dot(p.reshape(H * Q, n_cmp).astype(BF16), vc_ref[0])

    psum = p[0] + p[1] + p[2] + p[3]
    c_idx = lax.broadcasted_iota(jnp.int32, (n_cmp, n_sel), 0)
    j_idx = lax.broadcasted_iota(jnp.int32, (n_cmp, n_sel), 1)
    ratio = SEL_LEN // CMP_STRIDE
    n_in_sel = (SEL_LEN - CMP_LEN) // CMP_STRIDE + 1
    pool = ((c_idx // ratio == j_idx) & (c_idx % ratio < n_in_sel)).astype(BF16)
    p_hi = psum.astype(BF16)
    p_mid = (psum - p_hi.astype(F32)).astype(BF16)
    p_lo = (psum - p_hi.astype(F32) - p_mid.astype(F32)).astype(BF16)
    imp = _dot(p_hi, pool) + _dot(p_mid, pool) + _dot(p_lo, pool)

    jb = lax.broadcasted_iota(jnp.int32, (Q, n_sel), 1)
    cur = (t0 + lax.broadcasted_iota(jnp.int32, (Q, n_sel), 0)) // SEL_LEN
    forced = (jb == 0) | (jb == cur) | (jb == cur - 1)
    work = jnp.where(jb <= cur, imp + jnp.where(forced, SEL_FORCE, 0.0), NEG)
    sel = jnp.zeros((Q, n_sel), F32)
    for _ in range(min(SEL_TOPN, n_sel)):
        mx = jnp.max(work, axis=-1, keepdims=True)
        first = jnp.min(jnp.where(work == mx, jb, n_sel), axis=-1, keepdims=True)
        pick = jb == first
        sel = jnp.where(pick & (mx > 0.5 * NEG), 1.0, sel)
        work = jnp.where(pick, -jnp.inf, work)
    sel_bf = sel.astype(BF16)

    far_bias = jnp.stack([jnp.full((Q, 1), tab_ref[N_BUCKETS - 1, h], F32) for h in range(H)])
    m_scr[...] = jnp.full(m_scr.shape, NEG, F32)
    acc_scr[...] = jnp.zeros(acc_scr.shape, F32)

    def sel_step(start, n_keys, near, first_key):
        k = ks_ref[0, pl.ds(start, n_keys), :]
        v = vs_ref[0, pl.ds(start, n_keys), :]
        s = _dot_nt(q4, k).reshape(H, Q, n_keys)
        blk = (start // SEL_LEN) + lax.broadcasted_iota(jnp.int32, (n_sel, n_keys), 1) // SEL_LEN
        expand = (lax.broadcasted_iota(jnp.int32, (n_sel, n_keys), 0) == blk).astype(BF16)
        mask = _dot(sel_bf, expand) > 0.5
        if near:
            pos = start + lax.broadcasted_iota(jnp.int32, (Q, n_keys), 1)
            dist = t0 + lax.broadcasted_iota(jnp.int32, (Q, n_keys), 0) - pos
            mask = mask & (dist >= 0) & (pos >= first_key)
            s = s + _t5_bias_heads(dist, tab_ref)
        else:
            s = s + far_bias
        s = jnp.where(mask, s, NEG)
        m_old = m_scr[...].reshape(H, Q, 1)
        m_new = jnp.maximum(m_old, jnp.max(s, axis=-1, keepdims=True))
        p = jnp.where(mask, jnp.exp(s - m_new), 0.0).reshape(H * Q, n_keys)
        alpha = jnp.exp(m_old - m_new).reshape(H * Q, 1)
        acc_scr[...] = alpha * acc_scr[...] + _dot(p.astype(BF16), v)
        m_scr[...] = m_new.reshape(H * Q, 1)

    n_far = jnp.maximum(i - 1, 0) // 2

    def far_body(kt, carry):
        sel_step(pl.multiple_of(kt * FAR_TILE, FAR_TILE), FAR_TILE, False, 0)
        return carry

    lax.fori_loop(0, n_far, far_body, 0)
    near_start = pl.multiple_of(jnp.maximum(i - 2, 0) * Q, Q)
    sel_step(near_start, NEAR_KEYS, True, n_far * FAR_TILE)
    acc = acc_scr[...]
    o_s = acc[:, :D] * (1.0 / acc[:, D:D + 1])

    w_start = pl.multiple_of(jnp.maximum(i - WINDOW // Q, 0) * Q, Q)
    kw = kw_ref[0, pl.ds(w_start, WIN_KEYS), :]
    vw = vw_ref[0, pl.ds(w_start, WIN_KEYS), :]
    dist_w = (t0 + lax.broadcasted_iota(jnp.int32, (Q, WIN_KEYS), 0)
              - (w_start + lax.broadcasted_iota(jnp.int32, (Q, WIN_KEYS), 1)))
    valid_w = (dist_w >= 0) & (dist_w < WINDOW)
    s = _dot_nt(q4, kw).reshape(H, Q, WIN_KEYS) + _t5_bias_heads(dist_w, tab_ref)
    s = jnp.where(valid_w, s, NEG)
    p = jnp.where(valid_w, jnp.exp(s - jnp.max(s, axis=-1, keepdims=True)), 0.0)
    acc_w = _dot(p.reshape(H * Q, WIN_KEYS).astype(BF16), vw)
    o_w = acc_w[:, :D] * (1.0 / acc_w[:, D:D + 1])

    g = jax.nn.sigmoid(g_ref[0])
    outs = []
    for h in range(H):
        rows = slice(h * Q, (h + 1) * Q)
        outs.append(g[:, h:h + 1] * o_c[rows] + g[:, H + h:H + h + 1] * o_s[rows]
                    + g[:, 2 * H + h:2 * H + h + 1] * o_w[rows])
    o_ref[0] = jnp.concatenate(outs, axis=1)


def nsa_attention(t5_table, q, gates, k_cmp, v_cmp, ks, vs_aug, kw, vw_aug):
    b, s, _ = q.shape
    n_cmp = k_cmp.shape[1]
    full = lambda bi, i: (bi, 0, 0)
    return pl.pallas_call(
        _nsa_attn_kernel,
        grid=(b, s // Q_BLOCK),
        in_specs=[pl.BlockSpec(memory_space=pltpu.SMEM),
                  pl.BlockSpec((1, Q_BLOCK, GROUP_W), lambda bi, i: (bi, i, 0)),
                  pl.BlockSpec((1, Q_BLOCK, 3 * GROUP_HEADS), lambda bi, i: (bi, i, 0)),
                  pl.BlockSpec((1, n_cmp, HEAD_DIM), full),
                  pl.BlockSpec((1, n_cmp, HEAD_DIM), full),
                  pl.BlockSpec((1, s, HEAD_DIM), full),
                  pl.BlockSpec((1, s, 2 * HEAD_DIM), full),
                  pl.BlockSpec((1, s, HEAD_DIM), full),
                  pl.BlockSpec((1, s, 2 * HEAD_DIM), full)],
        out_specs=pl.BlockSpec((1, Q_BLOCK, GROUP_W), lambda bi, i: (bi, i, 0)),
        out_shape=jax.ShapeDtypeStruct((b, s, GROUP_W), F32),
        scratch_shapes=[pltpu.VMEM((GROUP_HEADS * Q_BLOCK, 1), F32),
                        pltpu.VMEM((GROUP_HEADS * Q_BLOCK, 2 * HEAD_DIM), F32)],
        compiler_params=pltpu.CompilerParams(dimension_semantics=("arbitrary", "arbitrary"),
                                             vmem_limit_bytes=VMEM_LIMIT),
        name="nsa_attention",
    )(t5_table, q, gates, k_cmp, v_cmp, ks, vs_aug, kw, vw_aug)


def _with_ones(v):
    b, s, d = v.shape
    pad = jnp.zeros((b, s, d), BF16).at[:, :, 0].set(1.0)
    return jnp.concatenate([v.astype(BF16), pad], axis=-1)


def nsa_group(h, w_cmp_k, w_cmp_v, cmp_pos, t5_table):
    b, s, _ = h.shape
    d = HEAD_DIM
    q = h[..., :GROUP_W]
    kc, vc, ks, vs, kw, vw = [h[..., GROUP_W + j * d: GROUP_W + (j + 1) * d] for j in range(6)]
    gates = h[..., GROUP_W + 6 * d:]
    n = s // CMP_STRIDE
    x = jnp.stack([kc, vc]).reshape(2, b, n, CMP_STRIDE * d)
    pos = cmp_pos.reshape(1, 2, CMP_STRIDE * d)
    w = jnp.stack([w_cmp_k, w_cmp_v]).reshape(2, 2, CMP_STRIDE * d, d).astype(BF16)
    cmp = nsa_compress(x, pos, w)
    return nsa_attention(t5_table, q, gates, cmp[0], cmp[1], ks.astype(BF16), _with_ones(vs),
                         kw.astype(BF16), _with_ones(vw))


PEER_HEADS = 8
PEER_DK = 256
N_KEYS = 128
PEER_TOPK = 16
PEER_SLOTS = PEER_HEADS * PEER_TOPK
ROUTE_TILE = 128
GATHER_TILE = 8


def _top_rows(work, order, n_top, payload=None):
    vals, args = [], []
    for _ in range(n_top):
        mx = jnp.max(work, axis=0, keepdims=True)
        first = jnp.min(jnp.where(work == mx, order, jnp.int32(2 ** 30)), axis=0, keepdims=True)
        pick = order == first
        vals.append(mx)
        if payload is None:
            args.append(first)
        else:
            args.append(jnp.sum(jnp.where(pick, payload, 0), axis=0, keepdims=True))
        work = jnp.where(pick, -jnp.inf, work)
    return jnp.concatenate(vals, axis=0), jnp.concatenate(args, axis=0)


def _pair_rows(s1, s2, combine):
    k = PEER_TOPK
    rows = [combine(s1[0:1], s2)]
    for a in range(1, k // 2):
        rows.append(combine(s1[a:a + 1], s2[0:k // 2]))
    rows.append(combine(s1[k // 2:], s2[0:1]))
    return jnp.concatenate(rows, axis=0)


def _peer_route_kernel(h_ref, g_ref, wq_ref, keys_ref, xn_ref, eidx_ref, gate_ref):
    x = h_ref[...]
    xn = x * lax.rsqrt(jnp.mean(x * x, axis=-1, keepdims=True) + NORM_EPS) * g_ref[...]
    xn_ref[...] = xn
    q = _dot(xn.astype(BF16), wq_ref[...]).astype(BF16)
    tq = x.shape[0]
    half = PEER_DK // 2
    n_iota = lax.broadcasted_iota(jnp.int32, (N_KEYS, tq), 0)
    k_iota = lax.broadcasted_iota(jnp.int32, (PEER_TOPK, tq), 0)
    order = _pair_rows(k_iota, k_iota, lambda a, b: a * PEER_TOPK + b)
    reachable = (order // PEER_TOPK + 1) * (order % PEER_TOPK + 1) <= PEER_TOPK
    for p in range(PEER_HEADS):
        tops = []
        for c in range(2):
            g = 2 * p + c
            sc = _dot_nt(keys_ref[g], q[:, g * half:(g + 1) * half])
            tops.append(_top_rows(sc, n_iota, PEER_TOPK))
        (s1, i1), (s2, i2) = tops
        cand = jnp.where(reachable, _pair_rows(s1, s2, lambda a, b: a + b), -jnp.inf)
        cidx = _pair_rows(i1, i2, lambda a, b: a * N_KEYS + b)
        top, eidx = _top_rows(cand, order, PEER_TOPK, payload=cidx)
        e = jnp.exp(top - top[0:1])
        eidx_ref[p * PEER_TOPK:(p + 1) * PEER_TOPK, :] = eidx
        gate_ref[p * PEER_TOPK:(p + 1) * PEER_TOPK, :] = e * (1.0 / jnp.sum(e, axis=0, keepdims=True))


def peer_route(h, g, w_q, keys):
    t, d = h.shape
    nq = w_q.shape[1]
    return pl.pallas_call(
        _peer_route_kernel,
        grid=(t // ROUTE_TILE,),
        in_specs=[pl.BlockSpec((ROUTE_TILE, d), lambda i: (i, 0)),
                  pl.BlockSpec((1, d), lambda i: (0, 0)),
                  pl.BlockSpec((d, nq), lambda i: (0, 0)),
                  pl.BlockSpec(keys.shape, lambda i: (0, 0, 0))],
        out_specs=[pl.BlockSpec((ROUTE_TILE, d), lambda i: (i, 0)),
                   pl.BlockSpec((PEER_SLOTS, ROUTE_TILE), lambda i: (0, i)),
                   pl.BlockSpec((PEER_SLOTS, ROUTE_TILE), lambda i: (0, i))],
        out_shape=[jax.ShapeDtypeStruct((t, d), F32),
                   jax.ShapeDtypeStruct((PEER_SLOTS, t), jnp.int32),
                   jax.ShapeDtypeStruct((PEER_SLOTS, t), F32)],
        compiler_params=pltpu.CompilerParams(dimension_semantics=("arbitrary",),
                                             vmem_limit_bytes=VMEM_LIMIT),
        name="peer_route",
    )(h, g.reshape(1, d), w_q, keys)


SUBLANES = 8
SLAB = 2 * SUBLANES
ISSUE_UNROLL = 8


def _gelu(x):
    return 0.5 * x * (1.0 + lax.erf(x * (2.0 ** -0.5)))


def _peer_gather_kernel(idx_ref, idx_next_ref, xn_ref, h_ref, gate_ref, uv_hbm, o_ref, buf, sem):
    i = pl.program_id(0)
    n = pl.num_programs(0)
    tt = xn_ref.shape[0]
    n_slabs = tt * PEER_SLOTS

    def request_slabs(ref, slot):
        def body(c, carry):
            for k in range(ISSUE_UNROLL):
                r = c * ISSUE_UNROLL + k
                src = pl.multiple_of(ref[0, 0, r] * SLAB, SLAB)
                dst = pl.multiple_of(r * SLAB, SLAB)
                pltpu.make_async_copy(uv_hbm.at[pl.ds(src, SLAB)], buf.at[slot, pl.ds(dst, SLAB)],
                                      sem.at[slot]).start()
            return carry
        lax.fori_loop(0, n_slabs // ISSUE_UNROLL, body, 0)

    @pl.when(i == 0)
    def _():
        request_slabs(idx_ref, 0)

    @pl.when(i + 1 < n)
    def _():
        request_slabs(idx_next_ref, (i + 1) % 2)

    slot = i % 2
    pltpu.make_async_copy(uv_hbm.at[pl.ds(0, n_slabs * SLAB)], buf.at[slot], sem.at[slot]).wait()
    gates = gate_ref[0]
    for t in range(tt):
        base = t * PEER_SLOTS * SLAB

        def plane(s):
            return buf[slot, pl.ds(base + s, PEER_SLOTS, stride=SLAB), :]

        acc = plane(0) * xn_ref[t, 0:1, :]
        for s in range(1, SUBLANES):
            acc = acc + plane(s) * xn_ref[t, s:s + 1, :]
        w = gates[:, t:t + 1] * _gelu(jnp.sum(acc, axis=-1, keepdims=True))
        out = [jnp.sum(plane(SUBLANES + s) * w, axis=0, keepdims=True) for s in range(SUBLANES)]
        o_ref[t] = h_ref[t] + jnp.concatenate(out, axis=0)


def peer_gather(idx, xn, h, gate_b, uv):
    t = xn.shape[0]
    nt = t // GATHER_TILE
    n_slabs = GATHER_TILE * PEER_SLOTS
    tok = pl.BlockSpec((GATHER_TILE, SUBLANES, LANE), lambda i: (i, 0, 0))
    return pl.pallas_call(
        _peer_gather_kernel,
        grid=(nt,),
        in_specs=[pl.BlockSpec((1, 1, n_slabs), lambda i: (i, 0, 0), memory_space=pltpu.SMEM),
                  pl.BlockSpec((1, 1, n_slabs), lambda i: (jnp.minimum(i + 1, nt - 1), 0, 0),
                               memory_space=pltpu.SMEM),
                  tok, tok,
                  pl.BlockSpec((1, PEER_SLOTS, GATHER_TILE), lambda i: (i, 0, 0)),
                  pl.BlockSpec(memory_space=pl.ANY)],
        out_specs=tok,
        out_shape=jax.ShapeDtypeStruct((t, SUBLANES, LANE), F32),
        scratch_shapes=[pltpu.VMEM((2, n_slabs * SLAB, LANE), F32), pltpu.SemaphoreType.DMA((2,))],
        compiler_params=pltpu.CompilerParams(dimension_semantics=("arbitrary",),
                                             vmem_limit_bytes=VMEM_LIMIT),
        name="peer_gather",
    )(idx, idx, xn, h, gate_b, uv)


def peer_ffn_residual(h, g, w_q, sub_keys, u_tab, v_tab):
    t, d = h.shape
    e = u_tab.shape[0]
    keys = sub_keys.reshape(PEER_HEADS * 2, N_KEYS, PEER_DK // 2).astype(BF16)
    xn, eidx_t, gate_t = peer_route(h, g, w_q.astype(BF16), keys)
    nt = t // GATHER_TILE
    idx = eidx_t.T.reshape(nt, 1, GATHER_TILE * PEER_SLOTS)
    gate_b = gate_t.reshape(PEER_SLOTS, nt, GATHER_TILE).transpose(1, 0, 2)
    uv = jnp.concatenate([u_tab.reshape(e, SUBLANES, LANE), v_tab.reshape(e, SUBLANES, LANE)],
                         axis=1).reshape(e * SLAB, LANE)
    out = peer_gather(idx, xn.reshape(t, SUBLANES, LANE), h.reshape(t, SUBLANES, LANE), gate_b, uv)
    return out.reshape(t, d)


Q_LORA = 256
KV_LORA = 128
QK_NOPE = 64
QK_ROPE = 32
V_HEAD = 64
ROPE_THETA = 10000.0
MLA_TQ = 256
MLA_TK = 512


def _rms(x, g):
    return x * lax.rsqrt(jnp.mean(x * x, axis=-1, keepdims=True) + NORM_EPS) * g


def _mla_prep_kernel(pos_ref, freq_ref, cq_ref, ckv_ref, kr_ref, gq_ref, gkv_ref, wq_ref, wk_ref, wv_ref,
                     q_out, k_out, v_out):
    rows = cq_ref.shape[0]
    half = QK_ROPE // 2
    cq = _rms(cq_ref[...], gq_ref[...]).astype(BF16)
    ckv = _rms(ckv_ref[...], gkv_ref[...]).astype(BF16)
    qf = _dot(cq, wq_ref[...])
    kf = _dot(ckv, wk_ref[...])
    vf = _dot(ckv, wv_ref[...])
    lane = lax.broadcasted_iota(jnp.int32, (rows, LANE), 1)
    ang = pos_ref[...].astype(F32) * freq_ref[...]
    cos, sin = jnp.cos(ang), jnp.sin(ang)
    in_rope = (lane >= QK_NOPE) & (lane < QK_NOPE + QK_ROPE)
    first = lane < QK_NOPE + half
    c_mul = jnp.where(lane < QK_NOPE, 1.0, jnp.where(in_rope, cos, 0.0))
    s_mul = jnp.where(in_rope, jnp.where(first, -sin, sin), 0.0)

    def rope(x):
        partner = jnp.where(first, pltpu.roll(x, LANE - half, axis=1), pltpu.roll(x, half, axis=1))
        return x * c_mul + partner * s_mul

    scale = (QK_NOPE + QK_ROPE) ** -0.5
    kr = rope(jnp.concatenate([jnp.zeros((rows, QK_NOPE), F32), kr_ref[...],
                               jnp.zeros((rows, LANE - QK_NOPE - QK_ROPE), F32)], axis=1))
    ones_col = jnp.where(lane == V_HEAD, 1.0, 0.0)
    for h in range(GROUP_HEADS):
        cols = slice(h * LANE, (h + 1) * LANE)
        q_out[:, cols] = (rope(qf[:, cols]) * scale).astype(BF16)
        k_out[:, cols] = (kf[:, cols] + kr).astype(BF16)
        v_out[:, cols] = (vf[:, cols] + ones_col).astype(BF16)


def mla_prep(pos, c_q, c_kv, k_r, gq, gkv, w_uq, w_ukv):
    t = c_q.shape[0]
    H = GROUP_HEADS
    wq = jnp.pad(w_uq.reshape(Q_LORA, H, QK_NOPE + QK_ROPE), ((0, 0), (0, 0), (0, LANE - QK_NOPE - QK_ROPE)))
    wkv = w_ukv.reshape(KV_LORA, H, QK_NOPE + V_HEAD)
    wk = jnp.pad(wkv[:, :, :QK_NOPE], ((0, 0), (0, 0), (0, LANE - QK_NOPE)))
    wv = jnp.pad(wkv[:, :, QK_NOPE:], ((0, 0), (0, 0), (0, LANE - V_HEAD)))
    wq, wk, wv = [w.reshape(w.shape[0], H * LANE).astype(BF16) for w in (wq, wk, wv)]
    inv_freq = ROPE_THETA ** (-np.arange(0, QK_ROPE, 2, dtype=np.float32) / QK_ROPE)
    freq = np.zeros((1, LANE), np.float32)
    freq[0, QK_NOPE:QK_NOPE + QK_ROPE] = np.tile(inv_freq, 2)
    row = lambda n: pl.BlockSpec((ROW_TILE, n), lambda i: (i, 0))
    fixed = lambda a: pl.BlockSpec(a.shape, lambda i: (0,) * a.ndim)
    gq2, gkv2, freq = gq.reshape(1, -1), gkv.reshape(1, -1), jnp.asarray(freq)
    out = jax.ShapeDtypeStruct((t, H * LANE), BF16)
    return pl.pallas_call(
        _mla_prep_kernel,
        grid=(t // ROW_TILE,),
        in_specs=[row(1), fixed(freq), row(Q_LORA), row(KV_LORA), row(QK_ROPE), fixed(gq2), fixed(gkv2),
                  fixed(wq), fixed(wk), fixed(wv)],
        out_specs=[row(H * LANE)] * 3,
        out_shape=[out] * 3,
        compiler_params=pltpu.CompilerParams(dimension_semantics=("arbitrary",)),
        name="mla_prep",
    )(pos, freq, c_q, c_kv, k_r, gq2, gkv2, wq, wk, wv)


def _mla_attn_kernel(q_ref, k_ref, v_ref, o_ref, m_scr, acc_scr):
    i = pl.program_id(1)
    tq = q_ref.shape[1]
    row = i * tq + lax.broadcasted_iota(jnp.int32, (tq, MLA_TK), 0)
    col = lax.broadcasted_iota(jnp.int32, (tq, MLA_TK), 1)
    n_full = (i * tq) // MLA_TK
    outs = []
    for h in range(GROUP_HEADS):
        cols = slice(h * LANE, (h + 1) * LANE)
        q = q_ref[0, :, cols]
        m_scr[...] = jnp.full(m_scr.shape, NEG, F32)
        acc_scr[...] = jnp.zeros(acc_scr.shape, F32)

        def step(j, masked):
            start = pl.multiple_of(j * MLA_TK, MLA_TK)
            s = _dot_nt(q, k_ref[0, pl.ds(start, MLA_TK), cols])
            if masked:
                s = jnp.where(start + col <= row, s, NEG)
            m_old = m_scr[...]
            m_new = jnp.maximum(m_old, jnp.max(s, axis=-1, keepdims=True))
            p = jnp.exp(s - m_new)
            acc_scr[...] = (jnp.exp(m_old - m_new) * acc_scr[...]
                            + _dot(p.astype(BF16), v_ref[0, pl.ds(start, MLA_TK), cols]))
            m_scr[...] = m_new

        def body(j, carry):
            step(j, False)
            return carry

        lax.fori_loop(0, n_full, body, 0)
        step(n_full, True)
        acc = acc_scr[...]
        outs.append(acc[:, :V_HEAD] * (1.0 / acc[:, V_HEAD:V_HEAD + 1]))
    o_ref[0] = jnp.concatenate(outs, axis=1)


def mla_attention(q, k, v):
    b, s, w = q.shape
    return pl.pallas_call(
        _mla_attn_kernel,
        grid=(b, s // MLA_TQ),
        in_specs=[pl.BlockSpec((1, MLA_TQ, w), lambda bi, i: (bi, i, 0)),
                  pl.BlockSpec((1, s, w), lambda bi, i: (bi, 0, 0)),
                  pl.BlockSpec((1, s, w), lambda bi, i: (bi, 0, 0))],
        out_specs=pl.BlockSpec((1, MLA_TQ, GROUP_HEADS * V_HEAD), lambda bi, i: (bi, i, 0)),
        out_shape=jax.ShapeDtypeStruct((b, s, GROUP_HEADS * V_HEAD), F32),
        scratch_shapes=[pltpu.VMEM((MLA_TQ, 1), F32), pltpu.VMEM((MLA_TQ, LANE), F32)],
        compiler_params=pltpu.CompilerParams(dimension_semantics=("arbitrary", "arbitrary"),
                                             vmem_limit_bytes=VMEM_LIMIT),
        name="mla_attention",
    )(q, k, v)


def mla_group(h, positions, q_norm_g, kv_norm_g, w_uq, w_ukv):
    b, s, _ = h.shape
    t = b * s
    h2 = h.reshape(t, -1)
    c_q, c_kv, k_r = h2[:, :Q_LORA], h2[:, Q_LORA:Q_LORA + KV_LORA], h2[:, Q_LORA + KV_LORA:]
    q, k, v = mla_prep(positions.reshape(t, 1), c_q, c_kv, k_r, q_norm_g, kv_norm_g, w_uq, w_ukv)
    w = GROUP_HEADS * LANE
    return mla_attention(q.reshape(b, s, w), k.reshape(b, s, w), v.reshape(b, s, w))


CHUNK = 64
CONV_W = 4


def _split3(a):
    hi = a.astype(BF16)
    r = a - hi.astype(F32)
    mid = r.astype(BF16)
    return hi, mid, (r - mid.astype(F32)).astype(BF16)


def _dot_exact_rhs(sel, a):
    return sum(_dot(sel, p) for p in _split3(a))


def _row_matrix(col, n_rows):
    l = col.shape[0]
    lane = lax.broadcasted_iota(jnp.int32, (l, SUBLANES), 1)
    padded = jnp.where(lane == 0, col, 0.0)
    ones = jnp.ones((n_rows, SUBLANES), BF16)
    return sum(_dot_nt(ones, p) for p in _split3(padded))


def _dot_x3(a, b):
    ah = a.astype(BF16)
    al = (a - ah.astype(F32)).astype(BF16)
    bh = b.astype(BF16)
    bl = (b - bh.astype(F32)).astype(BF16)
    return _dot(ah, bh) + _dot(ah, bl) + _dot(al, bh)


def _transpose_bf16(x_bf):
    d = x_bf.shape[1]
    eye = (lax.broadcasted_iota(jnp.int32, (d, d), 0) == lax.broadcasted_iota(jnp.int32, (d, d), 1)).astype(BF16)
    return _dot_nt(eye, x_bf).astype(BF16)


def _log_sigmoid(x):
    return jnp.minimum(x, 0.0) - jnp.log1p(jnp.exp(-jnp.abs(x)))


def _softplus(x):
    return jnp.maximum(x, 0.0) + jnp.log1p(jnp.exp(-jnp.abs(x)))


def _with_ones_col(v):
    l, d = v.shape
    lane = lax.broadcasted_iota(jnp.int32, (l, LANE - d), 1)
    return jnp.concatenate([v, jnp.where(lane == 0, 1.0, 0.0)], axis=1)


def _mlstm_kernel(q_ref, k_ref, v_ref, gate_ref, og_ref, bias_ref, ng_ref, o_ref, st_scr, m_scr):
    H, D, L = GROUP_HEADS, HEAD_DIM, CHUNK

    @pl.when(pl.program_id(1) == 0)
    def _():
        st_scr[...] = jnp.zeros(st_scr.shape, F32)
        m_scr[...] = jnp.zeros(m_scr.shape, F32)

    gates = gate_ref[0] + bias_ref[...]
    row = lax.broadcasted_iota(jnp.int32, (L, L), 0)
    col = lax.broadcasted_iota(jnp.int32, (L, L), 1)
    tri = row >= col
    i_pre = gates[:, :H]
    bcum = _dot_exact_rhs(tri.astype(BF16), _log_sigmoid(gates[:, H:]))
    outs = []
    for h in range(H):
        cols = slice(h * D, (h + 1) * D)
        q = q_ref[0, :, cols].astype(BF16)
        kt = _transpose_bf16((k_ref[0, :, cols] * (D ** -0.5)).astype(BF16))
        v_aug = _with_ones_col(v_ref[0, :, cols])
        b = bcum[:, h:h + 1]
        ic = i_pre[:, h:h + 1]
        m_st = m_scr[h, 0:1, 0:1]
        log_d = jnp.where(tri, b + _row_matrix(ic - b, L), -jnp.inf)
        m_inter = b + m_st
        m_t = jnp.maximum(jnp.max(log_d, axis=-1, keepdims=True), m_inter)
        dmat = jnp.exp(log_d - m_t)
        inter = jnp.exp(m_inter - m_t)
        sqk = _dot(q, kt) * dmat
        st = st_scr[h]
        tot = _dot(sqk.astype(BF16), v_aug.astype(BF16)) + inter * _dot(q, st.astype(BF16))
        den = jnp.maximum(jnp.abs(tot[:, D:D + 1]), jnp.exp(-m_t))
        h_t = tot[:, :D] * (1.0 / den)
        b_end = b[L - 1:L]
        log_w = b_end - b + ic
        m_new = jnp.maximum(b_end + m_st, jnp.max(log_w, axis=0, keepdims=True))
        w = jnp.exp(log_w - m_new)
        st_scr[h] = jnp.exp(b_end + m_st - m_new) * st + _dot(kt, (w * v_aug).astype(BF16))
        m_scr[h] = jnp.broadcast_to(m_new, m_scr.shape[1:])
        o = jax.nn.sigmoid(og_ref[0, :, cols]) * h_t
        outs.append(o * lax.rsqrt(jnp.mean(o * o, axis=-1, keepdims=True) + NORM_EPS) * ng_ref[:, cols])
    o_ref[0] = jnp.concatenate(outs, axis=1)


def mlstm_group(h, i_bias, f_bias, norm_g):
    b, s, _ = h.shape
    W, H = GROUP_W, GROUP_HEADS
    q, k, v = h[..., :W], h[..., W:2 * W], h[..., 2 * W:3 * W]
    gates = h[..., 3 * W:3 * W + 2 * H]
    og = h[..., 3 * W + 2 * H:]
    bias = jnp.concatenate([i_bias, f_bias]).reshape(1, 2 * H)
    tok = lambda n: pl.BlockSpec((1, CHUNK, n), lambda bi, i: (bi, i, 0))
    fixed = lambda n: pl.BlockSpec((1, n), lambda bi, i: (0, 0))
    return pl.pallas_call(
        _mlstm_kernel,
        grid=(b, s // CHUNK),
        in_specs=[tok(W), tok(W), tok(W), tok(2 * H), tok(W), fixed(2 * H), fixed(W)],
        out_specs=tok(W),
        out_shape=jax.ShapeDtypeStruct((b, s, W), F32),
        scratch_shapes=[pltpu.VMEM((H, HEAD_DIM, LANE), F32), pltpu.VMEM((H, SUBLANES, LANE), F32)],
        compiler_params=pltpu.CompilerParams(dimension_semantics=("arbitrary", "arbitrary")),
        name="mlstm",
    )(q, k, v, gates, og, bias, norm_g.reshape(1, W))


def _gdn_kernel(x_ref, xprev_ref, cw_ref, gate_ref, z_ref, par_ref, ng_ref, o_ref, st_scr):
    H, D, L = GROUP_HEADS, HEAD_DIM, CHUNK
    W = H * D
    first = pl.program_id(1) == 0

    @pl.when(first)
    def _():
        st_scr[...] = jnp.zeros(st_scr.shape, F32)

    prev = jnp.where(first, 0.0, xprev_ref[0])
    full = jnp.concatenate([prev, x_ref[0]], axis=0)
    acc = full[SUBLANES:] * cw_ref[CONV_W - 1:CONV_W, :]
    for j in range(CONV_W - 1):
        shifted = pltpu.roll(full, CONV_W - 1 - j, axis=0)[SUBLANES:]
        acc = acc + shifted * cw_ref[j:j + 1, :]
    qkv = acc * jax.nn.sigmoid(acc)

    gates = gate_ref[0]
    par = par_ref[...]
    beta4 = jax.nn.sigmoid(gates[:, :H])
    g4 = -jnp.exp(par[:, :H]) * _softplus(gates[:, H:] + par[:, H:])
    row = lax.broadcasted_iota(jnp.int32, (L, L), 0)
    col = lax.broadcasted_iota(jnp.int32, (L, L), 1)
    tri = row >= col
    eye = (row == col).astype(F32)
    gc4 = _dot_exact_rhs(tri.astype(BF16), g4)
    outs = []
    for h in range(H):
        cols = slice(h * D, (h + 1) * D)
        q = qkv[:, h * D:(h + 1) * D]
        k = qkv[:, W + h * D:W + (h + 1) * D]
        v = qkv[:, 2 * W + h * D:2 * W + (h + 1) * D]
        q = q * lax.rsqrt(jnp.sum(q * q, axis=-1, keepdims=True) + NORM_EPS) * (D ** -0.5)
        k = k * lax.rsqrt(jnp.sum(k * k, axis=-1, keepdims=True) + NORM_EPS)
        beta = beta4[:, h:h + 1]
        gc = gc4[:, h:h + 1]
        gc_row = _row_matrix(gc, L)
        decay = jnp.where(tri, jnp.exp(gc - gc_row), 0.0)
        g_end = gc[L - 1:L]
        kt = _transpose_bf16(k.astype(BF16))
        kb = k * beta
        a_mat = jnp.where(row > col, _dot(kb.astype(BF16), kt) * decay, 0.0)
        x = -a_mat
        t_inv = eye + x
        for _ in range(5):
            x = _dot_x3(x, x)
            t_inv = t_inv + _dot_x3(t_inv, x)
        u = _dot_x3(t_inv, v * beta)
        wm = _dot_x3(t_inv, kb * jnp.exp(gc))
        attn = _dot(q.astype(BF16), kt) * decay
        st = st_scr[h]
        st_bf = st.astype(BF16)
        v_new = u - _dot(wm.astype(BF16), st_bf)
        o = _dot((q * jnp.exp(gc)).astype(BF16), st_bf) + _dot(attn.astype(BF16), v_new.astype(BF16))
        k_dec_t = (kt.astype(F32) * jnp.exp(g_end - gc_row)).astype(BF16)
        st_scr[h] = st * jnp.exp(g_end) + _dot(k_dec_t, v_new.astype(BF16))
        z = z_ref[0, :, cols]
        y = o * lax.rsqrt(jnp.mean(o * o, axis=-1, keepdims=True) + NORM_EPS) * ng_ref[...]
        outs.append(y * (z * jax.nn.sigmoid(z)))
    o_ref[0] = jnp.concatenate(outs, axis=1)


def gdn_group(h, conv_w, a_log, dt_bias, norm_g):
    b, s, _ = h.shape
    W, H = GROUP_W, GROUP_HEADS
    x = h[..., :3 * W]
    gates = h[..., 3 * W:3 * W + 2 * H]
    z = h[..., 3 * W + 2 * H:]
    par = jnp.concatenate([a_log, dt_bias]).reshape(1, 2 * H)
    tok = lambda n: pl.BlockSpec((1, CHUNK, n), lambda bi, i: (bi, i, 0))
    per_chunk = CHUNK // SUBLANES
    prev = pl.BlockSpec((1, SUBLANES, 3 * W), lambda bi, i: (bi, jnp.maximum(i * per_chunk - 1, 0), 0))
    fixed = lambda r, n: pl.BlockSpec((r, n), lambda bi, i: (0, 0))
    return pl.pallas_call(
        _gdn_kernel,
        grid=(b, s // CHUNK),
        in_specs=[tok(3 * W), prev, fixed(CONV_W, 3 * W), tok(2 * H), tok(W), fixed(1, 2 * H), fixed(1, HEAD_DIM)],
        out_specs=tok(W),
        out_shape=jax.ShapeDtypeStruct((b, s, W), F32),
        scratch_shapes=[pltpu.VMEM((H, HEAD_DIM, HEAD_DIM), F32)],
        compiler_params=pltpu.CompilerParams(dimension_semantics=("arbitrary", "arbitrary")),
        name="gdn",
    )(x, x, conv_w, gates, z, par, norm_g.reshape(1, HEAD_DIM))


NSA_IN = GROUP_W + 6 * HEAD_DIM + 3 * GROUP_HEADS
MLSTM_IN = 4 * GROUP_W + 2 * GROUP_HEADS
GDN_IN = 4 * GROUP_W + 2 * GROUP_HEADS
MLA_IN = Q_LORA + KV_LORA + QK_ROPE


def kernel(x, positions, t5_table, norm1_g, norm2_g, final_norm_g, w_in, w_out, nsa_w_cmp_k, nsa_w_cmp_v, nsa_cmp_pos, mlstm_i_bias, mlstm_f_bias, mlstm_norm_g, gdn_conv_w, gdn_a_log, gdn_dt_bias, gdn_norm_g, mla_q_norm_g, mla_kv_norm_g, mla_w_uq, mla_w_ukv, peer_w_q, peer_sub_keys, peer_u, peer_v):
    b, s, d = x.shape
    t = b * s
    h = x.reshape(t, d)
    depth = w_in.shape[0]
    d_in = w_in.shape[2]
    d_in_pad = _round_up(d_in, LANE)
    for l in range(depth):
        w_in_l = jnp.pad(w_in[l], ((0, 0), (0, d_in_pad - d_in))).astype(BF16)
        hp = norm_matmul(h, norm1_g[l], w_in_l)[:, :d_in].reshape(b, s, d_in)
        c0, c1, c2 = NSA_IN, NSA_IN + MLSTM_IN, NSA_IN + MLSTM_IN + GDN_IN
        h_nsa, h_ml, h_gdn, h_mla = hp[..., :c0], hp[..., c0:c1], hp[..., c1:c2], hp[..., c2:]
        o = jnp.concatenate([
            nsa_group(h_nsa, nsa_w_cmp_k[l], nsa_w_cmp_v[l], nsa_cmp_pos[l], t5_table),
            mlstm_group(h_ml, mlstm_i_bias[l], mlstm_f_bias[l], mlstm_norm_g[l]),
            gdn_group(h_gdn, gdn_conv_w[l], gdn_a_log[l], gdn_dt_bias[l], gdn_norm_g[l]),
            mla_group(h_mla, positions, mla_q_norm_g[l], mla_kv_norm_g[l], mla_w_uq[l], mla_w_ukv[l]),
        ], axis=-1)
        h = matmul_residual(o.reshape(t, -1), w_out[l].astype(BF16), h)
        h = peer_ffn_residual(h, norm2_g[l], peer_w_q[l], peer_sub_keys[l], peer_u[l], peer_v[l])
    return rmsnorm(h, final_norm_g).reshape(b, s, d)
```

```python
import functools
import math

import jax
import jax.numpy as jnp
import numpy as np
from jax import lax
from jax.experimental import pallas as pl
from jax.experimental.pallas import tpu as pltpu

F32 = jnp.float32
BF16 = jnp.bfloat16
NORM_EPS = 1e-6
NEG = -1e30
LANE = 128
ROW_TILE = 512
VMEM_LIMIT = 56 * 1024 * 1024

HEAD_DIM = 64
GROUP_HEADS = 4
GROUP_W = GROUP_HEADS * HEAD_DIM
Q_BLOCK = 128
CMP_LEN = 32
CMP_STRIDE = 16
SEL_LEN = 64
SEL_TOPN = 16
WINDOW = 512
SEL_FORCE = 1e3
N_BUCKETS = 32
MAX_DISTANCE = 128


def _round_up(n, m):
    return (n + m - 1) // m * m


def _dot_nt(a, b):
    return lax.dot_general(a, b, (((1,), (1,)), ((), ())), preferred_element_type=F32)


def _dot(a, b):
    return jnp.dot(a, b, preferred_element_type=F32)


def _norm_matmul_kernel(x_ref, g_ref, w_ref, o_ref):
    x = x_ref[...]
    y = x * lax.rsqrt(jnp.mean(x * x, axis=-1, keepdims=True) + NORM_EPS) * g_ref[...]
    o_ref[...] = _dot(y.astype(BF16), w_ref[...])


def norm_matmul(x, g, w):
    t, d = x.shape
    n = w.shape[1]
    return pl.pallas_call(
        _norm_matmul_kernel,
        grid=(t // ROW_TILE,),
        in_specs=[pl.BlockSpec((ROW_TILE, d), lambda i: (i, 0)),
                  pl.BlockSpec((1, d), lambda i: (0, 0)),
                  pl.BlockSpec((d, n), lambda i: (0, 0))],
        out_specs=pl.BlockSpec((ROW_TILE, n), lambda i: (i, 0)),
        out_shape=jax.ShapeDtypeStruct((t, n), F32),
        compiler_params=pltpu.CompilerParams(dimension_semantics=("arbitrary",),
                                             vmem_limit_bytes=VMEM_LIMIT),
        name="norm_matmul",
    )(x, g.reshape(1, d), w)


def _matmul_residual_kernel(o_ref, w_ref, r_ref, out_ref):
    out_ref[...] = r_ref[...] + _dot(o_ref[...].astype(BF16), w_ref[...])


def matmul_residual(o, w, res):
    t, d = o.shape
    n = w.shape[1]
    return pl.pallas_call(
        _matmul_residual_kernel,
        grid=(t // ROW_TILE,),
        in_specs=[pl.BlockSpec((ROW_TILE, d), lambda i: (i, 0)),
                  pl.BlockSpec((d, n), lambda i: (0, 0)),
                  pl.BlockSpec((ROW_TILE, n), lambda i: (i, 0))],
        out_specs=pl.BlockSpec((ROW_TILE, n), lambda i: (i, 0)),
        out_shape=jax.ShapeDtypeStruct((t, n), F32),
        compiler_params=pltpu.CompilerParams(dimension_semantics=("arbitrary",),
                                             vmem_limit_bytes=VMEM_LIMIT),
        name="matmul_residual",
    )(o, w, res)


def _rmsnorm_kernel(x_ref, g_ref, o_ref):
    x = x_ref[...]
    o_ref[...] = x * lax.rsqrt(jnp.mean(x * x, axis=-1, keepdims=True) + NORM_EPS) * g_ref[...]


def rmsnorm(x, g):
    t, d = x.shape
    return pl.pallas_call(
        _rmsnorm_kernel,
        grid=(t // ROW_TILE,),
        in_specs=[pl.BlockSpec((ROW_TILE, d), lambda i: (i, 0)),
                  pl.BlockSpec((1, d), lambda i: (0, 0))],
        out_specs=pl.BlockSpec((ROW_TILE, d), lambda i: (i, 0)),
        out_shape=jax.ShapeDtypeStruct((t, d), F32),
        compiler_params=pltpu.CompilerParams(dimension_semantics=("arbitrary",)),
        name="rmsnorm",
    )(x, g.reshape(1, d))


def _t5_bucket_starts():
    d = np.arange(0, 2 * MAX_DISTANCE)
    max_exact = N_BUCKETS // 2
    val = (np.log(np.maximum(d, 1).astype(np.float32) / np.float32(max_exact))
           / np.float32(math.log(MAX_DISTANCE / max_exact)) * np.float32(N_BUCKETS - max_exact))
    large = np.minimum(max_exact + val.astype(np.int32), N_BUCKETS - 1)
    bucket = np.where(d < max_exact, d, large)
    assert np.all(np.diff(bucket) >= 0)
    return [(int(k), int(d[bucket == k].min())) for k in range(N_BUCKETS) if np.any(bucket == k)]


T5_STARTS = _t5_bucket_starts()


def _t5_bias_heads(dist, tab_ref):
    accs = [jnp.full(dist.shape, tab_ref[0, h], F32) for h in range(GROUP_HEADS)]
    for bucket, start in T5_STARTS[1:]:
        m = dist >= start
        accs = [jnp.where(m, tab_ref[bucket, h], accs[h]) for h in range(GROUP_HEADS)]
    return jnp.stack(accs)


def _nsa_compress_kernel(x_ref, pos_ref, w_ref, o_ref):
    x = x_ref[0, 0]
    n = x.shape[0]
    lo = _dot((x + pos_ref[0, 0:1, :]).astype(BF16), w_ref[0, 0])
    hi = _dot((x + pos_ref[0, 1:2, :]).astype(BF16), w_ref[0, 1])
    o_ref[0, 0] = (lo + pltpu.roll(hi, n - 1, axis=0)).astype(o_ref.dtype)


def nsa_compress(x, pos, w):
    two, b, n, c = x.shape
    return pl.pallas_call(
        _nsa_compress_kernel,
        grid=(two, b),
        in_specs=[pl.BlockSpec((1, 1, n, c), lambda k, i: (k, i, 0, 0)),
                  pl.BlockSpec((1, 2, c), lambda k, i: (0, 0, 0)),
                  pl.BlockSpec((1, 2, c, HEAD_DIM), lambda k, i: (k, 0, 0, 0))],
        out_specs=pl.BlockSpec((1, 1, n, HEAD_DIM), lambda k, i: (k, i, 0, 0)),
        out_shape=jax.ShapeDtypeStruct((two, b, n, HEAD_DIM), BF16),
        compiler_params=pltpu.CompilerParams(dimension_semantics=("arbitrary", "arbitrary")),
        name="nsa_compress",
    )(x, pos, w)


FAR_TILE = 256
NEAR_KEYS = 384
WIN_KEYS = WINDOW + Q_BLOCK


def _nsa_attn_kernel(tab_ref, q_ref, g_ref, kc_ref, vc_ref, ks_ref, vs_ref, kw_ref, vw_ref,
                     o_ref, m_scr, acc_scr):
    H, Q, D = GROUP_HEADS, Q_BLOCK, HEAD_DIM
    seq = ks_ref.shape[1]
    n_cmp = kc_ref.shape[1]
    n_sel = seq // SEL_LEN
    i = pl.program_id(1)
    t0 = i * Q

    q = q_ref[0] * (D ** -0.5)
    q4 = jnp.concatenate([q[:, h * D:(h + 1) * D] for h in range(H)], axis=0).astype(BF16)

    s = _dot_nt(q4, kc_ref[0]).reshape(H, Q, n_cmp)
    t_c = t0 + lax.broadcasted_iota(jnp.int32, (Q, n_cmp), 0)
    dist_c = t_c - (lax.broadcasted_iota(jnp.int32, (Q, n_cmp), 1) * CMP_STRIDE + CMP_LEN - 1)
    valid_c = dist_c >= 0
    s = jnp.where(valid_c, s + _t5_bias_heads(dist_c, tab_ref), NEG)
    p = jnp.where(valid_c, jnp.exp(s - jnp.max(s, axis=-1, keepdims=True)), 0.0)
    l = jnp.sum(p, axis=-1, keepdims=True)
    p = p * (1.0 / jnp.where(l > 0.0, l, 1.0))
    o_c = _dot(p.reshape(H * Q, n_cmp).astype(BF16), vc_ref[0])

    psum = p[0] + p[1] + p[2] + p[3]
    c_idx = lax.broadcasted_iota(jnp.int32, (n_cmp, n_sel), 0)
    j_idx = lax.broadcasted_iota(jnp.int32, (n_cmp, n_sel), 1)
    ratio = SEL_LEN // CMP_STRIDE
    n_in_sel = (SEL_LEN - CMP_LEN) // CMP_STRIDE + 1
    pool = ((c_idx // ratio == j_idx) & (c_idx % ratio < n_in_sel)).astype(BF16)
    p_hi = psum.astype(BF16)
    p_mid = (psum - p_hi.astype(F32)).astype(BF16)
    p_lo = (psum - p_hi.astype(F32) - p_mid.astype(F32)).astype(BF16)
    imp = _dot(p_hi, pool) + _dot(p_mid, pool) + _dot(p_lo, pool)

    jb = lax.broadcasted_iota(jnp.int32, (Q, n_sel), 1)
    cur = (t0 + lax.broadcasted_iota(jnp.int32, (Q, n_sel), 0)) // SEL_LEN
    forced = (jb == 0) | (jb == cur) | (jb == cur - 1)
    work = jnp.where(jb <= cur, imp + jnp.where(forced, SEL_FORCE, 0.0), NEG)
    sel = jnp.zeros((Q, n_sel), F32)
    for _ in range(min(SEL_TOPN, n_sel)):
        mx = jnp.max(work, axis=-1, keepdims=True)
        first = jnp.min(jnp.where(work == mx, jb, n_sel), axis=-1, keepdims=True)
        pick = jb == first
        sel = jnp.where(pick & (mx > 0.5 * NEG), 1.0, sel)
        work = jnp.where(pick, -jnp.inf, work)
    sel_bf = sel.astype(BF16)

    far_bias = jnp.stack([jnp.full((Q, 1), tab_ref[N_BUCKETS - 1, h], F32) for h in range(H)])
    m_scr[...] = jnp.full(m_scr.shape, NEG, F32)
    acc_scr[...] = jnp.zeros(acc_scr.shape, F32)

    def sel_step(start, n_keys, near, first_key):
        k = ks_ref[0, pl.ds(start, n_keys), :]
        v = vs_ref[0, pl.ds(start, n_keys), :]
        s = _dot_nt(q4, k).reshape(H, Q, n_keys)
        blk = (start // SEL_LEN) + lax.broadcasted_iota(jnp.int32, (n_sel, n_keys), 1) // SEL_LEN
        expand = (lax.broadcasted_iota(jnp.int32, (n_sel, n_keys), 0) == blk).astype(BF16)
        mask = _dot(sel_bf, expand) > 0.5
        if near:
            pos = start + lax.broadcasted_iota(jnp.int32, (Q, n_keys), 1)
            dist = t0 + lax.broadcasted_iota(jnp.int32, (Q, n_keys), 0) - pos
            mask = mask & (dist >= 0) & (pos >= first_key)
            s = s + _t5_bias_heads(dist, tab_ref)
        else:
            s = s + far_bias
        s = jnp.where(mask, s, NEG)
        m_old = m_scr[...].reshape(H, Q, 1)
        m_new = jnp.maximum(m_old, jnp.max(s, axis=-1, keepdims=True))
        p = jnp.where(mask, jnp.exp(s - m_new), 0.0).reshape(H * Q, n_keys)
        alpha = jnp.exp(m_old - m_new).reshape(H * Q, 1)
        acc_scr[...] = alpha * acc_scr[...] + _dot(p.astype(BF16), v)
        m_scr[...] = m_new.reshape(H * Q, 1)

    n_far = jnp.maximum(i - 1, 0) // 2

    def far_body(kt, carry):
        sel_step(pl.multiple_of(kt * FAR_TILE, FAR_TILE), FAR_TILE, False, 0)
        return carry

    lax.fori_loop(0, n_far, far_body, 0)
    near_start = pl.multiple_of(jnp.maximum(i - 2, 0) * Q, Q)
    sel_step(near_start, NEAR_KEYS, True, n_far * FAR_TILE)
    acc = acc_scr[...]
    o_s = acc[:, :D] * (1.0 / acc[:, D:D + 1])

    w_start = pl.multiple_of(jnp.maximum(i - WINDOW // Q, 0) * Q, Q)
    kw = kw_ref[0, pl.ds(w_start, WIN_KEYS), :]
    vw = vw_ref[0, pl.ds(w_start, WIN_KEYS), :]
    dist_w = (t0 + lax.broadcasted_iota(jnp.int32, (Q, WIN_KEYS), 0)
              - (w_start + lax.broadcasted_iota(jnp.int32, (Q, WIN_KEYS), 1)))
    valid_w = (dist_w >= 0) & (dist_w < WINDOW)
    s = _dot_nt(q4, kw).reshape(H, Q, WIN_KEYS) + _t5_bias_heads(dist_w, tab_ref)
    s = jnp.where(valid_w, s, NEG)
    p = jnp.where(valid_w, jnp.exp(s - jnp.max(s, axis=-1, keepdims=True)), 0.0)
    acc_w = _dot(p.reshape(H * Q, WIN_KEYS).astype(BF16), vw)
    o_w = acc_w[:, :D] * (1.0 / acc_w[:, D:D + 1])

    g = jax.nn.sigmoid(g_ref[0])
    outs = []
    for h in range(H):
        rows = slice(h * Q, (h + 1) * Q)
        outs.append(g[:, h:h + 1] * o_c[rows] + g[:, H + h:H + h + 1] * o_s[rows]
                    + g[:, 2 * H + h:2 * H + h + 1] * o_w[rows])
    o_ref[0] = jnp.concatenate(outs, axis=1)


def nsa_attention(t5_table, q, gates, k_cmp, v_cmp, ks, vs_aug, kw, vw_aug):
    b, s, _ = q.shape
    n_cmp = k_cmp.shape[1]
    full = lambda bi, i: (bi, 0, 0)
    return pl.pallas_call(
        _nsa_attn_kernel,
        grid=(b, s // Q_BLOCK),
        in_specs=[pl.BlockSpec(memory_space=pltpu.SMEM),
                  pl.BlockSpec((1, Q_BLOCK, GROUP_W), lambda bi, i: (bi, i, 0)),
                  pl.BlockSpec((1, Q_BLOCK, 3 * GROUP_HEADS), lambda bi, i: (bi, i, 0)),
                  pl.BlockSpec((1, n_cmp, HEAD_DIM), full),
                  pl.BlockSpec((1, n_cmp, HEAD_DIM), full),
                  pl.BlockSpec((1, s, HEAD_DIM), full),
                  pl.BlockSpec((1, s, 2 * HEAD_DIM), full),
                  pl.BlockSpec((1, s, HEAD_DIM), full),
                  pl.BlockSpec((1, s, 2 * HEAD_DIM), full)],
        out_specs=pl.BlockSpec((1, Q_BLOCK, GROUP_W), lambda bi, i: (bi, i, 0)),
        out_shape=jax.ShapeDtypeStruct((b, s, GROUP_W), F32),
        scratch_shapes=[pltpu.VMEM((GROUP_HEADS * Q_BLOCK, 1), F32),
                        pltpu.VMEM((GROUP_HEADS * Q_BLOCK, 2 * HEAD_DIM), F32)],
        compiler_params=pltpu.CompilerParams(dimension_semantics=("arbitrary", "arbitrary"),
                                             vmem_limit_bytes=VMEM_LIMIT),
        name="nsa_attention",
    )(t5_table, q, gates, k_cmp, v_cmp, ks, vs_aug, kw, vw_aug)


def _with_ones(v):
    b, s, d = v.shape
    pad = jnp.zeros((b, s, d), BF16).at[:, :, 0].set(1.0)
    return jnp.concatenate([v.astype(BF16), pad], axis=-1)


def nsa_group(h, w_cmp_k, w_cmp_v, cmp_pos, t5_table):
    b, s, _ = h.shape
    d = HEAD_DIM
    q = h[..., :GROUP_W]
    kc, vc, ks, vs, kw, vw = [h[..., GROUP_W + j * d: GROUP_W + (j + 1) * d] for j in range(6)]
    gates = h[..., GROUP_W + 6 * d:]
    n = s // CMP_STRIDE
    x = jnp.stack([kc, vc]).reshape(2, b, n, CMP_STRIDE * d)
    pos = cmp_pos.reshape(1, 2, CMP_STRIDE * d)
    w = jnp.stack([w_cmp_k, w_cmp_v]).reshape(2, 2, CMP_STRIDE * d, d).astype(BF16)
    cmp = nsa_compress(x, pos, w)
    return nsa_attention(t5_table, q, gates, cmp[0], cmp[1], ks.astype(BF16), _with_ones(vs),
                         kw.astype(BF16), _with_ones(vw))


PEER_HEADS = 8
PEER_DK = 256
N_KEYS = 128
PEER_TOPK = 16
PEER_SLOTS = PEER_HEADS * PEER_TOPK
ROUTE_TILE = 128
GATHER_TILE = 8


def _top_rows(work, order, n_top, payload=None):
    vals, args = [], []
    for _ in range(n_top):
        mx = jnp.max(work, axis=0, keepdims=True)
        first = jnp.min(jnp.where(work == mx, order, jnp.int32(2 ** 30)), axis=0, keepdims=True)
        pick = order == first
        vals.append(mx)
        if payload is None:
            args.append(first)
        else:
            args.append(jnp.sum(jnp.where(pick, payload, 0), axis=0, keepdims=True))
        work = jnp.where(pick, -jnp.inf, work)
    return jnp.concatenate(vals, axis=0), jnp.concatenate(args, axis=0)


def _pair_rows(s1, s2, combine):
    k = PEER_TOPK
    rows = [combine(s1[0:1], s2)]
    for a in range(1, k // 2):
        rows.append(combine(s1[a:a + 1], s2[0:k // 2]))
    rows.append(combine(s1[k // 2:], s2[0:1]))
    return jnp.concatenate(rows, axis=0)


def _peer_route_kernel(h_ref, g_ref, wq_ref, keys_ref, xn_ref, eidx_ref, gate_ref):
    x = h_ref[...]
    xn = x * lax.rsqrt(jnp.mean(x * x, axis=-1, keepdims=True) + NORM_EPS) * g_ref[...]
    xn_ref[...] = xn
    q = _dot(xn.astype(BF16), wq_ref[...]).astype(BF16)
    tq = x.shape[0]
    half = PEER_DK // 2
    n_iota = lax.broadcasted_iota(jnp.int32, (N_KEYS, tq), 0)
    k_iota = lax.broadcasted_iota(jnp.int32, (PEER_TOPK, tq), 0)
    order = _pair_rows(k_iota, k_iota, lambda a, b: a * PEER_TOPK + b)
    reachable = (order // PEER_TOPK + 1) * (order % PEER_TOPK + 1) <= PEER_TOPK
    for p in range(PEER_HEADS):
        tops = []
        for c in range(2):
            g = 2 * p + c
            sc = _dot_nt(keys_ref[g], q[:, g * half:(g + 1) * half])
            tops.append(_top_rows(sc, n_iota, PEER_TOPK))
        (s1, i1), (s2, i2) = tops
        cand = jnp.where(reachable, _pair_rows(s1, s2, lambda a, b: a + b), -jnp.inf)
        cidx = _pair_rows(i1, i2, lambda a, b: a * N_KEYS + b)
        top, eidx = _top_rows(cand, order, PEER_TOPK, payload=cidx)
        e = jnp.exp(top - top[0:1])
        eidx_ref[p * PEER_TOPK:(p + 1) * PEER_TOPK, :] = eidx
        gate_ref[p * PEER_TOPK:(p + 1) * PEER_TOPK, :] = e * (1.0 / jnp.sum(e, axis=0, keepdims=True))


def peer_route(h, g, w_q, keys):
    t, d = h.shape
    nq = w_q.shape[1]
    return pl.pallas_call(
        _peer_route_kernel,
        grid=(t // ROUTE_TILE,),
        in_specs=[pl.BlockSpec((ROUTE_TILE, d), lambda i: (i, 0)),
                  pl.BlockSpec((1, d), lambda i: (0, 0)),
                  pl.BlockSpec((d, nq), lambda i: (0, 0)),
                  pl.BlockSpec(keys.shape, lambda i: (0, 0, 0))],
        out_specs=[pl.BlockSpec((ROUTE_TILE, d), lambda i: (i, 0)),
                   pl.BlockSpec((PEER_SLOTS, ROUTE_TILE), lambda i: (0, i)),
                   pl.BlockSpec((PEER_SLOTS, ROUTE_TILE), lambda i: (0, i))],
        out_shape=[jax.ShapeDtypeStruct((t, d), F32),
                   jax.ShapeDtypeStruct((PEER_SLOTS, t), jnp.int32),
                   jax.ShapeDtypeStruct((PEER_SLOTS, t), F32)],
        compiler_params=pltpu.CompilerParams(dimension_semantics=("arbitrary",),
                                             vmem_limit_bytes=VMEM_LIMIT),
        name="peer_route",
    )(h, g.reshape(1, d), w_q, keys)


SUBLANES = 8
SLAB = 2 * SUBLANES
ISSUE_UNROLL = 8


def _gelu(x):
    return 0.5 * x * (1.0 + lax.erf(x * (2.0 ** -0.5)))


BITREV8 = (0, 4, 2, 6, 1, 5, 3, 7)


def _fold_pair(a, b, half, sub):
    low = (sub % (2 * half)) < half
    partner = jnp.where(low, pltpu.roll(a, SUBLANES - half, axis=0), pltpu.roll(b, half, axis=0))
    return jnp.where(low, a, b) + partner


def _peer_gather_kernel(idx_ref, idx_next_ref, xn_ref, h_ref, gate_ref, uv_hbm, o_ref, buf, w_scr, sem):
    i = pl.program_id(0)
    n = pl.num_programs(0)
    tt = xn_ref.shape[0]
    n_slabs = tt * PEER_SLOTS

    def request_tile(ref, slot):
        def body(c, carry):
            for k in range(ISSUE_UNROLL):
                r = pl.multiple_of(c * ISSUE_UNROLL, ISSUE_UNROLL) + k
                src = pl.multiple_of(ref[0, 0, r] * SLAB, SLAB)
                pltpu.make_async_copy(uv_hbm.at[pl.ds(src, SLAB)], buf.at[slot, pl.ds(r * SLAB, SLAB)],
                                      sem.at[slot]).start()
            return carry
        lax.fori_loop(0, n_slabs // ISSUE_UNROLL, body, 0)

    @pl.when(i == 0)
    def _():
        request_tile(idx_ref, 0)

    @pl.when(i + 1 < n)
    def _():
        request_tile(idx_next_ref, (i + 1) % 2)

    slot = i % 2
    pltpu.make_async_copy(uv_hbm.at[pl.ds(0, n_slabs * SLAB)], buf.at[slot], sem.at[slot]).wait()
    gates = gate_ref[0]
    sub = lax.broadcasted_iota(jnp.int32, (SUBLANES, LANE), 0)
    for t in range(tt):
        x = xn_ref[t]

        def slab(pos, half):
            return buf[slot, pl.ds((t * PEER_SLOTS + pos) * SLAB + half * SUBLANES, SUBLANES), :]

        groups = []
        for g in range(PEER_SLOTS // SUBLANES):
            p = [slab(g * SUBLANES + k, 0) * x for k in range(SUBLANES)]
            c = [_fold_pair(p[2 * k], p[2 * k + 1], 4, sub) for k in range(4)]
            d = [_fold_pair(c[0], c[1], 2, sub), _fold_pair(c[2], c[3], 2, sub)]
            groups.append(_fold_pair(d[0], d[1], 1, sub))
        act = jnp.sum(jnp.concatenate(groups, axis=0), axis=-1, keepdims=True)
        w_scr[...] = jnp.broadcast_to(gates[:, t:t + 1] * _gelu(act), (PEER_SLOTS, LANE))
        acc = h_ref[t]
        for g in range(PEER_SLOTS // SUBLANES):
            for k in range(SUBLANES):
                w_row = w_scr[pl.ds(g * SUBLANES + BITREV8[k], 1), :]
                acc = acc + slab(g * SUBLANES + k, 1) * w_row
        o_ref[t] = acc


def peer_gather(idx, xn, h, gate_b, uv):
    t = xn.shape[0]
    nt = t // GATHER_TILE
    n_slabs = GATHER_TILE * PEER_SLOTS
    tok = pl.BlockSpec((GATHER_TILE, SUBLANES, LANE), lambda i: (i, 0, 0))
    return pl.pallas_call(
        _peer_gather_kernel,
        grid=(nt,),
        in_specs=[pl.BlockSpec((1, 1, n_slabs), lambda i: (i, 0, 0), memory_space=pltpu.SMEM),
                  pl.BlockSpec((1, 1, n_slabs), lambda i: (jnp.minimum(i + 1, nt - 1), 0, 0),
                               memory_space=pltpu.SMEM),
                  tok, tok,
                  pl.BlockSpec((1, PEER_SLOTS, GATHER_TILE), lambda i: (i, 0, 0)),
                  pl.BlockSpec(memory_space=pl.ANY)],
        out_specs=tok,
        out_shape=jax.ShapeDtypeStruct((t, SUBLANES, LANE), F32),
        scratch_shapes=[pltpu.VMEM((2, n_slabs * SLAB, LANE), F32), pltpu.VMEM((PEER_SLOTS, LANE), F32),
                        pltpu.SemaphoreType.DMA((2,))],
        compiler_params=pltpu.CompilerParams(dimension_semantics=("arbitrary",),
                                             vmem_limit_bytes=VMEM_LIMIT),
        name="peer_gather",
    )(idx, idx, xn, h, gate_b, uv)


def peer_ffn_residual(h, g, w_q, sub_keys, u_tab, v_tab):
    t, d = h.shape
    e = u_tab.shape[0]
    keys = sub_keys.reshape(PEER_HEADS * 2, N_KEYS, PEER_DK // 2).astype(BF16)
    xn, eidx_t, gate_t = peer_route(h, g, w_q.astype(BF16), keys)
    nt = t // GATHER_TILE
    order = np.arange(PEER_SLOTS).reshape(-1, SUBLANES)[:, list(BITREV8)].reshape(-1)
    idx = eidx_t[order].T.reshape(nt, 1, GATHER_TILE * PEER_SLOTS)
    gate_b = gate_t.reshape(PEER_SLOTS, nt, GATHER_TILE).transpose(1, 0, 2)
    uv = jnp.concatenate([u_tab.reshape(e, SUBLANES, LANE), v_tab.reshape(e, SUBLANES, LANE)],
                         axis=1).reshape(e * SLAB, LANE)
    out = peer_gather(idx, xn.reshape(t, SUBLANES, LANE), h.reshape(t, SUBLANES, LANE), gate_b, uv)
    return out.reshape(t, d)


Q_LORA = 256
KV_LORA = 128
QK_NOPE = 64
QK_ROPE = 32
V_HEAD = 64
ROPE_THETA = 10000.0
MLA_TQ = 256
MLA_TK = 512


def _rms(x, g):
    return x * lax.rsqrt(jnp.mean(x * x, axis=-1, keepdims=True) + NORM_EPS) * g


def _mla_prep_kernel(pos_ref, freq_ref, cq_ref, ckv_ref, kr_ref, gq_ref, gkv_ref, wq_ref, wk_ref, wv_ref,
                     q_out, k_out, v_out):
    rows = cq_ref.shape[0]
    half = QK_ROPE // 2
    cq = _rms(cq_ref[...], gq_ref[...]).astype(BF16)
    ckv = _rms(ckv_ref[...], gkv_ref[...]).astype(BF16)
    qf = _dot(cq, wq_ref[...])
    kf = _dot(ckv, wk_ref[...])
    vf = _dot(ckv, wv_ref[...])
    lane = lax.broadcasted_iota(jnp.int32, (rows, LANE), 1)
    ang = pos_ref[...].astype(F32) * freq_ref[...]
    cos, sin = jnp.cos(ang), jnp.sin(ang)
    in_rope = (lane >= QK_NOPE) & (lane < QK_NOPE + QK_ROPE)
    first = lane < QK_NOPE + half
    c_mul = jnp.where(lane < QK_NOPE, 1.0, jnp.where(in_rope, cos, 0.0))
    s_mul = jnp.where(in_rope, jnp.where(first, -sin, sin), 0.0)

    def rope(x):
        partner = jnp.where(first, pltpu.roll(x, LANE - half, axis=1), pltpu.roll(x, half, axis=1))
        return x * c_mul + partner * s_mul

    scale = (QK_NOPE + QK_ROPE) ** -0.5
    kr = rope(jnp.concatenate([jnp.zeros((rows, QK_NOPE), F32), kr_ref[...],
                               jnp.zeros((rows, LANE - QK_NOPE - QK_ROPE), F32)], axis=1))
    ones_col = jnp.where(lane == V_HEAD, 1.0, 0.0)
    for h in range(GROUP_HEADS):
        cols = slice(h * LANE, (h + 1) * LANE)
        q_out[:, cols] = (rope(qf[:, cols]) * scale).astype(BF16)
        k_out[:, cols] = (kf[:, cols] + kr).astype(BF16)
        v_out[:, cols] = (vf[:, cols] + ones_col).astype(BF16)


def mla_prep(pos, c_q, c_kv, k_r, gq, gkv, w_uq, w_ukv):
    t = c_q.shape[0]
    H = GROUP_HEADS
    wq = jnp.pad(w_uq.reshape(Q_LORA, H, QK_NOPE + QK_ROPE), ((0, 0), (0, 0), (0, LANE - QK_NOPE - QK_ROPE)))
    wkv = w_ukv.reshape(KV_LORA, H, QK_NOPE + V_HEAD)
    wk = jnp.pad(wkv[:, :, :QK_NOPE], ((0, 0), (0, 0), (0, LANE - QK_NOPE)))
    wv = jnp.pad(wkv[:, :, QK_NOPE:], ((0, 0), (0, 0), (0, LANE - V_HEAD)))
    wq, wk, wv = [w.reshape(w.shape[0], H * LANE).astype(BF16) for w in (wq, wk, wv)]
    inv_freq = ROPE_THETA ** (-np.arange(0, QK_ROPE, 2, dtype=np.float32) / QK_ROPE)
    freq = np.zeros((1, LANE), np.float32)
    freq[0, QK_NOPE:QK_NOPE + QK_ROPE] = np.tile(inv_freq, 2)
    row = lambda n: pl.BlockSpec((ROW_TILE, n), lambda i: (i, 0))
    fixed = lambda a: pl.BlockSpec(a.shape, lambda i: (0,) * a.ndim)
    gq2, gkv2, freq = gq.reshape(1, -1), gkv.reshape(1, -1), jnp.asarray(freq)
    out = jax.ShapeDtypeStruct((t, H * LANE), BF16)
    return pl.pallas_call(
        _mla_prep_kernel,
        grid=(t // ROW_TILE,),
        in_specs=[row(1), fixed(freq), row(Q_LORA), row(KV_LORA), row(QK_ROPE), fixed(gq2), fixed(gkv2),
                  fixed(wq), fixed(wk), fixed(wv)],
        out_specs=[row(H * LANE)] * 3,
        out_shape=[out] * 3,
        compiler_params=pltpu.CompilerParams(dimension_semantics=("arbitrary",)),
        name="mla_prep",
    )(pos, freq, c_q, c_kv, k_r, gq2, gkv2, wq, wk, wv)


def _mla_attn_kernel(q_ref, k_ref, v_ref, o_ref, m_scr, acc_scr):
    i = pl.program_id(1)
    tq = q_ref.shape[1]
    row = i * tq + lax.broadcasted_iota(jnp.int32, (tq, MLA_TK), 0)
    col = lax.broadcasted_iota(jnp.int32, (tq, MLA_TK), 1)
    n_full = (i * tq) // MLA_TK
    outs = []
    for h in range(GROUP_HEADS):
        cols = slice(h * LANE, (h + 1) * LANE)
        q = q_ref[0, :, cols]
        m_scr[...] = jnp.full(m_scr.shape, NEG, F32)
        acc_scr[...] = jnp.zeros(acc_scr.shape, F32)

        def step(j, masked):
            start = pl.multiple_of(j * MLA_TK, MLA_TK)
            s = _dot_nt(q, k_ref[0, pl.ds(start, MLA_TK), cols])
            if masked:
                s = jnp.where(start + col <= row, s, NEG)
            m_old = m_scr[...]
            m_new = jnp.maximum(m_old, jnp.max(s, axis=-1, keepdims=True))
            p = jnp.exp(s - m_new)
            acc_scr[...] = (jnp.exp(m_old - m_new) * acc_scr[...]
                            + _dot(p.astype(BF16), v_ref[0, pl.ds(start, MLA_TK), cols]))
            m_scr[...] = m_new

        def body(j, carry):
            step(j, False)
            return carry

        lax.fori_loop(0, n_full, body, 0)
        step(n_full, True)
        acc = acc_scr[...]
        outs.append(acc[:, :V_HEAD] * (1.0 / acc[:, V_HEAD:V_HEAD + 1]))
    o_ref[0] = jnp.concatenate(outs, axis=1)


def mla_attention(q, k, v):
    b, s, w = q.shape
    return pl.pallas_call(
        _mla_attn_kernel,
        grid=(b, s // MLA_TQ),
        in_specs=[pl.BlockSpec((1, MLA_TQ, w), lambda bi, i: (bi, i, 0)),
                  pl.BlockSpec((1, s, w), lambda bi, i: (bi, 0, 0)),
                  pl.BlockSpec((1, s, w), lambda bi, i: (bi, 0, 0))],
        out_specs=pl.BlockSpec((1, MLA_TQ, GROUP_HEADS * V_HEAD), lambda bi, i: (bi, i, 0)),
        out_shape=jax.ShapeDtypeStruct((b, s, GROUP_HEADS * V_HEAD), F32),
        scratch_shapes=[pltpu.VMEM((MLA_TQ, 1), F32), pltpu.VMEM((MLA_TQ, LANE), F32)],
        compiler_params=pltpu.CompilerParams(dimension_semantics=("arbitrary", "arbitrary"),
                                             vmem_limit_bytes=VMEM_LIMIT),
        name="mla_attention",
    )(q, k, v)


def mla_group(h, positions, q_norm_g, kv_norm_g, w_uq, w_ukv):
    b, s, _ = h.shape
    t = b * s
    h2 = h.reshape(t, -1)
    c_q, c_kv, k_r = h2[:, :Q_LORA], h2[:, Q_LORA:Q_LORA + KV_LORA], h2[:, Q_LORA + KV_LORA:]
    q, k, v = mla_prep(positions.reshape(t, 1), c_q, c_kv, k_r, q_norm_g, kv_norm_g, w_uq, w_ukv)
    w = GROUP_HEADS * LANE
    return mla_attention(q.reshape(b, s, w), k.reshape(b, s, w), v.reshape(b, s, w))


CHUNK = 64
CONV_W = 4


def _split3(a):
    hi = a.astype(BF16)
    r = a - hi.astype(F32)
    mid = r.astype(BF16)
    return hi, mid, (r - mid.astype(F32)).astype(BF16)


def _dot_exact_rhs(sel, a):
    return sum(_dot(sel, p) for p in _split3(a))


def _row_matrix(col, n_rows):
    l = col.shape[0]
    lane = lax.broadcasted_iota(jnp.int32, (l, SUBLANES), 1)
    padded = jnp.where(lane == 0, col, 0.0)
    ones = jnp.ones((n_rows, SUBLANES), BF16)
    return sum(_dot_nt(ones, p) for p in _split3(padded))


def _dot_x3(a, b):
    ah = a.astype(BF16)
    al = (a - ah.astype(F32)).astype(BF16)
    bh = b.astype(BF16)
    bl = (b - bh.astype(F32)).astype(BF16)
    return _dot(ah, bh) + _dot(ah, bl) + _dot(al, bh)


def _transpose_bf16(x_bf):
    d = x_bf.shape[1]
    eye = (lax.broadcasted_iota(jnp.int32, (d, d), 0) == lax.broadcasted_iota(jnp.int32, (d, d), 1)).astype(BF16)
    return _dot_nt(eye, x_bf).astype(BF16)


def _log_sigmoid(x):
    return jnp.minimum(x, 0.0) - jnp.log1p(jnp.exp(-jnp.abs(x)))


def _softplus(x):
    return jnp.maximum(x, 0.0) + jnp.log1p(jnp.exp(-jnp.abs(x)))


def _with_ones_col(v):
    l, d = v.shape
    lane = lax.broadcasted_iota(jnp.int32, (l, LANE - d), 1)
    return jnp.concatenate([v, jnp.where(lane == 0, 1.0, 0.0)], axis=1)


def _mlstm_kernel(q_ref, k_ref, v_ref, gate_ref, og_ref, bias_ref, ng_ref, o_ref, st_scr, m_scr):
    H, D, L = GROUP_HEADS, HEAD_DIM, CHUNK

    @pl.when(pl.program_id(1) == 0)
    def _():
        st_scr[...] = jnp.zeros(st_scr.shape, F32)
        m_scr[...] = jnp.zeros(m_scr.shape, F32)

    gates = gate_ref[0] + bias_ref[...]
    row = lax.broadcasted_iota(jnp.int32, (L, L), 0)
    col = lax.broadcasted_iota(jnp.int32, (L, L), 1)
    tri = row >= col
    i_pre = gates[:, :H]
    bcum = _dot_exact_rhs(tri.astype(BF16), _log_sigmoid(gates[:, H:]))
    outs = []
    for h in range(H):
        cols = slice(h * D, (h + 1) * D)
        q = q_ref[0, :, cols].astype(BF16)
        kt = _transpose_bf16((k_ref[0, :, cols] * (D ** -0.5)).astype(BF16))
        v_aug = _with_ones_col(v_ref[0, :, cols])
        b = bcum[:, h:h + 1]
        ic = i_pre[:, h:h + 1]
        m_st = m_scr[h, 0:1, 0:1]
        log_d = jnp.where(tri, b + _row_matrix(ic - b, L), -jnp.inf)
        m_inter = b + m_st
        m_t = jnp.maximum(jnp.max(log_d, axis=-1, keepdims=True), m_inter)
        dmat = jnp.exp(log_d - m_t)
        inter = jnp.exp(m_inter - m_t)
        sqk = _dot(q, kt) * dmat
        st = st_scr[h]
        tot = _dot(sqk.astype(BF16), v_aug.astype(BF16)) + inter * _dot(q, st.astype(BF16))
        den = jnp.maximum(jnp.abs(tot[:, D:D + 1]), jnp.exp(-m_t))
        h_t = tot[:, :D] * (1.0 / den)
        b_end = b[L - 1:L]
        log_w = b_end - b + ic
        m_new = jnp.maximum(b_end + m_st, jnp.max(log_w, axis=0, keepdims=True))
        w = jnp.exp(log_w - m_new)
        st_scr[h] = jnp.exp(b_end + m_st - m_new) * st + _dot(kt, (w * v_aug).astype(BF16))
        m_scr[h] = jnp.broadcast_to(m_new, m_scr.shape[1:])
        o = jax.nn.sigmoid(og_ref[0, :, cols]) * h_t
        outs.append(o * lax.rsqrt(jnp.mean(o * o, axis=-1, keepdims=True) + NORM_EPS) * ng_ref[:, cols])
    o_ref[0] = jnp.concatenate(outs, axis=1)


def mlstm_group(h, i_bias, f_bias, norm_g):
    b, s, _ = h.shape
    W, H = GROUP_W, GROUP_HEADS
    q, k, v = h[..., :W], h[..., W:2 * W], h[..., 2 * W:3 * W]
    gates = h[..., 3 * W:3 * W + 2 * H]
    og = h[..., 3 * W + 2 * H:]
    bias = jnp.concatenate([i_bias, f_bias]).reshape(1, 2 * H)
    tok = lambda n: pl.BlockSpec((1, CHUNK, n), lambda bi, i: (bi, i, 0))
    fixed = lambda n: pl.BlockSpec((1, n), lambda bi, i: (0, 0))
    return pl.pallas_call(
        _mlstm_kernel,
        grid=(b, s // CHUNK),
        in_specs=[tok(W), tok(W), tok(W), tok(2 * H), tok(W), fixed(2 * H), fixed(W)],
        out_specs=tok(W),
        out_shape=jax.ShapeDtypeStruct((b, s, W), F32),
        scratch_shapes=[pltpu.VMEM((H, HEAD_DIM, LANE), F32), pltpu.VMEM((H, SUBLANES, LANE), F32)],
        compiler_params=pltpu.CompilerParams(dimension_semantics=("arbitrary", "arbitrary")),
        name="mlstm",
    )(q, k, v, gates, og, bias, norm_g.reshape(1, W))


def _gdn_kernel(x_ref, xprev_ref, cw_ref, gate_ref, z_ref, par_ref, ng_ref, o_ref, st_scr):
    H, D, L = GROUP_HEADS, HEAD_DIM, CHUNK
    W = H * D
    first = pl.program_id(1) == 0

    @pl.when(first)
    def _():
        st_scr[...] = jnp.zeros(st_scr.shape, F32)

    prev = jnp.where(first, 0.0, xprev_ref[0])
    full = jnp.concatenate([prev, x_ref[0]], axis=0)
    acc = full[SUBLANES:] * cw_ref[CONV_W - 1:CONV_W, :]
    for j in range(CONV_W - 1):
        shifted = pltpu.roll(full, CONV_W - 1 - j, axis=0)[SUBLANES:]
        acc = acc + shifted * cw_ref[j:j + 1, :]
    qkv = acc * jax.nn.sigmoid(acc)

    gates = gate_ref[0]
    par = par_ref[...]
    beta4 = jax.nn.sigmoid(gates[:, :H])
    g4 = -jnp.exp(par[:, :H]) * _softplus(gates[:, H:] + par[:, H:])
    row = lax.broadcasted_iota(jnp.int32, (L, L), 0)
    col = lax.broadcasted_iota(jnp.int32, (L, L), 1)
    tri = row >= col
    eye = (row == col).astype(F32)
    gc4 = _dot_exact_rhs(tri.astype(BF16), g4)
    outs = []
    for h in range(H):
        cols = slice(h * D, (h + 1) * D)
        q = qkv[:, h * D:(h + 1) * D]
        k = qkv[:, W + h * D:W + (h + 1) * D]
        v = qkv[:, 2 * W + h * D:2 * W + (h + 1) * D]
        q = q * lax.rsqrt(jnp.sum(q * q, axis=-1, keepdims=True) + NORM_EPS) * (D ** -0.5)
        k = k * lax.rsqrt(jnp.sum(k * k, axis=-1, keepdims=True) + NORM_EPS)
        beta = beta4[:, h:h + 1]
        gc = gc4[:, h:h + 1]
        gc_row = _row_matrix(gc, L)
        decay = jnp.where(tri, jnp.exp(gc - gc_row), 0.0)
        g_end = gc[L - 1:L]
        kt = _transpose_bf16(k.astype(BF16))
        kb = k * beta
        a_mat = jnp.where(row > col, _dot(kb.astype(BF16), kt) * decay, 0.0)
        x = -a_mat
        t_inv = eye + x
        for _ in range(5):
            x = _dot_x3(x, x)
            t_inv = t_inv + _dot_x3(t_inv, x)
        u = _dot_x3(t_inv, v * beta)
        wm = _dot_x3(t_inv, kb * jnp.exp(gc))
        attn = _dot(q.astype(BF16), kt) * decay
        st = st_scr[h]
        st_bf = st.astype(BF16)
        v_new = u - _dot(wm.astype(BF16), st_bf)
        o = _dot((q * jnp.exp(gc)).astype(BF16), st_bf) + _dot(attn.astype(BF16), v_new.astype(BF16))
        k_dec_t = (kt.astype(F32) * jnp.exp(g_end - gc_row)).astype(BF16)
        st_scr[h] = st * jnp.exp(g_end) + _dot(k_dec_t, v_new.astype(BF16))
        z = z_ref[0, :, cols]
        y = o * lax.rsqrt(jnp.mean(o * o, axis=-1, keepdims=True) + NORM_EPS) * ng_ref[...]
        outs.append(y * (z * jax.nn.sigmoid(z)))
    o_ref[0] = jnp.concatenate(outs, axis=1)


def gdn_group(h, conv_w, a_log, dt_bias, norm_g):
    b, s, _ = h.shape
    W, H = GROUP_W, GROUP_HEADS
    x = h[..., :3 * W]
    gates = h[..., 3 * W:3 * W + 2 * H]
    z = h[..., 3 * W + 2 * H:]
    par = jnp.concatenate([a_log, dt_bias]).reshape(1, 2 * H)
    tok = lambda n: pl.BlockSpec((1, CHUNK, n), lambda bi, i: (bi, i, 0))
    per_chunk = CHUNK // SUBLANES
    prev = pl.BlockSpec((1, SUBLANES, 3 * W), lambda bi, i: (bi, jnp.maximum(i * per_chunk - 1, 0), 0))
    fixed = lambda r, n: pl.BlockSpec((r, n), lambda bi, i: (0, 0))
    return pl.pallas_call(
        _gdn_kernel,
        grid=(b, s // CHUNK),
        in_specs=[tok(3 * W), prev, fixed(CONV_W, 3 * W), tok(2 * H), tok(W), fixed(1, 2 * H), fixed(1, HEAD_DIM)],
        out_specs=tok(W),
        out_shape=jax.ShapeDtypeStruct((b, s, W), F32),
        scratch_shapes=[pltpu.VMEM((H, HEAD_DIM, HEAD_DIM), F32)],
        compiler_params=pltpu.CompilerParams(dimension_semantics=("arbitrary", "arbitrary")),
        name="gdn",
    )(x, x, conv_w, gates, z, par, norm_g.reshape(1, HEAD_DIM))


NSA_IN = GROUP_W + 6 * HEAD_DIM + 3 * GROUP_HEADS
MLSTM_IN = 4 * GROUP_W + 2 * GROUP_HEADS
GDN_IN = 4 * GROUP_W + 2 * GROUP_HEADS
MLA_IN = Q_LORA + KV_LORA + QK_ROPE


def kernel(x, positions, t5_table, norm1_g, norm2_g, final_norm_g, w_in, w_out, nsa_w_cmp_k, nsa_w_cmp_v, nsa_cmp_pos, mlstm_i_bias, mlstm_f_bias, mlstm_norm_g, gdn_conv_w, gdn_a_log, gdn_dt_bias, gdn_norm_g, mla_q_norm_g, mla_kv_norm_g, mla_w_uq, mla_w_ukv, peer_w_q, peer_sub_keys, peer_u, peer_v):
    b, s, d = x.shape
    t = b * s
    h = x.reshape(t, d)
    depth = w_in.shape[0]
    d_in = w_in.shape[2]
    d_in_pad = _round_up(d_in, LANE)
    for l in range(depth):
        w_in_l = jnp.pad(w_in[l], ((0, 0), (0, d_in_pad - d_in))).astype(BF16)
        hp = norm_matmul(h, norm1_g[l], w_in_l)[:, :d_in].reshape(b, s, d_in)
        c0, c1, c2 = NSA_IN, NSA_IN + MLSTM_IN, NSA_IN + MLSTM_IN + GDN_IN
        h_nsa, h_ml, h_gdn, h_mla = hp[..., :c0], hp[..., c0:c1], hp[..., c1:c2], hp[..., c2:]
        o = jnp.concatenate([
            nsa_group(h_nsa, nsa_w_cmp_k[l], nsa_w_cmp_v[l], nsa_cmp_pos[l], t5_table),
            mlstm_group(h_ml, mlstm_i_bias[l], mlstm_f_bias[l], mlstm_norm_g[l]),
            gdn_group(h_gdn, gdn_conv_w[l], gdn_a_log[l], gdn_dt_bias[l], gdn_norm_g[l]),
            mla_group(h_mla, positions, mla_q_norm_g[l], mla_kv_norm_g[l], mla_w_uq[l], mla_w_ukv[l]),
        ], axis=-1)
        h = matmul_residual(o.reshape(t, -1), w_out[l].astype(BF16), h)
        h = peer_ffn_residual(h, norm2_g[l], peer_w_q[l], peer_sub_keys[l], peer_u[l], peer_v[l])
    return rmsnorm(h, final_norm_g).reshape(b, s, d)
```

```python
import functools
import math

import jax
import jax.numpy as jnp
import numpy as np
from jax import lax
from jax.experimental import pallas as pl
from jax.experimental.pallas import tpu as pltpu

F32 = jnp.float32
BF16 = jnp.bfloat16
NORM_EPS = 1e-6
NEG = -1e30
LANE = 128
ROW_TILE = 512
VMEM_LIMIT = 56 * 1024 * 1024

HEAD_DIM = 64
GROUP_HEADS = 4
GROUP_W = GROUP_HEADS * HEAD_DIM
Q_BLOCK = 128
CMP_LEN = 32
CMP_STRIDE = 16
SEL_LEN = 64
SEL_TOPN = 16
WINDOW = 512
SEL_FORCE = 1e3
N_BUCKETS = 32
MAX_DISTANCE = 128


def _round_up(n, m):
    return (n + m - 1) // m * m


def _dot_nt(a, b):
    return lax.dot_general(a, b, (((1,), (1,)), ((), ())), preferred_element_type=F32)


def _dot(a, b):
    return jnp.dot(a, b, preferred_element_type=F32)


def _norm_matmul_kernel(x_ref, g_ref, w_ref, o_ref):
    x = x_ref[...]
    y = x * lax.rsqrt(jnp.mean(x * x, axis=-1, keepdims=True) + NORM_EPS) * g_ref[...]
    o_ref[...] = _dot(y.astype(BF16), w_ref[...])


def norm_matmul(x, g, w):
    t, d = x.shape
    n = w.shape[1]
    return pl.pallas_call(
        _norm_matmul_kernel,
        grid=(t // ROW_TILE,),
        in_specs=[pl.BlockSpec((ROW_TILE, d), lambda i: (i, 0)),
                  pl.BlockSpec((1, d), lambda i: (0, 0)),
                  pl.BlockSpec((d, n), lambda i: (0, 0))],
        out_specs=pl.BlockSpec((ROW_TILE, n), lambda i: (i, 0)),
        out_shape=jax.ShapeDtypeStruct((t, n), F32),
        compiler_params=pltpu.CompilerParams(dimension_semantics=("arbitrary",),
                                             vmem_limit_bytes=VMEM_LIMIT),
        name="norm_matmul",
    )(x, g.reshape(1, d), w)


def _matmul_residual_kernel(o_ref, w_ref, r_ref, out_ref):
    out_ref[...] = r_ref[...] + _dot(o_ref[...].astype(BF16), w_ref[...])


def matmul_residual(o, w, res):
    t, d = o.shape
    n = w.shape[1]
    return pl.pallas_call(
        _matmul_residual_kernel,
        grid=(t // ROW_TILE,),
        in_specs=[pl.BlockSpec((ROW_TILE, d), lambda i: (i, 0)),
                  pl.BlockSpec((d, n), lambda i: (0, 0)),
                  pl.BlockSpec((ROW_TILE, n), lambda i: (i, 0))],
        out_specs=pl.BlockSpec((ROW_TILE, n), lambda i: (i, 0)),
        out_shape=jax.ShapeDtypeStruct((t, n), F32),
        compiler_params=pltpu.CompilerParams(dimension_semantics=("arbitrary",),
                                             vmem_limit_bytes=VMEM_LIMIT),
        name="matmul_residual",
    )(o, w, res)


def _rmsnorm_kernel(x_ref, g_ref, o_ref):
    x = x_ref[...]
    o_ref[...] = x * lax.rsqrt(jnp.mean(x * x, axis=-1, keepdims=True) + NORM_EPS) * g_ref[...]


def rmsnorm(x, g):
    t, d = x.shape
    return pl.pallas_call(
        _rmsnorm_kernel,
        grid=(t // ROW_TILE,),
        in_specs=[pl.BlockSpec((ROW_TILE, d), lambda i: (i, 0)),
                  pl.BlockSpec((1, d), lambda i: (0, 0))],
        out_specs=pl.BlockSpec((ROW_TILE, d), lambda i: (i, 0)),
        out_shape=jax.ShapeDtypeStruct((t, d), F32),
        compiler_params=pltpu.CompilerParams(dimension_semantics=("arbitrary",)),
        name="rmsnorm",
    )(x, g.reshape(1, d))


def _t5_bucket_starts():
    d = np.arange(0, 2 * MAX_DISTANCE)
    max_exact = N_BUCKETS // 2
    val = (np.log(np.maximum(d, 1).astype(np.float32) / np.float32(max_exact))
           / np.float32(math.log(MAX_DISTANCE / max_exact)) * np.float32(N_BUCKETS - max_exact))
    large = np.minimum(max_exact + val.astype(np.int32), N_BUCKETS - 1)
    bucket = np.where(d < max_exact, d, large)
    assert np.all(np.diff(bucket) >= 0)
    return [(int(k), int(d[bucket == k].min())) for k in range(N_BUCKETS) if np.any(bucket == k)]


T5_STARTS = _t5_bucket_starts()


def _t5_bias_heads(dist, tab_ref):
    accs = [jnp.full(dist.shape, tab_ref[0, h], F32) for h in range(GROUP_HEADS)]
    for bucket, start in T5_STARTS[1:]:
        m = dist >= start
        accs = [jnp.where(m, tab_ref[bucket, h], accs[h]) for h in range(GROUP_HEADS)]
    return jnp.stack(accs)


def _nsa_compress_kernel(x_ref, pos_ref, w_ref, o_ref):
    x = x_ref[0, 0]
    n = x.shape[0]
    lo = _dot((x + pos_ref[0, 0:1, :]).astype(BF16), w_ref[0, 0])
    hi = _dot((x + pos_ref[0, 1:2, :]).astype(BF16), w_ref[0, 1])
    o_ref[0, 0] = (lo + pltpu.roll(hi, n - 1, axis=0)).astype(o_ref.dtype)


def nsa_compress(x, pos, w):
    two, b, n, c = x.shape
    return pl.pallas_call(
        _nsa_compress_kernel,
        grid=(two, b),
        in_specs=[pl.BlockSpec((1, 1, n, c), lambda k, i: (k, i, 0, 0)),
                  pl.BlockSpec((1, 2, c), lambda k, i: (0, 0, 0)),
                  pl.BlockSpec((1, 2, c, HEAD_DIM), lambda k, i: (k, 0, 0, 0))],
        out_specs=pl.BlockSpec((1, 1, n, HEAD_DIM), lambda k, i: (k, i, 0, 0)),
        out_shape=jax.ShapeDtypeStruct((two, b, n, HEAD_DIM), BF16),
        compiler_params=pltpu.CompilerParams(dimension_semantics=("arbitrary", "arbitrary")),
        name="nsa_compress",
    )(x, pos, w)


FAR_TILE = 256
NEAR_KEYS = 384
WIN_KEYS = WINDOW + Q_BLOCK


def _nsa_attn_kernel(tab_ref, q_ref, g_ref, kc_ref, vc_ref, ks_ref, vs_ref, kw_ref, vw_ref,
                     o_ref, m_scr, acc_scr):
    H, Q, D = GROUP_HEADS, Q_BLOCK, HEAD_DIM
    seq = ks_ref.shape[1]
    n_cmp = kc_ref.shape[1]
    n_sel = seq // SEL_LEN
    i = pl.program_id(1)
    t0 = i * Q

    q = q_ref[0] * (D ** -0.5)
    q4 = jnp.concatenate([q[:, h * D:(h + 1) * D] for h in range(H)], axis=0).astype(BF16)

    s = _dot_nt(q4, kc_ref[0]).reshape(H, Q, n_cmp)
    t_c = t0 + lax.broadcasted_iota(jnp.int32, (Q, n_cmp), 0)
    dist_c = t_c - (lax.broadcasted_iota(jnp.int32, (Q, n_cmp), 1) * CMP_STRIDE + CMP_LEN - 1)
    valid_c = dist_c >= 0
    s = jnp.where(valid_c, s + _t5_bias_heads(dist_c, tab_ref), NEG)
    p = jnp.where(valid_c, jnp.exp(s - jnp.max(s, axis=-1, keepdims=True)), 0.0)
    l = jnp.sum(p, axis=-1, keepdims=True)
    p = p * (1.0 / jnp.where(l > 0.0, l, 1.0))
    o_c = _dot(p.reshape(H * Q, n_cmp).astype(BF16), vc_ref[0])

    psum = p[0] + p[1] + p[2] + p[3]
    c_idx = lax.broadcasted_iota(jnp.int32, (n_cmp, n_sel), 0)
    j_idx = lax.broadcasted_iota(jnp.int32, (n_cmp, n_sel), 1)
    ratio = SEL_LEN // CMP_STRIDE
    n_in_sel = (SEL_LEN - CMP_LEN) // CMP_STRIDE + 1
    pool = ((c_idx // ratio == j_idx) & (c_idx % ratio < n_in_sel)).astype(BF16)
    p_hi = psum.astype(BF16)
    p_mid = (psum - p_hi.astype(F32)).astype(BF16)
    p_lo = (psum - p_hi.astype(F32) - p_mid.astype(F32)).astype(BF16)
    imp = _dot(p_hi, pool) + _dot(p_mid, pool) + _dot(p_lo, pool)

    jb = lax.broadcasted_iota(jnp.int32, (Q, n_sel), 1)
    cur = (t0 + lax.broadcasted_iota(jnp.int32, (Q, n_sel), 0)) // SEL_LEN
    forced = (jb == 0) | (jb == cur) | (jb == cur - 1)
    work = jnp.where(jb <= cur, imp + jnp.where(forced, SEL_FORCE, 0.0), NEG)
    sel = jnp.zeros((Q, n_sel), F32)
    for _ in range(min(SEL_TOPN, n_sel)):
        mx = jnp.max(work, axis=-1, keepdims=True)
        first = jnp.min(jnp.where(work == mx, jb, n_sel), axis=-1, keepdims=True)
        pick = jb == first
        sel = jnp.where(pick & (mx > 0.5 * NEG), 1.0, sel)
        work = jnp.where(pick, -jnp.inf, work)
    sel_bf = sel.astype(BF16)

    far_bias = jnp.stack([jnp.full((Q, 1), tab_ref[N_BUCKETS - 1, h], F32) for h in range(H)])
    m_scr[...] = jnp.full(m_scr.shape, NEG, F32)
    acc_scr[...] = jnp.zeros(acc_scr.shape, F32)

    def sel_step(start, n_keys, near, first_key):
        k = ks_ref[0, pl.ds(start, n_keys), :]
        v = vs_ref[0, pl.ds(start, n_keys), :]
        s = _dot_nt(q4, k).reshape(H, Q, n_keys)
        blk = (start // SEL_LEN) + lax.broadcasted_iota(jnp.int32, (n_sel, n_keys), 1) // SEL_LEN
        expand = (lax.broadcasted_iota(jnp.int32, (n_sel, n_keys), 0) == blk).astype(BF16)
        mask = _dot(sel_bf, expand) > 0.5
        if near:
            pos = start + lax.broadcasted_iota(jnp.int32, (Q, n_keys), 1)
            dist = t0 + lax.broadcasted_iota(jnp.int32, (Q, n_keys), 0) - pos
            mask = mask & (dist >= 0) & (pos >= first_key)
            s = s + _t5_bias_heads(dist, tab_ref)
        else:
            s = s + far_bias
        s = jnp.where(mask, s, NEG)
        m_old = m_scr[...].reshape(H, Q, 1)
        m_new = jnp.maximum(m_old, jnp.max(s, axis=-1, keepdims=True))
        p = jnp.where(mask, jnp.exp(s - m_new), 0.0).reshape(H * Q, n_keys)
        alpha = jnp.exp(m_old - m_new).reshape(H * Q, 1)
        acc_scr[...] = alpha * acc_scr[...] + _dot(p.astype(BF16), v)
        m_scr[...] = m_new.reshape(H * Q, 1)

    n_far = jnp.maximum(i - 1, 0) // 2

    def far_body(kt, carry):
        sel_step(pl.multiple_of(kt * FAR_TILE, FAR_TILE), FAR_TILE, False, 0)
        return carry

    lax.fori_loop(0, n_far, far_body, 0)
    near_start = pl.multiple_of(jnp.maximum(i - 2, 0) * Q, Q)
    sel_step(near_start, NEAR_KEYS, True, n_far * FAR_TILE)
    acc = acc_scr[...]
    o_s = acc[:, :D] * (1.0 / acc[:, D:D + 1])

    w_start = pl.multiple_of(jnp.maximum(i - WINDOW // Q, 0) * Q, Q)
    kw = kw_ref[0, pl.ds(w_start, WIN_KEYS), :]
    vw = vw_ref[0, pl.ds(w_start, WIN_KEYS), :]
    dist_w = (t0 + lax.broadcasted_iota(jnp.int32, (Q, WIN_KEYS), 0)
              - (w_start + lax.broadcasted_iota(jnp.int32, (Q, WIN_KEYS), 1)))
    valid_w = (dist_w >= 0) & (dist_w < WINDOW)
    s = _dot_nt(q4, kw).reshape(H, Q, WIN_KEYS) + _t5_bias_heads(dist_w, tab_ref)
    s = jnp.where(valid_w, s, NEG)
    p = jnp.where(valid_w, jnp.exp(s - jnp.max(s, axis=-1, keepdims=True)), 0.0)
    acc_w = _dot(p.reshape(H * Q, WIN_KEYS).astype(BF16), vw)
    o_w = acc_w[:, :D] * (1.0 / acc_w[:, D:D + 1])

    g = jax.nn.sigmoid(g_ref[0])
    outs = []
    for h in range(H):
        rows = slice(h * Q, (h + 1) * Q)
        outs.append(g[:, h:h + 1] * o_c[rows] + g[:, H + h:H + h + 1] * o_s[rows]
                    + g[:, 2 * H + h:2 * H + h + 1] * o_w[rows])
    o_ref[0] = jnp.concatenate(outs, axis=1)


def nsa_attention(t5_table, q, gates, k_cmp, v_cmp, ks, vs_aug, kw, vw_aug):
    b, s, _ = q.shape
    n_cmp = k_cmp.shape[1]
    full = lambda bi, i: (bi, 0, 0)
    return pl.pallas_call(
        _nsa_attn_kernel,
        grid=(b, s // Q_BLOCK),
        in_specs=[pl.BlockSpec(memory_space=pltpu.SMEM),
                  pl.BlockSpec((1, Q_BLOCK, GROUP_W), lambda bi, i: (bi, i, 0)),
                  pl.BlockSpec((1, Q_BLOCK, 3 * GROUP_HEADS), lambda bi, i: (bi, i, 0)),
                  pl.BlockSpec((1, n_cmp, HEAD_DIM), full),
                  pl.BlockSpec((1, n_cmp, HEAD_DIM), full),
                  pl.BlockSpec((1, s, HEAD_DIM), full),
                  pl.BlockSpec((1, s, 2 * HEAD_DIM), full),
                  pl.BlockSpec((1, s, HEAD_DIM), full),
                  pl.BlockSpec((1, s, 2 * HEAD_DIM), full)],
        out_specs=pl.BlockSpec((1, Q_BLOCK, GROUP_W), lambda bi, i: (bi, i, 0)),
        out_shape=jax.ShapeDtypeStruct((b, s, GROUP_W), F32),
        scratch_shapes=[pltpu.VMEM((GROUP_HEADS * Q_BLOCK, 1), F32),
                        pltpu.VMEM((GROUP_HEADS * Q_BLOCK, 2 * HEAD_DIM), F32)],
        compiler_params=pltpu.CompilerParams(dimension_semantics=("arbitrary", "arbitrary"),
                                             vmem_limit_bytes=VMEM_LIMIT),
        name="nsa_attention",
    )(t5_table, q, gates, k_cmp, v_cmp, ks, vs_aug, kw, vw_aug)


def _with_ones(v):
    b, s, d = v.shape
    pad = jnp.zeros((b, s, d), BF16).at[:, :, 0].set(1.0)
    return jnp.concatenate([v.astype(BF16), pad], axis=-1)


def nsa_group(h, w_cmp_k, w_cmp_v, cmp_pos, t5_table):
    b, s, _ = h.shape
    d = HEAD_DIM
    q = h[..., :GROUP_W]
    kc, vc, ks, vs, kw, vw = [h[..., GROUP_W + j * d: GROUP_W + (j + 1) * d] for j in range(6)]
    gates = h[..., GROUP_W + 6 * d:]
    n = s // CMP_STRIDE
    x = jnp.stack([kc, vc]).reshape(2, b, n, CMP_STRIDE * d)
    pos = cmp_pos.reshape(1, 2, CMP_STRIDE * d)
    w = jnp.stack([w_cmp_k, w_cmp_v]).reshape(2, 2, CMP_STRIDE * d, d).astype(BF16)
    cmp = nsa_compress(x, pos, w)
    return nsa_attention(t5_table, q, gates, cmp[0], cmp[1], ks.astype(BF16), _with_ones(vs),
                         kw.astype(BF16), _with_ones(vw))


PEER_HEADS = 8
PEER_DK = 256
N_KEYS = 128
PEER_TOPK = 16
PEER_SLOTS = PEER_HEADS * PEER_TOPK
ROUTE_TILE = 128
GATHER_TILE = 8


def _top_rows(work, order, n_top, payload=None):
    vals, args = [], []
    for _ in range(n_top):
        mx = jnp.max(work, axis=0, keepdims=True)
        first = jnp.min(jnp.where(work == mx, order, jnp.int32(2 ** 30)), axis=0, keepdims=True)
        pick = order == first
        vals.append(mx)
        if payload is None:
            args.append(first)
        else:
            args.append(jnp.sum(jnp.where(pick, payload, 0), axis=0, keepdims=True))
        work = jnp.where(pick, -jnp.inf, work)
    return jnp.concatenate(vals, axis=0), jnp.concatenate(args, axis=0)


def _pair_rows(s1, s2, combine):
    k = PEER_TOPK
    rows = [combine(s1[0:1], s2)]
    for a in range(1, k // 2):
        rows.append(combine(s1[a:a + 1], s2[0:k // 2]))
    rows.append(combine(s1[k // 2:], s2[0:1]))
    return jnp.concatenate(rows, axis=0)


def _peer_route_kernel(h_ref, g_ref, wq_ref, keys_ref, xn_ref, eidx_ref, gate_ref):
    x = h_ref[...]
    xn = x * lax.rsqrt(jnp.mean(x * x, axis=-1, keepdims=True) + NORM_EPS) * g_ref[...]
    xn_ref[...] = xn
    q = _dot(xn.astype(BF16), wq_ref[...]).astype(BF16)
    tq = x.shape[0]
    half = PEER_DK // 2
    n_iota = lax.broadcasted_iota(jnp.int32, (N_KEYS, tq), 0)
    k_iota = lax.broadcasted_iota(jnp.int32, (PEER_TOPK, tq), 0)
    order = _pair_rows(k_iota, k_iota, lambda a, b: a * PEER_TOPK + b)
    reachable = (order // PEER_TOPK + 1) * (order % PEER_TOPK + 1) <= PEER_TOPK
    for p in range(PEER_HEADS):
        tops = []
        for c in range(2):
            g = 2 * p + c
            sc = _dot_nt(keys_ref[g], q[:, g * half:(g + 1) * half])
            tops.append(_top_rows(sc, n_iota, PEER_TOPK))
        (s1, i1), (s2, i2) = tops
        cand = jnp.where(reachable, _pair_rows(s1, s2, lambda a, b: a + b), -jnp.inf)
        cidx = _pair_rows(i1, i2, lambda a, b: a * N_KEYS + b)
        top, eidx = _top_rows(cand, order, PEER_TOPK, payload=cidx)
        e = jnp.exp(top - top[0:1])
        eidx_ref[p * PEER_TOPK:(p + 1) * PEER_TOPK, :] = eidx
        gate_ref[p * PEER_TOPK:(p + 1) * PEER_TOPK, :] = e * (1.0 / jnp.sum(e, axis=0, keepdims=True))


def peer_route(h, g, w_q, keys):
    t, d = h.shape
    nq = w_q.shape[1]
    return pl.pallas_call(
        _peer_route_kernel,
        grid=(t // ROUTE_TILE,),
        in_specs=[pl.BlockSpec((ROUTE_TILE, d), lambda i: (i, 0)),
                  pl.BlockSpec((1, d), lambda i: (0, 0)),
                  pl.BlockSpec((d, nq), lambda i: (0, 0)),
                  pl.BlockSpec(keys.shape, lambda i: (0, 0, 0))],
        out_specs=[pl.BlockSpec((ROUTE_TILE, d), lambda i: (i, 0)),
                   pl.BlockSpec((PEER_SLOTS, ROUTE_TILE), lambda i: (0, i)),
                   pl.BlockSpec((PEER_SLOTS, ROUTE_TILE), lambda i: (0, i))],
        out_shape=[jax.ShapeDtypeStruct((t, d), F32),
                   jax.ShapeDtypeStruct((PEER_SLOTS, t), jnp.int32),
                   jax.ShapeDtypeStruct((PEER_SLOTS, t), F32)],
        compiler_params=pltpu.CompilerParams(dimension_semantics=("arbitrary",),
                                             vmem_limit_bytes=VMEM_LIMIT),
        name="peer_route",
    )(h, g.reshape(1, d), w_q, keys)


SUBLANES = 8
SLAB = 2 * SUBLANES
ISSUE_UNROLL = 8


def _gelu(x):
    return 0.5 * x * (1.0 + lax.erf(x * (2.0 ** -0.5)))


BITREV8 = (0, 4, 2, 6, 1, 5, 3, 7)


def _fold_pair(a, b, half, sub):
    low = (sub % (2 * half)) < half
    partner = jnp.where(low, pltpu.roll(a, SUBLANES - half, axis=0), pltpu.roll(b, half, axis=0))
    return jnp.where(low, a, b) + partner


GATHER_DEPTH = 3


def _peer_gather_kernel(idx0_ref, idx1_ref, idx2_ref, xn_ref, h_ref, gate_ref, uv_hbm, o_ref,
                        buf_a, buf_b, buf_c, w_scr, sem):
    i = pl.program_id(0)
    n = pl.num_programs(0)
    tt = xn_ref.shape[0]
    n_slabs = tt * PEER_SLOTS
    bufs = (buf_a, buf_b, buf_c)

    def request(ref, k, r):
        src = pl.multiple_of(ref[0, 0, r] * SLAB, SLAB)
        pltpu.make_async_copy(uv_hbm.at[pl.ds(src, SLAB)], bufs[k].at[pl.ds(r * SLAB, SLAB)], sem.at[k]).start()

    def request_tile(ref, k):
        def body(c, carry):
            for j in range(ISSUE_UNROLL):
                request(ref, k, pl.multiple_of(c * ISSUE_UNROLL, ISSUE_UNROLL) + j)
            return carry
        lax.fori_loop(0, n_slabs // ISSUE_UNROLL, body, 0)

    def wait_tile(k):
        pltpu.make_async_copy(uv_hbm.at[pl.ds(0, n_slabs * SLAB)], bufs[k], sem.at[k]).wait()

    @pl.when(i == 0)
    def _():
        request_tile(idx0_ref, 0)
        request_tile(idx1_ref, 1)

    def step(cur, ahead):
        wait_tile(cur)
        gates = gate_ref[0]
        sub = lax.broadcasted_iota(jnp.int32, (SUBLANES, LANE), 0)
        for t in range(tt):
            for j in range(PEER_SLOTS):
                request(idx2_ref, ahead, t * PEER_SLOTS + j)
            x = xn_ref[t]

            def slab(pos, half):
                return bufs[cur][pl.ds((t * PEER_SLOTS + pos) * SLAB + half * SUBLANES, SUBLANES), :]

            groups = []
            for g in range(PEER_SLOTS // SUBLANES):
                p = [slab(g * SUBLANES + k, 0) * x for k in range(SUBLANES)]
                c = [_fold_pair(p[2 * k], p[2 * k + 1], 4, sub) for k in range(4)]
                d = [_fold_pair(c[0], c[1], 2, sub), _fold_pair(c[2], c[3], 2, sub)]
                groups.append(_fold_pair(d[0], d[1], 1, sub))
            act = jnp.sum(jnp.concatenate(groups, axis=0), axis=-1, keepdims=True)
            w_scr[...] = jnp.broadcast_to(gates[:, t:t + 1] * _gelu(act), (PEER_SLOTS, LANE))
            acc = h_ref[t]
            for g in range(PEER_SLOTS // SUBLANES):
                for k in range(SUBLANES):
                    w_row = w_scr[pl.ds(g * SUBLANES + BITREV8[k], 1), :]
                    acc = acc + slab(g * SUBLANES + k, 1) * w_row
            o_ref[t] = acc

    for k in range(GATHER_DEPTH):
        @pl.when(i % GATHER_DEPTH == k)
        def _(k=k):
            step(k, (k + GATHER_DEPTH - 1) % GATHER_DEPTH)

    @pl.when(i == n - 1)
    def _():
        for k in range(GATHER_DEPTH):
            @pl.when(i % GATHER_DEPTH == k)
            def _(k=k):
                wait_tile((k + 1) % GATHER_DEPTH)
                wait_tile((k + 2) % GATHER_DEPTH)


def peer_gather(idx, xn, h, gate_b, uv):
    t = xn.shape[0]
    nt = t // GATHER_TILE
    n_slabs = GATHER_TILE * PEER_SLOTS
    tok = pl.BlockSpec((GATHER_TILE, SUBLANES, LANE), lambda i: (i, 0, 0))
    idx_spec = lambda k: pl.BlockSpec((1, 1, n_slabs), lambda i: (jnp.minimum(i + k, nt - 1), 0, 0),
                                      memory_space=pltpu.SMEM)
    slab_buf = pltpu.VMEM((n_slabs * SLAB, LANE), F32)
    return pl.pallas_call(
        _peer_gather_kernel,
        grid=(nt,),
        in_specs=[idx_spec(0), idx_spec(1), idx_spec(2), tok, tok,
                  pl.BlockSpec((1, PEER_SLOTS, GATHER_TILE), lambda i: (i, 0, 0)),
                  pl.BlockSpec(memory_space=pl.ANY)],
        out_specs=tok,
        out_shape=jax.ShapeDtypeStruct((t, SUBLANES, LANE), F32),
        scratch_shapes=[slab_buf, slab_buf, slab_buf, pltpu.VMEM((PEER_SLOTS, LANE), F32),
                        pltpu.SemaphoreType.DMA((GATHER_DEPTH,))],
        compiler_params=pltpu.CompilerParams(dimension_semantics=("arbitrary",),
                                             vmem_limit_bytes=VMEM_LIMIT),
        name="peer_gather",
    )(idx, idx, idx, xn, h, gate_b, uv)


def peer_ffn_residual(h, g, w_q, sub_keys, u_tab, v_tab):
    t, d = h.shape
    e = u_tab.shape[0]
    keys = sub_keys.reshape(PEER_HEADS * 2, N_KEYS, PEER_DK // 2).astype(BF16)
    xn, eidx_t, gate_t = peer_route(h, g, w_q.astype(BF16), keys)
    nt = t // GATHER_TILE
    order = np.arange(PEER_SLOTS).reshape(-1, SUBLANES)[:, list(BITREV8)].reshape(-1)
    idx = eidx_t[order].T.reshape(nt, 1, GATHER_TILE * PEER_SLOTS)
    gate_b = gate_t.reshape(PEER_SLOTS, nt, GATHER_TILE).transpose(1, 0, 2)
    uv = jnp.concatenate([u_tab.reshape(e, SUBLANES, LANE), v_tab.reshape(e, SUBLANES, LANE)],
                         axis=1).reshape(e * SLAB, LANE)
    out = peer_gather(idx, xn.reshape(t, SUBLANES, LANE), h.reshape(t, SUBLANES, LANE), gate_b, uv)
    return out.reshape(t, d)


Q_LORA = 256
KV_LORA = 128
QK_NOPE = 64
QK_ROPE = 32
V_HEAD = 64
ROPE_THETA = 10000.0
MLA_TQ = 256
MLA_TK = 512


def _rms(x, g):
    return x * lax.rsqrt(jnp.mean(x * x, axis=-1, keepdims=True) + NORM_EPS) * g


def _mla_prep_kernel(pos_ref, freq_ref, cq_ref, ckv_ref, kr_ref, gq_ref, gkv_ref, wq_ref, wk_ref, wv_ref,
                     q_out, k_out, v_out):
    rows = cq_ref.shape[0]
    half = QK_ROPE // 2
    cq = _rms(cq_ref[...], gq_ref[...]).astype(BF16)
    ckv = _rms(ckv_ref[...], gkv_ref[...]).astype(BF16)
    qf = _dot(cq, wq_ref[...])
    kf = _dot(ckv, wk_ref[...])
    vf = _dot(ckv, wv_ref[...])
    lane = lax.broadcasted_iota(jnp.int32, (rows, LANE), 1)
    ang = pos_ref[...].astype(F32) * freq_ref[...]
    cos, sin = jnp.cos(ang), jnp.sin(ang)
    in_rope = (lane >= QK_NOPE) & (lane < QK_NOPE + QK_ROPE)
    first = lane < QK_NOPE + half
    c_mul = jnp.where(lane < QK_NOPE, 1.0, jnp.where(in_rope, cos, 0.0))
    s_mul = jnp.where(in_rope, jnp.where(first, -sin, sin), 0.0)

    def rope(x):
        partner = jnp.where(first, pltpu.roll(x, LANE - half, axis=1), pltpu.roll(x, half, axis=1))
        return x * c_mul + partner * s_mul

    scale = (QK_NOPE + QK_ROPE) ** -0.5
    kr = rope(jnp.concatenate([jnp.zeros((rows, QK_NOPE), F32), kr_ref[...],
                               jnp.zeros((rows, LANE - QK_NOPE - QK_ROPE), F32)], axis=1))
    ones_col = jnp.where(lane == V_HEAD, 1.0, 0.0)
    for h in range(GROUP_HEADS):
        cols = slice(h * LANE, (h + 1) * LANE)
        q_out[:, cols] = (rope(qf[:, cols]) * scale).astype(BF16)
        k_out[:, cols] = (kf[:, cols] + kr).astype(BF16)
        v_out[:, cols] = (vf[:, cols] + ones_col).astype(BF16)


def mla_prep(pos, c_q, c_kv, k_r, gq, gkv, w_uq, w_ukv):
    t = c_q.shape[0]
    H = GROUP_HEADS
    wq = jnp.pad(w_uq.reshape(Q_LORA, H, QK_NOPE + QK_ROPE), ((0, 0), (0, 0), (0, LANE - QK_NOPE - QK_ROPE)))
    wkv = w_ukv.reshape(KV_LORA, H, QK_NOPE + V_HEAD)
    wk = jnp.pad(wkv[:, :, :QK_NOPE], ((0, 0), (0, 0), (0, LANE - QK_NOPE)))
    wv = jnp.pad(wkv[:, :, QK_NOPE:], ((0, 0), (0, 0), (0, LANE - V_HEAD)))
    wq, wk, wv = [w.reshape(w.shape[0], H * LANE).astype(BF16) for w in (wq, wk, wv)]
    inv_freq = ROPE_THETA ** (-np.arange(0, QK_ROPE, 2, dtype=np.float32) / QK_ROPE)
    freq = np.zeros((1, LANE), np.float32)
    freq[0, QK_NOPE:QK_NOPE + QK_ROPE] = np.tile(inv_freq, 2)
    row = lambda n: pl.BlockSpec((ROW_TILE, n), lambda i: (i, 0))
    fixed = lambda a: pl.BlockSpec(a.shape, lambda i: (0,) * a.ndim)
    gq2, gkv2, freq = gq.reshape(1, -1), gkv.reshape(1, -1), jnp.asarray(freq)
    out = jax.ShapeDtypeStruct((t, H * LANE), BF16)
    return pl.pallas_call(
        _mla_prep_kernel,
        grid=(t // ROW_TILE,),
        in_specs=[row(1), fixed(freq), row(Q_LORA), row(KV_LORA), row(QK_ROPE), fixed(gq2), fixed(gkv2),
                  fixed(wq), fixed(wk), fixed(wv)],
        out_specs=[row(H * LANE)] * 3,
        out_shape=[out] * 3,
        compiler_params=pltpu.CompilerParams(dimension_semantics=("arbitrary",)),
        name="mla_prep",
    )(pos, freq, c_q, c_kv, k_r, gq2, gkv2, wq, wk, wv)


def _mla_attn_kernel(q_ref, k_ref, v_ref, o_ref, m_scr, acc_scr):
    i = pl.program_id(1)
    tq = q_ref.shape[1]
    row = i * tq + lax.broadcasted_iota(jnp.int32, (tq, MLA_TK), 0)
    col = lax.broadcasted_iota(jnp.int32, (tq, MLA_TK), 1)
    n_full = (i * tq) // MLA_TK
    outs = []
    for h in range(GROUP_HEADS):
        cols = slice(h * LANE, (h + 1) * LANE)
        q = q_ref[0, :, cols]
        m_scr[...] = jnp.full(m_scr.shape, NEG, F32)
        acc_scr[...] = jnp.zeros(acc_scr.shape, F32)

        def step(j, masked):
            start = pl.multiple_of(j * MLA_TK, MLA_TK)
            s = _dot_nt(q, k_ref[0, pl.ds(start, MLA_TK), cols])
            if masked:
                s = jnp.where(start + col <= row, s, NEG)
            m_old = m_scr[...]
            m_new = jnp.maximum(m_old, jnp.max(s, axis=-1, keepdims=True))
            p = jnp.exp(s - m_new)
            acc_scr[...] = (jnp.exp(m_old - m_new) * acc_scr[...]
                            + _dot(p.astype(BF16), v_ref[0, pl.ds(start, MLA_TK), cols]))
            m_scr[...] = m_new

        def body(j, carry):
            step(j, False)
            return carry

        lax.fori_loop(0, n_full, body, 0)
        step(n_full, True)
        acc = acc_scr[...]
        outs.append(acc[:, :V_HEAD] * (1.0 / acc[:, V_HEAD:V_HEAD + 1]))
    o_ref[0] = jnp.concatenate(outs, axis=1)


def mla_attention(q, k, v):
    b, s, w = q.shape
    return pl.pallas_call(
        _mla_attn_kernel,
        grid=(b, s // MLA_TQ),
        in_specs=[pl.BlockSpec((1, MLA_TQ, w), lambda bi, i: (bi, i, 0)),
                  pl.BlockSpec((1, s, w), lambda bi, i: (bi, 0, 0)),
                  pl.BlockSpec((1, s, w), lambda bi, i: (bi, 0, 0))],
        out_specs=pl.BlockSpec((1, MLA_TQ, GROUP_HEADS * V_HEAD), lambda bi, i: (bi, i, 0)),
        out_shape=jax.ShapeDtypeStruct((b, s, GROUP_HEADS * V_HEAD), F32),
        scratch_shapes=[pltpu.VMEM((MLA_TQ, 1), F32), pltpu.VMEM((MLA_TQ, LANE), F32)],
        compiler_params=pltpu.CompilerParams(dimension_semantics=("arbitrary", "arbitrary"),
                                             vmem_limit_bytes=VMEM_LIMIT),
        name="mla_attention",
    )(q, k, v)


def mla_group(h, positions, q_norm_g, kv_norm_g, w_uq, w_ukv):
    b, s, _ = h.shape
    t = b * s
    h2 = h.reshape(t, -1)
    c_q, c_kv, k_r = h2[:, :Q_LORA], h2[:, Q_LORA:Q_LORA + KV_LORA], h2[:, Q_LORA + KV_LORA:]
    q, k, v = mla_prep(positions.reshape(t, 1), c_q, c_kv, k_r, q_norm_g, kv_norm_g, w_uq, w_ukv)
    w = GROUP_HEADS * LANE
    return mla_attention(q.reshape(b, s, w), k.reshape(b, s, w), v.reshape(b, s, w))


CHUNK = 64
CONV_W = 4


def _split3(a):
    hi = a.astype(BF16)
    r = a - hi.astype(F32)
    mid = r.astype(BF16)
    return hi, mid, (r - mid.astype(F32)).astype(BF16)


def _dot_exact_rhs(sel, a):
    return sum(_dot(sel, p) for p in _split3(a))


def _row_matrix(col, n_rows):
    l = col.shape[0]
    lane = lax.broadcasted_iota(jnp.int32, (l, SUBLANES), 1)
    padded = jnp.where(lane == 0, col, 0.0)
    ones = jnp.ones((n_rows, SUBLANES), BF16)
    return sum(_dot_nt(ones, p) for p in _split3(padded))


def _dot_x3(a, b):
    ah = a.astype(BF16)
    al = (a - ah.astype(F32)).astype(BF16)
    bh = b.astype(BF16)
    bl = (b - bh.astype(F32)).astype(BF16)
    return _dot(ah, bh) + _dot(ah, bl) + _dot(al, bh)


def _transpose_bf16(x_bf):
    d = x_bf.shape[1]
    eye = (lax.broadcasted_iota(jnp.int32, (d, d), 0) == lax.broadcasted_iota(jnp.int32, (d, d), 1)).astype(BF16)
    return _dot_nt(eye, x_bf).astype(BF16)


def _log_sigmoid(x):
    return jnp.minimum(x, 0.0) - jnp.log1p(jnp.exp(-jnp.abs(x)))


def _softplus(x):
    return jnp.maximum(x, 0.0) + jnp.log1p(jnp.exp(-jnp.abs(x)))


def _with_ones_col(v):
    l, d = v.shape
    lane = lax.broadcasted_iota(jnp.int32, (l, LANE - d), 1)
    return jnp.concatenate([v, jnp.where(lane == 0, 1.0, 0.0)], axis=1)


def _mlstm_kernel(q_ref, k_ref, v_ref, gate_ref, og_ref, bias_ref, ng_ref, o_ref, st_scr, m_scr):
    H, D, L = GROUP_HEADS, HEAD_DIM, CHUNK

    @pl.when(pl.program_id(1) == 0)
    def _():
        st_scr[...] = jnp.zeros(st_scr.shape, F32)
        m_scr[...] = jnp.zeros(m_scr.shape, F32)

    gates = gate_ref[0] + bias_ref[...]
    row = lax.broadcasted_iota(jnp.int32, (L, L), 0)
    col = lax.broadcasted_iota(jnp.int32, (L, L), 1)
    tri = row >= col
    i_pre = gates[:, :H]
    bcum = _dot_exact_rhs(tri.astype(BF16), _log_sigmoid(gates[:, H:]))
    outs = []
    for h in range(H):
        cols = slice(h * D, (h + 1) * D)
        q = q_ref[0, :, cols].astype(BF16)
        kt = _transpose_bf16((k_ref[0, :, cols] * (D ** -0.5)).astype(BF16))
        v_aug = _with_ones_col(v_ref[0, :, cols])
        b = bcum[:, h:h + 1]
        ic = i_pre[:, h:h + 1]
        m_st = m_scr[h, 0:1, 0:1]
        log_d = jnp.where(tri, b + _row_matrix(ic - b, L), -jnp.inf)
        m_inter = b + m_st
        m_t = jnp.maximum(jnp.max(log_d, axis=-1, keepdims=True), m_inter)
        dmat = jnp.exp(log_d - m_t)
        inter = jnp.exp(m_inter - m_t)
        sqk = _dot(q, kt) * dmat
        st = st_scr[h]
        tot = _dot(sqk.astype(BF16), v_aug.astype(BF16)) + inter * _dot(q, st.astype(BF16))
        den = jnp.maximum(jnp.abs(tot[:, D:D + 1]), jnp.exp(-m_t))
        h_t = tot[:, :D] * (1.0 / den)
        b_end = b[L - 1:L]
        log_w = b_end - b + ic
        m_new = jnp.maximum(b_end + m_st, jnp.max(log_w, axis=0, keepdims=True))
        w = jnp.exp(log_w - m_new)
        st_scr[h] = jnp.exp(b_end + m_st - m_new) * st + _dot(kt, (w * v_aug).astype(BF16))
        m_scr[h] = jnp.broadcast_to(m_new, m_scr.shape[1:])
        o = jax.nn.sigmoid(og_ref[0, :, cols]) * h_t
        outs.append(o * lax.rsqrt(jnp.mean(o * o, axis=-1, keepdims=True) + NORM_EPS) * ng_ref[:, cols])
    o_ref[0] = jnp.concatenate(outs, axis=1)


def mlstm_group(h, i_bias, f_bias, norm_g):
    b, s, _ = h.shape
    W, H = GROUP_W, GROUP_HEADS
    q, k, v = h[..., :W], h[..., W:2 * W], h[..., 2 * W:3 * W]
    gates = h[..., 3 * W:3 * W + 2 * H]
    og = h[..., 3 * W + 2 * H:]
    bias = jnp.concatenate([i_bias, f_bias]).reshape(1, 2 * H)
    tok = lambda n: pl.BlockSpec((1, CHUNK, n), lambda bi, i: (bi, i, 0))
    fixed = lambda n: pl.BlockSpec((1, n), lambda bi, i: (0, 0))
    return pl.pallas_call(
        _mlstm_kernel,
        grid=(b, s // CHUNK),
        in_specs=[tok(W), tok(W), tok(W), tok(2 * H), tok(W), fixed(2 * H), fixed(W)],
        out_specs=tok(W),
        out_shape=jax.ShapeDtypeStruct((b, s, W), F32),
        scratch_shapes=[pltpu.VMEM((H, HEAD_DIM, LANE), F32), pltpu.VMEM((H, SUBLANES, LANE), F32)],
        compiler_params=pltpu.CompilerParams(dimension_semantics=("arbitrary", "arbitrary")),
        name="mlstm",
    )(q, k, v, gates, og, bias, norm_g.reshape(1, W))


def _gdn_kernel(x_ref, xprev_ref, cw_ref, gate_ref, z_ref, par_ref, ng_ref, o_ref, st_scr):
    H, D, L = GROUP_HEADS, HEAD_DIM, CHUNK
    W = H * D
    first = pl.program_id(1) == 0

    @pl.when(first)
    def _():
        st_scr[...] = jnp.zeros(st_scr.shape, F32)

    prev = jnp.where(first, 0.0, xprev_ref[0])
    full = jnp.concatenate([prev, x_ref[0]], axis=0)
    acc = full[SUBLANES:] * cw_ref[CONV_W - 1:CONV_W, :]
    for j in range(CONV_W - 1):
        shifted = pltpu.roll(full, CONV_W - 1 - j, axis=0)[SUBLANES:]
        acc = acc + shifted * cw_ref[j:j + 1, :]
    qkv = acc * jax.nn.sigmoid(acc)

    gates = gate_ref[0]
    par = par_ref[...]
    beta4 = jax.nn.sigmoid(gates[:, :H])
    g4 = -jnp.exp(par[:, :H]) * _softplus(gates[:, H:] + par[:, H:])
    row = lax.broadcasted_iota(jnp.int32, (L, L), 0)
    col = lax.broadcasted_iota(jnp.int32, (L, L), 1)
    tri = row >= col
    eye = (row == col).astype(F32)
    gc4 = _dot_exact_rhs(tri.astype(BF16), g4)
    outs = []
    for h in range(H):
        cols = slice(h * D, (h + 1) * D)
        q = qkv[:, h * D:(h + 1) * D]
        k = qkv[:, W + h * D:W + (h + 1) * D]
        v = qkv[:, 2 * W + h * D:2 * W + (h + 1) * D]
        q = q * lax.rsqrt(jnp.sum(q * q, axis=-1, keepdims=True) + NORM_EPS) * (D ** -0.5)
        k = k * lax.rsqrt(jnp.sum(k * k, axis=-1, keepdims=True) + NORM_EPS)
        beta = beta4[:, h:h + 1]
        gc = gc4[:, h:h + 1]
        gc_row = _row_matrix(gc, L)
        decay = jnp.where(tri, jnp.exp(gc - gc_row), 0.0)
        g_end = gc[L - 1:L]
        kt = _transpose_bf16(k.astype(BF16))
        kb = k * beta
        a_mat = jnp.where(row > col, _dot(kb.astype(BF16), kt) * decay, 0.0)
        x = -a_mat
        t_inv = eye + x
        for _ in range(5):
            x = _dot_x3(x, x)
            t_inv = t_inv + _dot_x3(t_inv, x)
        u = _dot_x3(t_inv, v * beta)
        wm = _dot_x3(t_inv, kb * jnp.exp(gc))
        attn = _dot(q.astype(BF16), kt) * decay
        st = st_scr[h]
        st_bf = st.astype(BF16)
        v_new = u - _dot(wm.astype(BF16), st_bf)
        o = _dot((q * jnp.exp(gc)).astype(BF16), st_bf) + _dot(attn.astype(BF16), v_new.astype(BF16))
        k_dec_t = (kt.astype(F32) * jnp.exp(g_end - gc_row)).astype(BF16)
        st_scr[h] = st * jnp.exp(g_end) + _dot(k_dec_t, v_new.astype(BF16))
        z = z_ref[0, :, cols]
        y = o * lax.rsqrt(jnp.mean(o * o, axis=-1, keepdims=True) + NORM_EPS) * ng_ref[...]
        outs.append(y * (z * jax.nn.sigmoid(z)))
    o_ref[0] = jnp.concatenate(outs, axis=1)


def gdn_group(h, conv_w, a_log, dt_bias, norm_g):
    b, s, _ = h.shape
    W, H = GROUP_W, GROUP_HEADS
    x = h[..., :3 * W]
    gates = h[..., 3 * W:3 * W + 2 * H]
    z = h[..., 3 * W + 2 * H:]
    par = jnp.concatenate([a_log, dt_bias]).reshape(1, 2 * H)
    tok = lambda n: pl.BlockSpec((1, CHUNK, n), lambda bi, i: (bi, i, 0))
    per_chunk = CHUNK // SUBLANES
    prev = pl.BlockSpec((1, SUBLANES, 3 * W), lambda bi, i: (bi, jnp.maximum(i * per_chunk - 1, 0), 0))
    fixed = lambda r, n: pl.BlockSpec((r, n), lambda bi, i: (0, 0))
    return pl.pallas_call(
        _gdn_kernel,
        grid=(b, s // CHUNK),
        in_specs=[tok(3 * W), prev, fixed(CONV_W, 3 * W), tok(2 * H), tok(W), fixed(1, 2 * H), fixed(1, HEAD_DIM)],
        out_specs=tok(W),
        out_shape=jax.ShapeDtypeStruct((b, s, W), F32),
        scratch_shapes=[pltpu.VMEM((H, HEAD_DIM, HEAD_DIM), F32)],
        compiler_params=pltpu.CompilerParams(dimension_semantics=("arbitrary", "arbitrary")),
        name="gdn",
    )(x, x, conv_w, gates, z, par, norm_g.reshape(1, HEAD_DIM))


NSA_IN = GROUP_W + 6 * HEAD_DIM + 3 * GROUP_HEADS
MLSTM_IN = 4 * GROUP_W + 2 * GROUP_HEADS
GDN_IN = 4 * GROUP_W + 2 * GROUP_HEADS
MLA_IN = Q_LORA + KV_LORA + QK_ROPE


def kernel(x, positions, t5_table, norm1_g, norm2_g, final_norm_g, w_in, w_out, nsa_w_cmp_k, nsa_w_cmp_v, nsa_cmp_pos, mlstm_i_bias, mlstm_f_bias, mlstm_norm_g, gdn_conv_w, gdn_a_log, gdn_dt_bias, gdn_norm_g, mla_q_norm_g, mla_kv_norm_g, mla_w_uq, mla_w_ukv, peer_w_q, peer_sub_keys, peer_u, peer_v):
    b, s, d = x.shape
    t = b * s
    h = x.reshape(t, d)
    depth = w_in.shape[0]
    d_in = w_in.shape[2]
    d_in_pad = _round_up(d_in, LANE)
    for l in range(depth):
        w_in_l = jnp.pad(w_in[l], ((0, 0), (0, d_in_pad - d_in))).astype(BF16)
        hp = norm_matmul(h, norm1_g[l], w_in_l)[:, :d_in].reshape(b, s, d_in)
        c0, c1, c2 = NSA_IN, NSA_IN + MLSTM_IN, NSA_IN + MLSTM_IN + GDN_IN
        h_nsa, h_ml, h_gdn, h_mla = hp[..., :c0], hp[..., c0:c1], hp[..., c1:c2], hp[..., c2:]
        o = jnp.concatenate([
            nsa_group(h_nsa, nsa_w_cmp_k[l], nsa_w_cmp_v[l], nsa_cmp_pos[l], t5_table),
            mlstm_group(h_ml, mlstm_i_bias[l], mlstm_f_bias[l], mlstm_norm_g[l]),
            gdn_group(h_gdn, gdn_conv_w[l], gdn_a_log[l], gdn_dt_bias[l], gdn_norm_g[l]),
            mla_group(h_mla, positions, mla_q_norm_g[l], mla_kv_norm_g[l], mla_w_uq[l], mla_w_ukv[l]),
        ], axis=-1)
        h = matmul_residual(o.reshape(t, -1), w_out[l].astype(BF16), h)
        h = peer_ffn_residual(h, norm2_g[l], peer_w_q[l], peer_sub_keys[l], peer_u[l], peer_v[l])
    return rmsnorm(h, final_norm_g).reshape(b, s, d)
```

```python
import functools
import math

import jax
import jax.numpy as jnp
import numpy as np
from jax import lax
from jax.experimental import pallas as pl
from jax.experimental.pallas import tpu as pltpu

F32 = jnp.float32
BF16 = jnp.bfloat16
NORM_EPS = 1e-6
NEG = -1e30
LANE = 128
ROW_TILE = 512
VMEM_LIMIT = 56 * 1024 * 1024

HEAD_DIM = 64
GROUP_HEADS = 4
GROUP_W = GROUP_HEADS * HEAD_DIM
Q_BLOCK = 128
CMP_LEN = 32
CMP_STRIDE = 16
SEL_LEN = 64
SEL_TOPN = 16
WINDOW = 512
SEL_FORCE = 1e3
N_BUCKETS = 32
MAX_DISTANCE = 128


def _round_up(n, m):
    return (n + m - 1) // m * m


def _dot_nt(a, b):
    return lax.dot_general(a, b, (((1,), (1,)), ((), ())), preferred_element_type=F32)


def _dot(a, b):
    return jnp.dot(a, b, preferred_element_type=F32)


def _norm_matmul_kernel(x_ref, g_ref, w_ref, o_ref):
    x = x_ref[...]
    y = x * lax.rsqrt(jnp.mean(x * x, axis=-1, keepdims=True) + NORM_EPS) * g_ref[...]
    o_ref[...] = _dot(y.astype(BF16), w_ref[...])


def norm_matmul(x, g, w):
    t, d = x.shape
    n = w.shape[1]
    return pl.pallas_call(
        _norm_matmul_kernel,
        grid=(t // ROW_TILE,),
        in_specs=[pl.BlockSpec((ROW_TILE, d), lambda i: (i, 0)),
                  pl.BlockSpec((1, d), lambda i: (0, 0)),
                  pl.BlockSpec((d, n), lambda i: (0, 0))],
        out_specs=pl.BlockSpec((ROW_TILE, n), lambda i: (i, 0)),
        out_shape=jax.ShapeDtypeStruct((t, n), F32),
        compiler_params=pltpu.CompilerParams(dimension_semantics=("arbitrary",),
                                             vmem_limit_bytes=VMEM_LIMIT),
        name="norm_matmul",
    )(x, g.reshape(1, d), w)


def _matmul_residual_kernel(o_ref, w_ref, r_ref, out_ref):
    out_ref[...] = r_ref[...] + _dot(o_ref[...].astype(BF16), w_ref[...])


def matmul_residual(o, w, res):
    t, d = o.shape
    n = w.shape[1]
    return pl.pallas_call(
        _matmul_residual_kernel,
        grid=(t // ROW_TILE,),
        in_specs=[pl.BlockSpec((ROW_TILE, d), lambda i: (i, 0)),
                  pl.BlockSpec((d, n), lambda i: (0, 0)),
                  pl.BlockSpec((ROW_TILE, n), lambda i: (i, 0))],
        out_specs=pl.BlockSpec((ROW_TILE, n), lambda i: (i, 0)),
        out_shape=jax.ShapeDtypeStruct((t, n), F32),
        compiler_params=pltpu.CompilerParams(dimension_semantics=("arbitrary",),
                                             vmem_limit_bytes=VMEM_LIMIT),
        name="matmul_residual",
    )(o, w, res)


def _rmsnorm_kernel(x_ref, g_ref, o_ref):
    x = x_ref[...]
    o_ref[...] = x * lax.rsqrt(jnp.mean(x * x, axis=-1, keepdims=True) + NORM_EPS) * g_ref[...]


def rmsnorm(x, g):
    t, d = x.shape
    return pl.pallas_call(
        _rmsnorm_kernel,
        grid=(t // ROW_TILE,),
        in_specs=[pl.BlockSpec((ROW_TILE, d), lambda i: (i, 0)),
                  pl.BlockSpec((1, d), lambda i: (0, 0))],
        out_specs=pl.BlockSpec((ROW_TILE, d), lambda i: (i, 0)),
        out_shape=jax.ShapeDtypeStruct((t, d), F32),
        compiler_params=pltpu.CompilerParams(dimension_semantics=("arbitrary",)),
        name="rmsnorm",
    )(x, g.reshape(1, d))


def _t5_bucket_starts():
    d = np.arange(0, 2 * MAX_DISTANCE)
    max_exact = N_BUCKETS // 2
    val = (np.log(np.maximum(d, 1).astype(np.float32) / np.float32(max_exact))
           / np.float32(math.log(MAX_DISTANCE / max_exact)) * np.float32(N_BUCKETS - max_exact))
    large = np.minimum(max_exact + val.astype(np.int32), N_BUCKETS - 1)
    bucket = np.where(d < max_exact, d, large)
    assert np.all(np.diff(bucket) >= 0)
    return [(int(k), int(d[bucket == k].min())) for k in range(N_BUCKETS) if np.any(bucket == k)]


T5_STARTS = _t5_bucket_starts()


def _t5_bias_heads(dist, tab_ref):
    accs = [jnp.full(dist.shape, tab_ref[0, h], F32) for h in range(GROUP_HEADS)]
    for bucket, start in T5_STARTS[1:]:
        m = dist >= start
        accs = [jnp.where(m, tab_ref[bucket, h], accs[h]) for h in range(GROUP_HEADS)]
    return jnp.stack(accs)


def _nsa_compress_kernel(x_ref, pos_ref, w_ref, o_ref):
    x = x_ref[0, 0]
    n = x.shape[0]
    lo = _dot((x + pos_ref[0, 0:1, :]).astype(BF16), w_ref[0, 0])
    hi = _dot((x + pos_ref[0, 1:2, :]).astype(BF16), w_ref[0, 1])
    o_ref[0, 0] = (lo + pltpu.roll(hi, n - 1, axis=0)).astype(o_ref.dtype)


def nsa_compress(x, pos, w):
    two, b, n, c = x.shape
    return pl.pallas_call(
        _nsa_compress_kernel,
        grid=(two, b),
        in_specs=[pl.BlockSpec((1, 1, n, c), lambda k, i: (k, i, 0, 0)),
                  pl.BlockSpec((1, 2, c), lambda k, i: (0, 0, 0)),
                  pl.BlockSpec((1, 2, c, HEAD_DIM), lambda k, i: (k, 0, 0, 0))],
        out_specs=pl.BlockSpec((1, 1, n, HEAD_DIM), lambda k, i: (k, i, 0, 0)),
        out_shape=jax.ShapeDtypeStruct((two, b, n, HEAD_DIM), BF16),
        compiler_params=pltpu.CompilerParams(dimension_semantics=("arbitrary", "arbitrary")),
        name="nsa_compress",
    )(x, pos, w)


FAR_TILE = 512
NEAR_KEYS = 768
WIN_KEYS = WINDOW + Q_BLOCK
NEAR_PAD = NEAR_KEYS - Q_BLOCK
CMP_TILE = LANE
CMP_TILES_BACK = CMP_TILE * CMP_STRIDE // Q_BLOCK
N_CMP_BIAS = CMP_TILES_BACK + 3


def _masked_bias(dist, valid, tab_ref):
    return jnp.where(valid, _t5_bias_heads(dist, tab_ref), NEG)


def _nsa_bias_kernel(tab_ref, near_ref, win_ref):
    q = lax.broadcasted_iota(jnp.int32, (Q_BLOCK, NEAR_KEYS), 0)
    k = lax.broadcasted_iota(jnp.int32, (Q_BLOCK, NEAR_KEYS), 1)
    dist = NEAR_PAD + q - k
    near_ref[...] = _masked_bias(dist, dist >= 0, tab_ref)
    q = lax.broadcasted_iota(jnp.int32, (Q_BLOCK, WIN_KEYS), 0)
    k = lax.broadcasted_iota(jnp.int32, (Q_BLOCK, WIN_KEYS), 1)
    dist = WINDOW + q - k
    win_ref[...] = _masked_bias(dist, (dist >= 0) & (dist < WINDOW), tab_ref)


def _nsa_cmp_bias_kernel(tab_ref, o_ref):
    e = pl.program_id(0)
    q = lax.broadcasted_iota(jnp.int32, (Q_BLOCK, CMP_TILE), 0)
    n = lax.broadcasted_iota(jnp.int32, (Q_BLOCK, CMP_TILE), 1)
    dist = (e - 1) * Q_BLOCK + q - (n * CMP_STRIDE + CMP_LEN - 1)
    o_ref[0] = _masked_bias(dist, dist >= 0, tab_ref)


def nsa_bias_tiles(t5_table):
    H, Q = GROUP_HEADS, Q_BLOCK
    smem = pl.BlockSpec(memory_space=pltpu.SMEM)
    near, win = pl.pallas_call(
        _nsa_bias_kernel, grid=(1,), in_specs=[smem],
        out_specs=[pl.BlockSpec((H, Q, NEAR_KEYS), lambda i: (0, 0, 0)),
                   pl.BlockSpec((H, Q, WIN_KEYS), lambda i: (0, 0, 0))],
        out_shape=[jax.ShapeDtypeStruct((H, Q, NEAR_KEYS), F32), jax.ShapeDtypeStruct((H, Q, WIN_KEYS), F32)],
        name="nsa_bias",
    )(t5_table)
    cmp = pl.pallas_call(
        _nsa_cmp_bias_kernel, grid=(N_CMP_BIAS,), in_specs=[smem],
        out_specs=pl.BlockSpec((1, H, Q, CMP_TILE), lambda e: (e, 0, 0, 0)),
        out_shape=jax.ShapeDtypeStruct((N_CMP_BIAS, H, Q, CMP_TILE), F32),
        name="nsa_cmp_bias",
    )(t5_table)
    return near, win, cmp


def _nsa_attn_kernel(tab_ref, q_ref, g_ref, kc_ref, vc_ref, ks_ref, vs_ref, kw_ref, vw_ref,
                     bnear_ref, bwin_ref, bcmp_ref, o_ref, m_scr, acc_scr):
    H, Q, D = GROUP_HEADS, Q_BLOCK, HEAD_DIM
    n_cmp = kc_ref.shape[1]
    n_sel = (ks_ref.shape[1] - NEAR_PAD) // SEL_LEN
    i = pl.program_id(1)
    t0 = i * Q

    q = q_ref[0] * (D ** -0.5)
    q4 = jnp.concatenate([q[:, h * D:(h + 1) * D] for h in range(H)], axis=0).astype(BF16)

    s = _dot_nt(q4, kc_ref[0]).reshape(H, Q, n_cmp)
    tiles = []
    for g in range(n_cmp // CMP_TILE):
        e = jnp.clip(i - g * CMP_TILES_BACK, -1, CMP_TILES_BACK + 1) + 1
        tiles.append(s[:, :, g * CMP_TILE:(g + 1) * CMP_TILE] + bcmp_ref[e])
    s = jnp.concatenate(tiles, axis=-1)
    t_c = t0 + lax.broadcasted_iota(jnp.int32, (Q, n_cmp), 0)
    valid_c = t_c >= lax.broadcasted_iota(jnp.int32, (Q, n_cmp), 1) * CMP_STRIDE + CMP_LEN - 1
    p = jnp.where(valid_c, jnp.exp(s - jnp.max(s, axis=-1, keepdims=True)), 0.0)
    l = jnp.sum(p, axis=-1, keepdims=True)
    p = p * (1.0 / jnp.where(l > 0.0, l, 1.0))
    o_c = _dot(p.reshape(H * Q, n_cmp).astype(BF16), vc_ref[0])

    psum = p[0] + p[1] + p[2] + p[3]
    c_idx = lax.broadcasted_iota(jnp.int32, (n_cmp, n_sel), 0)
    j_idx = lax.broadcasted_iota(jnp.int32, (n_cmp, n_sel), 1)
    ratio = SEL_LEN // CMP_STRIDE
    n_in_sel = (SEL_LEN - CMP_LEN) // CMP_STRIDE + 1
    pool = ((c_idx // ratio == j_idx) & (c_idx % ratio < n_in_sel)).astype(BF16)
    p_hi = psum.astype(BF16)
    p_mid = (psum - p_hi.astype(F32)).astype(BF16)
    p_lo = (psum - p_hi.astype(F32) - p_mid.astype(F32)).astype(BF16)
    imp = _dot(p_hi, pool) + _dot(p_mid, pool) + _dot(p_lo, pool)

    jb = lax.broadcasted_iota(jnp.int32, (Q, n_sel), 1)
    cur = (t0 + lax.broadcasted_iota(jnp.int32, (Q, n_sel), 0)) // SEL_LEN
    forced = (jb == 0) | (jb == cur) | (jb == cur - 1)
    work = jnp.where(jb <= cur, imp + jnp.where(forced, SEL_FORCE, 0.0), NEG)
    sel = jnp.zeros((Q, n_sel), F32)
    for _ in range(min(SEL_TOPN, n_sel)):
        mx = jnp.max(work, axis=-1, keepdims=True)
        first = jnp.min(jnp.where(work == mx, jb, n_sel), axis=-1, keepdims=True)
        pick = jb == first
        sel = jnp.where(pick & (mx > 0.5 * NEG), 1.0, sel)
        work = jnp.where(pick, -jnp.inf, work)
    sel_bf = sel.astype(BF16)

    far_bias = jnp.stack([jnp.full((Q, 1), tab_ref[N_BUCKETS - 1, h], F32) for h in range(H)])
    m_scr[...] = jnp.full(m_scr.shape, NEG, F32)
    acc_scr[...] = jnp.zeros(acc_scr.shape, F32)

    def sel_step(row0, n_keys, bias, first_key):
        k = ks_ref[0, pl.ds(row0, n_keys), :]
        v = vs_ref[0, pl.ds(row0, n_keys), :]
        s = _dot_nt(q4, k).reshape(H, Q, n_keys) + bias
        pos = row0 - NEAR_PAD + lax.broadcasted_iota(jnp.int32, (n_sel, n_keys), 1)
        blk = jnp.where(pos >= first_key, pos // SEL_LEN, -1)
        expand = (lax.broadcasted_iota(jnp.int32, (n_sel, n_keys), 0) == blk).astype(BF16)
        mask = _dot(sel_bf, expand) > 0.5
        s = jnp.where(mask, s, NEG)
        m_old = m_scr[...].reshape(H, Q, 1)
        m_new = jnp.maximum(m_old, jnp.max(s, axis=-1, keepdims=True))
        p = jnp.where(mask, jnp.exp(s - m_new), 0.0).reshape(H * Q, n_keys)
        alpha = jnp.exp(m_old - m_new).reshape(H * Q, 1)
        acc_scr[...] = alpha * acc_scr[...] + _dot(p.astype(BF16), v)
        m_scr[...] = m_new.reshape(H * Q, 1)

    n_far = jnp.maximum(i - 1, 0) // (FAR_TILE // Q)

    def far_body(kt, carry):
        sel_step(pl.multiple_of(NEAR_PAD + kt * FAR_TILE, Q), FAR_TILE, far_bias, 0)
        return carry

    lax.fori_loop(0, n_far, far_body, 0)
    row0 = pl.multiple_of(t0, Q)
    sel_step(row0, NEAR_KEYS, bnear_ref[...], n_far * FAR_TILE)
    acc = acc_scr[...]
    o_s = acc[:, :D] * (1.0 / acc[:, D:D + 1])

    kw = kw_ref[0, pl.ds(row0, WIN_KEYS), :]
    vw = vw_ref[0, pl.ds(row0, WIN_KEYS), :]
    in_seq = lax.broadcasted_iota(jnp.int32, (1, WIN_KEYS), 1) >= WINDOW - t0
    s = _dot_nt(q4, kw).reshape(H, Q, WIN_KEYS) + bwin_ref[...] + jnp.where(in_seq, 0.0, NEG)
    p = jnp.exp(s - jnp.max(s, axis=-1, keepdims=True))
    acc_w = _dot(p.reshape(H * Q, WIN_KEYS).astype(BF16), vw)
    o_w = acc_w[:, :D] * (1.0 / acc_w[:, D:D + 1])

    g = jax.nn.sigmoid(g_ref[0])
    outs = []
    for h in range(H):
        rows = slice(h * Q, (h + 1) * Q)
        outs.append(g[:, h:h + 1] * o_c[rows] + g[:, H + h:H + h + 1] * o_s[rows]
                    + g[:, 2 * H + h:2 * H + h + 1] * o_w[rows])
    o_ref[0] = jnp.concatenate(outs, axis=1)


def nsa_attention(t5_table, bias_tiles, q, gates, k_cmp, v_cmp, ks, vs_aug, kw, vw_aug):
    b, s, _ = q.shape
    bnear, bwin, bcmp = bias_tiles
    full = lambda a: pl.BlockSpec((1,) + a.shape[1:], lambda bi, i: (bi,) + (0,) * (a.ndim - 1))
    fixed = lambda a: pl.BlockSpec(a.shape, lambda bi, i: (0,) * a.ndim)
    return pl.pallas_call(
        _nsa_attn_kernel,
        grid=(b, s // Q_BLOCK),
        in_specs=[pl.BlockSpec(memory_space=pltpu.SMEM),
                  pl.BlockSpec((1, Q_BLOCK, GROUP_W), lambda bi, i: (bi, i, 0)),
                  pl.BlockSpec((1, Q_BLOCK, 3 * GROUP_HEADS), lambda bi, i: (bi, i, 0)),
                  full(k_cmp), full(v_cmp), full(ks), full(vs_aug), full(kw), full(vw_aug),
                  fixed(bnear), fixed(bwin), fixed(bcmp)],
        out_specs=pl.BlockSpec((1, Q_BLOCK, GROUP_W), lambda bi, i: (bi, i, 0)),
        out_shape=jax.ShapeDtypeStruct((b, s, GROUP_W), F32),
        scratch_shapes=[pltpu.VMEM((GROUP_HEADS * Q_BLOCK, 1), F32),
                        pltpu.VMEM((GROUP_HEADS * Q_BLOCK, 2 * HEAD_DIM), F32)],
        compiler_params=pltpu.CompilerParams(dimension_semantics=("arbitrary", "arbitrary"),
                                             vmem_limit_bytes=VMEM_LIMIT),
        name="nsa_attention",
    )(t5_table, q, gates, k_cmp, v_cmp, ks, vs_aug, kw, vw_aug, bnear, bwin, bcmp)


def _front_pad(x, rows):
    return jnp.pad(x, ((0, 0), (rows, 0), (0, 0)))


def _with_ones(v):
    b, s, d = v.shape
    pad = jnp.zeros((b, s, d), BF16).at[:, :, 0].set(1.0)
    return jnp.concatenate([v.astype(BF16), pad], axis=-1)


def nsa_group(h, w_cmp_k, w_cmp_v, cmp_pos, t5_table, bias_tiles=None):
    if bias_tiles is None:
        bias_tiles = nsa_bias_tiles(t5_table)
    b, s, _ = h.shape
    d = HEAD_DIM
    q = h[..., :GROUP_W]
    kc, vc, ks, vs, kw, vw = [h[..., GROUP_W + j * d: GROUP_W + (j + 1) * d] for j in range(6)]
    gates = h[..., GROUP_W + 6 * d:]
    n = s // CMP_STRIDE
    x = jnp.stack([kc, vc]).reshape(2, b, n, CMP_STRIDE * d)
    pos = cmp_pos.reshape(1, 2, CMP_STRIDE * d)
    w = jnp.stack([w_cmp_k, w_cmp_v]).reshape(2, 2, CMP_STRIDE * d, d).astype(BF16)
    cmp = nsa_compress(x, pos, w)
    return nsa_attention(t5_table, bias_tiles, q, gates, cmp[0], cmp[1],
                         _front_pad(ks.astype(BF16), NEAR_PAD), _front_pad(_with_ones(vs), NEAR_PAD),
                         _front_pad(kw.astype(BF16), WINDOW), _front_pad(_with_ones(vw), WINDOW))


PEER_HEADS = 8
PEER_DK = 256
N_KEYS = 128
PEER_TOPK = 16
PEER_SLOTS = PEER_HEADS * PEER_TOPK
ROUTE_TILE = 128
GATHER_TILE = 8


def _top_rows(work, order, n_top, payload=None):
    vals, args = [], []
    for _ in range(n_top):
        mx = jnp.max(work, axis=0, keepdims=True)
        first = jnp.min(jnp.where(work == mx, order, jnp.int32(2 ** 30)), axis=0, keepdims=True)
        pick = order == first
        vals.append(mx)
        if payload is None:
            args.append(first)
        else:
            args.append(jnp.sum(jnp.where(pick, payload, 0), axis=0, keepdims=True))
        work = jnp.where(pick, -jnp.inf, work)
    return jnp.concatenate(vals, axis=0), jnp.concatenate(args, axis=0)


def _pair_rows(s1, s2, combine):
    k = PEER_TOPK
    rows = [combine(s1[0:1], s2)]
    for a in range(1, k // 2):
        rows.append(combine(s1[a:a + 1], s2[0:k // 2]))
    rows.append(combine(s1[k // 2:], s2[0:1]))
    return jnp.concatenate(rows, axis=0)


def _peer_route_kernel(h_ref, g_ref, wq_ref, keys_ref, xn_ref, eidx_ref, gate_ref):
    x = h_ref[...]
    xn = x * lax.rsqrt(jnp.mean(x * x, axis=-1, keepdims=True) + NORM_EPS) * g_ref[...]
    xn_ref[...] = xn
    q = _dot(xn.astype(BF16), wq_ref[...]).astype(BF16)
    tq = x.shape[0]
    half = PEER_DK // 2
    n_iota = lax.broadcasted_iota(jnp.int32, (N_KEYS, tq), 0)
    k_iota = lax.broadcasted_iota(jnp.int32, (PEER_TOPK, tq), 0)
    order = _pair_rows(k_iota, k_iota, lambda a, b: a * PEER_TOPK + b)
    reachable = (order // PEER_TOPK + 1) * (order % PEER_TOPK + 1) <= PEER_TOPK
    for p in range(PEER_HEADS):
        tops = []
        for c in range(2):
            g = 2 * p + c
            sc = _dot_nt(keys_ref[g], q[:, g * half:(g + 1) * half])
            tops.append(_top_rows(sc, n_iota, PEER_TOPK))
        (s1, i1), (s2, i2) = tops
        cand = jnp.where(reachable, _pair_rows(s1, s2, lambda a, b: a + b), -jnp.inf)
        cidx = _pair_rows(i1, i2, lambda a, b: a * N_KEYS + b)
        top, eidx = _top_rows(cand, order, PEER_TOPK, payload=cidx)
        e = jnp.exp(top - top[0:1])
        eidx_ref[p * PEER_TOPK:(p + 1) * PEER_TOPK, :] = eidx
        gate_ref[p * PEER_TOPK:(p + 1) * PEER_TOPK, :] = e * (1.0 / jnp.sum(e, axis=0, keepdims=True))


def peer_route(h, g, w_q, keys):
    t, d = h.shape
    nq = w_q.shape[1]
    return pl.pallas_call(
        _peer_route_kernel,
        grid=(t // ROUTE_TILE,),
        in_specs=[pl.BlockSpec((ROUTE_TILE, d), lambda i: (i, 0)),
                  pl.BlockSpec((1, d), lambda i: (0, 0)),
                  pl.BlockSpec((d, nq), lambda i: (0, 0)),
                  pl.BlockSpec(keys.shape, lambda i: (0, 0, 0))],
        out_specs=[pl.BlockSpec((ROUTE_TILE, d), lambda i: (i, 0)),
                   pl.BlockSpec((PEER_SLOTS, ROUTE_TILE), lambda i: (0, i)),
                   pl.BlockSpec((PEER_SLOTS, ROUTE_TILE), lambda i: (0, i))],
        out_shape=[jax.ShapeDtypeStruct((t, d), F32),
                   jax.ShapeDtypeStruct((PEER_SLOTS, t), jnp.int32),
                   jax.ShapeDtypeStruct((PEER_SLOTS, t), F32)],
        compiler_params=pltpu.CompilerParams(dimension_semantics=("arbitrary",),
                                             vmem_limit_bytes=VMEM_LIMIT),
        name="peer_route",
    )(h, g.reshape(1, d), w_q, keys)


SUBLANES = 8
SLAB = 2 * SUBLANES
ISSUE_UNROLL = 8


def _gelu(x):
    return 0.5 * x * (1.0 + lax.erf(x * (2.0 ** -0.5)))


BITREV8 = (0, 4, 2, 6, 1, 5, 3, 7)


def _fold_pair(a, b, half, sub):
    low = (sub % (2 * half)) < half
    partner = jnp.where(low, pltpu.roll(a, SUBLANES - half, axis=0), pltpu.roll(b, half, axis=0))
    return jnp.where(low, a, b) + partner


GATHER_DEPTH = 3


def _peer_gather_kernel(idx0_ref, idx1_ref, idx2_ref, xn_ref, h_ref, gate_ref, uv_hbm, o_ref,
                        buf_a, buf_b, buf_c, w_scr, sem):
    i = pl.program_id(0)
    n = pl.num_programs(0)
    tt = xn_ref.shape[0]
    n_slabs = tt * PEER_SLOTS
    bufs = (buf_a, buf_b, buf_c)

    def request(ref, k, r):
        src = pl.multiple_of(ref[0, 0, r] * SLAB, SLAB)
        pltpu.make_async_copy(uv_hbm.at[pl.ds(src, SLAB)], bufs[k].at[pl.ds(r * SLAB, SLAB)], sem.at[k]).start()

    def request_tile(ref, k):
        def body(c, carry):
            for j in range(ISSUE_UNROLL):
                request(ref, k, pl.multiple_of(c * ISSUE_UNROLL, ISSUE_UNROLL) + j)
            return carry
        lax.fori_loop(0, n_slabs // ISSUE_UNROLL, body, 0)

    def wait_tile(k):
        pltpu.make_async_copy(uv_hbm.at[pl.ds(0, n_slabs * SLAB)], bufs[k], sem.at[k]).wait()

    @pl.when(i == 0)
    def _():
        request_tile(idx0_ref, 0)
        request_tile(idx1_ref, 1)

    def step(cur, ahead):
        wait_tile(cur)
        gates = gate_ref[0]
        sub = lax.broadcasted_iota(jnp.int32, (SUBLANES, LANE), 0)
        for t in range(tt):
            for j in range(PEER_SLOTS):
                request(idx2_ref, ahead, t * PEER_SLOTS + j)
            x = xn_ref[t]

            def slab(pos, half):
                return bufs[cur][pl.ds((t * PEER_SLOTS + pos) * SLAB + half * SUBLANES, SUBLANES), :]

            groups = []
            for g in range(PEER_SLOTS // SUBLANES):
                p = [slab(g * SUBLANES + k, 0) * x for k in range(SUBLANES)]
                c = [_fold_pair(p[2 * k], p[2 * k + 1], 4, sub) for k in range(4)]
                d = [_fold_pair(c[0], c[1], 2, sub), _fold_pair(c[2], c[3], 2, sub)]
                groups.append(_fold_pair(d[0], d[1], 1, sub))
            act = jnp.sum(jnp.concatenate(groups, axis=0), axis=-1, keepdims=True)
            w_scr[...] = jnp.broadcast_to(gates[:, t:t + 1] * _gelu(act), (PEER_SLOTS, LANE))
            acc = h_ref[t]
            for g in range(PEER_SLOTS // SUBLANES):
                for k in range(SUBLANES):
                    w_row = w_scr[pl.ds(g * SUBLANES + BITREV8[k], 1), :]
                    acc = acc + slab(g * SUBLANES + k, 1) * w_row
            o_ref[t] = acc

    for k in range(GATHER_DEPTH):
        @pl.when(i % GATHER_DEPTH == k)
        def _(k=k):
            step(k, (k + GATHER_DEPTH - 1) % GATHER_DEPTH)

    @pl.when(i == n - 1)
    def _():
        for k in range(GATHER_DEPTH):
            @pl.when(i % GATHER_DEPTH == k)
            def _(k=k):
                wait_tile((k + 1) % GATHER_DEPTH)
                wait_tile((k + 2) % GATHER_DEPTH)


def peer_gather(idx, xn, h, gate_b, uv):
    t = xn.shape[0]
    nt = t // GATHER_TILE
    n_slabs = GATHER_TILE * PEER_SLOTS
    tok = pl.BlockSpec((GATHER_TILE, SUBLANES, LANE), lambda i: (i, 0, 0))
    idx_spec = lambda k: pl.BlockSpec((1, 1, n_slabs), lambda i: (jnp.minimum(i + k, nt - 1), 0, 0),
                                      memory_space=pltpu.SMEM)
    slab_buf = pltpu.VMEM((n_slabs * SLAB, LANE), F32)
    return pl.pallas_call(
        _peer_gather_kernel,
        grid=(nt,),
        in_specs=[idx_spec(0), idx_spec(1), idx_spec(2), tok, tok,
                  pl.BlockSpec((1, PEER_SLOTS, GATHER_TILE), lambda i: (i, 0, 0)),
                  pl.BlockSpec(memory_space=pl.ANY)],
        out_specs=tok,
        out_shape=jax.ShapeDtypeStruct((t, SUBLANES, LANE), F32),
        scratch_shapes=[slab_buf, slab_buf, slab_buf, pltpu.VMEM((PEER_SLOTS, LANE), F32),
                        pltpu.SemaphoreType.DMA((GATHER_DEPTH,))],
        compiler_params=pltpu.CompilerParams(dimension_semantics=("arbitrary",),
                                             vmem_limit_bytes=VMEM_LIMIT),
        name="peer_gather",
    )(idx, idx, idx, xn, h, gate_b, uv)


def peer_ffn_residual(h, g, w_q, sub_keys, u_tab, v_tab):
    t, d = h.shape
    e = u_tab.shape[0]
    keys = sub_keys.reshape(PEER_HEADS * 2, N_KEYS, PEER_DK // 2).astype(BF16)
    xn, eidx_t, gate_t = peer_route(h, g, w_q.astype(BF16), keys)
    nt = t // GATHER_TILE
    order = np.arange(PEER_SLOTS).reshape(-1, SUBLANES)[:, list(BITREV8)].reshape(-1)
    idx = eidx_t[order].T.reshape(nt, 1, GATHER_TILE * PEER_SLOTS)
    gate_b = gate_t.reshape(PEER_SLOTS, nt, GATHER_TILE).transpose(1, 0, 2)
    uv = jnp.concatenate([u_tab.reshape(e, SUBLANES, LANE), v_tab.reshape(e, SUBLANES, LANE)],
                         axis=1).reshape(e * SLAB, LANE)
    out = peer_gather(idx, xn.reshape(t, SUBLANES, LANE), h.reshape(t, SUBLANES, LANE), gate_b, uv)
    return out.reshape(t, d)


Q_LORA = 256
KV_LORA = 128
QK_NOPE = 64
QK_ROPE = 32
V_HEAD = 64
ROPE_THETA = 10000.0
MLA_TQ = 256
MLA_TK = 512


def _rms(x, g):
    return x * lax.rsqrt(jnp.mean(x * x, axis=-1, keepdims=True) + NORM_EPS) * g


def _mla_prep_kernel(pos_ref, freq_ref, cq_ref, ckv_ref, kr_ref, gq_ref, gkv_ref, wq_ref, wk_ref, wv_ref,
                     q_out, k_out, v_out):
    rows = cq_ref.shape[0]
    half = QK_ROPE // 2
    cq = _rms(cq_ref[...], gq_ref[...]).astype(BF16)
    ckv = _rms(ckv_ref[...], gkv_ref[...]).astype(BF16)
    qf = _dot(cq, wq_ref[...])
    kf = _dot(ckv, wk_ref[...])
    vf = _dot(ckv, wv_ref[...])
    lane = lax.broadcasted_iota(jnp.int32, (rows, LANE), 1)
    ang = pos_ref[...].astype(F32) * freq_ref[...]
    cos, sin = jnp.cos(ang), jnp.sin(ang)
    in_rope = (lane >= QK_NOPE) & (lane < QK_NOPE + QK_ROPE)
    first = lane < QK_NOPE + half
    c_mul = jnp.where(lane < QK_NOPE, 1.0, jnp.where(in_rope, cos, 0.0))
    s_mul = jnp.where(in_rope, jnp.where(first, -sin, sin), 0.0)

    def rope(x):
        partner = jnp.where(first, pltpu.roll(x, LANE - half, axis=1), pltpu.roll(x, half, axis=1))
        return x * c_mul + partner * s_mul

    scale = (QK_NOPE + QK_ROPE) ** -0.5
    kr = rope(jnp.concatenate([jnp.zeros((rows, QK_NOPE), F32), kr_ref[...],
                               jnp.zeros((rows, LANE - QK_NOPE - QK_ROPE), F32)], axis=1))
    ones_col = jnp.where(lane == V_HEAD, 1.0, 0.0)
    for h in range(GROUP_HEADS):
        cols = slice(h * LANE, (h + 1) * LANE)
        q_out[:, cols] = (rope(qf[:, cols]) * scale).astype(BF16)
        k_out[:, cols] = (kf[:, cols] + kr).astype(BF16)
        v_out[:, cols] = (vf[:, cols] + ones_col).astype(BF16)


def mla_prep(pos, c_q, c_kv, k_r, gq, gkv, w_uq, w_ukv):
    t = c_q.shape[0]
    H = GROUP_HEADS
    wq = jnp.pad(w_uq.reshape(Q_LORA, H, QK_NOPE + QK_ROPE), ((0, 0), (0, 0), (0, LANE - QK_NOPE - QK_ROPE)))
    wkv = w_ukv.reshape(KV_LORA, H, QK_NOPE + V_HEAD)
    wk = jnp.pad(wkv[:, :, :QK_NOPE], ((0, 0), (0, 0), (0, LANE - QK_NOPE)))
    wv = jnp.pad(wkv[:, :, QK_NOPE:], ((0, 0), (0, 0), (0, LANE - V_HEAD)))
    wq, wk, wv = [w.reshape(w.shape[0], H * LANE).astype(BF16) for w in (wq, wk, wv)]
    inv_freq = ROPE_THETA ** (-np.arange(0, QK_ROPE, 2, dtype=np.float32) / QK_ROPE)
    freq = np.zeros((1, LANE), np.float32)
    freq[0, QK_NOPE:QK_NOPE + QK_ROPE] = np.tile(inv_freq, 2)
    row = lambda n: pl.BlockSpec((ROW_TILE, n), lambda i: (i, 0))
    fixed = lambda a: pl.BlockSpec(a.shape, lambda i: (0,) * a.ndim)
    gq2, gkv2, freq = gq.reshape(1, -1), gkv.reshape(1, -1), jnp.asarray(freq)
    out = jax.ShapeDtypeStruct((t, H * LANE), BF16)
    return pl.pallas_call(
        _mla_prep_kernel,
        grid=(t // ROW_TILE,),
        in_specs=[row(1), fixed(freq), row(Q_LORA), row(KV_LORA), row(QK_ROPE), fixed(gq2), fixed(gkv2),
                  fixed(wq), fixed(wk), fixed(wv)],
        out_specs=[row(H * LANE)] * 3,
        out_shape=[out] * 3,
        compiler_params=pltpu.CompilerParams(dimension_semantics=("arbitrary",)),
        name="mla_prep",
    )(pos, freq, c_q, c_kv, k_r, gq2, gkv2, wq, wk, wv)


def _mla_attn_kernel(q_ref, k_ref, v_ref, o_ref, m_scr, acc_scr):
    i = pl.program_id(1)
    tq = q_ref.shape[1]
    row = i * tq + lax.broadcasted_iota(jnp.int32, (tq, MLA_TK), 0)
    col = lax.broadcasted_iota(jnp.int32, (tq, MLA_TK), 1)
    n_full = (i * tq) // MLA_TK
    outs = []
    for h in range(GROUP_HEADS):
        cols = slice(h * LANE, (h + 1) * LANE)
        q = q_ref[0, :, cols]
        m_scr[...] = jnp.full(m_scr.shape, NEG, F32)
        acc_scr[...] = jnp.zeros(acc_scr.shape, F32)

        def step(j, masked):
            start = pl.multiple_of(j * MLA_TK, MLA_TK)
            s = _dot_nt(q, k_ref[0, pl.ds(start, MLA_TK), cols])
            if masked:
                s = jnp.where(start + col <= row, s, NEG)
            m_old = m_scr[...]
            m_new = jnp.maximum(m_old, jnp.max(s, axis=-1, keepdims=True))
            p = jnp.exp(s - m_new)
            acc_scr[...] = (jnp.exp(m_old - m_new) * acc_scr[...]
                            + _dot(p.astype(BF16), v_ref[0, pl.ds(start, MLA_TK), cols]))
            m_scr[...] = m_new

        def body(j, carry):
            step(j, False)
            return carry

        lax.fori_loop(0, n_full, body, 0)
        step(n_full, True)
        acc = acc_scr[...]
        outs.append(acc[:, :V_HEAD] * (1.0 / acc[:, V_HEAD:V_HEAD + 1]))
    o_ref[0] = jnp.concatenate(outs, axis=1)


def mla_attention(q, k, v):
    b, s, w = q.shape
    return pl.pallas_call(
        _mla_attn_kernel,
        grid=(b, s // MLA_TQ),
        in_specs=[pl.BlockSpec((1, MLA_TQ, w), lambda bi, i: (bi, i, 0)),
                  pl.BlockSpec((1, s, w), lambda bi, i: (bi, 0, 0)),
                  pl.BlockSpec((1, s, w), lambda bi, i: (bi, 0, 0))],
        out_specs=pl.BlockSpec((1, MLA_TQ, GROUP_HEADS * V_HEAD), lambda bi, i: (bi, i, 0)),
        out_shape=jax.ShapeDtypeStruct((b, s, GROUP_HEADS * V_HEAD), F32),
        scratch_shapes=[pltpu.VMEM((MLA_TQ, 1), F32), pltpu.VMEM((MLA_TQ, LANE), F32)],
        compiler_params=pltpu.CompilerParams(dimension_semantics=("arbitrary", "arbitrary"),
                                             vmem_limit_bytes=VMEM_LIMIT),
        name="mla_attention",
    )(q, k, v)


def mla_group(h, positions, q_norm_g, kv_norm_g, w_uq, w_ukv):
    b, s, _ = h.shape
    t = b * s
    h2 = h.reshape(t, -1)
    c_q, c_kv, k_r = h2[:, :Q_LORA], h2[:, Q_LORA:Q_LORA + KV_LORA], h2[:, Q_LORA + KV_LORA:]
    q, k, v = mla_prep(positions.reshape(t, 1), c_q, c_kv, k_r, q_norm_g, kv_norm_g, w_uq, w_ukv)
    w = GROUP_HEADS * LANE
    return mla_attention(q.reshape(b, s, w), k.reshape(b, s, w), v.reshape(b, s, w))


CHUNK = 64
CONV_W = 4


def _split3(a):
    hi = a.astype(BF16)
    r = a - hi.astype(F32)
    mid = r.astype(BF16)
    return hi, mid, (r - mid.astype(F32)).astype(BF16)


def _dot_exact_rhs(sel, a):
    return sum(_dot(sel, p) for p in _split3(a))


def _row_matrix(col, n_rows):
    l = col.shape[0]
    lane = lax.broadcasted_iota(jnp.int32, (l, SUBLANES), 1)
    padded = jnp.where(lane == 0, col, 0.0)
    ones = jnp.ones((n_rows, SUBLANES), BF16)
    return sum(_dot_nt(ones, p) for p in _split3(padded))


def _dot_x3(a, b):
    ah = a.astype(BF16)
    al = (a - ah.astype(F32)).astype(BF16)
    bh = b.astype(BF16)
    bl = (b - bh.astype(F32)).astype(BF16)
    return _dot(ah, bh) + _dot(ah, bl) + _dot(al, bh)


def _transpose_bf16(x_bf):
    d = x_bf.shape[1]
    eye = (lax.broadcasted_iota(jnp.int32, (d, d), 0) == lax.broadcasted_iota(jnp.int32, (d, d), 1)).astype(BF16)
    return _dot_nt(eye, x_bf).astype(BF16)


def _log_sigmoid(x):
    return jnp.minimum(x, 0.0) - jnp.log1p(jnp.exp(-jnp.abs(x)))


def _softplus(x):
    return jnp.maximum(x, 0.0) + jnp.log1p(jnp.exp(-jnp.abs(x)))


def _with_ones_col(v):
    l, d = v.shape
    lane = lax.broadcasted_iota(jnp.int32, (l, LANE - d), 1)
    return jnp.concatenate([v, jnp.where(lane == 0, 1.0, 0.0)], axis=1)


def _mlstm_kernel(q_ref, k_ref, v_ref, gate_ref, og_ref, bias_ref, ng_ref, o_ref, st_scr, m_scr):
    H, D, L = GROUP_HEADS, HEAD_DIM, CHUNK

    @pl.when(pl.program_id(1) == 0)
    def _():
        st_scr[...] = jnp.zeros(st_scr.shape, F32)
        m_scr[...] = jnp.zeros(m_scr.shape, F32)

    gates = gate_ref[0] + bias_ref[...]
    row = lax.broadcasted_iota(jnp.int32, (L, L), 0)
    col = lax.broadcasted_iota(jnp.int32, (L, L), 1)
    tri = row >= col
    i_pre = gates[:, :H]
    bcum = _dot_exact_rhs(tri.astype(BF16), _log_sigmoid(gates[:, H:]))
    outs = []
    for h in range(H):
        cols = slice(h * D, (h + 1) * D)
        q = q_ref[0, :, cols].astype(BF16)
        kt = _transpose_bf16((k_ref[0, :, cols] * (D ** -0.5)).astype(BF16))
        v_aug = _with_ones_col(v_ref[0, :, cols])
        b = bcum[:, h:h + 1]
        ic = i_pre[:, h:h + 1]
        m_st = m_scr[h, 0:1, 0:1]
        log_d = jnp.where(tri, b + _row_matrix(ic - b, L), -jnp.inf)
        m_inter = b + m_st
        m_t = jnp.maximum(jnp.max(log_d, axis=-1, keepdims=True), m_inter)
        dmat = jnp.exp(log_d - m_t)
        inter = jnp.exp(m_inter - m_t)
        sqk = _dot(q, kt) * dmat
        st = st_scr[h]
        tot = _dot(sqk.astype(BF16), v_aug.astype(BF16)) + inter * _dot(q, st.astype(BF16))
        den = jnp.maximum(jnp.abs(tot[:, D:D + 1]), jnp.exp(-m_t))
        h_t = tot[:, :D] * (1.0 / den)
        b_end = b[L - 1:L]
        log_w = b_end - b + ic
        m_new = jnp.maximum(b_end + m_st, jnp.max(log_w, axis=0, keepdims=True))
        w = jnp.exp(log_w - m_new)
        st_scr[h] = jnp.exp(b_end + m_st - m_new) * st + _dot(kt, (w * v_aug).astype(BF16))
        m_scr[h] = jnp.broadcast_to(m_new, m_scr.shape[1:])
        o = jax.nn.sigmoid(og_ref[0, :, cols]) * h_t
        outs.append(o * lax.rsqrt(jnp.mean(o * o, axis=-1, keepdims=True) + NORM_EPS) * ng_ref[:, cols])
    o_ref[0] = jnp.concatenate(outs, axis=1)


def mlstm_group(h, i_bias, f_bias, norm_g):
    b, s, _ = h.shape
    W, H = GROUP_W, GROUP_HEADS
    q, k, v = h[..., :W], h[..., W:2 * W], h[..., 2 * W:3 * W]
    gates = h[..., 3 * W:3 * W + 2 * H]
    og = h[..., 3 * W + 2 * H:]
    bias = jnp.concatenate([i_bias, f_bias]).reshape(1, 2 * H)
    tok = lambda n: pl.BlockSpec((1, CHUNK, n), lambda bi, i: (bi, i, 0))
    fixed = lambda n: pl.BlockSpec((1, n), lambda bi, i: (0, 0))
    return pl.pallas_call(
        _mlstm_kernel,
        grid=(b, s // CHUNK),
        in_specs=[tok(W), tok(W), tok(W), tok(2 * H), tok(W), fixed(2 * H), fixed(W)],
        out_specs=tok(W),
        out_shape=jax.ShapeDtypeStruct((b, s, W), F32),
        scratch_shapes=[pltpu.VMEM((H, HEAD_DIM, LANE), F32), pltpu.VMEM((H, SUBLANES, LANE), F32)],
        compiler_params=pltpu.CompilerParams(dimension_semantics=("arbitrary", "arbitrary")),
        name="mlstm",
    )(q, k, v, gates, og, bias, norm_g.reshape(1, W))


def _gdn_kernel(x_ref, xprev_ref, cw_ref, gate_ref, z_ref, par_ref, ng_ref, o_ref, st_scr):
    H, D, L = GROUP_HEADS, HEAD_DIM, CHUNK
    W = H * D
    first = pl.program_id(1) == 0

    @pl.when(first)
    def _():
        st_scr[...] = jnp.zeros(st_scr.shape, F32)

    prev = jnp.where(first, 0.0, xprev_ref[0])
    full = jnp.concatenate([prev, x_ref[0]], axis=0)
    acc = full[SUBLANES:] * cw_ref[CONV_W - 1:CONV_W, :]
    for j in range(CONV_W - 1):
        shifted = pltpu.roll(full, CONV_W - 1 - j, axis=0)[SUBLANES:]
        acc = acc + shifted * cw_ref[j:j + 1, :]
    qkv = acc * jax.nn.sigmoid(acc)

    gates = gate_ref[0]
    par = par_ref[...]
    beta4 = jax.nn.sigmoid(gates[:, :H])
    g4 = -jnp.exp(par[:, :H]) * _softplus(gates[:, H:] + par[:, H:])
    row = lax.broadcasted_iota(jnp.int32, (L, L), 0)
    col = lax.broadcasted_iota(jnp.int32, (L, L), 1)
    tri = row >= col
    eye = (row == col).astype(F32)
    gc4 = _dot_exact_rhs(tri.astype(BF16), g4)
    outs = []
    for h in range(H):
        cols = slice(h * D, (h + 1) * D)
        q = qkv[:, h * D:(h + 1) * D]
        k = qkv[:, W + h * D:W + (h + 1) * D]
        v = qkv[:, 2 * W + h * D:2 * W + (h + 1) * D]
        q = q * lax.rsqrt(jnp.sum(q * q, axis=-1, keepdims=True) + NORM_EPS) * (D ** -0.5)
        k = k * lax.rsqrt(jnp.sum(k * k, axis=-1, keepdims=True) + NORM_EPS)
        beta = beta4[:, h:h + 1]
        gc = gc4[:, h:h + 1]
        gc_row = _row_matrix(gc, L)
        decay = jnp.where(tri, jnp.exp(gc - gc_row), 0.0)
        g_end = gc[L - 1:L]
        kt = _transpose_bf16(k.astype(BF16))
        kb = k * beta
        a_mat = jnp.where(row > col, _dot(kb.astype(BF16), kt) * decay, 0.0)
        x = -a_mat
        t_inv = eye + x
        for _ in range(5):
            x = _dot_x3(x, x)
            t_inv = t_inv + _dot_x3(t_inv, x)
        u = _dot_x3(t_inv, v * beta)
        wm = _dot_x3(t_inv, kb * jnp.exp(gc))
        attn = _dot(q.astype(BF16), kt) * decay
        st = st_scr[h]
        st_bf = st.astype(BF16)
        v_new = u - _dot(wm.astype(BF16), st_bf)
        o = _dot((q * jnp.exp(gc)).astype(BF16), st_bf) + _dot(attn.astype(BF16), v_new.astype(BF16))
        k_dec_t = (kt.astype(F32) * jnp.exp(g_end - gc_row)).astype(BF16)
        st_scr[h] = st * jnp.exp(g_end) + _dot(k_dec_t, v_new.astype(BF16))
        z = z_ref[0, :, cols]
        y = o * lax.rsqrt(jnp.mean(o * o, axis=-1, keepdims=True) + NORM_EPS) * ng_ref[...]
        outs.append(y * (z * jax.nn.sigmoid(z)))
    o_ref[0] = jnp.concatenate(outs, axis=1)


def gdn_group(h, conv_w, a_log, dt_bias, norm_g):
    b, s, _ = h.shape
    W, H = GROUP_W, GROUP_HEADS
    x = h[..., :3 * W]
    gates = h[..., 3 * W:3 * W + 2 * H]
    z = h[..., 3 * W + 2 * H:]
    par = jnp.concatenate([a_log, dt_bias]).reshape(1, 2 * H)
    tok = lambda n: pl.BlockSpec((1, CHUNK, n), lambda bi, i: (bi, i, 0))
    per_chunk = CHUNK // SUBLANES
    prev = pl.BlockSpec((1, SUBLANES, 3 * W), lambda bi, i: (bi, jnp.maximum(i * per_chunk - 1, 0), 0))
    fixed = lambda r, n: pl.BlockSpec((r, n), lambda bi, i: (0, 0))
    return pl.pallas_call(
        _gdn_kernel,
        grid=(b, s // CHUNK),
        in_specs=[tok(3 * W), prev, fixed(CONV_W, 3 * W), tok(2 * H), tok(W), fixed(1, 2 * H), fixed(1, HEAD_DIM)],
        out_specs=tok(W),
        out_shape=jax.ShapeDtypeStruct((b, s, W), F32),
        scratch_shapes=[pltpu.VMEM((H, HEAD_DIM, HEAD_DIM), F32)],
        compiler_params=pltpu.CompilerParams(dimension_semantics=("arbitrary", "arbitrary")),
        name="gdn",
    )(x, x, conv_w, gates, z, par, norm_g.reshape(1, HEAD_DIM))


NSA_IN = GROUP_W + 6 * HEAD_DIM + 3 * GROUP_HEADS
MLSTM_IN = 4 * GROUP_W + 2 * GROUP_HEADS
GDN_IN = 4 * GROUP_W + 2 * GROUP_HEADS
MLA_IN = Q_LORA + KV_LORA + QK_ROPE


def kernel(x, positions, t5_table, norm1_g, norm2_g, final_norm_g, w_in, w_out, nsa_w_cmp_k, nsa_w_cmp_v, nsa_cmp_pos, mlstm_i_bias, mlstm_f_bias, mlstm_norm_g, gdn_conv_w, gdn_a_log, gdn_dt_bias, gdn_norm_g, mla_q_norm_g, mla_kv_norm_g, mla_w_uq, mla_w_ukv, peer_w_q, peer_sub_keys, peer_u, peer_v):
    b, s, d = x.shape
    t = b * s
    h = x.reshape(t, d)
    depth = w_in.shape[0]
    d_in = w_in.shape[2]
    d_in_pad = _round_up(d_in, LANE)
    bias_tiles = nsa_bias_tiles(t5_table)
    for l in range(depth):
        w_in_l = jnp.pad(w_in[l], ((0, 0), (0, d_in_pad - d_in))).astype(BF16)
        hp = norm_matmul(h, norm1_g[l], w_in_l)[:, :d_in].reshape(b, s, d_in)
        c0, c1, c2 = NSA_IN, NSA_IN + MLSTM_IN, NSA_IN + MLSTM_IN + GDN_IN
        h_nsa, h_ml, h_gdn, h_mla = hp[..., :c0], hp[..., c0:c1], hp[..., c1:c2], hp[..., c2:]
        o = jnp.concatenate([
            nsa_group(h_nsa, nsa_w_cmp_k[l], nsa_w_cmp_v[l], nsa_cmp_pos[l], t5_table, bias_tiles),
            mlstm_group(h_ml, mlstm_i_bias[l], mlstm_f_bias[l], mlstm_norm_g[l]),
            gdn_group(h_gdn, gdn_conv_w[l], gdn_a_log[l], gdn_dt_bias[l], gdn_norm_g[l]),
            mla_group(h_mla, positions, mla_q_norm_g[l], mla_kv_norm_g[l], mla_w_uq[l], mla_w_ukv[l]),
        ], axis=-1)
        h = matmul_residual(o.reshape(t, -1), w_out[l].astype(BF16), h)
        h = peer_ffn_residual(h, norm2_g[l], peer_w_q[l], peer_sub_keys[l], peer_u[l], peer_v[l])
    return rmsnorm(h, final_norm_g).reshape(b, s, d)
```

```python
import functools
import math

import jax
import jax.numpy as jnp
import numpy as np
from jax import lax
from jax.experimental import pallas as pl
from jax.experimental.pallas import tpu as pltpu

F32 = jnp.float32
BF16 = jnp.bfloat16
NORM_EPS = 1e-6
NEG = -1e30
LANE = 128
ROW_TILE = 512
VMEM_LIMIT = 56 * 1024 * 1024

HEAD_DIM = 64
GROUP_HEADS = 4
GROUP_W = GROUP_HEADS * HEAD_DIM
Q_BLOCK = 128
CMP_LEN = 32
CMP_STRIDE = 16
SEL_LEN = 64
SEL_TOPN = 16
WINDOW = 512
SEL_FORCE = 1e3
N_BUCKETS = 32
MAX_DISTANCE = 128


def _round_up(n, m):
    return (n + m - 1) // m * m


def _dot_nt(a, b):
    return lax.dot_general(a, b, (((1,), (1,)), ((), ())), preferred_element_type=F32)


def _dot(a, b):
    return jnp.dot(a, b, preferred_element_type=F32)


def _norm_matmul_kernel(x_ref, g_ref, w_ref, o_ref):
    x = x_ref[...]
    y = x * lax.rsqrt(jnp.mean(x * x, axis=-1, keepdims=True) + NORM_EPS) * g_ref[...]
    o_ref[...] = _dot(y.astype(BF16), w_ref[...])


def norm_matmul(x, g, w):
    t, d = x.shape
    n = w.shape[1]
    return pl.pallas_call(
        _norm_matmul_kernel,
        grid=(t // ROW_TILE,),
        in_specs=[pl.BlockSpec((ROW_TILE, d), lambda i: (i, 0)),
                  pl.BlockSpec((1, d), lambda i: (0, 0)),
                  pl.BlockSpec((d, n), lambda i: (0, 0))],
        out_specs=pl.BlockSpec((ROW_TILE, n), lambda i: (i, 0)),
        out_shape=jax.ShapeDtypeStruct((t, n), F32),
        compiler_params=pltpu.CompilerParams(dimension_semantics=("arbitrary",),
                                             vmem_limit_bytes=VMEM_LIMIT),
        name="norm_matmul",
    )(x, g.reshape(1, d), w)


def _matmul_residual_kernel(o_ref, w_ref, r_ref, out_ref):
    out_ref[...] = r_ref[...] + _dot(o_ref[...].astype(BF16), w_ref[...])


def matmul_residual(o, w, res):
    t, d = o.shape
    n = w.shape[1]
    return pl.pallas_call(
        _matmul_residual_kernel,
        grid=(t // ROW_TILE,),
        in_specs=[pl.BlockSpec((ROW_TILE, d), lambda i: (i, 0)),
                  pl.BlockSpec((d, n), lambda i: (0, 0)),
                  pl.BlockSpec((ROW_TILE, n), lambda i: (i, 0))],
        out_specs=pl.BlockSpec((ROW_TILE, n), lambda i: (i, 0)),
        out_shape=jax.ShapeDtypeStruct((t, n), F32),
        compiler_params=pltpu.CompilerParams(dimension_semantics=("arbitrary",),
                                             vmem_limit_bytes=VMEM_LIMIT),
        name="matmul_residual",
    )(o, w, res)


def _rmsnorm_kernel(x_ref, g_ref, o_ref):
    x = x_ref[...]
    o_ref[...] = x * lax.rsqrt(jnp.mean(x * x, axis=-1, keepdims=True) + NORM_EPS) * g_ref[...]


def rmsnorm(x, g):
    t, d = x.shape
    return pl.pallas_call(
        _rmsnorm_kernel,
        grid=(t // ROW_TILE,),
        in_specs=[pl.BlockSpec((ROW_TILE, d), lambda i: (i, 0)),
                  pl.BlockSpec((1, d), lambda i: (0, 0))],
        out_specs=pl.BlockSpec((ROW_TILE, d), lambda i: (i, 0)),
        out_shape=jax.ShapeDtypeStruct((t, d), F32),
        compiler_params=pltpu.CompilerParams(dimension_semantics=("arbitrary",)),
        name="rmsnorm",
    )(x, g.reshape(1, d))


def _t5_bucket_starts():
    d = np.arange(0, 2 * MAX_DISTANCE)
    max_exact = N_BUCKETS // 2
    val = (np.log(np.maximum(d, 1).astype(np.float32) / np.float32(max_exact))
           / np.float32(math.log(MAX_DISTANCE / max_exact)) * np.float32(N_BUCKETS - max_exact))
    large = np.minimum(max_exact + val.astype(np.int32), N_BUCKETS - 1)
    bucket = np.where(d < max_exact, d, large)
    assert np.all(np.diff(bucket) >= 0)
    return [(int(k), int(d[bucket == k].min())) for k in range(N_BUCKETS) if np.any(bucket == k)]


T5_STARTS = _t5_bucket_starts()


def _t5_bias_heads(dist, tab_ref):
    accs = [jnp.full(dist.shape, tab_ref[0, h], F32) for h in range(GROUP_HEADS)]
    for bucket, start in T5_STARTS[1:]:
        m = dist >= start
        accs = [jnp.where(m, tab_ref[bucket, h], accs[h]) for h in range(GROUP_HEADS)]
    return jnp.stack(accs)


def _nsa_compress_kernel(x_ref, pos_ref, w_ref, o_ref):
    x = x_ref[0, 0]
    n = x.shape[0]
    lo = _dot((x + pos_ref[0, 0:1, :]).astype(BF16), w_ref[0, 0])
    hi = _dot((x + pos_ref[0, 1:2, :]).astype(BF16), w_ref[0, 1])
    o_ref[0, 0] = (lo + pltpu.roll(hi, n - 1, axis=0)).astype(o_ref.dtype)


def nsa_compress(x, pos, w):
    two, b, n, c = x.shape
    return pl.pallas_call(
        _nsa_compress_kernel,
        grid=(two, b),
        in_specs=[pl.BlockSpec((1, 1, n, c), lambda k, i: (k, i, 0, 0)),
                  pl.BlockSpec((1, 2, c), lambda k, i: (0, 0, 0)),
                  pl.BlockSpec((1, 2, c, HEAD_DIM), lambda k, i: (k, 0, 0, 0))],
        out_specs=pl.BlockSpec((1, 1, n, HEAD_DIM), lambda k, i: (k, i, 0, 0)),
        out_shape=jax.ShapeDtypeStruct((two, b, n, HEAD_DIM), BF16),
        compiler_params=pltpu.CompilerParams(dimension_semantics=("arbitrary", "arbitrary")),
        name="nsa_compress",
    )(x, pos, w)


FAR_TILE = 512
NEAR_KEYS = 768
WIN_KEYS = WINDOW + Q_BLOCK
NEAR_PAD = NEAR_KEYS - Q_BLOCK
CMP_TILE = LANE
CMP_TILES_BACK = CMP_TILE * CMP_STRIDE // Q_BLOCK
N_CMP_BIAS = CMP_TILES_BACK + 3


def _masked_bias(dist, valid, tab_ref):
    return jnp.where(valid, _t5_bias_heads(dist, tab_ref), NEG)


def _nsa_bias_kernel(tab_ref, near_ref, win_ref):
    q = lax.broadcasted_iota(jnp.int32, (Q_BLOCK, NEAR_KEYS), 0)
    k = lax.broadcasted_iota(jnp.int32, (Q_BLOCK, NEAR_KEYS), 1)
    dist = NEAR_PAD + q - k
    near_ref[...] = _masked_bias(dist, dist >= 0, tab_ref)
    q = lax.broadcasted_iota(jnp.int32, (Q_BLOCK, WIN_KEYS), 0)
    k = lax.broadcasted_iota(jnp.int32, (Q_BLOCK, WIN_KEYS), 1)
    dist = WINDOW + q - k
    win_ref[...] = _masked_bias(dist, (dist >= 0) & (dist < WINDOW), tab_ref)


def _nsa_cmp_bias_kernel(tab_ref, o_ref):
    e = pl.program_id(0)
    q = lax.broadcasted_iota(jnp.int32, (Q_BLOCK, CMP_TILE), 0)
    n = lax.broadcasted_iota(jnp.int32, (Q_BLOCK, CMP_TILE), 1)
    dist = (e - 1) * Q_BLOCK + q - (n * CMP_STRIDE + CMP_LEN - 1)
    o_ref[0] = _masked_bias(dist, dist >= 0, tab_ref)


def nsa_bias_tiles(t5_table):
    H, Q = GROUP_HEADS, Q_BLOCK
    smem = pl.BlockSpec(memory_space=pltpu.SMEM)
    near, win = pl.pallas_call(
        _nsa_bias_kernel, grid=(1,), in_specs=[smem],
        out_specs=[pl.BlockSpec((H, Q, NEAR_KEYS), lambda i: (0, 0, 0)),
                   pl.BlockSpec((H, Q, WIN_KEYS), lambda i: (0, 0, 0))],
        out_shape=[jax.ShapeDtypeStruct((H, Q, NEAR_KEYS), F32), jax.ShapeDtypeStruct((H, Q, WIN_KEYS), F32)],
        name="nsa_bias",
    )(t5_table)
    cmp = pl.pallas_call(
        _nsa_cmp_bias_kernel, grid=(N_CMP_BIAS,), in_specs=[smem],
        out_specs=pl.BlockSpec((1, H, Q, CMP_TILE), lambda e: (e, 0, 0, 0)),
        out_shape=jax.ShapeDtypeStruct((N_CMP_BIAS, H, Q, CMP_TILE), F32),
        name="nsa_cmp_bias",
    )(t5_table)
    return near, win, cmp


def _nsa_attn_kernel(tab_ref, q_ref, g_ref, kc_ref, vc_ref, ks_ref, vs_ref, kw_ref, vw_ref,
                     bnear_ref, bwin_ref, bcmp_ref, o_ref, m_scr, acc_scr):
    H, Q, D = GROUP_HEADS, Q_BLOCK, HEAD_DIM
    n_cmp = kc_ref.shape[1]
    n_sel = (ks_ref.shape[1] - NEAR_PAD) // SEL_LEN
    i = pl.program_id(1)
    t0 = i * Q

    q = q_ref[0] * (D ** -0.5)
    q4 = jnp.concatenate([q[:, h * D:(h + 1) * D] for h in range(H)], axis=0).astype(BF16)

    s = _dot_nt(q4, kc_ref[0]).reshape(H, Q, n_cmp)
    tiles = []
    for g in range(n_cmp // CMP_TILE):
        e = jnp.clip(i - g * CMP_TILES_BACK, -1, CMP_TILES_BACK + 1) + 1
        tiles.append(s[:, :, g * CMP_TILE:(g + 1) * CMP_TILE] + bcmp_ref[e])
    s = jnp.concatenate(tiles, axis=-1)
    t_c = t0 + lax.broadcasted_iota(jnp.int32, (Q, n_cmp), 0)
    valid_c = t_c >= lax.broadcasted_iota(jnp.int32, (Q, n_cmp), 1) * CMP_STRIDE + CMP_LEN - 1
    p = jnp.where(valid_c, jnp.exp(s - jnp.max(s, axis=-1, keepdims=True)), 0.0)
    l = jnp.sum(p, axis=-1, keepdims=True)
    p = p * (1.0 / jnp.where(l > 0.0, l, 1.0))
    o_c = _dot(p.reshape(H * Q, n_cmp).astype(BF16), vc_ref[0])

    psum = p[0] + p[1] + p[2] + p[3]
    c_idx = lax.broadcasted_iota(jnp.int32, (n_cmp, n_sel), 0)
    j_idx = lax.broadcasted_iota(jnp.int32, (n_cmp, n_sel), 1)
    ratio = SEL_LEN // CMP_STRIDE
    n_in_sel = (SEL_LEN - CMP_LEN) // CMP_STRIDE + 1
    pool = ((c_idx // ratio == j_idx) & (c_idx % ratio < n_in_sel)).astype(BF16)
    p_hi = psum.astype(BF16)
    p_mid = (psum - p_hi.astype(F32)).astype(BF16)
    p_lo = (psum - p_hi.astype(F32) - p_mid.astype(F32)).astype(BF16)
    imp = _dot(p_hi, pool) + _dot(p_mid, pool) + _dot(p_lo, pool)

    jb = lax.broadcasted_iota(jnp.int32, (Q, n_sel), 1)
    cur = (t0 + lax.broadcasted_iota(jnp.int32, (Q, n_sel), 0)) // SEL_LEN
    forced = (jb == 0) | (jb == cur) | (jb == cur - 1)
    work = jnp.where(jb <= cur, imp + jnp.where(forced, SEL_FORCE, 0.0), NEG)
    sel = jnp.zeros((Q, n_sel), F32)
    for _ in range(min(SEL_TOPN, n_sel)):
        mx = jnp.max(work, axis=-1, keepdims=True)
        first = jnp.min(jnp.where(work == mx, jb, n_sel), axis=-1, keepdims=True)
        pick = jb == first
        sel = jnp.where(pick & (mx > 0.5 * NEG), 1.0, sel)
        work = jnp.where(pick, -jnp.inf, work)
    sel_bf = sel.astype(BF16)

    far_bias = jnp.stack([jnp.full((Q, 1), tab_ref[N_BUCKETS - 1, h], F32) for h in range(H)])
    m_scr[...] = jnp.full(m_scr.shape, NEG, F32)
    acc_scr[...] = jnp.zeros(acc_scr.shape, F32)

    def sel_step(row0, n_keys, bias, first_key):
        k = ks_ref[0, pl.ds(row0, n_keys), :]
        v = vs_ref[0, pl.ds(row0, n_keys), :]
        s = _dot_nt(q4, k).reshape(H, Q, n_keys) + bias
        pos = row0 - NEAR_PAD + lax.broadcasted_iota(jnp.int32, (n_sel, n_keys), 1)
        blk = jnp.where(pos >= first_key, pos // SEL_LEN, -1)
        expand = (lax.broadcasted_iota(jnp.int32, (n_sel, n_keys), 0) == blk).astype(BF16)
        mask = _dot(sel_bf, expand) > 0.5
        s = jnp.where(mask, s, NEG)
        m_old = m_scr[...].reshape(H, Q, 1)
        m_new = jnp.maximum(m_old, jnp.max(s, axis=-1, keepdims=True))
        p = jnp.where(mask, jnp.exp(s - m_new), 0.0).reshape(H * Q, n_keys)
        alpha = jnp.exp(m_old - m_new).reshape(H * Q, 1)
        acc_scr[...] = alpha * acc_scr[...] + _dot(p.astype(BF16), v)
        m_scr[...] = m_new.reshape(H * Q, 1)

    n_far = jnp.maximum(i - 1, 0) // (FAR_TILE // Q)

    def far_body(kt, carry):
        sel_step(pl.multiple_of(NEAR_PAD + kt * FAR_TILE, Q), FAR_TILE, far_bias, 0)
        return carry

    lax.fori_loop(0, n_far, far_body, 0)
    row0 = pl.multiple_of(t0, Q)
    sel_step(row0, NEAR_KEYS, bnear_ref[...], n_far * FAR_TILE)
    acc = acc_scr[...]
    o_s = acc[:, :D] * (1.0 / acc[:, D:D + 1])

    kw = kw_ref[0, pl.ds(row0, WIN_KEYS), :]
    vw = vw_ref[0, pl.ds(row0, WIN_KEYS), :]
    in_seq = lax.broadcasted_iota(jnp.int32, (1, WIN_KEYS), 1) >= WINDOW - t0
    s = _dot_nt(q4, kw).reshape(H, Q, WIN_KEYS) + bwin_ref[...] + jnp.where(in_seq, 0.0, NEG)
    p = jnp.exp(s - jnp.max(s, axis=-1, keepdims=True))
    acc_w = _dot(p.reshape(H * Q, WIN_KEYS).astype(BF16), vw)
    o_w = acc_w[:, :D] * (1.0 / acc_w[:, D:D + 1])

    g = jax.nn.sigmoid(g_ref[0])
    outs = []
    for h in range(H):
        rows = slice(h * Q, (h + 1) * Q)
        outs.append(g[:, h:h + 1] * o_c[rows] + g[:, H + h:H + h + 1] * o_s[rows]
                    + g[:, 2 * H + h:2 * H + h + 1] * o_w[rows])
    o_ref[0] = jnp.concatenate(outs, axis=1)


def nsa_attention(t5_table, bias_tiles, q, gates, k_cmp, v_cmp, ks, vs_aug, kw, vw_aug):
    b, s, _ = q.shape
    bnear, bwin, bcmp = bias_tiles
    full = lambda a: pl.BlockSpec((1,) + a.shape[1:], lambda bi, i: (bi,) + (0,) * (a.ndim - 1))
    fixed = lambda a: pl.BlockSpec(a.shape, lambda bi, i: (0,) * a.ndim)
    return pl.pallas_call(
        _nsa_attn_kernel,
        grid=(b, s // Q_BLOCK),
        in_specs=[pl.BlockSpec(memory_space=pltpu.SMEM),
                  pl.BlockSpec((1, Q_BLOCK, GROUP_W), lambda bi, i: (bi, i, 0)),
                  pl.BlockSpec((1, Q_BLOCK, 3 * GROUP_HEADS), lambda bi, i: (bi, i, 0)),
                  full(k_cmp), full(v_cmp), full(ks), full(vs_aug), full(kw), full(vw_aug),
                  fixed(bnear), fixed(bwin), fixed(bcmp)],
        out_specs=pl.BlockSpec((1, Q_BLOCK, GROUP_W), lambda bi, i: (bi, i, 0)),
        out_shape=jax.ShapeDtypeStruct((b, s, GROUP_W), F32),
        scratch_shapes=[pltpu.VMEM((GROUP_HEADS * Q_BLOCK, 1), F32),
                        pltpu.VMEM((GROUP_HEADS * Q_BLOCK, 2 * HEAD_DIM), F32)],
        compiler_params=pltpu.CompilerParams(dimension_semantics=("arbitrary", "arbitrary"),
                                             vmem_limit_bytes=VMEM_LIMIT),
        name="nsa_attention",
    )(t5_table, q, gates, k_cmp, v_cmp, ks, vs_aug, kw, vw_aug, bnear, bwin, bcmp)


def _front_pad(x, rows):
    return jnp.pad(x, ((0, 0), (rows, 0), (0, 0)))


def _with_ones(v):
    b, s, d = v.shape
    pad = jnp.zeros((b, s, d), BF16).at[:, :, 0].set(1.0)
    return jnp.concatenate([v.astype(BF16), pad], axis=-1)


def nsa_group(h, w_cmp_k, w_cmp_v, cmp_pos, t5_table, bias_tiles=None):
    if bias_tiles is None:
        bias_tiles = nsa_bias_tiles(t5_table)
    b, s, _ = h.shape
    d = HEAD_DIM
    q = h[..., :GROUP_W]
    kc, vc, ks, vs, kw, vw = [h[..., GROUP_W + j * d: GROUP_W + (j + 1) * d] for j in range(6)]
    gates = h[..., GROUP_W + 6 * d:]
    n = s // CMP_STRIDE
    x = jnp.stack([kc, vc]).reshape(2, b, n, CMP_STRIDE * d)
    pos = cmp_pos.reshape(1, 2, CMP_STRIDE * d)
    w = jnp.stack([w_cmp_k, w_cmp_v]).reshape(2, 2, CMP_STRIDE * d, d).astype(BF16)
    cmp = nsa_compress(x, pos, w)
    return nsa_attention(t5_table, bias_tiles, q, gates, cmp[0], cmp[1],
                         _front_pad(ks.astype(BF16), NEAR_PAD), _front_pad(_with_ones(vs), NEAR_PAD),
                         _front_pad(kw.astype(BF16), WINDOW), _front_pad(_with_ones(vw), WINDOW))


PEER_HEADS = 8
PEER_DK = 256
N_KEYS = 128
PEER_TOPK = 16
PEER_SLOTS = PEER_HEADS * PEER_TOPK
ROUTE_TILE = 128
GATHER_TILE = 8


def _top_rows(work, order, n_top, payload=None):
    vals, args = [], []
    for _ in range(n_top):
        mx = jnp.max(work, axis=0, keepdims=True)
        first = jnp.min(jnp.where(work == mx, order, jnp.int32(2 ** 30)), axis=0, keepdims=True)
        pick = order == first
        vals.append(mx)
        if payload is None:
            args.append(first)
        else:
            args.append(jnp.sum(jnp.where(pick, payload, 0), axis=0, keepdims=True))
        work = jnp.where(pick, -jnp.inf, work)
    return jnp.concatenate(vals, axis=0), jnp.concatenate(args, axis=0)


def _pair_rows(s1, s2, combine):
    k = PEER_TOPK
    rows = [combine(s1[0:1], s2)]
    for a in range(1, k // 2):
        rows.append(combine(s1[a:a + 1], s2[0:k // 2]))
    rows.append(combine(s1[k // 2:], s2[0:1]))
    return jnp.concatenate(rows, axis=0)


def _peer_route_kernel(h_ref, g_ref, wq_ref, keys_ref, xn_ref, eidx_ref, gate_ref):
    x = h_ref[...]
    xn = x * lax.rsqrt(jnp.mean(x * x, axis=-1, keepdims=True) + NORM_EPS) * g_ref[...]
    xn_ref[...] = xn
    q = _dot(xn.astype(BF16), wq_ref[...]).astype(BF16)
    tq = x.shape[0]
    half = PEER_DK // 2
    n_iota = lax.broadcasted_iota(jnp.int32, (N_KEYS, tq), 0)
    k_iota = lax.broadcasted_iota(jnp.int32, (PEER_TOPK, tq), 0)
    order = _pair_rows(k_iota, k_iota, lambda a, b: a * PEER_TOPK + b)
    reachable = (order // PEER_TOPK + 1) * (order % PEER_TOPK + 1) <= PEER_TOPK
    for p in range(PEER_HEADS):
        tops = []
        for c in range(2):
            g = 2 * p + c
            sc = _dot_nt(keys_ref[g], q[:, g * half:(g + 1) * half])
            tops.append(_top_rows(sc, n_iota, PEER_TOPK))
        (s1, i1), (s2, i2) = tops
        cand = jnp.where(reachable, _pair_rows(s1, s2, lambda a, b: a + b), -jnp.inf)
        cidx = _pair_rows(i1, i2, lambda a, b: a * N_KEYS + b)
        top, eidx = _top_rows(cand, order, PEER_TOPK, payload=cidx)
        e = jnp.exp(top - top[0:1])
        eidx_ref[p * PEER_TOPK:(p + 1) * PEER_TOPK, :] = eidx
        gate_ref[p * PEER_TOPK:(p + 1) * PEER_TOPK, :] = e * (1.0 / jnp.sum(e, axis=0, keepdims=True))


def peer_route(h, g, w_q, keys):
    t, d = h.shape
    nq = w_q.shape[1]
    return pl.pallas_call(
        _peer_route_kernel,
        grid=(t // ROUTE_TILE,),
        in_specs=[pl.BlockSpec((ROUTE_TILE, d), lambda i: (i, 0)),
                  pl.BlockSpec((1, d), lambda i: (0, 0)),
                  pl.BlockSpec((d, nq), lambda i: (0, 0)),
                  pl.BlockSpec(keys.shape, lambda i: (0, 0, 0))],
        out_specs=[pl.BlockSpec((ROUTE_TILE, d), lambda i: (i, 0)),
                   pl.BlockSpec((PEER_SLOTS, ROUTE_TILE), lambda i: (0, i)),
                   pl.BlockSpec((PEER_SLOTS, ROUTE_TILE), lambda i: (0, i))],
        out_shape=[jax.ShapeDtypeStruct((t, d), F32),
                   jax.ShapeDtypeStruct((PEER_SLOTS, t), jnp.int32),
                   jax.ShapeDtypeStruct((PEER_SLOTS, t), F32)],
        compiler_params=pltpu.CompilerParams(dimension_semantics=("arbitrary",),
                                             vmem_limit_bytes=VMEM_LIMIT),
        name="peer_route",
    )(h, g.reshape(1, d), w_q, keys)


SUBLANES = 8
SLAB = 2 * SUBLANES
ISSUE_UNROLL = 8


def _gelu(x):
    return 0.5 * x * (1.0 + lax.erf(x * (2.0 ** -0.5)))


BITREV8 = (0, 4, 2, 6, 1, 5, 3, 7)


def _fold_pair(a, b, half, sub):
    low = (sub % (2 * half)) < half
    partner = jnp.where(low, pltpu.roll(a, SUBLANES - half, axis=0), pltpu.roll(b, half, axis=0))
    return jnp.where(low, a, b) + partner


GATHER_DEPTH = 3


def _peer_gather_kernel(idx0_ref, idx1_ref, idx2_ref, xn_ref, h_ref, gate_ref, uv_hbm, o_ref,
                        buf_a, buf_b, buf_c, w_scr, sem):
    i = pl.program_id(0)
    n = pl.num_programs(0)
    tt = xn_ref.shape[0]
    n_slabs = tt * PEER_SLOTS
    bufs = (buf_a, buf_b, buf_c)

    def request(ref, k, r):
        src = pl.multiple_of(ref[0, 0, r] * SLAB, SLAB)
        pltpu.make_async_copy(uv_hbm.at[pl.ds(src, SLAB)], bufs[k].at[pl.ds(r * SLAB, SLAB)], sem.at[k]).start()

    def request_tile(ref, k):
        def body(c, carry):
            for j in range(ISSUE_UNROLL):
                request(ref, k, pl.multiple_of(c * ISSUE_UNROLL, ISSUE_UNROLL) + j)
            return carry
        lax.fori_loop(0, n_slabs // ISSUE_UNROLL, body, 0)

    def wait_tile(k):
        pltpu.make_async_copy(uv_hbm.at[pl.ds(0, n_slabs * SLAB)], bufs[k], sem.at[k]).wait()

    @pl.when(i == 0)
    def _():
        request_tile(idx0_ref, 0)
        request_tile(idx1_ref, 1)

    def step(cur, ahead):
        wait_tile(cur)
        gates = gate_ref[0]
        sub = lax.broadcasted_iota(jnp.int32, (SUBLANES, LANE), 0)
        for t in range(tt):
            for j in range(PEER_SLOTS):
                request(idx2_ref, ahead, t * PEER_SLOTS + j)
            x = xn_ref[t]

            def slab(pos, half):
                return bufs[cur][pl.ds((t * PEER_SLOTS + pos) * SLAB + half * SUBLANES, SUBLANES), :]

            groups = []
            for g in range(PEER_SLOTS // SUBLANES):
                p = [slab(g * SUBLANES + k, 0) * x for k in range(SUBLANES)]
                c = [_fold_pair(p[2 * k], p[2 * k + 1], 4, sub) for k in range(4)]
                d = [_fold_pair(c[0], c[1], 2, sub), _fold_pair(c[2], c[3], 2, sub)]
                groups.append(_fold_pair(d[0], d[1], 1, sub))
            act = jnp.sum(jnp.concatenate(groups, axis=0), axis=-1, keepdims=True)
            w_scr[...] = jnp.broadcast_to(gates[:, t:t + 1] * _gelu(act), (PEER_SLOTS, LANE))
            acc = h_ref[t]
            for g in range(PEER_SLOTS // SUBLANES):
                for k in range(SUBLANES):
                    w_row = w_scr[pl.ds(g * SUBLANES + BITREV8[k], 1), :]
                    acc = acc + slab(g * SUBLANES + k, 1) * w_row
            o_ref[t] = acc

    for k in range(GATHER_DEPTH):
        @pl.when(i % GATHER_DEPTH == k)
        def _(k=k):
            step(k, (k + GATHER_DEPTH - 1) % GATHER_DEPTH)

    @pl.when(i == n - 1)
    def _():
        for k in range(GATHER_DEPTH):
            @pl.when(i % GATHER_DEPTH == k)
            def _(k=k):
                wait_tile((k + 1) % GATHER_DEPTH)
                wait_tile((k + 2) % GATHER_DEPTH)


def peer_gather(idx, xn, h, gate_b, uv):
    t = xn.shape[0]
    nt = t // GATHER_TILE
    n_slabs = GATHER_TILE * PEER_SLOTS
    tok = pl.BlockSpec((GATHER_TILE, SUBLANES, LANE), lambda i: (i, 0, 0))
    idx_spec = lambda k: pl.BlockSpec((1, 1, n_slabs), lambda i: (jnp.minimum(i + k, nt - 1), 0, 0),
                                      memory_space=pltpu.SMEM)
    slab_buf = pltpu.VMEM((n_slabs * SLAB, LANE), F32)
    return pl.pallas_call(
        _peer_gather_kernel,
        grid=(nt,),
        in_specs=[idx_spec(0), idx_spec(1), idx_spec(2), tok, tok,
                  pl.BlockSpec((1, PEER_SLOTS, GATHER_TILE), lambda i: (i, 0, 0)),
                  pl.BlockSpec(memory_space=pl.ANY)],
        out_specs=tok,
        out_shape=jax.ShapeDtypeStruct((t, SUBLANES, LANE), F32),
        scratch_shapes=[slab_buf, slab_buf, slab_buf, pltpu.VMEM((PEER_SLOTS, LANE), F32),
                        pltpu.SemaphoreType.DMA((GATHER_DEPTH,))],
        compiler_params=pltpu.CompilerParams(dimension_semantics=("arbitrary",),
                                             vmem_limit_bytes=VMEM_LIMIT),
        name="peer_gather",
    )(idx, idx, idx, xn, h, gate_b, uv)


def peer_ffn_residual(h, g, w_q, sub_keys, u_tab, v_tab):
    t, d = h.shape
    e = u_tab.shape[0]
    keys = sub_keys.reshape(PEER_HEADS * 2, N_KEYS, PEER_DK // 2).astype(BF16)
    xn, eidx_t, gate_t = peer_route(h, g, w_q.astype(BF16), keys)
    nt = t // GATHER_TILE
    order = np.arange(PEER_SLOTS).reshape(-1, SUBLANES)[:, list(BITREV8)].reshape(-1)
    idx = eidx_t[order].T.reshape(nt, 1, GATHER_TILE * PEER_SLOTS)
    gate_b = gate_t.reshape(PEER_SLOTS, nt, GATHER_TILE).transpose(1, 0, 2)
    uv = jnp.concatenate([u_tab.reshape(e, SUBLANES, LANE), v_tab.reshape(e, SUBLANES, LANE)],
                         axis=1).reshape(e * SLAB, LANE)
    out = peer_gather(idx, xn.reshape(t, SUBLANES, LANE), h.reshape(t, SUBLANES, LANE), gate_b, uv)
    return out.reshape(t, d)


Q_LORA = 256
KV_LORA = 128
QK_NOPE = 64
QK_ROPE = 32
V_HEAD = 64
ROPE_THETA = 10000.0
MLA_TQ = 256
MLA_TK = 512


def _rms(x, g):
    return x * lax.rsqrt(jnp.mean(x * x, axis=-1, keepdims=True) + NORM_EPS) * g


def _mla_prep_kernel(pos_ref, freq_ref, cq_ref, ckv_ref, kr_ref, gq_ref, gkv_ref, wq_ref, wk_ref, wv_ref,
                     q_out, k_out, v_out):
    rows = cq_ref.shape[0]
    half = QK_ROPE // 2
    cq = _rms(cq_ref[...], gq_ref[...]).astype(BF16)
    ckv = _rms(ckv_ref[...], gkv_ref[...]).astype(BF16)
    qf = _dot(cq, wq_ref[...])
    kf = _dot(ckv, wk_ref[...])
    vf = _dot(ckv, wv_ref[...])
    lane = lax.broadcasted_iota(jnp.int32, (rows, LANE), 1)
    ang = pos_ref[...].astype(F32) * freq_ref[...]
    cos, sin = jnp.cos(ang), jnp.sin(ang)
    in_rope = (lane >= QK_NOPE) & (lane < QK_NOPE + QK_ROPE)
    first = lane < QK_NOPE + half
    c_mul = jnp.where(lane < QK_NOPE, 1.0, jnp.where(in_rope, cos, 0.0))
    s_mul = jnp.where(in_rope, jnp.where(first, -sin, sin), 0.0)

    def rope(x):
        partner = jnp.where(first, pltpu.roll(x, LANE - half, axis=1), pltpu.roll(x, half, axis=1))
        return x * c_mul + partner * s_mul

    scale = (QK_NOPE + QK_ROPE) ** -0.5
    kr = rope(jnp.concatenate([jnp.zeros((rows, QK_NOPE), F32), kr_ref[...],
                               jnp.zeros((rows, LANE - QK_NOPE - QK_ROPE), F32)], axis=1))
    ones_col = jnp.where(lane == V_HEAD, 1.0, 0.0)
    for h in range(GROUP_HEADS):
        cols = slice(h * LANE, (h + 1) * LANE)
        q_out[:, cols] = (rope(qf[:, cols]) * scale).astype(BF16)
        k_out[:, cols] = (kf[:, cols] + kr).astype(BF16)
        v_out[:, cols] = (vf[:, cols] + ones_col).astype(BF16)


def mla_prep(pos, c_q, c_kv, k_r, gq, gkv, w_uq, w_ukv):
    t = c_q.shape[0]
    H = GROUP_HEADS
    wq = jnp.pad(w_uq.reshape(Q_LORA, H, QK_NOPE + QK_ROPE), ((0, 0), (0, 0), (0, LANE - QK_NOPE - QK_ROPE)))
    wkv = w_ukv.reshape(KV_LORA, H, QK_NOPE + V_HEAD)
    wk = jnp.pad(wkv[:, :, :QK_NOPE], ((0, 0), (0, 0), (0, LANE - QK_NOPE)))
    wv = jnp.pad(wkv[:, :, QK_NOPE:], ((0, 0), (0, 0), (0, LANE - V_HEAD)))
    wq, wk, wv = [w.reshape(w.shape[0], H * LANE).astype(BF16) for w in (wq, wk, wv)]
    inv_freq = ROPE_THETA ** (-np.arange(0, QK_ROPE, 2, dtype=np.float32) / QK_ROPE)
    freq = np.zeros((1, LANE), np.float32)
    freq[0, QK_NOPE:QK_NOPE + QK_ROPE] = np.tile(inv_freq, 2)
    row = lambda n: pl.BlockSpec((ROW_TILE, n), lambda i: (i, 0))
    fixed = lambda a: pl.BlockSpec(a.shape, lambda i: (0,) * a.ndim)
    gq2, gkv2, freq = gq.reshape(1, -1), gkv.reshape(1, -1), jnp.asarray(freq)
    out = jax.ShapeDtypeStruct((t, H * LANE), BF16)
    return pl.pallas_call(
        _mla_prep_kernel,
        grid=(t // ROW_TILE,),
        in_specs=[row(1), fixed(freq), row(Q_LORA), row(KV_LORA), row(QK_ROPE), fixed(gq2), fixed(gkv2),
                  fixed(wq), fixed(wk), fixed(wv)],
        out_specs=[row(H * LANE)] * 3,
        out_shape=[out] * 3,
        compiler_params=pltpu.CompilerParams(dimension_semantics=("arbitrary",)),
        name="mla_prep",
    )(pos, freq, c_q, c_kv, k_r, gq2, gkv2, wq, wk, wv)


def _mla_attn_kernel(q_ref, k_ref, v_ref, o_ref, m_scr, acc_scr):
    i = pl.program_id(1)
    tq = q_ref.shape[1]
    row = i * tq + lax.broadcasted_iota(jnp.int32, (tq, MLA_TK), 0)
    col = lax.broadcasted_iota(jnp.int32, (tq, MLA_TK), 1)
    n_full = (i * tq) // MLA_TK
    heads = range(GROUP_HEADS)
    cols = [slice(h * LANE, (h + 1) * LANE) for h in heads]
    q = [q_ref[0, :, cols[h]] for h in heads]
    m_scr[...] = jnp.full(m_scr.shape, NEG, F32)
    acc_scr[...] = jnp.zeros(acc_scr.shape, F32)

    def step(j, masked):
        start = pl.multiple_of(j * MLA_TK, MLA_TK)
        s = [_dot_nt(q[h], k_ref[0, pl.ds(start, MLA_TK), cols[h]]) for h in heads]
        if masked:
            s = [jnp.where(start + col <= row, sh, NEG) for sh in s]
        m_old = [m_scr[h] for h in heads]
        m_new = [jnp.maximum(m_old[h], jnp.max(s[h], axis=-1, keepdims=True)) for h in heads]
        p = [jnp.exp(s[h] - m_new[h]).astype(BF16) for h in heads]
        pv = [_dot(p[h], v_ref[0, pl.ds(start, MLA_TK), cols[h]]) for h in heads]
        for h in heads:
            acc_scr[h] = jnp.exp(m_old[h] - m_new[h]) * acc_scr[h] + pv[h]
            m_scr[h] = m_new[h]

    def body(j, carry):
        step(j, False)
        return carry

    lax.fori_loop(0, n_full, body, 0)
    step(n_full, True)
    outs = [acc_scr[h][:, :V_HEAD] * (1.0 / acc_scr[h][:, V_HEAD:V_HEAD + 1]) for h in heads]
    o_ref[0] = jnp.concatenate(outs, axis=1)


def mla_attention(q, k, v):
    b, s, w = q.shape
    return pl.pallas_call(
        _mla_attn_kernel,
        grid=(b, s // MLA_TQ),
        in_specs=[pl.BlockSpec((1, MLA_TQ, w), lambda bi, i: (bi, i, 0)),
                  pl.BlockSpec((1, s, w), lambda bi, i: (bi, 0, 0)),
                  pl.BlockSpec((1, s, w), lambda bi, i: (bi, 0, 0))],
        out_specs=pl.BlockSpec((1, MLA_TQ, GROUP_HEADS * V_HEAD), lambda bi, i: (bi, i, 0)),
        out_shape=jax.ShapeDtypeStruct((b, s, GROUP_HEADS * V_HEAD), F32),
        scratch_shapes=[pltpu.VMEM((GROUP_HEADS, MLA_TQ, 1), F32), pltpu.VMEM((GROUP_HEADS, MLA_TQ, LANE), F32)],
        compiler_params=pltpu.CompilerParams(dimension_semantics=("arbitrary", "arbitrary"),
                                             vmem_limit_bytes=VMEM_LIMIT),
        name="mla_attention",
    )(q, k, v)


def mla_group(h, positions, q_norm_g, kv_norm_g, w_uq, w_ukv):
    b, s, _ = h.shape
    t = b * s
    h2 = h.reshape(t, -1)
    c_q, c_kv, k_r = h2[:, :Q_LORA], h2[:, Q_LORA:Q_LORA + KV_LORA], h2[:, Q_LORA + KV_LORA:]
    q, k, v = mla_prep(positions.reshape(t, 1), c_q, c_kv, k_r, q_norm_g, kv_norm_g, w_uq, w_ukv)
    w = GROUP_HEADS * LANE
    return mla_attention(q.reshape(b, s, w), k.reshape(b, s, w), v.reshape(b, s, w))


CHUNK = 64
CONV_W = 4


def _split3(a):
    hi = a.astype(BF16)
    r = a - hi.astype(F32)
    mid = r.astype(BF16)
    return hi, mid, (r - mid.astype(F32)).astype(BF16)


def _dot_exact_rhs(sel, a):
    return sum(_dot(sel, p) for p in _split3(a))


def _row_matrix(col, n_rows):
    l = col.shape[0]
    lane = lax.broadcasted_iota(jnp.int32, (l, SUBLANES), 1)
    padded = jnp.where(lane == 0, col, 0.0)
    ones = jnp.ones((n_rows, SUBLANES), BF16)
    return sum(_dot_nt(ones, p) for p in _split3(padded))


def _transpose_bf16(x_bf):
    d = x_bf.shape[1]
    eye = (lax.broadcasted_iota(jnp.int32, (d, d), 0) == lax.broadcasted_iota(jnp.int32, (d, d), 1)).astype(BF16)
    return _dot_nt(eye, x_bf).astype(BF16)


def _log_sigmoid(x):
    return jnp.minimum(x, 0.0) - jnp.log1p(jnp.exp(-jnp.abs(x)))


def _softplus(x):
    return jnp.maximum(x, 0.0) + jnp.log1p(jnp.exp(-jnp.abs(x)))


def _with_ones_col(v):
    l, d = v.shape
    lane = lax.broadcasted_iota(jnp.int32, (l, LANE - d), 1)
    return jnp.concatenate([v, jnp.where(lane == 0, 1.0, 0.0)], axis=1)


def _bf(x):
    return x.astype(BF16)


def _mlstm_kernel(q_ref, k_ref, v_ref, gate_ref, og_ref, bias_ref, ng_ref, o_ref, st_scr, m_scr):
    nb = q_ref.shape[0]
    H, D, L = GROUP_HEADS, HEAD_DIM, CHUNK
    chains = [(b, h) for b in range(nb) for h in range(H)]
    cols = lambda h: slice(h * D, (h + 1) * D)
    each = lambda fn: [fn(c, b, h) for c, (b, h) in enumerate(chains)]

    @pl.when(pl.program_id(0) == 0)
    def _():
        st_scr[...] = jnp.zeros(st_scr.shape, F32)
        m_scr[...] = jnp.zeros(m_scr.shape, F32)

    row = lax.broadcasted_iota(jnp.int32, (L, L), 0)
    col = lax.broadcasted_iota(jnp.int32, (L, L), 1)
    tri = row >= col
    tri_bf = tri.astype(BF16)
    gates = [gate_ref[b] + bias_ref[...] for b in range(nb)]
    bcum = [_dot_exact_rhs(tri_bf, _log_sigmoid(g[:, H:])) for g in gates]
    q = each(lambda c, b, h: _bf(q_ref[b, :, cols(h)]))
    kt = each(lambda c, b, h: _transpose_bf16(_bf(k_ref[b, :, cols(h)] * (D ** -0.5))))
    v_aug = each(lambda c, b, h: _with_ones_col(v_ref[b, :, cols(h)]))
    bc = each(lambda c, b, h: bcum[b][:, h:h + 1])
    ic = each(lambda c, b, h: gates[b][:, h:h + 1])
    m_st = each(lambda c, b, h: m_scr[c, 0:1, 0:1])
    st = each(lambda c, b, h: st_scr[c])
    log_d = each(lambda c, b, h: jnp.where(tri, bc[c] + _row_matrix(ic[c] - bc[c], L), -jnp.inf))
    m_t = each(lambda c, b, h: jnp.maximum(jnp.max(log_d[c], axis=-1, keepdims=True), bc[c] + m_st[c]))
    sqk = each(lambda c, b, h: _dot(q[c], kt[c]) * jnp.exp(log_d[c] - m_t[c]))
    inter = each(lambda c, b, h: jnp.exp(bc[c] + m_st[c] - m_t[c]))
    tot = each(lambda c, b, h: _dot(_bf(sqk[c]), _bf(v_aug[c])) + inter[c] * _dot(q[c], _bf(st[c])))
    h_t = each(lambda c, b, h: tot[c][:, :D] * (1.0 / jnp.maximum(jnp.abs(tot[c][:, D:D + 1]), jnp.exp(-m_t[c]))))
    b_end = each(lambda c, b, h: bc[c][L - 1:L])
    log_w = each(lambda c, b, h: b_end[c] - bc[c] + ic[c])
    m_new = each(lambda c, b, h: jnp.maximum(b_end[c] + m_st[c], jnp.max(log_w[c], axis=0, keepdims=True)))
    upd = each(lambda c, b, h: _dot(kt[c], _bf(jnp.exp(log_w[c] - m_new[c]) * v_aug[c])))
    for c in range(len(chains)):
        st_scr[c] = jnp.exp(b_end[c] + m_st[c] - m_new[c]) * st[c] + upd[c]
        m_scr[c] = jnp.broadcast_to(m_new[c], m_scr.shape[1:])
    for b in range(nb):
        outs = []
        for h in range(H):
            o = jax.nn.sigmoid(og_ref[b, :, cols(h)]) * h_t[b * H + h]
            outs.append(o * lax.rsqrt(jnp.mean(o * o, axis=-1, keepdims=True) + NORM_EPS) * ng_ref[:, cols(h)])
        o_ref[b] = jnp.concatenate(outs, axis=1)


def mlstm_group(h, i_bias, f_bias, norm_g):
    b, s, _ = h.shape
    W, H = GROUP_W, GROUP_HEADS
    q, k, v = h[..., :W], h[..., W:2 * W], h[..., 2 * W:3 * W]
    gates = h[..., 3 * W:3 * W + 2 * H]
    og = h[..., 3 * W + 2 * H:]
    bias = jnp.concatenate([i_bias, f_bias]).reshape(1, 2 * H)
    tok = lambda n: pl.BlockSpec((b, CHUNK, n), lambda i: (0, i, 0))
    fixed = lambda n: pl.BlockSpec((1, n), lambda i: (0, 0))
    return pl.pallas_call(
        _mlstm_kernel,
        grid=(s // CHUNK,),
        in_specs=[tok(W), tok(W), tok(W), tok(2 * H), tok(W), fixed(2 * H), fixed(W)],
        out_specs=tok(W),
        out_shape=jax.ShapeDtypeStruct((b, s, W), F32),
        scratch_shapes=[pltpu.VMEM((b * H, HEAD_DIM, LANE), F32), pltpu.VMEM((b * H, SUBLANES, LANE), F32)],
        compiler_params=pltpu.CompilerParams(dimension_semantics=("arbitrary",)),
        name="mlstm",
    )(q, k, v, gates, og, bias, norm_g.reshape(1, W))


def _gdn_kernel(x_ref, xprev_ref, cw_ref, gate_ref, z_ref, par_ref, ng_ref, o_ref, st_scr):
    nb = x_ref.shape[0]
    H, D, L = GROUP_HEADS, HEAD_DIM, CHUNK
    W = H * D
    chains = [(b, h) for b in range(nb) for h in range(H)]
    each = lambda fn: [fn(c, b, h) for c, (b, h) in enumerate(chains)]
    first = pl.program_id(0) == 0

    @pl.when(first)
    def _():
        st_scr[...] = jnp.zeros(st_scr.shape, F32)

    row = lax.broadcasted_iota(jnp.int32, (L, L), 0)
    col = lax.broadcasted_iota(jnp.int32, (L, L), 1)
    tri = row >= col
    tri_bf = tri.astype(BF16)
    eye = (row == col).astype(F32)
    par = par_ref[...]
    qkv, beta4, gc4 = [], [], []
    for b in range(nb):
        prev = jnp.where(first, 0.0, xprev_ref[b])
        full = jnp.concatenate([prev, x_ref[b]], axis=0)
        acc = full[SUBLANES:] * cw_ref[CONV_W - 1:CONV_W, :]
        for j in range(CONV_W - 1):
            shifted = pltpu.roll(full, CONV_W - 1 - j, axis=0)[SUBLANES:]
            acc = acc + shifted * cw_ref[j:j + 1, :]
        qkv.append(acc * jax.nn.sigmoid(acc))
        gates = gate_ref[b]
        beta4.append(jax.nn.sigmoid(gates[:, :H]))
        gc4.append(_dot_exact_rhs(tri_bf, -jnp.exp(par[:, :H]) * _softplus(gates[:, H:] + par[:, H:])))

    def l2n(t):
        return t * lax.rsqrt(jnp.sum(t * t, axis=-1, keepdims=True) + NORM_EPS)

    q = each(lambda c, b, h: l2n(qkv[b][:, h * D:(h + 1) * D]) * (D ** -0.5))
    k = each(lambda c, b, h: l2n(qkv[b][:, W + h * D:W + (h + 1) * D]))
    v = each(lambda c, b, h: qkv[b][:, 2 * W + h * D:2 * W + (h + 1) * D])
    beta = each(lambda c, b, h: beta4[b][:, h:h + 1])
    gc = each(lambda c, b, h: gc4[b][:, h:h + 1])
    gc_row = each(lambda c, b, h: _row_matrix(gc[c], L))
    decay = each(lambda c, b, h: jnp.where(tri, jnp.exp(gc[c] - gc_row[c]), 0.0))
    kt = each(lambda c, b, h: _transpose_bf16(_bf(k[c])))
    kb = each(lambda c, b, h: k[c] * beta[c])
    x = each(lambda c, b, h: -jnp.where(row > col, _dot(_bf(kb[c]), kt[c]) * decay[c], 0.0))
    t_inv = [eye + xc for xc in x]
    for _ in range(5):
        x = [_dot(_bf(xc), _bf(xc)) for xc in x]
        t_inv = [tc + _dot(_bf(tc), _bf(xc)) for tc, xc in zip(t_inv, x)]
    u = each(lambda c, b, h: _dot(_bf(t_inv[c]), _bf(v[c] * beta[c])))
    wm = each(lambda c, b, h: _dot(_bf(t_inv[c]), _bf(kb[c] * jnp.exp(gc[c]))))
    attn = each(lambda c, b, h: _dot(_bf(q[c]), kt[c]) * decay[c])
    st = each(lambda c, b, h: st_scr[c])
    v_new = each(lambda c, b, h: u[c] - _dot(_bf(wm[c]), _bf(st[c])))
    o = each(lambda c, b, h: _dot(_bf(q[c] * jnp.exp(gc[c])), _bf(st[c])) + _dot(_bf(attn[c]), _bf(v_new[c])))
    g_end = each(lambda c, b, h: gc[c][L - 1:L])
    upd = each(lambda c, b, h: _dot(_bf(kt[c].astype(F32) * jnp.exp(g_end[c] - gc_row[c])), _bf(v_new[c])))
    for c in range(len(chains)):
        st_scr[c] = st[c] * jnp.exp(g_end[c]) + upd[c]
    for b in range(nb):
        outs = []
        for h in range(H):
            oc = o[b * H + h]
            z = z_ref[b, :, h * D:(h + 1) * D]
            y = oc * lax.rsqrt(jnp.mean(oc * oc, axis=-1, keepdims=True) + NORM_EPS) * ng_ref[...]
            outs.append(y * (z * jax.nn.sigmoid(z)))
        o_ref[b] = jnp.concatenate(outs, axis=1)


def gdn_group(h, conv_w, a_log, dt_bias, norm_g):
    b, s, _ = h.shape
    W, H = GROUP_W, GROUP_HEADS
    x = h[..., :3 * W]
    gates = h[..., 3 * W:3 * W + 2 * H]
    z = h[..., 3 * W + 2 * H:]
    par = jnp.concatenate([a_log, dt_bias]).reshape(1, 2 * H)
    tok = lambda n: pl.BlockSpec((b, CHUNK, n), lambda i: (0, i, 0))
    per_chunk = CHUNK // SUBLANES
    prev = pl.BlockSpec((b, SUBLANES, 3 * W), lambda i: (0, jnp.maximum(i * per_chunk - 1, 0), 0))
    fixed = lambda r, n: pl.BlockSpec((r, n), lambda i: (0, 0))
    return pl.pallas_call(
        _gdn_kernel,
        grid=(s // CHUNK,),
        in_specs=[tok(3 * W), prev, fixed(CONV_W, 3 * W), tok(2 * H), tok(W), fixed(1, 2 * H), fixed(1, HEAD_DIM)],
        out_specs=tok(W),
        out_shape=jax.ShapeDtypeStruct((b, s, W), F32),
        scratch_shapes=[pltpu.VMEM((b * H, HEAD_DIM, HEAD_DIM), F32)],
        compiler_params=pltpu.CompilerParams(dimension_semantics=("arbitrary",)),
        name="gdn",
    )(x, x, conv_w, gates, z, par, norm_g.reshape(1, HEAD_DIM))


NSA_IN = GROUP_W + 6 * HEAD_DIM + 3 * GROUP_HEADS
MLSTM_IN = 4 * GROUP_W + 2 * GROUP_HEADS
GDN_IN = 4 * GROUP_W + 2 * GROUP_HEADS
MLA_IN = Q_LORA + KV_LORA + QK_ROPE


def kernel(x, positions, t5_table, norm1_g, norm2_g, final_norm_g, w_in, w_out, nsa_w_cmp_k, nsa_w_cmp_v, nsa_cmp_pos, mlstm_i_bias, mlstm_f_bias, mlstm_norm_g, gdn_conv_w, gdn_a_log, gdn_dt_bias, gdn_norm_g, mla_q_norm_g, mla_kv_norm_g, mla_w_uq, mla_w_ukv, peer_w_q, peer_sub_keys, peer_u, peer_v):
    b, s, d = x.shape
    t = b * s
    h = x.reshape(t, d)
    depth = w_in.shape[0]
    d_in = w_in.shape[2]
    d_in_pad = _round_up(d_in, LANE)
    bias_tiles = nsa_bias_tiles(t5_table)
    for l in range(depth):
        w_in_l = jnp.pad(w_in[l], ((0, 0), (0, d_in_pad - d_in))).astype(BF16)
        hp = norm_matmul(h, norm1_g[l], w_in_l)[:, :d_in].reshape(b, s, d_in)
        c0, c1, c2 = NSA_IN, NSA_IN + MLSTM_IN, NSA_IN + MLSTM_IN + GDN_IN
        h_nsa, h_ml, h_gdn, h_mla = hp[..., :c0], hp[..., c0:c1], hp[..., c1:c2], hp[..., c2:]
        o = jnp.concatenate([
            nsa_group(h_nsa, nsa_w_cmp_k[l], nsa_w_cmp_v[l], nsa_cmp_pos[l], t5_table, bias_tiles),
            mlstm_group(h_ml, mlstm_i_bias[l], mlstm_f_bias[l], mlstm_norm_g[l]),
            gdn_group(h_gdn, gdn_conv_w[l], gdn_a_log[l], gdn_dt_bias[l], gdn_norm_g[l]),
            mla_group(h_mla, positions, mla_q_norm_g[l], mla_kv_norm_g[l], mla_w_uq[l], mla_w_ukv[l]),
        ], axis=-1)
        h = matmul_residual(o.reshape(t, -1), w_out[l].astype(BF16), h)
        h = peer_ffn_residual(h, norm2_g[l], peer_w_q[l], peer_sub_keys[l], peer_u[l], peer_v[l])
    return rmsnorm(h, final_norm_g).reshape(b, s, d)
```

```python
import functools
import math

import jax
import jax.numpy as jnp
import numpy as np
from jax import lax
from jax.experimental import pallas as pl
from jax.experimental.pallas import tpu as pltpu

F32 = jnp.float32
BF16 = jnp.bfloat16
NORM_EPS = 1e-6
NEG = -1e30
LANE = 128
ROW_TILE = 512
VMEM_LIMIT = 56 * 1024 * 1024

HEAD_DIM = 64
GROUP_HEADS = 4
GROUP_W = GROUP_HEADS * HEAD_DIM
Q_BLOCK = 128
CMP_LEN = 32
CMP_STRIDE = 16
SEL_LEN = 64
SEL_TOPN = 16
WINDOW = 512
SEL_FORCE = 1e3
N_BUCKETS = 32
MAX_DISTANCE = 128


def _round_up(n, m):
    return (n + m - 1) // m * m


def _dot_nt(a, b):
    return lax.dot_general(a, b, (((1,), (1,)), ((), ())), preferred_element_type=F32)


def _dot(a, b):
    return jnp.dot(a, b, preferred_element_type=F32)


def _norm_matmul_kernel(x_ref, g_ref, w_ref, o_ref):
    x = x_ref[...]
    y = x * lax.rsqrt(jnp.mean(x * x, axis=-1, keepdims=True) + NORM_EPS) * g_ref[...]
    o_ref[...] = _dot(y.astype(BF16), w_ref[...])


def norm_matmul(x, g, w):
    t, d = x.shape
    n = w.shape[1]
    return pl.pallas_call(
        _norm_matmul_kernel,
        grid=(t // ROW_TILE,),
        in_specs=[pl.BlockSpec((ROW_TILE, d), lambda i: (i, 0)),
                  pl.BlockSpec((1, d), lambda i: (0, 0)),
                  pl.BlockSpec((d, n), lambda i: (0, 0))],
        out_specs=pl.BlockSpec((ROW_TILE, n), lambda i: (i, 0)),
        out_shape=jax.ShapeDtypeStruct((t, n), F32),
        compiler_params=pltpu.CompilerParams(dimension_semantics=("arbitrary",),
                                             vmem_limit_bytes=VMEM_LIMIT),
        name="norm_matmul",
    )(x, g.reshape(1, d), w)


def _matmul_residual_kernel(o_ref, w_ref, r_ref, out_ref):
    out_ref[...] = r_ref[...] + _dot(o_ref[...].astype(BF16), w_ref[...])


def matmul_residual(o, w, res):
    t, d = o.shape
    n = w.shape[1]
    return pl.pallas_call(
        _matmul_residual_kernel,
        grid=(t // ROW_TILE,),
        in_specs=[pl.BlockSpec((ROW_TILE, d), lambda i: (i, 0)),
                  pl.BlockSpec((d, n), lambda i: (0, 0)),
                  pl.BlockSpec((ROW_TILE, n), lambda i: (i, 0))],
        out_specs=pl.BlockSpec((ROW_TILE, n), lambda i: (i, 0)),
        out_shape=jax.ShapeDtypeStruct((t, n), F32),
        compiler_params=pltpu.CompilerParams(dimension_semantics=("arbitrary",),
                                             vmem_limit_bytes=VMEM_LIMIT),
        name="matmul_residual",
    )(o, w, res)


def _rmsnorm_kernel(x_ref, g_ref, o_ref):
    x = x_ref[...]
    o_ref[...] = x * lax.rsqrt(jnp.mean(x * x, axis=-1, keepdims=True) + NORM_EPS) * g_ref[...]


def rmsnorm(x, g):
    t, d = x.shape
    return pl.pallas_call(
        _rmsnorm_kernel,
        grid=(t // ROW_TILE,),
        in_specs=[pl.BlockSpec((ROW_TILE, d), lambda i: (i, 0)),
                  pl.BlockSpec((1, d), lambda i: (0, 0))],
        out_specs=pl.BlockSpec((ROW_TILE, d), lambda i: (i, 0)),
        out_shape=jax.ShapeDtypeStruct((t, d), F32),
        compiler_params=pltpu.CompilerParams(dimension_semantics=("arbitrary",)),
        name="rmsnorm",
    )(x, g.reshape(1, d))


def _t5_bucket_starts():
    d = np.arange(0, 2 * MAX_DISTANCE)
    max_exact = N_BUCKETS // 2
    val = (np.log(np.maximum(d, 1).astype(np.float32) / np.float32(max_exact))
           / np.float32(math.log(MAX_DISTANCE / max_exact)) * np.float32(N_BUCKETS - max_exact))
    large = np.minimum(max_exact + val.astype(np.int32), N_BUCKETS - 1)
    bucket = np.where(d < max_exact, d, large)
    assert np.all(np.diff(bucket) >= 0)
    return [(int(k), int(d[bucket == k].min())) for k in range(N_BUCKETS) if np.any(bucket == k)]


T5_STARTS = _t5_bucket_starts()


def _t5_bias_heads(dist, tab_ref):
    accs = [jnp.full(dist.shape, tab_ref[0, h], F32) for h in range(GROUP_HEADS)]
    for bucket, start in T5_STARTS[1:]:
        m = dist >= start
        accs = [jnp.where(m, tab_ref[bucket, h], accs[h]) for h in range(GROUP_HEADS)]
    return jnp.stack(accs)


def _nsa_compress_kernel(x_ref, pos_ref, w_ref, o_ref):
    x = x_ref[0, 0]
    n = x.shape[0]
    lo = _dot((x + pos_ref[0, 0:1, :]).astype(BF16), w_ref[0, 0])
    hi = _dot((x + pos_ref[0, 1:2, :]).astype(BF16), w_ref[0, 1])
    o_ref[0, 0] = (lo + pltpu.roll(hi, n - 1, axis=0)).astype(o_ref.dtype)


def nsa_compress(x, pos, w):
    two, b, n, c = x.shape
    return pl.pallas_call(
        _nsa_compress_kernel,
        grid=(two, b),
        in_specs=[pl.BlockSpec((1, 1, n, c), lambda k, i: (k, i, 0, 0)),
                  pl.BlockSpec((1, 2, c), lambda k, i: (0, 0, 0)),
                  pl.BlockSpec((1, 2, c, HEAD_DIM), lambda k, i: (k, 0, 0, 0))],
        out_specs=pl.BlockSpec((1, 1, n, HEAD_DIM), lambda k, i: (k, i, 0, 0)),
        out_shape=jax.ShapeDtypeStruct((two, b, n, HEAD_DIM), BF16),
        compiler_params=pltpu.CompilerParams(dimension_semantics=("arbitrary", "arbitrary")),
        name="nsa_compress",
    )(x, pos, w)


FAR_TILE = 512
NEAR_KEYS = 768
WIN_KEYS = WINDOW + Q_BLOCK
NEAR_PAD = NEAR_KEYS - Q_BLOCK
CMP_TILE = LANE
CMP_TILES_BACK = CMP_TILE * CMP_STRIDE // Q_BLOCK
N_CMP_BIAS = CMP_TILES_BACK + 3


def _masked_bias(dist, valid, tab_ref):
    return jnp.where(valid, _t5_bias_heads(dist, tab_ref), NEG)


def _nsa_bias_kernel(tab_ref, near_ref, win_ref):
    q = lax.broadcasted_iota(jnp.int32, (Q_BLOCK, NEAR_KEYS), 0)
    k = lax.broadcasted_iota(jnp.int32, (Q_BLOCK, NEAR_KEYS), 1)
    dist = NEAR_PAD + q - k
    near_ref[...] = _masked_bias(dist, dist >= 0, tab_ref)
    q = lax.broadcasted_iota(jnp.int32, (Q_BLOCK, WIN_KEYS), 0)
    k = lax.broadcasted_iota(jnp.int32, (Q_BLOCK, WIN_KEYS), 1)
    dist = WINDOW + q - k
    win_ref[...] = _masked_bias(dist, (dist >= 0) & (dist < WINDOW), tab_ref)


def _nsa_cmp_bias_kernel(tab_ref, o_ref):
    e = pl.program_id(0)
    q = lax.broadcasted_iota(jnp.int32, (Q_BLOCK, CMP_TILE), 0)
    n = lax.broadcasted_iota(jnp.int32, (Q_BLOCK, CMP_TILE), 1)
    dist = (e - 1) * Q_BLOCK + q - (n * CMP_STRIDE + CMP_LEN - 1)
    o_ref[0] = _masked_bias(dist, dist >= 0, tab_ref)


def nsa_bias_tiles(t5_table):
    H, Q = GROUP_HEADS, Q_BLOCK
    smem = pl.BlockSpec(memory_space=pltpu.SMEM)
    near, win = pl.pallas_call(
        _nsa_bias_kernel, grid=(1,), in_specs=[smem],
        out_specs=[pl.BlockSpec((H, Q, NEAR_KEYS), lambda i: (0, 0, 0)),
                   pl.BlockSpec((H, Q, WIN_KEYS), lambda i: (0, 0, 0))],
        out_shape=[jax.ShapeDtypeStruct((H, Q, NEAR_KEYS), F32), jax.ShapeDtypeStruct((H, Q, WIN_KEYS), F32)],
        name="nsa_bias",
    )(t5_table)
    cmp = pl.pallas_call(
        _nsa_cmp_bias_kernel, grid=(N_CMP_BIAS,), in_specs=[smem],
        out_specs=pl.BlockSpec((1, H, Q, CMP_TILE), lambda e: (e, 0, 0, 0)),
        out_shape=jax.ShapeDtypeStruct((N_CMP_BIAS, H, Q, CMP_TILE), F32),
        name="nsa_cmp_bias",
    )(t5_table)
    return near, win, cmp


def _nsa_attn_kernel(tab_ref, q_ref, g_ref, kc_ref, vc_ref, ks_ref, vs_ref, kw_ref, vw_ref,
                     bnear_ref, bwin_ref, bcmp_ref, o_ref, m_scr, acc_scr):
    H, Q, D = GROUP_HEADS, Q_BLOCK, HEAD_DIM
    n_cmp = kc_ref.shape[1]
    n_sel = (ks_ref.shape[1] - NEAR_PAD) // SEL_LEN
    i = pl.program_id(1)
    t0 = i * Q

    q = q_ref[0] * (D ** -0.5)
    q4 = jnp.concatenate([q[:, h * D:(h + 1) * D] for h in range(H)], axis=0).astype(BF16)

    s = _dot_nt(q4, kc_ref[0]).reshape(H, Q, n_cmp)
    tiles = []
    for g in range(n_cmp // CMP_TILE):
        e = jnp.clip(i - g * CMP_TILES_BACK, -1, CMP_TILES_BACK + 1) + 1
        tiles.append(s[:, :, g * CMP_TILE:(g + 1) * CMP_TILE] + bcmp_ref[e])
    s = jnp.concatenate(tiles, axis=-1)
    t_c = t0 + lax.broadcasted_iota(jnp.int32, (Q, n_cmp), 0)
    valid_c = t_c >= lax.broadcasted_iota(jnp.int32, (Q, n_cmp), 1) * CMP_STRIDE + CMP_LEN - 1
    p = jnp.where(valid_c, jnp.exp(s - jnp.max(s, axis=-1, keepdims=True)), 0.0)
    l = jnp.sum(p, axis=-1, keepdims=True)
    p = p * (1.0 / jnp.where(l > 0.0, l, 1.0))
    o_c = _dot(p.reshape(H * Q, n_cmp).astype(BF16), vc_ref[0])

    psum = p[0] + p[1] + p[2] + p[3]
    c_idx = lax.broadcasted_iota(jnp.int32, (n_cmp, n_sel), 0)
    j_idx = lax.broadcasted_iota(jnp.int32, (n_cmp, n_sel), 1)
    ratio = SEL_LEN // CMP_STRIDE
    n_in_sel = (SEL_LEN - CMP_LEN) // CMP_STRIDE + 1
    pool = ((c_idx // ratio == j_idx) & (c_idx % ratio < n_in_sel)).astype(BF16)
    p_hi = psum.astype(BF16)
    p_mid = (psum - p_hi.astype(F32)).astype(BF16)
    p_lo = (psum - p_hi.astype(F32) - p_mid.astype(F32)).astype(BF16)
    imp = _dot(p_hi, pool) + _dot(p_mid, pool) + _dot(p_lo, pool)

    jb = lax.broadcasted_iota(jnp.int32, (Q, n_sel), 1)
    cur = (t0 + lax.broadcasted_iota(jnp.int32, (Q, n_sel), 0)) // SEL_LEN
    forced = (jb == 0) | (jb == cur) | (jb == cur - 1)
    work = jnp.where(jb <= cur, imp + jnp.where(forced, SEL_FORCE, 0.0), NEG)
    sel = jnp.zeros((Q, n_sel), F32)
    for _ in range(min(SEL_TOPN, n_sel)):
        mx = jnp.max(work, axis=-1, keepdims=True)
        first = jnp.min(jnp.where(work == mx, jb, n_sel), axis=-1, keepdims=True)
        pick = jb == first
        sel = jnp.where(pick & (mx > 0.5 * NEG), 1.0, sel)
        work = jnp.where(pick, -jnp.inf, work)
    sel_bf = sel.astype(BF16)

    far_bias = jnp.stack([jnp.full((Q, 1), tab_ref[N_BUCKETS - 1, h], F32) for h in range(H)])
    m_scr[...] = jnp.full(m_scr.shape, NEG, F32)
    acc_scr[...] = jnp.zeros(acc_scr.shape, F32)

    def sel_step(row0, n_keys, bias, first_key):
        k = ks_ref[0, pl.ds(row0, n_keys), :]
        v = vs_ref[0, pl.ds(row0, n_keys), :]
        s = _dot_nt(q4, k).reshape(H, Q, n_keys) + bias
        pos = row0 - NEAR_PAD + lax.broadcasted_iota(jnp.int32, (n_sel, n_keys), 1)
        blk = jnp.where(pos >= first_key, pos // SEL_LEN, -1)
        expand = (lax.broadcasted_iota(jnp.int32, (n_sel, n_keys), 0) == blk).astype(BF16)
        mask = _dot(sel_bf, expand) > 0.5
        s = jnp.where(mask, s, NEG)
        m_old = m_scr[...].reshape(H, Q, 1)
        m_new = jnp.maximum(m_old, jnp.max(s, axis=-1, keepdims=True))
        p = jnp.where(mask, jnp.exp(s - m_new), 0.0).reshape(H * Q, n_keys)
        alpha = jnp.exp(m_old - m_new).reshape(H * Q, 1)
        acc_scr[...] = alpha * acc_scr[...] + _dot(p.astype(BF16), v)
        m_scr[...] = m_new.reshape(H * Q, 1)

    n_far = jnp.maximum(i - 1, 0) // (FAR_TILE // Q)

    def far_body(kt, carry):
        sel_step(pl.multiple_of(NEAR_PAD + kt * FAR_TILE, Q), FAR_TILE, far_bias, 0)
        return carry

    lax.fori_loop(0, n_far, far_body, 0)
    row0 = pl.multiple_of(t0, Q)
    sel_step(row0, NEAR_KEYS, bnear_ref[...], n_far * FAR_TILE)
    acc = acc_scr[...]
    o_s = acc[:, :D] * (1.0 / acc[:, D:D + 1])

    kw = kw_ref[0, pl.ds(row0, WIN_KEYS), :]
    vw = vw_ref[0, pl.ds(row0, WIN_KEYS), :]
    in_seq = lax.broadcasted_iota(jnp.int32, (1, WIN_KEYS), 1) >= WINDOW - t0
    s = _dot_nt(q4, kw).reshape(H, Q, WIN_KEYS) + bwin_ref[...] + jnp.where(in_seq, 0.0, NEG)
    p = jnp.exp(s - jnp.max(s, axis=-1, keepdims=True))
    acc_w = _dot(p.reshape(H * Q, WIN_KEYS).astype(BF16), vw)
    o_w = acc_w[:, :D] * (1.0 / acc_w[:, D:D + 1])

    g = jax.nn.sigmoid(g_ref[0])
    outs = []
    for h in range(H):
        rows = slice(h * Q, (h + 1) * Q)
        outs.append(g[:, h:h + 1] * o_c[rows] + g[:, H + h:H + h + 1] * o_s[rows]
                    + g[:, 2 * H + h:2 * H + h + 1] * o_w[rows])
    o_ref[0] = jnp.concatenate(outs, axis=1)


def nsa_attention(t5_table, bias_tiles, q, gates, k_cmp, v_cmp, ks, vs_aug, kw, vw_aug):
    b, s, _ = q.shape
    bnear, bwin, bcmp = bias_tiles
    full = lambda a: pl.BlockSpec((1,) + a.shape[1:], lambda bi, i: (bi,) + (0,) * (a.ndim - 1))
    fixed = lambda a: pl.BlockSpec(a.shape, lambda bi, i: (0,) * a.ndim)
    return pl.pallas_call(
        _nsa_attn_kernel,
        grid=(b, s // Q_BLOCK),
        in_specs=[pl.BlockSpec(memory_space=pltpu.SMEM),
                  pl.BlockSpec((1, Q_BLOCK, GROUP_W), lambda bi, i: (bi, i, 0)),
                  pl.BlockSpec((1, Q_BLOCK, 3 * GROUP_HEADS), lambda bi, i: (bi, i, 0)),
                  full(k_cmp), full(v_cmp), full(ks), full(vs_aug), full(kw), full(vw_aug),
                  fixed(bnear), fixed(bwin), fixed(bcmp)],
        out_specs=pl.BlockSpec((1, Q_BLOCK, GROUP_W), lambda bi, i: (bi, i, 0)),
        out_shape=jax.ShapeDtypeStruct((b, s, GROUP_W), F32),
        scratch_shapes=[pltpu.VMEM((GROUP_HEADS * Q_BLOCK, 1), F32),
                        pltpu.VMEM((GROUP_HEADS * Q_BLOCK, 2 * HEAD_DIM), F32)],
        compiler_params=pltpu.CompilerParams(dimension_semantics=("arbitrary", "arbitrary"),
                                             vmem_limit_bytes=VMEM_LIMIT),
        name="nsa_attention",
    )(t5_table, q, gates, k_cmp, v_cmp, ks, vs_aug, kw, vw_aug, bnear, bwin, bcmp)


def _front_pad(x, rows):
    return jnp.pad(x, ((0, 0), (rows, 0), (0, 0)))


def _with_ones(v):
    b, s, d = v.shape
    pad = jnp.zeros((b, s, d), BF16).at[:, :, 0].set(1.0)
    return jnp.concatenate([v.astype(BF16), pad], axis=-1)


def nsa_group(h, w_cmp_k, w_cmp_v, cmp_pos, t5_table, bias_tiles=None):
    if bias_tiles is None:
        bias_tiles = nsa_bias_tiles(t5_table)
    b, s, _ = h.shape
    d = HEAD_DIM
    q = h[..., :GROUP_W]
    kc, vc, ks, vs, kw, vw = [h[..., GROUP_W + j * d: GROUP_W + (j + 1) * d] for j in range(6)]
    gates = h[..., GROUP_W + 6 * d:]
    n = s // CMP_STRIDE
    x = jnp.stack([kc, vc]).reshape(2, b, n, CMP_STRIDE * d)
    pos = cmp_pos.reshape(1, 2, CMP_STRIDE * d)
    w = jnp.stack([w_cmp_k, w_cmp_v]).reshape(2, 2, CMP_STRIDE * d, d).astype(BF16)
    cmp = nsa_compress(x, pos, w)
    return nsa_attention(t5_table, bias_tiles, q, gates, cmp[0], cmp[1],
                         _front_pad(ks.astype(BF16), NEAR_PAD), _front_pad(_with_ones(vs), NEAR_PAD),
                         _front_pad(kw.astype(BF16), WINDOW), _front_pad(_with_ones(vw), WINDOW))


PEER_HEADS = 8
PEER_DK = 256
N_KEYS = 128
PEER_TOPK = 16
PEER_SLOTS = PEER_HEADS * PEER_TOPK
ROUTE_TILE = 128
GATHER_TILE = 8


def _top_rows(work, order, n_top, payload=None):
    vals, args = [], []
    for _ in range(n_top):
        mx = jnp.max(work, axis=0, keepdims=True)
        first = jnp.min(jnp.where(work == mx, order, jnp.int32(2 ** 30)), axis=0, keepdims=True)
        pick = order == first
        vals.append(mx)
        if payload is None:
            args.append(first)
        else:
            args.append(jnp.sum(jnp.where(pick, payload, 0), axis=0, keepdims=True))
        work = jnp.where(pick, -jnp.inf, work)
    return jnp.concatenate(vals, axis=0), jnp.concatenate(args, axis=0)


def _pair_rows(s1, s2, combine):
    k = PEER_TOPK
    rows = [combine(s1[0:1], s2)]
    for a in range(1, k // 2):
        rows.append(combine(s1[a:a + 1], s2[0:k // 2]))
    rows.append(combine(s1[k // 2:], s2[0:1]))
    return jnp.concatenate(rows, axis=0)


def _peer_route_kernel(h_ref, g_ref, wq_ref, keys_ref, xn_ref, eidx_ref, gate_ref):
    x = h_ref[...]
    xn = x * lax.rsqrt(jnp.mean(x * x, axis=-1, keepdims=True) + NORM_EPS) * g_ref[...]
    xn_ref[...] = xn
    q = _dot(xn.astype(BF16), wq_ref[...]).astype(BF16)
    tq = x.shape[0]
    half = PEER_DK // 2
    n_iota = lax.broadcasted_iota(jnp.int32, (N_KEYS, tq), 0)
    k_iota = lax.broadcasted_iota(jnp.int32, (PEER_TOPK, tq), 0)
    order = _pair_rows(k_iota, k_iota, lambda a, b: a * PEER_TOPK + b)
    reachable = (order // PEER_TOPK + 1) * (order % PEER_TOPK + 1) <= PEER_TOPK
    for p in range(PEER_HEADS):
        tops = []
        for c in range(2):
            g = 2 * p + c
            sc = _dot_nt(keys_ref[g], q[:, g * half:(g + 1) * half])
            tops.append(_top_rows(sc, n_iota, PEER_TOPK))
        (s1, i1), (s2, i2) = tops
        cand = jnp.where(reachable, _pair_rows(s1, s2, lambda a, b: a + b), -jnp.inf)
        cidx = _pair_rows(i1, i2, lambda a, b: a * N_KEYS + b)
        top, eidx = _top_rows(cand, order, PEER_TOPK, payload=cidx)
        e = jnp.exp(top - top[0:1])
        eidx_ref[p * PEER_TOPK:(p + 1) * PEER_TOPK, :] = eidx
        gate_ref[p * PEER_TOPK:(p + 1) * PEER_TOPK, :] = e * (1.0 / jnp.sum(e, axis=0, keepdims=True))


def peer_route(h, g, w_q, keys):
    t, d = h.shape
    nq = w_q.shape[1]
    return pl.pallas_call(
        _peer_route_kernel,
        grid=(t // ROUTE_TILE,),
        in_specs=[pl.BlockSpec((ROUTE_TILE, d), lambda i: (i, 0)),
                  pl.BlockSpec((1, d), lambda i: (0, 0)),
                  pl.BlockSpec((d, nq), lambda i: (0, 0)),
                  pl.BlockSpec(keys.shape, lambda i: (0, 0, 0))],
        out_specs=[pl.BlockSpec((ROUTE_TILE, d), lambda i: (i, 0)),
                   pl.BlockSpec((PEER_SLOTS, ROUTE_TILE), lambda i: (0, i)),
                   pl.BlockSpec((PEER_SLOTS, ROUTE_TILE), lambda i: (0, i))],
        out_shape=[jax.ShapeDtypeStruct((t, d), F32),
                   jax.ShapeDtypeStruct((PEER_SLOTS, t), jnp.int32),
                   jax.ShapeDtypeStruct((PEER_SLOTS, t), F32)],
        compiler_params=pltpu.CompilerParams(dimension_semantics=("arbitrary",),
                                             vmem_limit_bytes=VMEM_LIMIT),
        name="peer_route",
    )(h, g.reshape(1, d), w_q, keys)


SUBLANES = 8
SLAB = 2 * SUBLANES
ISSUE_UNROLL = 8


def _gelu(x):
    return 0.5 * x * (1.0 + lax.erf(x * (2.0 ** -0.5)))


BITREV8 = (0, 4, 2, 6, 1, 5, 3, 7)


def _fold_pair(a, b, half, sub):
    low = (sub % (2 * half)) < half
    partner = jnp.where(low, pltpu.roll(a, SUBLANES - half, axis=0), pltpu.roll(b, half, axis=0))
    return jnp.where(low, a, b) + partner


GATHER_DEPTH = 3
DMA_QUEUES = 2


def _peer_gather_kernel(idx0_ref, idx1_ref, idx2_ref, xn_ref, h_ref, gate_ref, uv_hbm, o_ref,
                        buf_a, buf_b, buf_c, w_scr, sem):
    i = pl.program_id(0)
    n = pl.num_programs(0)
    tt = xn_ref.shape[0]
    n_slabs = tt * PEER_SLOTS
    bufs = (buf_a, buf_b, buf_c)

    def request(ref, k, r, queue):
        src = pl.multiple_of(ref[0, 0, r] * SLAB, SLAB)
        pltpu.make_async_copy(uv_hbm.at[pl.ds(src, SLAB)], bufs[k].at[pl.ds(r * SLAB, SLAB)],
                              sem.at[k]).start(priority=queue)

    def request_tile(ref, k):
        def body(c, carry):
            for j in range(ISSUE_UNROLL):
                request(ref, k, pl.multiple_of(c * ISSUE_UNROLL, ISSUE_UNROLL) + j, j % DMA_QUEUES)
            return carry
        lax.fori_loop(0, n_slabs // ISSUE_UNROLL, body, 0)

    def wait_tile(k):
        pltpu.make_async_copy(uv_hbm.at[pl.ds(0, n_slabs * SLAB)], bufs[k], sem.at[k]).wait()

    @pl.when(i == 0)
    def _():
        request_tile(idx0_ref, 0)
        request_tile(idx1_ref, 1)

    def step(cur, ahead):
        wait_tile(cur)
        gates = gate_ref[0]
        sub = lax.broadcasted_iota(jnp.int32, (SUBLANES, LANE), 0)
        for t in range(tt):
            for j in range(PEER_SLOTS):
                request(idx2_ref, ahead, t * PEER_SLOTS + j, j % DMA_QUEUES)
            x = xn_ref[t]

            def slab(pos, half):
                return bufs[cur][pl.ds((t * PEER_SLOTS + pos) * SLAB + half * SUBLANES, SUBLANES), :]

            groups = []
            for g in range(PEER_SLOTS // SUBLANES):
                p = [slab(g * SUBLANES + k, 0) * x for k in range(SUBLANES)]
                c = [_fold_pair(p[2 * k], p[2 * k + 1], 4, sub) for k in range(4)]
                d = [_fold_pair(c[0], c[1], 2, sub), _fold_pair(c[2], c[3], 2, sub)]
                groups.append(_fold_pair(d[0], d[1], 1, sub))
            act = jnp.sum(jnp.concatenate(groups, axis=0), axis=-1, keepdims=True)
            w_scr[...] = jnp.broadcast_to(gates[:, t:t + 1] * _gelu(act), (PEER_SLOTS, LANE))
            acc = h_ref[t]
            for g in range(PEER_SLOTS // SUBLANES):
                for k in range(SUBLANES):
                    w_row = w_scr[pl.ds(g * SUBLANES + BITREV8[k], 1), :]
                    acc = acc + slab(g * SUBLANES + k, 1) * w_row
            o_ref[t] = acc

    for k in range(GATHER_DEPTH):
        @pl.when(i % GATHER_DEPTH == k)
        def _(k=k):
            step(k, (k + GATHER_DEPTH - 1) % GATHER_DEPTH)

    @pl.when(i == n - 1)
    def _():
        for k in range(GATHER_DEPTH):
            @pl.when(i % GATHER_DEPTH == k)
            def _(k=k):
                wait_tile((k + 1) % GATHER_DEPTH)
                wait_tile((k + 2) % GATHER_DEPTH)


def peer_gather(idx, xn, h, gate_b, uv):
    t = xn.shape[0]
    nt = t // GATHER_TILE
    n_slabs = GATHER_TILE * PEER_SLOTS
    tok = pl.BlockSpec((GATHER_TILE, SUBLANES, LANE), lambda i: (i, 0, 0))
    idx_spec = lambda k: pl.BlockSpec((1, 1, n_slabs), lambda i: (jnp.minimum(i + k, nt - 1), 0, 0),
                                      memory_space=pltpu.SMEM)
    slab_buf = pltpu.VMEM((n_slabs * SLAB, LANE), F32)
    return pl.pallas_call(
        _peer_gather_kernel,
        grid=(nt,),
        in_specs=[idx_spec(0), idx_spec(1), idx_spec(2), tok, tok,
                  pl.BlockSpec((1, PEER_SLOTS, GATHER_TILE), lambda i: (i, 0, 0)),
                  pl.BlockSpec(memory_space=pl.ANY)],
        out_specs=tok,
        out_shape=jax.ShapeDtypeStruct((t, SUBLANES, LANE), F32),
        scratch_shapes=[slab_buf, slab_buf, slab_buf, pltpu.VMEM((PEER_SLOTS, LANE), F32),
                        pltpu.SemaphoreType.DMA((GATHER_DEPTH,))],
        compiler_params=pltpu.CompilerParams(dimension_semantics=("arbitrary",),
                                             vmem_limit_bytes=VMEM_LIMIT),
        name="peer_gather",
    )(idx, idx, idx, xn, h, gate_b, uv)


def peer_ffn_residual(h, g, w_q, sub_keys, u_tab, v_tab):
    t, d = h.shape
    e = u_tab.shape[0]
    keys = sub_keys.reshape(PEER_HEADS * 2, N_KEYS, PEER_DK // 2).astype(BF16)
    xn, eidx_t, gate_t = peer_route(h, g, w_q.astype(BF16), keys)
    nt = t // GATHER_TILE
    order = np.arange(PEER_SLOTS).reshape(-1, SUBLANES)[:, list(BITREV8)].reshape(-1)
    idx = eidx_t[order].T.reshape(nt, 1, GATHER_TILE * PEER_SLOTS)
    gate_b = gate_t.reshape(PEER_SLOTS, nt, GATHER_TILE).transpose(1, 0, 2)
    uv = jnp.concatenate([u_tab.reshape(e, SUBLANES, LANE), v_tab.reshape(e, SUBLANES, LANE)],
                         axis=1).reshape(e * SLAB, LANE)
    out = peer_gather(idx, xn.reshape(t, SUBLANES, LANE), h.reshape(t, SUBLANES, LANE), gate_b, uv)
    return out.reshape(t, d)


Q_LORA = 256
KV_LORA = 128
QK_NOPE = 64
QK_ROPE = 32
V_HEAD = 64
ROPE_THETA = 10000.0
MLA_TQ = 256
MLA_TK = 512


def _rms(x, g):
    return x * lax.rsqrt(jnp.mean(x * x, axis=-1, keepdims=True) + NORM_EPS) * g


def _mla_prep_kernel(pos_ref, freq_ref, cq_ref, ckv_ref, kr_ref, gq_ref, gkv_ref, wq_ref, wk_ref, wv_ref,
                     q_out, k_out, v_out):
    rows = cq_ref.shape[0]
    half = QK_ROPE // 2
    cq = _rms(cq_ref[...], gq_ref[...]).astype(BF16)
    ckv = _rms(ckv_ref[...], gkv_ref[...]).astype(BF16)
    qf = _dot(cq, wq_ref[...])
    kf = _dot(ckv, wk_ref[...])
    vf = _dot(ckv, wv_ref[...])
    lane = lax.broadcasted_iota(jnp.int32, (rows, LANE), 1)
    ang = pos_ref[...].astype(F32) * freq_ref[...]
    cos, sin = jnp.cos(ang), jnp.sin(ang)
    in_rope = (lane >= QK_NOPE) & (lane < QK_NOPE + QK_ROPE)
    first = lane < QK_NOPE + half
    c_mul = jnp.where(lane < QK_NOPE, 1.0, jnp.where(in_rope, cos, 0.0))
    s_mul = jnp.where(in_rope, jnp.where(first, -sin, sin), 0.0)

    def rope(x):
        partner = jnp.where(first, pltpu.roll(x, LANE - half, axis=1), pltpu.roll(x, half, axis=1))
        return x * c_mul + partner * s_mul

    scale = (QK_NOPE + QK_ROPE) ** -0.5
    kr = rope(jnp.concatenate([jnp.zeros((rows, QK_NOPE), F32), kr_ref[...],
                               jnp.zeros((rows, LANE - QK_NOPE - QK_ROPE), F32)], axis=1))
    ones_col = jnp.where(lane == V_HEAD, 1.0, 0.0)
    for h in range(GROUP_HEADS):
        cols = slice(h * LANE, (h + 1) * LANE)
        q_out[:, cols] = (rope(qf[:, cols]) * scale).astype(BF16)
        k_out[:, cols] = (kf[:, cols] + kr).astype(BF16)
        v_out[:, cols] = (vf[:, cols] + ones_col).astype(BF16)


def mla_prep(pos, c_q, c_kv, k_r, gq, gkv, w_uq, w_ukv):
    t = c_q.shape[0]
    H = GROUP_HEADS
    wq = jnp.pad(w_uq.reshape(Q_LORA, H, QK_NOPE + QK_ROPE), ((0, 0), (0, 0), (0, LANE - QK_NOPE - QK_ROPE)))
    wkv = w_ukv.reshape(KV_LORA, H, QK_NOPE + V_HEAD)
    wk = jnp.pad(wkv[:, :, :QK_NOPE], ((0, 0), (0, 0), (0, LANE - QK_NOPE)))
    wv = jnp.pad(wkv[:, :, QK_NOPE:], ((0, 0), (0, 0), (0, LANE - V_HEAD)))
    wq, wk, wv = [w.reshape(w.shape[0], H * LANE).astype(BF16) for w in (wq, wk, wv)]
    inv_freq = ROPE_THETA ** (-np.arange(0, QK_ROPE, 2, dtype=np.float32) / QK_ROPE)
    freq = np.zeros((1, LANE), np.float32)
    freq[0, QK_NOPE:QK_NOPE + QK_ROPE] = np.tile(inv_freq, 2)
    row = lambda n: pl.BlockSpec((ROW_TILE, n), lambda i: (i, 0))
    fixed = lambda a: pl.BlockSpec(a.shape, lambda i: (0,) * a.ndim)
    gq2, gkv2, freq = gq.reshape(1, -1), gkv.reshape(1, -1), jnp.asarray(freq)
    out = jax.ShapeDtypeStruct((t, H * LANE), BF16)
    return pl.pallas_call(
        _mla_prep_kernel,
        grid=(t // ROW_TILE,),
        in_specs=[row(1), fixed(freq), row(Q_LORA), row(KV_LORA), row(QK_ROPE), fixed(gq2), fixed(gkv2),
                  fixed(wq), fixed(wk), fixed(wv)],
        out_specs=[row(H * LANE)] * 3,
        out_shape=[out] * 3,
        compiler_params=pltpu.CompilerParams(dimension_semantics=("arbitrary",)),
        name="mla_prep",
    )(pos, freq, c_q, c_kv, k_r, gq2, gkv2, wq, wk, wv)


def _mla_attn_kernel(q_ref, k_ref, v_ref, o_ref, m_scr, acc_scr):
    i = pl.program_id(1)
    tq = q_ref.shape[1]
    row = i * tq + lax.broadcasted_iota(jnp.int32, (tq, MLA_TK), 0)
    col = lax.broadcasted_iota(jnp.int32, (tq, MLA_TK), 1)
    n_full = (i * tq) // MLA_TK
    heads = range(GROUP_HEADS)
    cols = [slice(h * LANE, (h + 1) * LANE) for h in heads]
    q = [q_ref[0, :, cols[h]] for h in heads]
    m_scr[...] = jnp.full(m_scr.shape, NEG, F32)
    acc_scr[...] = jnp.zeros(acc_scr.shape, F32)

    def step(j, masked):
        start = pl.multiple_of(j * MLA_TK, MLA_TK)
        s = [_dot_nt(q[h], k_ref[0, pl.ds(start, MLA_TK), cols[h]]) for h in heads]
        if masked:
            s = [jnp.where(start + col <= row, sh, NEG) for sh in s]
        m_old = [m_scr[h] for h in heads]
        m_new = [jnp.maximum(m_old[h], jnp.max(s[h], axis=-1, keepdims=True)) for h in heads]
        p = [jnp.exp(s[h] - m_new[h]).astype(BF16) for h in heads]
        pv = [_dot(p[h], v_ref[0, pl.ds(start, MLA_TK), cols[h]]) for h in heads]
        for h in heads:
            acc_scr[h] = jnp.exp(m_old[h] - m_new[h]) * acc_scr[h] + pv[h]
            m_scr[h] = m_new[h]

    def body(j, carry):
        step(j, False)
        return carry

    lax.fori_loop(0, n_full, body, 0)
    step(n_full, True)
    outs = [acc_scr[h][:, :V_HEAD] * (1.0 / acc_scr[h][:, V_HEAD:V_HEAD + 1]) for h in heads]
    o_ref[0] = jnp.concatenate(outs, axis=1)


def mla_attention(q, k, v):
    b, s, w = q.shape
    return pl.pallas_call(
        _mla_attn_kernel,
        grid=(b, s // MLA_TQ),
        in_specs=[pl.BlockSpec((1, MLA_TQ, w), lambda bi, i: (bi, i, 0)),
                  pl.BlockSpec((1, s, w), lambda bi, i: (bi, 0, 0)),
                  pl.BlockSpec((1, s, w), lambda bi, i: (bi, 0, 0))],
        out_specs=pl.BlockSpec((1, MLA_TQ, GROUP_HEADS * V_HEAD), lambda bi, i: (bi, i, 0)),
        out_shape=jax.ShapeDtypeStruct((b, s, GROUP_HEADS * V_HEAD), F32),
        scratch_shapes=[pltpu.VMEM((GROUP_HEADS, MLA_TQ, 1), F32), pltpu.VMEM((GROUP_HEADS, MLA_TQ, LANE), F32)],
        compiler_params=pltpu.CompilerParams(dimension_semantics=("arbitrary", "arbitrary"),
                                             vmem_limit_bytes=VMEM_LIMIT),
        name="mla_attention",
    )(q, k, v)


def mla_group(h, positions, q_norm_g, kv_norm_g, w_uq, w_ukv):
    b, s, _ = h.shape
    t = b * s
    h2 = h.reshape(t, -1)
    c_q, c_kv, k_r = h2[:, :Q_LORA], h2[:, Q_LORA:Q_LORA + KV_LORA], h2[:, Q_LORA + KV_LORA:]
    q, k, v = mla_prep(positions.reshape(t, 1), c_q, c_kv, k_r, q_norm_g, kv_norm_g, w_uq, w_ukv)
    w = GROUP_HEADS * LANE
    return mla_attention(q.reshape(b, s, w), k.reshape(b, s, w), v.reshape(b, s, w))


CHUNK = 64
CONV_W = 4


def _split3(a):
    hi = a.astype(BF16)
    r = a - hi.astype(F32)
    mid = r.astype(BF16)
    return hi, mid, (r - mid.astype(F32)).astype(BF16)


def _dot_exact_rhs(sel, a):
    return sum(_dot(sel, p) for p in _split3(a))


def _row_matrix(col, n_rows):
    l = col.shape[0]
    lane = lax.broadcasted_iota(jnp.int32, (l, SUBLANES), 1)
    padded = jnp.where(lane == 0, col, 0.0)
    ones = jnp.ones((n_rows, SUBLANES), BF16)
    return sum(_dot_nt(ones, p) for p in _split3(padded))


def _transpose_bf16(x_bf):
    d = x_bf.shape[1]
    eye = (lax.broadcasted_iota(jnp.int32, (d, d), 0) == lax.broadcasted_iota(jnp.int32, (d, d), 1)).astype(BF16)
    return _dot_nt(eye, x_bf).astype(BF16)


def _log_sigmoid(x):
    return jnp.minimum(x, 0.0) - jnp.log1p(jnp.exp(-jnp.abs(x)))


def _softplus(x):
    return jnp.maximum(x, 0.0) + jnp.log1p(jnp.exp(-jnp.abs(x)))


def _with_ones_col(v):
    l, d = v.shape
    lane = lax.broadcasted_iota(jnp.int32, (l, LANE - d), 1)
    return jnp.concatenate([v, jnp.where(lane == 0, 1.0, 0.0)], axis=1)


def _bf(x):
    return x.astype(BF16)


def _mlstm_kernel(q_ref, k_ref, v_ref, gate_ref, og_ref, bias_ref, ng_ref, o_ref, st_scr, m_scr):
    nb = q_ref.shape[0]
    H, D, L = GROUP_HEADS, HEAD_DIM, CHUNK
    chains = [(b, h) for b in range(nb) for h in range(H)]
    cols = lambda h: slice(h * D, (h + 1) * D)
    each = lambda fn: [fn(c, b, h) for c, (b, h) in enumerate(chains)]

    @pl.when(pl.program_id(0) == 0)
    def _():
        st_scr[...] = jnp.zeros(st_scr.shape, F32)
        m_scr[...] = jnp.zeros(m_scr.shape, F32)

    row = lax.broadcasted_iota(jnp.int32, (L, L), 0)
    col = lax.broadcasted_iota(jnp.int32, (L, L), 1)
    tri = row >= col
    tri_bf = tri.astype(BF16)
    gates = [gate_ref[b] + bias_ref[...] for b in range(nb)]
    bcum = [_dot_exact_rhs(tri_bf, _log_sigmoid(g[:, H:])) for g in gates]
    q = each(lambda c, b, h: _bf(q_ref[b, :, cols(h)]))
    kt = each(lambda c, b, h: _transpose_bf16(_bf(k_ref[b, :, cols(h)] * (D ** -0.5))))
    v_aug = each(lambda c, b, h: _with_ones_col(v_ref[b, :, cols(h)]))
    bc = each(lambda c, b, h: bcum[b][:, h:h + 1])
    ic = each(lambda c, b, h: gates[b][:, h:h + 1])
    m_st = each(lambda c, b, h: m_scr[c, 0:1, 0:1])
    st = each(lambda c, b, h: st_scr[c])
    log_d = each(lambda c, b, h: jnp.where(tri, bc[c] + _row_matrix(ic[c] - bc[c], L), -jnp.inf))
    m_t = each(lambda c, b, h: jnp.maximum(jnp.max(log_d[c], axis=-1, keepdims=True), bc[c] + m_st[c]))
    sqk = each(lambda c, b, h: _dot(q[c], kt[c]) * jnp.exp(log_d[c] - m_t[c]))
    inter = each(lambda c, b, h: jnp.exp(bc[c] + m_st[c] - m_t[c]))
    tot = each(lambda c, b, h: _dot(_bf(sqk[c]), _bf(v_aug[c])) + inter[c] * _dot(q[c], _bf(st[c])))
    h_t = each(lambda c, b, h: tot[c][:, :D] * (1.0 / jnp.maximum(jnp.abs(tot[c][:, D:D + 1]), jnp.exp(-m_t[c]))))
    b_end = each(lambda c, b, h: bc[c][L - 1:L])
    log_w = each(lambda c, b, h: b_end[c] - bc[c] + ic[c])
    m_new = each(lambda c, b, h: jnp.maximum(b_end[c] + m_st[c], jnp.max(log_w[c], axis=0, keepdims=True)))
    upd = each(lambda c, b, h: _dot(kt[c], _bf(jnp.exp(log_w[c] - m_new[c]) * v_aug[c])))
    for c in range(len(chains)):
        st_scr[c] = jnp.exp(b_end[c] + m_st[c] - m_new[c]) * st[c] + upd[c]
        m_scr[c] = jnp.broadcast_to(m_new[c], m_scr.shape[1:])
    for b in range(nb):
        outs = []
        for h in range(H):
            o = jax.nn.sigmoid(og_ref[b, :, cols(h)]) * h_t[b * H + h]
            outs.append(o * lax.rsqrt(jnp.mean(o * o, axis=-1, keepdims=True) + NORM_EPS) * ng_ref[:, cols(h)])
        o_ref[b] = jnp.concatenate(outs, axis=1)


def mlstm_group(h, i_bias, f_bias, norm_g):
    b, s, _ = h.shape
    W, H = GROUP_W, GROUP_HEADS
    q, k, v = h[..., :W], h[..., W:2 * W], h[..., 2 * W:3 * W]
    gates = h[..., 3 * W:3 * W + 2 * H]
    og = h[..., 3 * W + 2 * H:]
    bias = jnp.concatenate([i_bias, f_bias]).reshape(1, 2 * H)
    tok = lambda n: pl.BlockSpec((b, CHUNK, n), lambda i: (0, i, 0))
    fixed = lambda n: pl.BlockSpec((1, n), lambda i: (0, 0))
    return pl.pallas_call(
        _mlstm_kernel,
        grid=(s // CHUNK,),
        in_specs=[tok(W), tok(W), tok(W), tok(2 * H), tok(W), fixed(2 * H), fixed(W)],
        out_specs=tok(W),
        out_shape=jax.ShapeDtypeStruct((b, s, W), F32),
        scratch_shapes=[pltpu.VMEM((b * H, HEAD_DIM, LANE), F32), pltpu.VMEM((b * H, SUBLANES, LANE), F32)],
        compiler_params=pltpu.CompilerParams(dimension_semantics=("arbitrary",)),
        name="mlstm",
    )(q, k, v, gates, og, bias, norm_g.reshape(1, W))


def _gdn_kernel(x_ref, xprev_ref, cw_ref, gate_ref, z_ref, par_ref, ng_ref, o_ref, st_scr):
    nb = x_ref.shape[0]
    H, D, L = GROUP_HEADS, HEAD_DIM, CHUNK
    W = H * D
    chains = [(b, h) for b in range(nb) for h in range(H)]
    each = lambda fn: [fn(c, b, h) for c, (b, h) in enumerate(chains)]
    first = pl.program_id(0) == 0

    @pl.when(first)
    def _():
        st_scr[...] = jnp.zeros(st_scr.shape, F32)

    row = lax.broadcasted_iota(jnp.int32, (L, L), 0)
    col = lax.broadcasted_iota(jnp.int32, (L, L), 1)
    tri = row >= col
    tri_bf = tri.astype(BF16)
    eye = (row == col).astype(F32)
    par = par_ref[...]
    qkv, beta4, gc4 = [], [], []
    for b in range(nb):
        prev = jnp.where(first, 0.0, xprev_ref[b])
        full = jnp.concatenate([prev, x_ref[b]], axis=0)
        acc = full[SUBLANES:] * cw_ref[CONV_W - 1:CONV_W, :]
        for j in range(CONV_W - 1):
            shifted = pltpu.roll(full, CONV_W - 1 - j, axis=0)[SUBLANES:]
            acc = acc + shifted * cw_ref[j:j + 1, :]
        qkv.append(acc * jax.nn.sigmoid(acc))
        gates = gate_ref[b]
        beta4.append(jax.nn.sigmoid(gates[:, :H]))
        gc4.append(_dot_exact_rhs(tri_bf, -jnp.exp(par[:, :H]) * _softplus(gates[:, H:] + par[:, H:])))

    def l2n(t):
        return t * lax.rsqrt(jnp.sum(t * t, axis=-1, keepdims=True) + NORM_EPS)

    q = each(lambda c, b, h: l2n(qkv[b][:, h * D:(h + 1) * D]) * (D ** -0.5))
    k = each(lambda c, b, h: l2n(qkv[b][:, W + h * D:W + (h + 1) * D]))
    v = each(lambda c, b, h: qkv[b][:, 2 * W + h * D:2 * W + (h + 1) * D])
    beta = each(lambda c, b, h: beta4[b][:, h:h + 1])
    gc = each(lambda c, b, h: gc4[b][:, h:h + 1])
    gc_row = each(lambda c, b, h: _row_matrix(gc[c], L))
    decay = each(lambda c, b, h: jnp.where(tri, jnp.exp(gc[c] - gc_row[c]), 0.0))
    kt = each(lambda c, b, h: _transpose_bf16(_bf(k[c])))
    kb = each(lambda c, b, h: k[c] * beta[c])
    x = each(lambda c, b, h: -jnp.where(row > col, _dot(_bf(kb[c]), kt[c]) * decay[c], 0.0))
    t_inv = [eye + xc for xc in x]
    for _ in range(5):
        x = [_dot(_bf(xc), _bf(xc)) for xc in x]
        t_inv = [tc + _dot(_bf(tc), _bf(xc)) for tc, xc in zip(t_inv, x)]
    u = each(lambda c, b, h: _dot(_bf(t_inv[c]), _bf(v[c] * beta[c])))
    wm = each(lambda c, b, h: _dot(_bf(t_inv[c]), _bf(kb[c] * jnp.exp(gc[c]))))
    attn = each(lambda c, b, h: _dot(_bf(q[c]), kt[c]) * decay[c])
    st = each(lambda c, b, h: st_scr[c])
    v_new = each(lambda c, b, h: u[c] - _dot(_bf(wm[c]), _bf(st[c])))
    o = each(lambda c, b, h: _dot(_bf(q[c] * jnp.exp(gc[c])), _bf(st[c])) + _dot(_bf(attn[c]), _bf(v_new[c])))
    g_end = each(lambda c, b, h: gc[c][L - 1:L])
    upd = each(lambda c, b, h: _dot(_bf(kt[c].astype(F32) * jnp.exp(g_end[c] - gc_row[c])), _bf(v_new[c])))
    for c in range(len(chains)):
        st_scr[c] = st[c] * jnp.exp(g_end[c]) + upd[c]
    for b in range(nb):
        outs = []
        for h in range(H):
            oc = o[b * H + h]
            z = z_ref[b, :, h * D:(h + 1) * D]
            y = oc * lax.rsqrt(jnp.mean(oc * oc, axis=-1, keepdims=True) + NORM_EPS) * ng_ref[...]
            outs.append(y * (z * jax.nn.sigmoid(z)))
        o_ref[b] = jnp.concatenate(outs, axis=1)


def gdn_group(h, conv_w, a_log, dt_bias, norm_g):
    b, s, _ = h.shape
    W, H = GROUP_W, GROUP_HEADS
    x = h[..., :3 * W]
    gates = h[..., 3 * W:3 * W + 2 * H]
    z = h[..., 3 * W + 2 * H:]
    par = jnp.concatenate([a_log, dt_bias]).reshape(1, 2 * H)
    tok = lambda n: pl.BlockSpec((b, CHUNK, n), lambda i: (0, i, 0))
    per_chunk = CHUNK // SUBLANES
    prev = pl.BlockSpec((b, SUBLANES, 3 * W), lambda i: (0, jnp.maximum(i * per_chunk - 1, 0), 0))
    fixed = lambda r, n: pl.BlockSpec((r, n), lambda i: (0, 0))
    return pl.pallas_call(
        _gdn_kernel,
        grid=(s // CHUNK,),
        in_specs=[tok(3 * W), prev, fixed(CONV_W, 3 * W), tok(2 * H), tok(W), fixed(1, 2 * H), fixed(1, HEAD_DIM)],
        out_specs=tok(W),
        out_shape=jax.ShapeDtypeStruct((b, s, W), F32),
        scratch_shapes=[pltpu.VMEM((b * H, HEAD_DIM, HEAD_DIM), F32)],
        compiler_params=pltpu.CompilerParams(dimension_semantics=("arbitrary",)),
        name="gdn",
    )(x, x, conv_w, gates, z, par, norm_g.reshape(1, HEAD_DIM))


NSA_IN = GROUP_W + 6 * HEAD_DIM + 3 * GROUP_HEADS
MLSTM_IN = 4 * GROUP_W + 2 * GROUP_HEADS
GDN_IN = 4 * GROUP_W + 2 * GROUP_HEADS
MLA_IN = Q_LORA + KV_LORA + QK_ROPE


def kernel(x, positions, t5_table, norm1_g, norm2_g, final_norm_g, w_in, w_out, nsa_w_cmp_k, nsa_w_cmp_v, nsa_cmp_pos, mlstm_i_bias, mlstm_f_bias, mlstm_norm_g, gdn_conv_w, gdn_a_log, gdn_dt_bias, gdn_norm_g, mla_q_norm_g, mla_kv_norm_g, mla_w_uq, mla_w_ukv, peer_w_q, peer_sub_keys, peer_u, peer_v):
    b, s, d = x.shape
    t = b * s
    h = x.reshape(t, d)
    depth = w_in.shape[0]
    d_in = w_in.shape[2]
    d_in_pad = _round_up(d_in, LANE)
    bias_tiles = nsa_bias_tiles(t5_table)
    for l in range(depth):
        w_in_l = jnp.pad(w_in[l], ((0, 0), (0, d_in_pad - d_in))).astype(BF16)
        hp = norm_matmul(h, norm1_g[l], w_in_l)[:, :d_in].reshape(b, s, d_in)
        c0, c1, c2 = NSA_IN, NSA_IN + MLSTM_IN, NSA_IN + MLSTM_IN + GDN_IN
        h_nsa, h_ml, h_gdn, h_mla = hp[..., :c0], hp[..., c0:c1], hp[..., c1:c2], hp[..., c2:]
        o = jnp.concatenate([
            nsa_group(h_nsa, nsa_w_cmp_k[l], nsa_w_cmp_v[l], nsa_cmp_pos[l], t5_table, bias_tiles),
            mlstm_group(h_ml, mlstm_i_bias[l], mlstm_f_bias[l], mlstm_norm_g[l]),
            gdn_group(h_gdn, gdn_conv_w[l], gdn_a_log[l], gdn_dt_bias[l], gdn_norm_g[l]),
            mla_group(h_mla, positions, mla_q_norm_g[l], mla_kv_norm_g[l], mla_w_uq[l], mla_w_ukv[l]),
        ], axis=-1)
        h = matmul_residual(o.reshape(t, -1), w_out[l].astype(BF16), h)
        h = peer_ffn_residual(h, norm2_g[l], peer_w_q[l], peer_sub_keys[l], peer_u[l], peer_v[l])
    return rmsnorm(h, final_norm_g).reshape(b, s, d)
```

```python
import functools
import math

import jax
import jax.numpy as jnp
import numpy as np
from jax import lax
from jax.experimental import pallas as pl
from jax.experimental.pallas import tpu as pltpu

F32 = jnp.float32
BF16 = jnp.bfloat16
NORM_EPS = 1e-6
NEG = -1e30
LANE = 128
ROW_TILE = 512
VMEM_LIMIT = 56 * 1024 * 1024

HEAD_DIM = 64
GROUP_HEADS = 4
GROUP_W = GROUP_HEADS * HEAD_DIM
Q_BLOCK = 128
CMP_LEN = 32
CMP_STRIDE = 16
SEL_LEN = 64
SEL_TOPN = 16
WINDOW = 512
SEL_FORCE = 1e3
N_BUCKETS = 32
MAX_DISTANCE = 128


def _round_up(n, m):
    return (n + m - 1) // m * m


def _dot_nt(a, b):
    return lax.dot_general(a, b, (((1,), (1,)), ((), ())), preferred_element_type=F32)


def _dot(a, b):
    return jnp.dot(a, b, preferred_element_type=F32)


def _norm_matmul_kernel(x_ref, g_ref, w_ref, o_ref):
    x = x_ref[...]
    y = x * lax.rsqrt(jnp.mean(x * x, axis=-1, keepdims=True) + NORM_EPS) * g_ref[...]
    o_ref[...] = _dot(y.astype(BF16), w_ref[...])


def norm_matmul(x, g, w):
    t, d = x.shape
    n = w.shape[1]
    return pl.pallas_call(
        _norm_matmul_kernel,
        grid=(t // ROW_TILE,),
        in_specs=[pl.BlockSpec((ROW_TILE, d), lambda i: (i, 0)),
                  pl.BlockSpec((1, d), lambda i: (0, 0)),
                  pl.BlockSpec((d, n), lambda i: (0, 0))],
        out_specs=pl.BlockSpec((ROW_TILE, n), lambda i: (i, 0)),
        out_shape=jax.ShapeDtypeStruct((t, n), F32),
        compiler_params=pltpu.CompilerParams(dimension_semantics=("arbitrary",),
                                             vmem_limit_bytes=VMEM_LIMIT),
        name="norm_matmul",
    )(x, g.reshape(1, d), w)


def _matmul_residual_kernel(o_ref, w_ref, r_ref, out_ref):
    out_ref[...] = r_ref[...] + _dot(o_ref[...].astype(BF16), w_ref[...])


def matmul_residual(o, w, res):
    t, d = o.shape
    n = w.shape[1]
    return pl.pallas_call(
        _matmul_residual_kernel,
        grid=(t // ROW_TILE,),
        in_specs=[pl.BlockSpec((ROW_TILE, d), lambda i: (i, 0)),
                  pl.BlockSpec((d, n), lambda i: (0, 0)),
                  pl.BlockSpec((ROW_TILE, n), lambda i: (i, 0))],
        out_specs=pl.BlockSpec((ROW_TILE, n), lambda i: (i, 0)),
        out_shape=jax.ShapeDtypeStruct((t, n), F32),
        compiler_params=pltpu.CompilerParams(dimension_semantics=("arbitrary",),
                                             vmem_limit_bytes=VMEM_LIMIT),
        name="matmul_residual",
    )(o, w, res)


def _rmsnorm_kernel(x_ref, g_ref, o_ref):
    x = x_ref[...]
    o_ref[...] = x * lax.rsqrt(jnp.mean(x * x, axis=-1, keepdims=True) + NORM_EPS) * g_ref[...]


def rmsnorm(x, g):
    t, d = x.shape
    return pl.pallas_call(
        _rmsnorm_kernel,
        grid=(t // ROW_TILE,),
        in_specs=[pl.BlockSpec((ROW_TILE, d), lambda i: (i, 0)),
                  pl.BlockSpec((1, d), lambda i: (0, 0))],
        out_specs=pl.BlockSpec((ROW_TILE, d), lambda i: (i, 0)),
        out_shape=jax.ShapeDtypeStruct((t, d), F32),
        compiler_params=pltpu.CompilerParams(dimension_semantics=("arbitrary",)),
        name="rmsnorm",
    )(x, g.reshape(1, d))


def _t5_bucket_starts():
    d = np.arange(0, 2 * MAX_DISTANCE)
    max_exact = N_BUCKETS // 2
    val = (np.log(np.maximum(d, 1).astype(np.float32) / np.float32(max_exact))
           / np.float32(math.log(MAX_DISTANCE / max_exact)) * np.float32(N_BUCKETS - max_exact))
    large = np.minimum(max_exact + val.astype(np.int32), N_BUCKETS - 1)
    bucket = np.where(d < max_exact, d, large)
    assert np.all(np.diff(bucket) >= 0)
    return [(int(k), int(d[bucket == k].min())) for k in range(N_BUCKETS) if np.any(bucket == k)]


T5_STARTS = _t5_bucket_starts()


def _t5_bias_heads(dist, tab_ref):
    accs = [jnp.full(dist.shape, tab_ref[0, h], F32) for h in range(GROUP_HEADS)]
    for bucket, start in T5_STARTS[1:]:
        m = dist >= start
        accs = [jnp.where(m, tab_ref[bucket, h], accs[h]) for h in range(GROUP_HEADS)]
    return jnp.stack(accs)


def _nsa_compress_kernel(x_ref, pos_ref, w_ref, o_ref):
    x = x_ref[0, 0]
    n = x.shape[0]
    lo = _dot((x + pos_ref[0, 0:1, :]).astype(BF16), w_ref[0, 0])
    hi = _dot((x + pos_ref[0, 1:2, :]).astype(BF16), w_ref[0, 1])
    o_ref[0, 0] = (lo + pltpu.roll(hi, n - 1, axis=0)).astype(o_ref.dtype)


def nsa_compress(x, pos, w):
    two, b, n, c = x.shape
    return pl.pallas_call(
        _nsa_compress_kernel,
        grid=(two, b),
        in_specs=[pl.BlockSpec((1, 1, n, c), lambda k, i: (k, i, 0, 0)),
                  pl.BlockSpec((1, 2, c), lambda k, i: (0, 0, 0)),
                  pl.BlockSpec((1, 2, c, HEAD_DIM), lambda k, i: (k, 0, 0, 0))],
        out_specs=pl.BlockSpec((1, 1, n, HEAD_DIM), lambda k, i: (k, i, 0, 0)),
        out_shape=jax.ShapeDtypeStruct((two, b, n, HEAD_DIM), BF16),
        compiler_params=pltpu.CompilerParams(dimension_semantics=("arbitrary", "arbitrary")),
        name="nsa_compress",
    )(x, pos, w)


FAR_TILE = 512
NEAR_KEYS = 768
WIN_KEYS = WINDOW + Q_BLOCK
NEAR_PAD = NEAR_KEYS - Q_BLOCK
CMP_TILE = LANE
CMP_TILES_BACK = CMP_TILE * CMP_STRIDE // Q_BLOCK
N_CMP_BIAS = CMP_TILES_BACK + 3


def _masked_bias(dist, valid, tab_ref):
    return jnp.where(valid, _t5_bias_heads(dist, tab_ref), NEG)


def _nsa_bias_kernel(tab_ref, near_ref, win_ref):
    q = lax.broadcasted_iota(jnp.int32, (Q_BLOCK, NEAR_KEYS), 0)
    k = lax.broadcasted_iota(jnp.int32, (Q_BLOCK, NEAR_KEYS), 1)
    dist = NEAR_PAD + q - k
    near_ref[...] = _masked_bias(dist, dist >= 0, tab_ref)
    q = lax.broadcasted_iota(jnp.int32, (Q_BLOCK, WIN_KEYS), 0)
    k = lax.broadcasted_iota(jnp.int32, (Q_BLOCK, WIN_KEYS), 1)
    dist = WINDOW + q - k
    win_ref[...] = _masked_bias(dist, (dist >= 0) & (dist < WINDOW), tab_ref)


def _nsa_cmp_bias_kernel(tab_ref, o_ref):
    e = pl.program_id(0)
    q = lax.broadcasted_iota(jnp.int32, (Q_BLOCK, CMP_TILE), 0)
    n = lax.broadcasted_iota(jnp.int32, (Q_BLOCK, CMP_TILE), 1)
    dist = (e - 1) * Q_BLOCK + q - (n * CMP_STRIDE + CMP_LEN - 1)
    o_ref[0] = _masked_bias(dist, dist >= 0, tab_ref)


def nsa_bias_tiles(t5_table):
    H, Q = GROUP_HEADS, Q_BLOCK
    smem = pl.BlockSpec(memory_space=pltpu.SMEM)
    near, win = pl.pallas_call(
        _nsa_bias_kernel, grid=(1,), in_specs=[smem],
        out_specs=[pl.BlockSpec((H, Q, NEAR_KEYS), lambda i: (0, 0, 0)),
                   pl.BlockSpec((H, Q, WIN_KEYS), lambda i: (0, 0, 0))],
        out_shape=[jax.ShapeDtypeStruct((H, Q, NEAR_KEYS), F32), jax.ShapeDtypeStruct((H, Q, WIN_KEYS), F32)],
        name="nsa_bias",
    )(t5_table)
    cmp = pl.pallas_call(
        _nsa_cmp_bias_kernel, grid=(N_CMP_BIAS,), in_specs=[smem],
        out_specs=pl.BlockSpec((1, H, Q, CMP_TILE), lambda e: (e, 0, 0, 0)),
        out_shape=jax.ShapeDtypeStruct((N_CMP_BIAS, H, Q, CMP_TILE), F32),
        name="nsa_cmp_bias",
    )(t5_table)
    return near, win, cmp


def _nsa_attn_kernel(tab_ref, q_ref, g_ref, kc_ref, vc_ref, ks_ref, vs_ref, kw_ref, vw_ref,
                     bnear_ref, bwin_ref, bcmp_ref, o_ref, m_scr, acc_scr):
    H, Q, D = GROUP_HEADS, Q_BLOCK, HEAD_DIM
    n_cmp = kc_ref.shape[1]
    n_sel = (ks_ref.shape[1] - NEAR_PAD) // SEL_LEN
    i = pl.program_id(1)
    t0 = i * Q

    q = q_ref[0] * (D ** -0.5)
    q4 = jnp.concatenate([q[:, h * D:(h + 1) * D] for h in range(H)], axis=0).astype(BF16)

    s = _dot_nt(q4, kc_ref[0]).reshape(H, Q, n_cmp)
    tiles = []
    for g in range(n_cmp // CMP_TILE):
        e = jnp.clip(i - g * CMP_TILES_BACK, -1, CMP_TILES_BACK + 1) + 1
        tiles.append(s[:, :, g * CMP_TILE:(g + 1) * CMP_TILE] + bcmp_ref[e])
    s = jnp.concatenate(tiles, axis=-1)
    t_c = t0 + lax.broadcasted_iota(jnp.int32, (Q, n_cmp), 0)
    valid_c = t_c >= lax.broadcasted_iota(jnp.int32, (Q, n_cmp), 1) * CMP_STRIDE + CMP_LEN - 1
    p = jnp.where(valid_c, jnp.exp(s - jnp.max(s, axis=-1, keepdims=True)), 0.0)
    l = jnp.sum(p, axis=-1, keepdims=True)
    p = p * (1.0 / jnp.where(l > 0.0, l, 1.0))
    o_c = _dot(p.reshape(H * Q, n_cmp).astype(BF16), vc_ref[0])

    psum = p[0] + p[1] + p[2] + p[3]
    c_idx = lax.broadcasted_iota(jnp.int32, (n_cmp, n_sel), 0)
    j_idx = lax.broadcasted_iota(jnp.int32, (n_cmp, n_sel), 1)
    ratio = SEL_LEN // CMP_STRIDE
    n_in_sel = (SEL_LEN - CMP_LEN) // CMP_STRIDE + 1
    pool = ((c_idx // ratio == j_idx) & (c_idx % ratio < n_in_sel)).astype(BF16)
    p_hi = psum.astype(BF16)
    p_mid = (psum - p_hi.astype(F32)).astype(BF16)
    p_lo = (psum - p_hi.astype(F32) - p_mid.astype(F32)).astype(BF16)
    imp = _dot(p_hi, pool) + _dot(p_mid, pool) + _dot(p_lo, pool)

    jb = lax.broadcasted_iota(jnp.int32, (Q, n_sel), 1)
    cur = (t0 + lax.broadcasted_iota(jnp.int32, (Q, n_sel), 0)) // SEL_LEN
    forced = (jb == 0) | (jb == cur) | (jb == cur - 1)
    work = jnp.where(jb <= cur, imp + jnp.where(forced, SEL_FORCE, 0.0), NEG)
    sel = jnp.zeros((Q, n_sel), F32)
    for _ in range(min(SEL_TOPN, n_sel)):
        mx = jnp.max(work, axis=-1, keepdims=True)
        first = jnp.min(jnp.where(work == mx, jb, n_sel), axis=-1, keepdims=True)
        pick = jb == first
        sel = jnp.where(pick & (mx > 0.5 * NEG), 1.0, sel)
        work = jnp.where(pick, -jnp.inf, work)
    sel_bf = sel.astype(BF16)

    far_bias = jnp.stack([jnp.full((Q, 1), tab_ref[N_BUCKETS - 1, h], F32) for h in range(H)])
    m_scr[...] = jnp.full(m_scr.shape, NEG, F32)
    acc_scr[...] = jnp.zeros(acc_scr.shape, F32)

    def sel_step(row0, n_keys, bias, first_key):
        k = ks_ref[0, pl.ds(row0, n_keys), :]
        v = vs_ref[0, pl.ds(row0, n_keys), :]
        s = _dot_nt(q4, k).reshape(H, Q, n_keys) + bias
        pos = row0 - NEAR_PAD + lax.broadcasted_iota(jnp.int32, (n_sel, n_keys), 1)
        blk = jnp.where(pos >= first_key, pos // SEL_LEN, -1)
        expand = (lax.broadcasted_iota(jnp.int32, (n_sel, n_keys), 0) == blk).astype(BF16)
        mask = _dot(sel_bf, expand) > 0.5
        s = jnp.where(mask, s, NEG)
        m_old = m_scr[...].reshape(H, Q, 1)
        m_new = jnp.maximum(m_old, jnp.max(s, axis=-1, keepdims=True))
        p = jnp.where(mask, jnp.exp(s - m_new), 0.0).reshape(H * Q, n_keys)
        alpha = jnp.exp(m_old - m_new).reshape(H * Q, 1)
        acc_scr[...] = alpha * acc_scr[...] + _dot(p.astype(BF16), v)
        m_scr[...] = m_new.reshape(H * Q, 1)

    n_far = jnp.maximum(i - 1, 0) // (FAR_TILE // Q)

    def far_body(kt, carry):
        sel_step(pl.multiple_of(NEAR_PAD + kt * FAR_TILE, Q), FAR_TILE, far_bias, 0)
        return carry

    lax.fori_loop(0, n_far, far_body, 0)
    row0 = pl.multiple_of(t0, Q)
    sel_step(row0, NEAR_KEYS, bnear_ref[...], n_far * FAR_TILE)
    acc = acc_scr[...]
    o_s = acc[:, :D] * (1.0 / acc[:, D:D + 1])

    kw = kw_ref[0, pl.ds(row0, WIN_KEYS), :]
    vw = vw_ref[0, pl.ds(row0, WIN_KEYS), :]
    in_seq = lax.broadcasted_iota(jnp.int32, (1, WIN_KEYS), 1) >= WINDOW - t0
    s = _dot_nt(q4, kw).reshape(H, Q, WIN_KEYS) + bwin_ref[...] + jnp.where(in_seq, 0.0, NEG)
    p = jnp.exp(s - jnp.max(s, axis=-1, keepdims=True))
    acc_w = _dot(p.reshape(H * Q, WIN_KEYS).astype(BF16), vw)
    o_w = acc_w[:, :D] * (1.0 / acc_w[:, D:D + 1])

    g = jax.nn.sigmoid(g_ref[0])
    outs = []
    for h in range(H):
        rows = slice(h * Q, (h + 1) * Q)
        outs.append(g[:, h:h + 1] * o_c[rows] + g[:, H + h:H + h + 1] * o_s[rows]
                    + g[:, 2 * H + h:2 * H + h + 1] * o_w[rows])
    o_ref[0] = jnp.concatenate(outs, axis=1)


def nsa_attention(t5_table, bias_tiles, q, gates, k_cmp, v_cmp, ks, vs_aug, kw, vw_aug):
    b, s, _ = q.shape
    bnear, bwin, bcmp = bias_tiles
    full = lambda a: pl.BlockSpec((1,) + a.shape[1:], lambda bi, i: (bi,) + (0,) * (a.ndim - 1))
    fixed = lambda a: pl.BlockSpec(a.shape, lambda bi, i: (0,) * a.ndim)
    return pl.pallas_call(
        _nsa_attn_kernel,
        grid=(b, s // Q_BLOCK),
        in_specs=[pl.BlockSpec(memory_space=pltpu.SMEM),
                  pl.BlockSpec((1, Q_BLOCK, GROUP_W), lambda bi, i: (bi, i, 0)),
                  pl.BlockSpec((1, Q_BLOCK, 3 * GROUP_HEADS), lambda bi, i: (bi, i, 0)),
                  full(k_cmp), full(v_cmp), full(ks), full(vs_aug), full(kw), full(vw_aug),
                  fixed(bnear), fixed(bwin), fixed(bcmp)],
        out_specs=pl.BlockSpec((1, Q_BLOCK, GROUP_W), lambda bi, i: (bi, i, 0)),
        out_shape=jax.ShapeDtypeStruct((b, s, GROUP_W), F32),
        scratch_shapes=[pltpu.VMEM((GROUP_HEADS * Q_BLOCK, 1), F32),
                        pltpu.VMEM((GROUP_HEADS * Q_BLOCK, 2 * HEAD_DIM), F32)],
        compiler_params=pltpu.CompilerParams(dimension_semantics=("arbitrary", "arbitrary"),
                                             vmem_limit_bytes=VMEM_LIMIT),
        name="nsa_attention",
    )(t5_table, q, gates, k_cmp, v_cmp, ks, vs_aug, kw, vw_aug, bnear, bwin, bcmp)


def _front_pad(x, rows):
    return jnp.pad(x, ((0, 0), (rows, 0), (0, 0)))


def _with_ones(v):
    b, s, d = v.shape
    pad = jnp.zeros((b, s, d), BF16).at[:, :, 0].set(1.0)
    return jnp.concatenate([v.astype(BF16), pad], axis=-1)


def nsa_group(h, w_cmp_k, w_cmp_v, cmp_pos, t5_table, bias_tiles=None):
    if bias_tiles is None:
        bias_tiles = nsa_bias_tiles(t5_table)
    b, s, _ = h.shape
    d = HEAD_DIM
    q = h[..., :GROUP_W]
    kc, vc, ks, vs, kw, vw = [h[..., GROUP_W + j * d: GROUP_W + (j + 1) * d] for j in range(6)]
    gates = h[..., GROUP_W + 6 * d:]
    n = s // CMP_STRIDE
    x = jnp.stack([kc, vc]).reshape(2, b, n, CMP_STRIDE * d)
    pos = cmp_pos.reshape(1, 2, CMP_STRIDE * d)
    w = jnp.stack([w_cmp_k, w_cmp_v]).reshape(2, 2, CMP_STRIDE * d, d).astype(BF16)
    cmp = nsa_compress(x, pos, w)
    return nsa_attention(t5_table, bias_tiles, q, gates, cmp[0], cmp[1],
                         _front_pad(ks.astype(BF16), NEAR_PAD), _front_pad(_with_ones(vs), NEAR_PAD),
                         _front_pad(kw.astype(BF16), WINDOW), _front_pad(_with_ones(vw), WINDOW))


PEER_HEADS = 8
PEER_DK = 256
N_KEYS = 128
PEER_TOPK = 16
PEER_SLOTS = PEER_HEADS * PEER_TOPK
ROUTE_TILE = 128
GATHER_TILE = 8


def _top_rows(work, order, n_top, payload=None):
    vals, args = [], []
    for _ in range(n_top):
        mx = jnp.max(work, axis=0, keepdims=True)
        first = jnp.min(jnp.where(work == mx, order, jnp.int32(2 ** 30)), axis=0, keepdims=True)
        pick = order == first
        vals.append(mx)
        if payload is None:
            args.append(first)
        else:
            args.append(jnp.sum(jnp.where(pick, payload, 0), axis=0, keepdims=True))
        work = jnp.where(pick, -jnp.inf, work)
    return jnp.concatenate(vals, axis=0), jnp.concatenate(args, axis=0)


def _pair_rows(s1, s2, combine):
    k = PEER_TOPK
    rows = [combine(s1[0:1], s2)]
    for a in range(1, k // 2):
        rows.append(combine(s1[a:a + 1], s2[0:k // 2]))
    rows.append(combine(s1[k // 2:], s2[0:1]))
    return jnp.concatenate(rows, axis=0)


def _peer_route_kernel(h_ref, g_ref, wq_ref, keys_ref, xn_ref, eidx_ref, gate_ref):
    x = h_ref[...]
    xn = x * lax.rsqrt(jnp.mean(x * x, axis=-1, keepdims=True) + NORM_EPS) * g_ref[...]
    xn_ref[...] = xn
    q = _dot(xn.astype(BF16), wq_ref[...]).astype(BF16)
    tq = x.shape[0]
    half = PEER_DK // 2
    n_iota = lax.broadcasted_iota(jnp.int32, (N_KEYS, tq), 0)
    k_iota = lax.broadcasted_iota(jnp.int32, (PEER_TOPK, tq), 0)
    order = _pair_rows(k_iota, k_iota, lambda a, b: a * PEER_TOPK + b)
    reachable = (order // PEER_TOPK + 1) * (order % PEER_TOPK + 1) <= PEER_TOPK
    for p in range(PEER_HEADS):
        tops = []
        for c in range(2):
            g = 2 * p + c
            sc = _dot_nt(keys_ref[g], q[:, g * half:(g + 1) * half])
            tops.append(_top_rows(sc, n_iota, PEER_TOPK))
        (s1, i1), (s2, i2) = tops
        cand = jnp.where(reachable, _pair_rows(s1, s2, lambda a, b: a + b), -jnp.inf)
        cidx = _pair_rows(i1, i2, lambda a, b: a * N_KEYS + b)
        top, eidx = _top_rows(cand, order, PEER_TOPK, payload=cidx)
        e = jnp.exp(top - top[0:1])
        eidx_ref[p * PEER_TOPK:(p + 1) * PEER_TOPK, :] = eidx
        gate_ref[p * PEER_TOPK:(p + 1) * PEER_TOPK, :] = e * (1.0 / jnp.sum(e, axis=0, keepdims=True))


def peer_route(h, g, w_q, keys):
    t, d = h.shape
    nq = w_q.shape[1]
    return pl.pallas_call(
        _peer_route_kernel,
        grid=(t // ROUTE_TILE,),
        in_specs=[pl.BlockSpec((ROUTE_TILE, d), lambda i: (i, 0)),
                  pl.BlockSpec((1, d), lambda i: (0, 0)),
                  pl.BlockSpec((d, nq), lambda i: (0, 0)),
                  pl.BlockSpec(keys.shape, lambda i: (0, 0, 0))],
        out_specs=[pl.BlockSpec((ROUTE_TILE, d), lambda i: (i, 0)),
                   pl.BlockSpec((PEER_SLOTS, ROUTE_TILE), lambda i: (0, i)),
                   pl.BlockSpec((PEER_SLOTS, ROUTE_TILE), lambda i: (0, i))],
        out_shape=[jax.ShapeDtypeStruct((t, d), F32),
                   jax.ShapeDtypeStruct((PEER_SLOTS, t), jnp.int32),
                   jax.ShapeDtypeStruct((PEER_SLOTS, t), F32)],
        compiler_params=pltpu.CompilerParams(dimension_semantics=("arbitrary",),
                                             vmem_limit_bytes=VMEM_LIMIT),
        name="peer_route",
    )(h, g.reshape(1, d), w_q, keys)


SUBLANES = 8
SLAB = SUBLANES
ISSUE_UNROLL = 8


def _gelu(x):
    return 0.5 * x * (1.0 + lax.erf(x * (2.0 ** -0.5)))


BITREV8 = (0, 4, 2, 6, 1, 5, 3, 7)


def _fold_pair(a, b, half, sub):
    low = (sub % (2 * half)) < half
    partner = jnp.where(low, pltpu.roll(a, SUBLANES - half, axis=0), pltpu.roll(b, half, axis=0))
    return jnp.where(low, a, b) + partner


GATHER_DEPTH = 3
DMA_QUEUES = 2


def _peer_gather_kernel(idx0_ref, idx1_ref, idx2_ref, xn_ref, h_ref, gate_ref, uv_hbm, o_ref,
                        buf_a, buf_b, buf_c, w_scr, sem):
    i = pl.program_id(0)
    n = pl.num_programs(0)
    tt = xn_ref.shape[0]
    n_slabs = tt * PEER_SLOTS
    bufs = (buf_a, buf_b, buf_c)

    def request(ref, k, r, queue):
        src = pl.multiple_of(ref[0, 0, r] * SLAB, SLAB)
        pltpu.make_async_copy(uv_hbm.at[pl.ds(src, SLAB)], bufs[k].at[pl.ds(r * SLAB, SLAB)],
                              sem.at[k]).start(priority=queue)

    def request_tile(ref, k):
        def body(c, carry):
            for j in range(ISSUE_UNROLL):
                request(ref, k, pl.multiple_of(c * ISSUE_UNROLL, ISSUE_UNROLL) + j, j % DMA_QUEUES)
            return carry
        lax.fori_loop(0, n_slabs // ISSUE_UNROLL, body, 0)

    def wait_tile(k):
        pltpu.make_async_copy(uv_hbm.at[pl.ds(0, n_slabs * SLAB)], bufs[k], sem.at[k]).wait()

    @pl.when(i == 0)
    def _():
        request_tile(idx0_ref, 0)
        request_tile(idx1_ref, 1)

    def step(cur, ahead):
        wait_tile(cur)
        gates = gate_ref[0]
        sub = lax.broadcasted_iota(jnp.int32, (SUBLANES, LANE), 0)
        for t in range(tt):
            for j in range(PEER_SLOTS):
                request(idx2_ref, ahead, t * PEER_SLOTS + j, j % DMA_QUEUES)
            x = xn_ref[t]

            def slab(pos, half):
                w = bufs[cur][pl.ds((t * PEER_SLOTS + pos) * SLAB, SLAB), :]
                bits = w & jnp.uint32(0xFFFF0000) if half else w << 16
                return lax.bitcast_convert_type(bits, F32)

            groups = []
            for g in range(PEER_SLOTS // SUBLANES):
                p = [slab(g * SUBLANES + k, 0) * x for k in range(SUBLANES)]
                c = [_fold_pair(p[2 * k], p[2 * k + 1], 4, sub) for k in range(4)]
                d = [_fold_pair(c[0], c[1], 2, sub), _fold_pair(c[2], c[3], 2, sub)]
                groups.append(_fold_pair(d[0], d[1], 1, sub))
            act = jnp.sum(jnp.concatenate(groups, axis=0), axis=-1, keepdims=True)
            w_scr[...] = jnp.broadcast_to(gates[:, t:t + 1] * _gelu(act), (PEER_SLOTS, LANE))
            acc = h_ref[t]
            for g in range(PEER_SLOTS // SUBLANES):
                for k in range(SUBLANES):
                    w_row = w_scr[pl.ds(g * SUBLANES + BITREV8[k], 1), :]
                    acc = acc + slab(g * SUBLANES + k, 1) * w_row
            o_ref[t] = acc

    for k in range(GATHER_DEPTH):
        @pl.when(i % GATHER_DEPTH == k)
        def _(k=k):
            step(k, (k + GATHER_DEPTH - 1) % GATHER_DEPTH)

    @pl.when(i == n - 1)
    def _():
        for k in range(GATHER_DEPTH):
            @pl.when(i % GATHER_DEPTH == k)
            def _(k=k):
                wait_tile((k + 1) % GATHER_DEPTH)
                wait_tile((k + 2) % GATHER_DEPTH)


def peer_gather(idx, xn, h, gate_b, uv):
    t = xn.shape[0]
    nt = t // GATHER_TILE
    n_slabs = GATHER_TILE * PEER_SLOTS
    tok = pl.BlockSpec((GATHER_TILE, SUBLANES, LANE), lambda i: (i, 0, 0))
    idx_spec = lambda k: pl.BlockSpec((1, 1, n_slabs), lambda i: (jnp.minimum(i + k, nt - 1), 0, 0),
                                      memory_space=pltpu.SMEM)
    slab_buf = pltpu.VMEM((n_slabs * SLAB, LANE), jnp.uint32)
    return pl.pallas_call(
        _peer_gather_kernel,
        grid=(nt,),
        in_specs=[idx_spec(0), idx_spec(1), idx_spec(2), tok, tok,
                  pl.BlockSpec((1, PEER_SLOTS, GATHER_TILE), lambda i: (i, 0, 0)),
                  pl.BlockSpec(memory_space=pl.ANY)],
        out_specs=tok,
        out_shape=jax.ShapeDtypeStruct((t, SUBLANES, LANE), F32),
        scratch_shapes=[slab_buf, slab_buf, slab_buf, pltpu.VMEM((PEER_SLOTS, LANE), F32),
                        pltpu.SemaphoreType.DMA((GATHER_DEPTH,))],
        compiler_params=pltpu.CompilerParams(dimension_semantics=("arbitrary",),
                                             vmem_limit_bytes=VMEM_LIMIT),
        name="peer_gather",
    )(idx, idx, idx, xn, h, gate_b, uv)


def peer_ffn_residual(h, g, w_q, sub_keys, u_tab, v_tab):
    t, d = h.shape
    e = u_tab.shape[0]
    keys = sub_keys.reshape(PEER_HEADS * 2, N_KEYS, PEER_DK // 2).astype(BF16)
    xn, eidx_t, gate_t = peer_route(h, g, w_q.astype(BF16), keys)
    nt = t // GATHER_TILE
    order = np.arange(PEER_SLOTS).reshape(-1, SUBLANES)[:, list(BITREV8)].reshape(-1)
    idx = eidx_t[order].T.reshape(nt, 1, GATHER_TILE * PEER_SLOTS)
    gate_b = gate_t.reshape(PEER_SLOTS, nt, GATHER_TILE).transpose(1, 0, 2)
    half_bits = lambda tab: lax.bitcast_convert_type(tab.astype(BF16), jnp.uint16).astype(jnp.uint32)
    uv = ((half_bits(v_tab) << 16) | half_bits(u_tab)).reshape(e * SLAB, LANE)
    out = peer_gather(idx, xn.reshape(t, SUBLANES, LANE), h.reshape(t, SUBLANES, LANE), gate_b, uv)
    return out.reshape(t, d)


Q_LORA = 256
KV_LORA = 128
QK_NOPE = 64
QK_ROPE = 32
V_HEAD = 64
ROPE_THETA = 10000.0
MLA_TQ = 256
MLA_TK = 512


def _rms(x, g):
    return x * lax.rsqrt(jnp.mean(x * x, axis=-1, keepdims=True) + NORM_EPS) * g


def _mla_prep_kernel(pos_ref, freq_ref, cq_ref, ckv_ref, kr_ref, gq_ref, gkv_ref, wq_ref, wk_ref, wv_ref,
                     q_out, k_out, v_out):
    rows = cq_ref.shape[0]
    half = QK_ROPE // 2
    cq = _rms(cq_ref[...], gq_ref[...]).astype(BF16)
    ckv = _rms(ckv_ref[...], gkv_ref[...]).astype(BF16)
    qf = _dot(cq, wq_ref[...])
    kf = _dot(ckv, wk_ref[...])
    vf = _dot(ckv, wv_ref[...])
    lane = lax.broadcasted_iota(jnp.int32, (rows, LANE), 1)
    ang = pos_ref[...].astype(F32) * freq_ref[...]
    cos, sin = jnp.cos(ang), jnp.sin(ang)
    in_rope = (lane >= QK_NOPE) & (lane < QK_NOPE + QK_ROPE)
    first = lane < QK_NOPE + half
    c_mul = jnp.where(lane < QK_NOPE, 1.0, jnp.where(in_rope, cos, 0.0))
    s_mul = jnp.where(in_rope, jnp.where(first, -sin, sin), 0.0)

    def rope(x):
        partner = jnp.where(first, pltpu.roll(x, LANE - half, axis=1), pltpu.roll(x, half, axis=1))
        return x * c_mul + partner * s_mul

    scale = (QK_NOPE + QK_ROPE) ** -0.5
    kr = rope(jnp.concatenate([jnp.zeros((rows, QK_NOPE), F32), kr_ref[...],
                               jnp.zeros((rows, LANE - QK_NOPE - QK_ROPE), F32)], axis=1))
    ones_col = jnp.where(lane == V_HEAD, 1.0, 0.0)
    for h in range(GROUP_HEADS):
        cols = slice(h * LANE, (h + 1) * LANE)
        q_out[:, cols] = (rope(qf[:, cols]) * scale).astype(BF16)
        k_out[:, cols] = (kf[:, cols] + kr).astype(BF16)
        v_out[:, cols] = (vf[:, cols] + ones_col).astype(BF16)


def mla_prep(pos, c_q, c_kv, k_r, gq, gkv, w_uq, w_ukv):
    t = c_q.shape[0]
    H = GROUP_HEADS
    wq = jnp.pad(w_uq.reshape(Q_LORA, H, QK_NOPE + QK_ROPE), ((0, 0), (0, 0), (0, LANE - QK_NOPE - QK_ROPE)))
    wkv = w_ukv.reshape(KV_LORA, H, QK_NOPE + V_HEAD)
    wk = jnp.pad(wkv[:, :, :QK_NOPE], ((0, 0), (0, 0), (0, LANE - QK_NOPE)))
    wv = jnp.pad(wkv[:, :, QK_NOPE:], ((0, 0), (0, 0), (0, LANE - V_HEAD)))
    wq, wk, wv = [w.reshape(w.shape[0], H * LANE).astype(BF16) for w in (wq, wk, wv)]
    inv_freq = ROPE_THETA ** (-np.arange(0, QK_ROPE, 2, dtype=np.float32) / QK_ROPE)
    freq = np.zeros((1, LANE), np.float32)
    freq[0, QK_NOPE:QK_NOPE + QK_ROPE] = np.tile(inv_freq, 2)
    row = lambda n: pl.BlockSpec((ROW_TILE, n), lambda i: (i, 0))
    fixed = lambda a: pl.BlockSpec(a.shape, lambda i: (0,) * a.ndim)
    gq2, gkv2, freq = gq.reshape(1, -1), gkv.reshape(1, -1), jnp.asarray(freq)
    out = jax.ShapeDtypeStruct((t, H * LANE), BF16)
    return pl.pallas_call(
        _mla_prep_kernel,
        grid=(t // ROW_TILE,),
        in_specs=[row(1), fixed(freq), row(Q_LORA), row(KV_LORA), row(QK_ROPE), fixed(gq2), fixed(gkv2),
                  fixed(wq), fixed(wk), fixed(wv)],
        out_specs=[row(H * LANE)] * 3,
        out_shape=[out] * 3,
        compiler_params=pltpu.CompilerParams(dimension_semantics=("arbitrary",)),
        name="mla_prep",
    )(pos, freq, c_q, c_kv, k_r, gq2, gkv2, wq, wk, wv)


def _mla_attn_kernel(q_ref, k_ref, v_ref, o_ref, m_scr, acc_scr):
    i = pl.program_id(1)
    tq = q_ref.shape[1]
    row = i * tq + lax.broadcasted_iota(jnp.int32, (tq, MLA_TK), 0)
    col = lax.broadcasted_iota(jnp.int32, (tq, MLA_TK), 1)
    n_full = (i * tq) // MLA_TK
    heads = range(GROUP_HEADS)
    cols = [slice(h * LANE, (h + 1) * LANE) for h in heads]
    q = [q_ref[0, :, cols[h]] for h in heads]
    m_scr[...] = jnp.full(m_scr.shape, NEG, F32)
    acc_scr[...] = jnp.zeros(acc_scr.shape, F32)

    def step(j, masked):
        start = pl.multiple_of(j * MLA_TK, MLA_TK)
        s = [_dot_nt(q[h], k_ref[0, pl.ds(start, MLA_TK), cols[h]]) for h in heads]
        if masked:
            s = [jnp.where(start + col <= row, sh, NEG) for sh in s]
        m_old = [m_scr[h] for h in heads]
        m_new = [jnp.maximum(m_old[h], jnp.max(s[h], axis=-1, keepdims=True)) for h in heads]
        p = [jnp.exp(s[h] - m_new[h]).astype(BF16) for h in heads]
        pv = [_dot(p[h], v_ref[0, pl.ds(start, MLA_TK), cols[h]]) for h in heads]
        for h in heads:
            acc_scr[h] = jnp.exp(m_old[h] - m_new[h]) * acc_scr[h] + pv[h]
            m_scr[h] = m_new[h]

    def body(j, carry):
        step(j, False)
        return carry

    lax.fori_loop(0, n_full, body, 0)
    step(n_full, True)
    outs = [acc_scr[h][:, :V_HEAD] * (1.0 / acc_scr[h][:, V_HEAD:V_HEAD + 1]) for h in heads]
    o_ref[0] = jnp.concatenate(outs, axis=1)


def mla_attention(q, k, v):
    b, s, w = q.shape
    return pl.pallas_call(
        _mla_attn_kernel,
        grid=(b, s // MLA_TQ),
        in_specs=[pl.BlockSpec((1, MLA_TQ, w), lambda bi, i: (bi, i, 0)),
                  pl.BlockSpec((1, s, w), lambda bi, i: (bi, 0, 0)),
                  pl.BlockSpec((1, s, w), lambda bi, i: (bi, 0, 0))],
        out_specs=pl.BlockSpec((1, MLA_TQ, GROUP_HEADS * V_HEAD), lambda bi, i: (bi, i, 0)),
        out_shape=jax.ShapeDtypeStruct((b, s, GROUP_HEADS * V_HEAD), F32),
        scratch_shapes=[pltpu.VMEM((GROUP_HEADS, MLA_TQ, 1), F32), pltpu.VMEM((GROUP_HEADS, MLA_TQ, LANE), F32)],
        compiler_params=pltpu.CompilerParams(dimension_semantics=("arbitrary", "arbitrary"),
                                             vmem_limit_bytes=VMEM_LIMIT),
        name="mla_attention",
    )(q, k, v)


def mla_group(h, positions, q_norm_g, kv_norm_g, w_uq, w_ukv):
    b, s, _ = h.shape
    t = b * s
    h2 = h.reshape(t, -1)
    c_q, c_kv, k_r = h2[:, :Q_LORA], h2[:, Q_LORA:Q_LORA + KV_LORA], h2[:, Q_LORA + KV_LORA:]
    q, k, v = mla_prep(positions.reshape(t, 1), c_q, c_kv, k_r, q_norm_g, kv_norm_g, w_uq, w_ukv)
    w = GROUP_HEADS * LANE
    return mla_attention(q.reshape(b, s, w), k.reshape(b, s, w), v.reshape(b, s, w))


CHUNK = 64
CONV_W = 4


def _split3(a):
    hi = a.astype(BF16)
    r = a - hi.astype(F32)
    mid = r.astype(BF16)
    return hi, mid, (r - mid.astype(F32)).astype(BF16)


def _dot_exact_rhs(sel, a):
    return sum(_dot(sel, p) for p in _split3(a))


def _row_matrix(col, n_rows):
    l = col.shape[0]
    lane = lax.broadcasted_iota(jnp.int32, (l, SUBLANES), 1)
    padded = jnp.where(lane == 0, col, 0.0)
    ones = jnp.ones((n_rows, SUBLANES), BF16)
    return sum(_dot_nt(ones, p) for p in _split3(padded))


def _transpose_bf16(x_bf):
    d = x_bf.shape[1]
    eye = (lax.broadcasted_iota(jnp.int32, (d, d), 0) == lax.broadcasted_iota(jnp.int32, (d, d), 1)).astype(BF16)
    return _dot_nt(eye, x_bf).astype(BF16)


def _log_sigmoid(x):
    return jnp.minimum(x, 0.0) - jnp.log1p(jnp.exp(-jnp.abs(x)))


def _softplus(x):
    return jnp.maximum(x, 0.0) + jnp.log1p(jnp.exp(-jnp.abs(x)))


def _with_ones_col(v):
    l, d = v.shape
    lane = lax.broadcasted_iota(jnp.int32, (l, LANE - d), 1)
    return jnp.concatenate([v, jnp.where(lane == 0, 1.0, 0.0)], axis=1)


def _bf(x):
    return x.astype(BF16)


def _mlstm_kernel(q_ref, k_ref, v_ref, gate_ref, og_ref, bias_ref, ng_ref, o_ref, st_scr, m_scr):
    nb = q_ref.shape[0]
    H, D, L = GROUP_HEADS, HEAD_DIM, CHUNK
    chains = [(b, h) for b in range(nb) for h in range(H)]
    cols = lambda h: slice(h * D, (h + 1) * D)
    each = lambda fn: [fn(c, b, h) for c, (b, h) in enumerate(chains)]

    @pl.when(pl.program_id(0) == 0)
    def _():
        st_scr[...] = jnp.zeros(st_scr.shape, F32)
        m_scr[...] = jnp.zeros(m_scr.shape, F32)

    row = lax.broadcasted_iota(jnp.int32, (L, L), 0)
    col = lax.broadcasted_iota(jnp.int32, (L, L), 1)
    tri = row >= col
    tri_bf = tri.astype(BF16)
    gates = [gate_ref[b] + bias_ref[...] for b in range(nb)]
    bcum = [_dot_exact_rhs(tri_bf, _log_sigmoid(g[:, H:])) for g in gates]
    q = each(lambda c, b, h: _bf(q_ref[b, :, cols(h)]))
    kt = each(lambda c, b, h: _transpose_bf16(_bf(k_ref[b, :, cols(h)] * (D ** -0.5))))
    v_aug = each(lambda c, b, h: _with_ones_col(v_ref[b, :, cols(h)]))
    bc = each(lambda c, b, h: bcum[b][:, h:h + 1])
    ic = each(lambda c, b, h: gates[b][:, h:h + 1])
    m_st = each(lambda c, b, h: m_scr[c, 0:1, 0:1])
    st = each(lambda c, b, h: st_scr[c])
    log_d = each(lambda c, b, h: jnp.where(tri, bc[c] + _row_matrix(ic[c] - bc[c], L), -jnp.inf))
    m_t = each(lambda c, b, h: jnp.maximum(jnp.max(log_d[c], axis=-1, keepdims=True), bc[c] + m_st[c]))
    sqk = each(lambda c, b, h: _dot(q[c], kt[c]) * jnp.exp(log_d[c] - m_t[c]))
    inter = each(lambda c, b, h: jnp.exp(bc[c] + m_st[c] - m_t[c]))
    tot = each(lambda c, b, h: _dot(_bf(sqk[c]), _bf(v_aug[c])) + inter[c] * _dot(q[c], _bf(st[c])))
    h_t = each(lambda c, b, h: tot[c][:, :D] * (1.0 / jnp.maximum(jnp.abs(tot[c][:, D:D + 1]), jnp.exp(-m_t[c]))))
    b_end = each(lambda c, b, h: bc[c][L - 1:L])
    log_w = each(lambda c, b, h: b_end[c] - bc[c] + ic[c])
    m_new = each(lambda c, b, h: jnp.maximum(b_end[c] + m_st[c], jnp.max(log_w[c], axis=0, keepdims=True)))
    upd = each(lambda c, b, h: _dot(kt[c], _bf(jnp.exp(log_w[c] - m_new[c]) * v_aug[c])))
    for c in range(len(chains)):
        st_scr[c] = jnp.exp(b_end[c] + m_st[c] - m_new[c]) * st[c] + upd[c]
        m_scr[c] = jnp.broadcast_to(m_new[c], m_scr.shape[1:])
    for b in range(nb):
        outs = []
        for h in range(H):
            o = jax.nn.sigmoid(og_ref[b, :, cols(h)]) * h_t[b * H + h]
            outs.append(o * lax.rsqrt(jnp.mean(o * o, axis=-1, keepdims=True) + NORM_EPS) * ng_ref[:, cols(h)])
        o_ref[b] = jnp.concatenate(outs, axis=1)


def mlstm_group(h, i_bias, f_bias, norm_g):
    b, s, _ = h.shape
    W, H = GROUP_W, GROUP_HEADS
    q, k, v = h[..., :W], h[..., W:2 * W], h[..., 2 * W:3 * W]
    gates = h[..., 3 * W:3 * W + 2 * H]
    og = h[..., 3 * W + 2 * H:]
    bias = jnp.concatenate([i_bias, f_bias]).reshape(1, 2 * H)
    tok = lambda n: pl.BlockSpec((b, CHUNK, n), lambda i: (0, i, 0))
    fixed = lambda n: pl.BlockSpec((1, n), lambda i: (0, 0))
    return pl.pallas_call(
        _mlstm_kernel,
        grid=(s // CHUNK,),
        in_specs=[tok(W), tok(W), tok(W), tok(2 * H), tok(W), fixed(2 * H), fixed(W)],
        out_specs=tok(W),
        out_shape=jax.ShapeDtypeStruct((b, s, W), F32),
        scratch_shapes=[pltpu.VMEM((b * H, HEAD_DIM, LANE), F32), pltpu.VMEM((b * H, SUBLANES, LANE), F32)],
        compiler_params=pltpu.CompilerParams(dimension_semantics=("arbitrary",)),
        name="mlstm",
    )(q, k, v, gates, og, bias, norm_g.reshape(1, W))


def _gdn_kernel(x_ref, xprev_ref, cw_ref, gate_ref, z_ref, par_ref, ng_ref, o_ref, st_scr):
    nb = x_ref.shape[0]
    H, D, L = GROUP_HEADS, HEAD_DIM, CHUNK
    W = H * D
    chains = [(b, h) for b in range(nb) for h in range(H)]
    each = lambda fn: [fn(c, b, h) for c, (b, h) in enumerate(chains)]
    first = pl.program_id(0) == 0

    @pl.when(first)
    def _():
        st_scr[...] = jnp.zeros(st_scr.shape, F32)

    row = lax.broadcasted_iota(jnp.int32, (L, L), 0)
    col = lax.broadcasted_iota(jnp.int32, (L, L), 1)
    tri = row >= col
    tri_bf = tri.astype(BF16)
    eye = (row == col).astype(F32)
    par = par_ref[...]
    qkv, beta4, gc4 = [], [], []
    for b in range(nb):
        prev = jnp.where(first, 0.0, xprev_ref[b])
        full = jnp.concatenate([prev, x_ref[b]], axis=0)
        acc = full[SUBLANES:] * cw_ref[CONV_W - 1:CONV_W, :]
        for j in range(CONV_W - 1):
            shifted = pltpu.roll(full, CONV_W - 1 - j, axis=0)[SUBLANES:]
            acc = acc + shifted * cw_ref[j:j + 1, :]
        qkv.append(acc * jax.nn.sigmoid(acc))
        gates = gate_ref[b]
        beta4.append(jax.nn.sigmoid(gates[:, :H]))
        gc4.append(_dot_exact_rhs(tri_bf, -jnp.exp(par[:, :H]) * _softplus(gates[:, H:] + par[:, H:])))

    def l2n(t):
        return t * lax.rsqrt(jnp.sum(t * t, axis=-1, keepdims=True) + NORM_EPS)

    q = each(lambda c, b, h: l2n(qkv[b][:, h * D:(h + 1) * D]) * (D ** -0.5))
    k = each(lambda c, b, h: l2n(qkv[b][:, W + h * D:W + (h + 1) * D]))
    v = each(lambda c, b, h: qkv[b][:, 2 * W + h * D:2 * W + (h + 1) * D])
    beta = each(lambda c, b, h: beta4[b][:, h:h + 1])
    gc = each(lambda c, b, h: gc4[b][:, h:h + 1])
    gc_row = each(lambda c, b, h: _row_matrix(gc[c], L))
    decay = each(lambda c, b, h: jnp.where(tri, jnp.exp(gc[c] - gc_row[c]), 0.0))
    kt = each(lambda c, b, h: _transpose_bf16(_bf(k[c])))
    kb = each(lambda c, b, h: k[c] * beta[c])
    x = each(lambda c, b, h: -jnp.where(row > col, _dot(_bf(kb[c]), kt[c]) * decay[c], 0.0))
    t_inv = [eye + xc for xc in x]
    for _ in range(5):
        x = [_dot(_bf(xc), _bf(xc)) for xc in x]
        t_inv = [tc + _dot(_bf(tc), _bf(xc)) for tc, xc in zip(t_inv, x)]
    u = each(lambda c, b, h: _dot(_bf(t_inv[c]), _bf(v[c] * beta[c])))
    wm = each(lambda c, b, h: _dot(_bf(t_inv[c]), _bf(kb[c] * jnp.exp(gc[c]))))
    attn = each(lambda c, b, h: _dot(_bf(q[c]), kt[c]) * decay[c])
    st = each(lambda c, b, h: st_scr[c])
    v_new = each(lambda c, b, h: u[c] - _dot(_bf(wm[c]), _bf(st[c])))
    o = each(lambda c, b, h: _dot(_bf(q[c] * jnp.exp(gc[c])), _bf(st[c])) + _dot(_bf(attn[c]), _bf(v_new[c])))
    g_end = each(lambda c, b, h: gc[c][L - 1:L])
    upd = each(lambda c, b, h: _dot(_bf(kt[c].astype(F32) * jnp.exp(g_end[c] - gc_row[c])), _bf(v_new[c])))
    for c in range(len(chains)):
        st_scr[c] = st[c] * jnp.exp(g_end[c]) + upd[c]
    for b in range(nb):
        outs = []
        for h in range(H):
            oc = o[b * H + h]
            z = z_ref[b, :, h * D:(h + 1) * D]
            y = oc * lax.rsqrt(jnp.mean(oc * oc, axis=-1, keepdims=True) + NORM_EPS) * ng_ref[...]
            outs.append(y * (z * jax.nn.sigmoid(z)))
        o_ref[b] = jnp.concatenate(outs, axis=1)


def gdn_group(h, conv_w, a_log, dt_bias, norm_g):
    b, s, _ = h.shape
    W, H = GROUP_W, GROUP_HEADS
    x = h[..., :3 * W]
    gates = h[..., 3 * W:3 * W + 2 * H]
    z = h[..., 3 * W + 2 * H:]
    par = jnp.concatenate([a_log, dt_bias]).reshape(1, 2 * H)
    tok = lambda n: pl.BlockSpec((b, CHUNK, n), lambda i: (0, i, 0))
    per_chunk = CHUNK // SUBLANES
    prev = pl.BlockSpec((b, SUBLANES, 3 * W), lambda i: (0, jnp.maximum(i * per_chunk - 1, 0), 0))
    fixed = lambda r, n: pl.BlockSpec((r, n), lambda i: (0, 0))
    return pl.pallas_call(
        _gdn_kernel,
        grid=(s // CHUNK,),
        in_specs=[tok(3 * W), prev, fixed(CONV_W, 3 * W), tok(2 * H), tok(W), fixed(1, 2 * H), fixed(1, HEAD_DIM)],
        out_specs=tok(W),
        out_shape=jax.ShapeDtypeStruct((b, s, W), F32),
        scratch_shapes=[pltpu.VMEM((b * H, HEAD_DIM, HEAD_DIM), F32)],
        compiler_params=pltpu.CompilerParams(dimension_semantics=("arbitrary",)),
        name="gdn",
    )(x, x, conv_w, gates, z, par, norm_g.reshape(1, HEAD_DIM))


NSA_IN = GROUP_W + 6 * HEAD_DIM + 3 * GROUP_HEADS
MLSTM_IN = 4 * GROUP_W + 2 * GROUP_HEADS
GDN_IN = 4 * GROUP_W + 2 * GROUP_HEADS
MLA_IN = Q_LORA + KV_LORA + QK_ROPE


def kernel(x, positions, t5_table, norm1_g, norm2_g, final_norm_g, w_in, w_out, nsa_w_cmp_k, nsa_w_cmp_v, nsa_cmp_pos, mlstm_i_bias, mlstm_f_bias, mlstm_norm_g, gdn_conv_w, gdn_a_log, gdn_dt_bias, gdn_norm_g, mla_q_norm_g, mla_kv_norm_g, mla_w_uq, mla_w_ukv, peer_w_q, peer_sub_keys, peer_u, peer_v):
    b, s, d = x.shape
    t = b * s
    h = x.reshape(t, d)
    depth = w_in.shape[0]
    d_in = w_in.shape[2]
    d_in_pad = _round_up(d_in, LANE)
    bias_tiles = nsa_bias_tiles(t5_table)
    for l in range(depth):
        w_in_l = jnp.pad(w_in[l], ((0, 0), (0, d_in_pad - d_in))).astype(BF16)
        hp = norm_matmul(h, norm1_g[l], w_in_l)[:, :d_in].reshape(b, s, d_in)
        c0, c1, c2 = NSA_IN, NSA_IN + MLSTM_IN, NSA_IN + MLSTM_IN + GDN_IN
        h_nsa, h_ml, h_gdn, h_mla = hp[..., :c0], hp[..., c0:c1], hp[..., c1:c2], hp[..., c2:]
        o = jnp.concatenate([
            nsa_group(h_nsa, nsa_w_cmp_k[l], nsa_w_cmp_v[l], nsa_cmp_pos[l], t5_table, bias_tiles),
            mlstm_group(h_ml, mlstm_i_bias[l], mlstm_f_bias[l], mlstm_norm_g[l]),
            gdn_group(h_gdn, gdn_conv_w[l], gdn_a_log[l], gdn_dt_bias[l], gdn_norm_g[l]),
            mla_group(h_mla, positions, mla_q_norm_g[l], mla_kv_norm_g[l], mla_w_uq[l], mla_w_ukv[l]),
        ], axis=-1)
        h = matmul_residual(o.reshape(t, -1), w_out[l].astype(BF16), h)
        h = peer_ffn_residual(h, norm2_g[l], peer_w_q[l], peer_sub_keys[l], peer_u[l], peer_v[l])
    return rmsnorm(h, final_norm_g).reshape(b, s, d)
```

```python
import functools
import math

import jax
import jax.numpy as jnp
import numpy as np
from jax import lax
from jax.experimental import pallas as pl
from jax.experimental.pallas import tpu as pltpu

F32 = jnp.float32
BF16 = jnp.bfloat16
NORM_EPS = 1e-6
NEG = -1e30
LANE = 128
ROW_TILE = 512
VMEM_LIMIT = 56 * 1024 * 1024

HEAD_DIM = 64
GROUP_HEADS = 4
GROUP_W = GROUP_HEADS * HEAD_DIM
Q_BLOCK = 128
CMP_LEN = 32
CMP_STRIDE = 16
SEL_LEN = 64
SEL_TOPN = 16
WINDOW = 512
SEL_FORCE = 1e3
N_BUCKETS = 32
MAX_DISTANCE = 128


def _round_up(n, m):
    return (n + m - 1) // m * m


def _dot_nt(a, b):
    return lax.dot_general(a, b, (((1,), (1,)), ((), ())), preferred_element_type=F32)


def _dot(a, b):
    return jnp.dot(a, b, preferred_element_type=F32)


def _norm_matmul_kernel(x_ref, g_ref, w_ref, o_ref):
    x = x_ref[...]
    y = x * lax.rsqrt(jnp.mean(x * x, axis=-1, keepdims=True) + NORM_EPS) * g_ref[...]
    o_ref[...] = _dot(y.astype(BF16), w_ref[...])


def norm_matmul(x, g, w):
    t, d = x.shape
    n = w.shape[1]
    return pl.pallas_call(
        _norm_matmul_kernel,
        grid=(t // ROW_TILE,),
        in_specs=[pl.BlockSpec((ROW_TILE, d), lambda i: (i, 0)),
                  pl.BlockSpec((1, d), lambda i: (0, 0)),
                  pl.BlockSpec((d, n), lambda i: (0, 0))],
        out_specs=pl.BlockSpec((ROW_TILE, n), lambda i: (i, 0)),
        out_shape=jax.ShapeDtypeStruct((t, n), F32),
        compiler_params=pltpu.CompilerParams(dimension_semantics=("arbitrary",),
                                             vmem_limit_bytes=VMEM_LIMIT),
        name="norm_matmul",
    )(x, g.reshape(1, d), w)


def _matmul_residual_kernel(o_ref, w_ref, r_ref, out_ref):
    out_ref[...] = r_ref[...] + _dot(o_ref[...].astype(BF16), w_ref[...])


def matmul_residual(o, w, res):
    t, d = o.shape
    n = w.shape[1]
    return pl.pallas_call(
        _matmul_residual_kernel,
        grid=(t // ROW_TILE,),
        in_specs=[pl.BlockSpec((ROW_TILE, d), lambda i: (i, 0)),
                  pl.BlockSpec((d, n), lambda i: (0, 0)),
                  pl.BlockSpec((ROW_TILE, n), lambda i: (i, 0))],
        out_specs=pl.BlockSpec((ROW_TILE, n), lambda i: (i, 0)),
        out_shape=jax.ShapeDtypeStruct((t, n), F32),
        compiler_params=pltpu.CompilerParams(dimension_semantics=("arbitrary",),
                                             vmem_limit_bytes=VMEM_LIMIT),
        name="matmul_residual",
    )(o, w, res)


def _rmsnorm_kernel(x_ref, g_ref, o_ref):
    x = x_ref[...]
    o_ref[...] = x * lax.rsqrt(jnp.mean(x * x, axis=-1, keepdims=True) + NORM_EPS) * g_ref[...]


def rmsnorm(x, g):
    t, d = x.shape
    return pl.pallas_call(
        _rmsnorm_kernel,
        grid=(t // ROW_TILE,),
        in_specs=[pl.BlockSpec((ROW_TILE, d), lambda i: (i, 0)),
                  pl.BlockSpec((1, d), lambda i: (0, 0))],
        out_specs=pl.BlockSpec((ROW_TILE, d), lambda i: (i, 0)),
        out_shape=jax.ShapeDtypeStruct((t, d), F32),
        compiler_params=pltpu.CompilerParams(dimension_semantics=("arbitrary",)),
        name="rmsnorm",
    )(x, g.reshape(1, d))


def _t5_bucket_starts():
    d = np.arange(0, 2 * MAX_DISTANCE)
    max_exact = N_BUCKETS // 2
    val = (np.log(np.maximum(d, 1).astype(np.float32) / np.float32(max_exact))
           / np.float32(math.log(MAX_DISTANCE / max_exact)) * np.float32(N_BUCKETS - max_exact))
    large = np.minimum(max_exact + val.astype(np.int32), N_BUCKETS - 1)
    bucket = np.where(d < max_exact, d, large)
    assert np.all(np.diff(bucket) >= 0)
    return [(int(k), int(d[bucket == k].min())) for k in range(N_BUCKETS) if np.any(bucket == k)]


T5_STARTS = _t5_bucket_starts()


def _t5_bias_heads(dist, tab_ref):
    accs = [jnp.full(dist.shape, tab_ref[0, h], F32) for h in range(GROUP_HEADS)]
    for bucket, start in T5_STARTS[1:]:
        m = dist >= start
        accs = [jnp.where(m, tab_ref[bucket, h], accs[h]) for h in range(GROUP_HEADS)]
    return jnp.stack(accs)


def _nsa_compress_kernel(x_ref, pos_ref, w_ref, o_ref):
    x = x_ref[0, 0]
    n = x.shape[0]
    lo = _dot((x + pos_ref[0, 0:1, :]).astype(BF16), w_ref[0, 0])
    hi = _dot((x + pos_ref[0, 1:2, :]).astype(BF16), w_ref[0, 1])
    o_ref[0, 0] = (lo + pltpu.roll(hi, n - 1, axis=0)).astype(o_ref.dtype)


def nsa_compress(x, pos, w):
    two, b, n, c = x.shape
    return pl.pallas_call(
        _nsa_compress_kernel,
        grid=(two, b),
        in_specs=[pl.BlockSpec((1, 1, n, c), lambda k, i: (k, i, 0, 0)),
                  pl.BlockSpec((1, 2, c), lambda k, i: (0, 0, 0)),
                  pl.BlockSpec((1, 2, c, HEAD_DIM), lambda k, i: (k, 0, 0, 0))],
        out_specs=pl.BlockSpec((1, 1, n, HEAD_DIM), lambda k, i: (k, i, 0, 0)),
        out_shape=jax.ShapeDtypeStruct((two, b, n, HEAD_DIM), BF16),
        compiler_params=pltpu.CompilerParams(dimension_semantics=("arbitrary", "arbitrary")),
        name="nsa_compress",
    )(x, pos, w)


FAR_TILE = 512
NEAR_KEYS = 768
WIN_KEYS = WINDOW + Q_BLOCK
NEAR_PAD = NEAR_KEYS - Q_BLOCK
CMP_TILE = LANE
CMP_TILES_BACK = CMP_TILE * CMP_STRIDE // Q_BLOCK
N_CMP_BIAS = CMP_TILES_BACK + 3


def _masked_bias(dist, valid, tab_ref):
    return jnp.where(valid, _t5_bias_heads(dist, tab_ref), NEG)


def _nsa_bias_kernel(tab_ref, near_ref, win_ref):
    q = lax.broadcasted_iota(jnp.int32, (Q_BLOCK, NEAR_KEYS), 0)
    k = lax.broadcasted_iota(jnp.int32, (Q_BLOCK, NEAR_KEYS), 1)
    dist = NEAR_PAD + q - k
    near_ref[...] = _masked_bias(dist, dist >= 0, tab_ref)
    q = lax.broadcasted_iota(jnp.int32, (Q_BLOCK, WIN_KEYS), 0)
    k = lax.broadcasted_iota(jnp.int32, (Q_BLOCK, WIN_KEYS), 1)
    dist = WINDOW + q - k
    win_ref[...] = _masked_bias(dist, (dist >= 0) & (dist < WINDOW), tab_ref)


def _nsa_cmp_bias_kernel(tab_ref, o_ref):
    e = pl.program_id(0)
    q = lax.broadcasted_iota(jnp.int32, (Q_BLOCK, CMP_TILE), 0)
    n = lax.broadcasted_iota(jnp.int32, (Q_BLOCK, CMP_TILE), 1)
    dist = (e - 1) * Q_BLOCK + q - (n * CMP_STRIDE + CMP_LEN - 1)
    o_ref[0] = _masked_bias(dist, dist >= 0, tab_ref)


def nsa_bias_tiles(t5_table):
    H, Q = GROUP_HEADS, Q_BLOCK
    smem = pl.BlockSpec(memory_space=pltpu.SMEM)
    near, win = pl.pallas_call(
        _nsa_bias_kernel, grid=(1,), in_specs=[smem],
        out_specs=[pl.BlockSpec((H, Q, NEAR_KEYS), lambda i: (0, 0, 0)),
                   pl.BlockSpec((H, Q, WIN_KEYS), lambda i: (0, 0, 0))],
        out_shape=[jax.ShapeDtypeStruct((H, Q, NEAR_KEYS), F32), jax.ShapeDtypeStruct((H, Q, WIN_KEYS), F32)],
        name="nsa_bias",
    )(t5_table)
    cmp = pl.pallas_call(
        _nsa_cmp_bias_kernel, grid=(N_CMP_BIAS,), in_specs=[smem],
        out_specs=pl.BlockSpec((1, H, Q, CMP_TILE), lambda e: (e, 0, 0, 0)),
        out_shape=jax.ShapeDtypeStruct((N_CMP_BIAS, H, Q, CMP_TILE), F32),
        name="nsa_cmp_bias",
    )(t5_table)
    return near, win, cmp


def _nsa_attn_kernel(tab_ref, q_ref, g_ref, kc_ref, vc_ref, ks_ref, vs_ref, kw_ref, vw_ref,
                     bnear_ref, bwin_ref, bcmp_ref, o_ref, m_scr, acc_scr):
    H, Q, D = GROUP_HEADS, Q_BLOCK, HEAD_DIM
    n_cmp = kc_ref.shape[1]
    n_sel = (ks_ref.shape[1] - NEAR_PAD) // SEL_LEN
    i = pl.program_id(1)
    t0 = i * Q

    q = q_ref[0] * (D ** -0.5)
    q4 = jnp.concatenate([q[:, h * D:(h + 1) * D] for h in range(H)], axis=0).astype(BF16)

    s = _dot_nt(q4, kc_ref[0]).reshape(H, Q, n_cmp)
    tiles = []
    for g in range(n_cmp // CMP_TILE):
        e = jnp.clip(i - g * CMP_TILES_BACK, -1, CMP_TILES_BACK + 1) + 1
        tiles.append(s[:, :, g * CMP_TILE:(g + 1) * CMP_TILE] + bcmp_ref[e])
    s = jnp.concatenate(tiles, axis=-1)
    t_c = t0 + lax.broadcasted_iota(jnp.int32, (Q, n_cmp), 0)
    valid_c = t_c >= lax.broadcasted_iota(jnp.int32, (Q, n_cmp), 1) * CMP_STRIDE + CMP_LEN - 1
    p = jnp.where(valid_c, jnp.exp(s - jnp.max(s, axis=-1, keepdims=True)), 0.0)
    l = jnp.sum(p, axis=-1, keepdims=True)
    p = p * (1.0 / jnp.where(l > 0.0, l, 1.0))
    o_c = _dot(p.reshape(H * Q, n_cmp).astype(BF16), vc_ref[0])

    psum = p[0] + p[1] + p[2] + p[3]
    c_idx = lax.broadcasted_iota(jnp.int32, (n_cmp, n_sel), 0)
    j_idx = lax.broadcasted_iota(jnp.int32, (n_cmp, n_sel), 1)
    ratio = SEL_LEN // CMP_STRIDE
    n_in_sel = (SEL_LEN - CMP_LEN) // CMP_STRIDE + 1
    pool = ((c_idx // ratio == j_idx) & (c_idx % ratio < n_in_sel)).astype(BF16)
    p_hi = psum.astype(BF16)
    p_mid = (psum - p_hi.astype(F32)).astype(BF16)
    p_lo = (psum - p_hi.astype(F32) - p_mid.astype(F32)).astype(BF16)
    imp = _dot(p_hi, pool) + _dot(p_mid, pool) + _dot(p_lo, pool)

    jb = lax.broadcasted_iota(jnp.int32, (Q, n_sel), 1)
    cur = (t0 + lax.broadcasted_iota(jnp.int32, (Q, n_sel), 0)) // SEL_LEN
    forced = (jb == 0) | (jb == cur) | (jb == cur - 1)
    work = jnp.where(jb <= cur, imp + jnp.where(forced, SEL_FORCE, 0.0), NEG)
    sel = jnp.zeros((Q, n_sel), F32)
    jb_f = jb.astype(F32)
    for _ in range(min(SEL_TOPN, n_sel)):
        mx = jnp.max(work, axis=-1, keepdims=True)
        first = jnp.min(jnp.where(work == mx, jb_f, jnp.inf), axis=-1, keepdims=True)
        pick = jb_f == first
        sel = jnp.where(pick & (mx > 0.5 * NEG), 1.0, sel)
        work = jnp.where(pick, -jnp.inf, work)
    sel_bf = sel.astype(BF16)

    far_bias = jnp.stack([jnp.full((Q, 1), tab_ref[N_BUCKETS - 1, h], F32) for h in range(H)])
    m_scr[...] = jnp.full(m_scr.shape, NEG, F32)
    acc_scr[...] = jnp.zeros(acc_scr.shape, F32)

    def sel_step(row0, n_keys, bias, first_key):
        k = ks_ref[0, pl.ds(row0, n_keys), :]
        v = vs_ref[0, pl.ds(row0, n_keys), :]
        s = _dot_nt(q4, k).reshape(H, Q, n_keys) + bias
        pos = row0 - NEAR_PAD + lax.broadcasted_iota(jnp.int32, (n_sel, n_keys), 1)
        blk = jnp.where(pos >= first_key, pos // SEL_LEN, -1)
        expand = (lax.broadcasted_iota(jnp.int32, (n_sel, n_keys), 0) == blk).astype(BF16)
        mask = _dot(sel_bf, expand) > 0.5
        s = jnp.where(mask, s, NEG)
        m_old = m_scr[...].reshape(H, Q, 1)
        m_new = jnp.maximum(m_old, jnp.max(s, axis=-1, keepdims=True))
        p = jnp.where(mask, jnp.exp(s - m_new), 0.0).reshape(H * Q, n_keys)
        alpha = jnp.exp(m_old - m_new).reshape(H * Q, 1)
        acc_scr[...] = alpha * acc_scr[...] + _dot(p.astype(BF16), v)
        m_scr[...] = m_new.reshape(H * Q, 1)

    n_far = jnp.maximum(i - 1, 0) // (FAR_TILE // Q)

    def far_body(kp, carry):
        sel_step(pl.multiple_of(NEAR_PAD + kp * (2 * FAR_TILE), Q), 2 * FAR_TILE, far_bias, 0)
        return carry

    lax.fori_loop(0, n_far // 2, far_body, 0)

    @pl.when(n_far % 2 == 1)
    def _():
        sel_step(pl.multiple_of(NEAR_PAD + (n_far - 1) * FAR_TILE, Q), FAR_TILE, far_bias, 0)
    row0 = pl.multiple_of(t0, Q)
    sel_step(row0, NEAR_KEYS, bnear_ref[...], n_far * FAR_TILE)
    acc = acc_scr[...]
    o_s = acc[:, :D] * (1.0 / acc[:, D:D + 1])

    kw = kw_ref[0, pl.ds(row0, WIN_KEYS), :]
    vw = vw_ref[0, pl.ds(row0, WIN_KEYS), :]
    in_seq = lax.broadcasted_iota(jnp.int32, (1, WIN_KEYS), 1) >= WINDOW - t0
    s = _dot_nt(q4, kw).reshape(H, Q, WIN_KEYS) + bwin_ref[...] + jnp.where(in_seq, 0.0, NEG)
    p = jnp.exp(s - jnp.max(s, axis=-1, keepdims=True))
    acc_w = _dot(p.reshape(H * Q, WIN_KEYS).astype(BF16), vw)
    o_w = acc_w[:, :D] * (1.0 / acc_w[:, D:D + 1])

    g = jax.nn.sigmoid(g_ref[0])
    outs = []
    for h in range(H):
        rows = slice(h * Q, (h + 1) * Q)
        outs.append(g[:, h:h + 1] * o_c[rows] + g[:, H + h:H + h + 1] * o_s[rows]
                    + g[:, 2 * H + h:2 * H + h + 1] * o_w[rows])
    o_ref[0] = jnp.concatenate(outs, axis=1)


def nsa_attention(t5_table, bias_tiles, q, gates, k_cmp, v_cmp, ks, vs_aug, kw, vw_aug):
    b, s, _ = q.shape
    bnear, bwin, bcmp = bias_tiles
    full = lambda a: pl.BlockSpec((1,) + a.shape[1:], lambda bi, i: (bi,) + (0,) * (a.ndim - 1))
    fixed = lambda a: pl.BlockSpec(a.shape, lambda bi, i: (0,) * a.ndim)
    return pl.pallas_call(
        _nsa_attn_kernel,
        grid=(b, s // Q_BLOCK),
        in_specs=[pl.BlockSpec(memory_space=pltpu.SMEM),
                  pl.BlockSpec((1, Q_BLOCK, GROUP_W), lambda bi, i: (bi, i, 0)),
                  pl.BlockSpec((1, Q_BLOCK, 3 * GROUP_HEADS), lambda bi, i: (bi, i, 0)),
                  full(k_cmp), full(v_cmp), full(ks), full(vs_aug), full(kw), full(vw_aug),
                  fixed(bnear), fixed(bwin), fixed(bcmp)],
        out_specs=pl.BlockSpec((1, Q_BLOCK, GROUP_W), lambda bi, i: (bi, i, 0)),
        out_shape=jax.ShapeDtypeStruct((b, s, GROUP_W), F32),
        scratch_shapes=[pltpu.VMEM((GROUP_HEADS * Q_BLOCK, 1), F32),
                        pltpu.VMEM((GROUP_HEADS * Q_BLOCK, 2 * HEAD_DIM), F32)],
        compiler_params=pltpu.CompilerParams(dimension_semantics=("arbitrary", "arbitrary"),
                                             vmem_limit_bytes=VMEM_LIMIT),
        name="nsa_attention",
    )(t5_table, q, gates, k_cmp, v_cmp, ks, vs_aug, kw, vw_aug, bnear, bwin, bcmp)


def _front_pad(x, rows):
    return jnp.pad(x, ((0, 0), (rows, 0), (0, 0)))


def _with_ones(v):
    b, s, d = v.shape
    pad = jnp.zeros((b, s, d), BF16).at[:, :, 0].set(1.0)
    return jnp.concatenate([v.astype(BF16), pad], axis=-1)


def nsa_group(h, w_cmp_k, w_cmp_v, cmp_pos, t5_table, bias_tiles=None):
    if bias_tiles is None:
        bias_tiles = nsa_bias_tiles(t5_table)
    b, s, _ = h.shape
    d = HEAD_DIM
    q = h[..., :GROUP_W]
    kc, vc, ks, vs, kw, vw = [h[..., GROUP_W + j * d: GROUP_W + (j + 1) * d] for j in range(6)]
    gates = h[..., GROUP_W + 6 * d:]
    n = s // CMP_STRIDE
    x = jnp.stack([kc, vc]).reshape(2, b, n, CMP_STRIDE * d)
    pos = cmp_pos.reshape(1, 2, CMP_STRIDE * d)
    w = jnp.stack([w_cmp_k, w_cmp_v]).reshape(2, 2, CMP_STRIDE * d, d).astype(BF16)
    cmp = nsa_compress(x, pos, w)
    return nsa_attention(t5_table, bias_tiles, q, gates, cmp[0], cmp[1],
                         _front_pad(ks.astype(BF16), NEAR_PAD), _front_pad(_with_ones(vs), NEAR_PAD),
                         _front_pad(kw.astype(BF16), WINDOW), _front_pad(_with_ones(vw), WINDOW))


PEER_HEADS = 8
PEER_DK = 256
N_KEYS = 128
PEER_TOPK = 16
PEER_SLOTS = PEER_HEADS * PEER_TOPK
ROUTE_TILE = 128
GATHER_TILE = 8


def _top_rows(work, order, n_top, payload=None):
    vals, args = [], []
    order_f = order.astype(F32)
    for _ in range(n_top):
        mx = jnp.max(work, axis=0, keepdims=True)
        first = jnp.min(jnp.where(work == mx, order_f, jnp.inf), axis=0, keepdims=True)
        pick = order_f == first
        vals.append(mx)
        if payload is None:
            args.append(first.astype(jnp.int32))
        else:
            args.append(jnp.sum(jnp.where(pick, payload, 0), axis=0, keepdims=True))
        work = jnp.where(pick, -jnp.inf, work)
    return jnp.concatenate(vals, axis=0), jnp.concatenate(args, axis=0)


def _pair_rows(s1, s2, combine):
    k = PEER_TOPK
    rows = [combine(s1[0:1], s2)]
    for a in range(1, k // 2):
        rows.append(combine(s1[a:a + 1], s2[0:k // 2]))
    rows.append(combine(s1[k // 2:], s2[0:1]))
    return jnp.concatenate(rows, axis=0)


def _peer_route_kernel(h_ref, g_ref, wq_ref, keys_ref, xn_ref, eidx_ref, gate_ref):
    x = h_ref[...]
    xn = x * lax.rsqrt(jnp.mean(x * x, axis=-1, keepdims=True) + NORM_EPS) * g_ref[...]
    xn_ref[...] = xn
    q = _dot(xn.astype(BF16), wq_ref[...]).astype(BF16)
    tq = x.shape[0]
    half = PEER_DK // 2
    n_iota = lax.broadcasted_iota(jnp.int32, (N_KEYS, tq), 0)
    k_iota = lax.broadcasted_iota(jnp.int32, (PEER_TOPK, tq), 0)
    order = _pair_rows(k_iota, k_iota, lambda a, b: a * PEER_TOPK + b)
    reachable = (order // PEER_TOPK + 1) * (order % PEER_TOPK + 1) <= PEER_TOPK
    for p in range(PEER_HEADS):
        tops = []
        for c in range(2):
            g = 2 * p + c
            sc = _dot_nt(keys_ref[g], q[:, g * half:(g + 1) * half])
            tops.append(_top_rows(sc, n_iota, PEER_TOPK))
        (s1, i1), (s2, i2) = tops
        cand = jnp.where(reachable, _pair_rows(s1, s2, lambda a, b: a + b), -jnp.inf)
        cidx = _pair_rows(i1, i2, lambda a, b: a * N_KEYS + b)
        top, eidx = _top_rows(cand, order, PEER_TOPK, payload=cidx)
        e = jnp.exp(top - top[0:1])
        eidx_ref[p * PEER_TOPK:(p + 1) * PEER_TOPK, :] = eidx
        gate_ref[p * PEER_TOPK:(p + 1) * PEER_TOPK, :] = e * (1.0 / jnp.sum(e, axis=0, keepdims=True))


def peer_route(h, g, w_q, keys):
    t, d = h.shape
    nq = w_q.shape[1]
    return pl.pallas_call(
        _peer_route_kernel,
        grid=(t // ROUTE_TILE,),
        in_specs=[pl.BlockSpec((ROUTE_TILE, d), lambda i: (i, 0)),
                  pl.BlockSpec((1, d), lambda i: (0, 0)),
                  pl.BlockSpec((d, nq), lambda i: (0, 0)),
                  pl.BlockSpec(keys.shape, lambda i: (0, 0, 0))],
        out_specs=[pl.BlockSpec((ROUTE_TILE, d), lambda i: (i, 0)),
                   pl.BlockSpec((PEER_SLOTS, ROUTE_TILE), lambda i: (0, i)),
                   pl.BlockSpec((PEER_SLOTS, ROUTE_TILE), lambda i: (0, i))],
        out_shape=[jax.ShapeDtypeStruct((t, d), F32),
                   jax.ShapeDtypeStruct((PEER_SLOTS, t), jnp.int32),
                   jax.ShapeDtypeStruct((PEER_SLOTS, t), F32)],
        compiler_params=pltpu.CompilerParams(dimension_semantics=("arbitrary",),
                                             vmem_limit_bytes=VMEM_LIMIT),
        name="peer_route",
    )(h, g.reshape(1, d), w_q, keys)


SUBLANES = 8
SLAB = 2 * SUBLANES
ISSUE_UNROLL = 8


def _gelu(x):
    return 0.5 * x * (1.0 + lax.erf(x * (2.0 ** -0.5)))


BITREV8 = (0, 4, 2, 6, 1, 5, 3, 7)


def _fold_pair(a, b, half, sub):
    low = (sub % (2 * half)) < half
    partner = jnp.where(low, pltpu.roll(a, SUBLANES - half, axis=0), pltpu.roll(b, half, axis=0))
    return jnp.where(low, a, b) + partner


GATHER_DEPTH = 3
DMA_QUEUES = 2


def _peer_gather_kernel(idx0_ref, idx1_ref, idx2_ref, xn_ref, h_ref, gate_ref, uv_hbm, o_ref,
                        buf_a, buf_b, buf_c, w_scr, sem):
    i = pl.program_id(0)
    n = pl.num_programs(0)
    tt = xn_ref.shape[0]
    n_slabs = tt * PEER_SLOTS
    bufs = (buf_a, buf_b, buf_c)

    def request(ref, k, r, queue):
        src = pl.multiple_of(ref[0, 0, r] * SLAB, SLAB)
        pltpu.make_async_copy(uv_hbm.at[pl.ds(src, SLAB)], bufs[k].at[pl.ds(r * SLAB, SLAB)],
                              sem.at[k]).start(priority=queue)

    def request_tile(ref, k):
        def body(c, carry):
            for j in range(ISSUE_UNROLL):
                request(ref, k, pl.multiple_of(c * ISSUE_UNROLL, ISSUE_UNROLL) + j, j % DMA_QUEUES)
            return carry
        lax.fori_loop(0, n_slabs // ISSUE_UNROLL, body, 0)

    def wait_tile(k):
        pltpu.make_async_copy(uv_hbm.at[pl.ds(0, n_slabs * SLAB)], bufs[k], sem.at[k]).wait()

    @pl.when(i == 0)
    def _():
        request_tile(idx0_ref, 0)
        request_tile(idx1_ref, 1)

    def step(cur, ahead):
        wait_tile(cur)
        gates = gate_ref[0]
        sub = lax.broadcasted_iota(jnp.int32, (SUBLANES, LANE), 0)
        for t in range(tt):
            for j in range(PEER_SLOTS):
                request(idx2_ref, ahead, t * PEER_SLOTS + j, j % DMA_QUEUES)
            x = xn_ref[t]

            def slab(pos, half):
                return bufs[cur][pl.ds((t * PEER_SLOTS + pos) * SLAB + half * SUBLANES, SUBLANES), :]

            groups = []
            for g in range(PEER_SLOTS // SUBLANES):
                p = [slab(g * SUBLANES + k, 0) * x for k in range(SUBLANES)]
                c = [_fold_pair(p[2 * k], p[2 * k + 1], 4, sub) for k in range(4)]
                d = [_fold_pair(c[0], c[1], 2, sub), _fold_pair(c[2], c[3], 2, sub)]
                groups.append(_fold_pair(d[0], d[1], 1, sub))
            act = jnp.sum(jnp.concatenate(groups, axis=0), axis=-1, keepdims=True)
            w_scr[...] = jnp.broadcast_to(gates[:, t:t + 1] * _gelu(act), (PEER_SLOTS, LANE))
            acc = h_ref[t]
            for g in range(PEER_SLOTS // SUBLANES):
                for k in range(SUBLANES):
                    w_row = w_scr[pl.ds(g * SUBLANES + BITREV8[k], 1), :]
                    acc = acc + slab(g * SUBLANES + k, 1) * w_row
            o_ref[t] = acc

    for k in range(GATHER_DEPTH):
        @pl.when(i % GATHER_DEPTH == k)
        def _(k=k):
            step(k, (k + GATHER_DEPTH - 1) % GATHER_DEPTH)

    @pl.when(i == n - 1)
    def _():
        for k in range(GATHER_DEPTH):
            @pl.when(i % GATHER_DEPTH == k)
            def _(k=k):
                wait_tile((k + 1) % GATHER_DEPTH)
                wait_tile((k + 2) % GATHER_DEPTH)


def peer_gather(idx, xn, h, gate_b, uv):
    t = xn.shape[0]
    nt = t // GATHER_TILE
    n_slabs = GATHER_TILE * PEER_SLOTS
    tok = pl.BlockSpec((GATHER_TILE, SUBLANES, LANE), lambda i: (i, 0, 0))
    idx_spec = lambda k: pl.BlockSpec((1, 1, n_slabs), lambda i: (jnp.minimum(i + k, nt - 1), 0, 0),
                                      memory_space=pltpu.SMEM)
    slab_buf = pltpu.VMEM((n_slabs * SLAB, LANE), F32)
    return pl.pallas_call(
        _peer_gather_kernel,
        grid=(nt,),
        in_specs=[idx_spec(0), idx_spec(1), idx_spec(2), tok, tok,
                  pl.BlockSpec((1, PEER_SLOTS, GATHER_TILE), lambda i: (i, 0, 0)),
                  pl.BlockSpec(memory_space=pl.ANY)],
        out_specs=tok,
        out_shape=jax.ShapeDtypeStruct((t, SUBLANES, LANE), F32),
        scratch_shapes=[slab_buf, slab_buf, slab_buf, pltpu.VMEM((PEER_SLOTS, LANE), F32),
                        pltpu.SemaphoreType.DMA((GATHER_DEPTH,))],
        compiler_params=pltpu.CompilerParams(dimension_semantics=("arbitrary",),
                                             vmem_limit_bytes=VMEM_LIMIT),
        name="peer_gather",
    )(idx, idx, idx, xn, h, gate_b, uv)


def peer_ffn_residual(h, g, w_q, sub_keys, u_tab, v_tab):
    t, d = h.shape
    e = u_tab.shape[0]
    keys = sub_keys.reshape(PEER_HEADS * 2, N_KEYS, PEER_DK // 2).astype(BF16)
    xn, eidx_t, gate_t = peer_route(h, g, w_q.astype(BF16), keys)
    nt = t // GATHER_TILE
    order = np.arange(PEER_SLOTS).reshape(-1, SUBLANES)[:, list(BITREV8)].reshape(-1)
    idx = eidx_t[order].T.reshape(nt, 1, GATHER_TILE * PEER_SLOTS)
    gate_b = gate_t.reshape(PEER_SLOTS, nt, GATHER_TILE).transpose(1, 0, 2)
    uv = jnp.concatenate([u_tab.reshape(e, SUBLANES, LANE), v_tab.reshape(e, SUBLANES, LANE)],
                         axis=1).reshape(e * SLAB, LANE)
    out = peer_gather(idx, xn.reshape(t, SUBLANES, LANE), h.reshape(t, SUBLANES, LANE), gate_b, uv)
    return out.reshape(t, d)


Q_LORA = 256
KV_LORA = 128
QK_NOPE = 64
QK_ROPE = 32
V_HEAD = 64
ROPE_THETA = 10000.0
MLA_TQ = 256
MLA_TK = 512


def _rms(x, g):
    return x * lax.rsqrt(jnp.mean(x * x, axis=-1, keepdims=True) + NORM_EPS) * g


def _mla_prep_kernel(pos_ref, freq_ref, cq_ref, ckv_ref, kr_ref, gq_ref, gkv_ref, wq_ref, wk_ref, wv_ref,
                     q_out, k_out, v_out):
    rows = cq_ref.shape[0]
    half = QK_ROPE // 2
    cq = _rms(cq_ref[...], gq_ref[...]).astype(BF16)
    ckv = _rms(ckv_ref[...], gkv_ref[...]).astype(BF16)
    qf = _dot(cq, wq_ref[...])
    kf = _dot(ckv, wk_ref[...])
    vf = _dot(ckv, wv_ref[...])
    lane = lax.broadcasted_iota(jnp.int32, (rows, LANE), 1)
    ang = pos_ref[...].astype(F32) * freq_ref[...]
    cos, sin = jnp.cos(ang), jnp.sin(ang)
    in_rope = (lane >= QK_NOPE) & (lane < QK_NOPE + QK_ROPE)
    first = lane < QK_NOPE + half
    c_mul = jnp.where(lane < QK_NOPE, 1.0, jnp.where(in_rope, cos, 0.0))
    s_mul = jnp.where(in_rope, jnp.where(first, -sin, sin), 0.0)

    def rope(x):
        partner = jnp.where(first, pltpu.roll(x, LANE - half, axis=1), pltpu.roll(x, half, axis=1))
        return x * c_mul + partner * s_mul

    scale = (QK_NOPE + QK_ROPE) ** -0.5
    kr = rope(jnp.concatenate([jnp.zeros((rows, QK_NOPE), F32), kr_ref[...],
                               jnp.zeros((rows, LANE - QK_NOPE - QK_ROPE), F32)], axis=1))
    ones_col = jnp.where(lane == V_HEAD, 1.0, 0.0)
    for h in range(GROUP_HEADS):
        cols = slice(h * LANE, (h + 1) * LANE)
        q_out[:, cols] = (rope(qf[:, cols]) * scale).astype(BF16)
        k_out[:, cols] = (kf[:, cols] + kr).astype(BF16)
        v_out[:, cols] = (vf[:, cols] + ones_col).astype(BF16)


def mla_prep(pos, c_q, c_kv, k_r, gq, gkv, w_uq, w_ukv):
    t = c_q.shape[0]
    H = GROUP_HEADS
    wq = jnp.pad(w_uq.reshape(Q_LORA, H, QK_NOPE + QK_ROPE), ((0, 0), (0, 0), (0, LANE - QK_NOPE - QK_ROPE)))
    wkv = w_ukv.reshape(KV_LORA, H, QK_NOPE + V_HEAD)
    wk = jnp.pad(wkv[:, :, :QK_NOPE], ((0, 0), (0, 0), (0, LANE - QK_NOPE)))
    wv = jnp.pad(wkv[:, :, QK_NOPE:], ((0, 0), (0, 0), (0, LANE - V_HEAD)))
    wq, wk, wv = [w.reshape(w.shape[0], H * LANE).astype(BF16) for w in (wq, wk, wv)]
    inv_freq = ROPE_THETA ** (-np.arange(0, QK_ROPE, 2, dtype=np.float32) / QK_ROPE)
    freq = np.zeros((1, LANE), np.float32)
    freq[0, QK_NOPE:QK_NOPE + QK_ROPE] = np.tile(inv_freq, 2)
    row = lambda n: pl.BlockSpec((ROW_TILE, n), lambda i: (i, 0))
    fixed = lambda a: pl.BlockSpec(a.shape, lambda i: (0,) * a.ndim)
    gq2, gkv2, freq = gq.reshape(1, -1), gkv.reshape(1, -1), jnp.asarray(freq)
    out = jax.ShapeDtypeStruct((t, H * LANE), BF16)
    return pl.pallas_call(
        _mla_prep_kernel,
        grid=(t // ROW_TILE,),
        in_specs=[row(1), fixed(freq), row(Q_LORA), row(KV_LORA), row(QK_ROPE), fixed(gq2), fixed(gkv2),
                  fixed(wq), fixed(wk), fixed(wv)],
        out_specs=[row(H * LANE)] * 3,
        out_shape=[out] * 3,
        compiler_params=pltpu.CompilerParams(dimension_semantics=("arbitrary",)),
        name="mla_prep",
    )(pos, freq, c_q, c_kv, k_r, gq2, gkv2, wq, wk, wv)


def _mla_attn_kernel(q_ref, k_ref, v_ref, o_ref, m_scr, acc_scr):
    i = pl.program_id(1)
    tq = q_ref.shape[1]
    row = i * tq + lax.broadcasted_iota(jnp.int32, (tq, MLA_TK), 0)
    col = lax.broadcasted_iota(jnp.int32, (tq, MLA_TK), 1)
    n_full = (i * tq) // MLA_TK
    heads = range(GROUP_HEADS)
    cols = [slice(h * LANE, (h + 1) * LANE) for h in heads]
    q = [q_ref[0, :, cols[h]] for h in heads]
    m_scr[...] = jnp.full(m_scr.shape, NEG, F32)
    acc_scr[...] = jnp.zeros(acc_scr.shape, F32)

    def step(j, masked):
        start = pl.multiple_of(j * MLA_TK, MLA_TK)
        s = [_dot_nt(q[h], k_ref[0, pl.ds(start, MLA_TK), cols[h]]) for h in heads]
        if masked:
            s = [jnp.where(start + col <= row, sh, NEG) for sh in s]
        m_old = [m_scr[h] for h in heads]
        m_new = [jnp.maximum(m_old[h], jnp.max(s[h], axis=-1, keepdims=True)) for h in heads]
        p = [jnp.exp(s[h] - m_new[h]).astype(BF16) for h in heads]
        pv = [_dot(p[h], v_ref[0, pl.ds(start, MLA_TK), cols[h]]) for h in heads]
        for h in heads:
            acc_scr[h] = jnp.exp(m_old[h] - m_new[h]) * acc_scr[h] + pv[h]
            m_scr[h] = m_new[h]

    def body(j, carry):
        step(j, False)
        return carry

    lax.fori_loop(0, n_full, body, 0)
    step(n_full, True)
    outs = [acc_scr[h][:, :V_HEAD] * (1.0 / acc_scr[h][:, V_HEAD:V_HEAD + 1]) for h in heads]
    o_ref[0] = jnp.concatenate(outs, axis=1)


def mla_attention(q, k, v):
    b, s, w = q.shape
    return pl.pallas_call(
        _mla_attn_kernel,
        grid=(b, s // MLA_TQ),
        in_specs=[pl.BlockSpec((1, MLA_TQ, w), lambda bi, i: (bi, i, 0)),
                  pl.BlockSpec((1, s, w), lambda bi, i: (bi, 0, 0)),
                  pl.BlockSpec((1, s, w), lambda bi, i: (bi, 0, 0))],
        out_specs=pl.BlockSpec((1, MLA_TQ, GROUP_HEADS * V_HEAD), lambda bi, i: (bi, i, 0)),
        out_shape=jax.ShapeDtypeStruct((b, s, GROUP_HEADS * V_HEAD), F32),
        scratch_shapes=[pltpu.VMEM((GROUP_HEADS, MLA_TQ, 1), F32), pltpu.VMEM((GROUP_HEADS, MLA_TQ, LANE), F32)],
        compiler_params=pltpu.CompilerParams(dimension_semantics=("arbitrary", "arbitrary"),
                                             vmem_limit_bytes=VMEM_LIMIT),
        name="mla_attention",
    )(q, k, v)


def mla_group(h, positions, q_norm_g, kv_norm_g, w_uq, w_ukv):
    b, s, _ = h.shape
    t = b * s
    h2 = h.reshape(t, -1)
    c_q, c_kv, k_r = h2[:, :Q_LORA], h2[:, Q_LORA:Q_LORA + KV_LORA], h2[:, Q_LORA + KV_LORA:]
    q, k, v = mla_prep(positions.reshape(t, 1), c_q, c_kv, k_r, q_norm_g, kv_norm_g, w_uq, w_ukv)
    w = GROUP_HEADS * LANE
    return mla_attention(q.reshape(b, s, w), k.reshape(b, s, w), v.reshape(b, s, w))


CHUNK = 64
CONV_W = 4


def _split3(a):
    hi = a.astype(BF16)
    r = a - hi.astype(F32)
    mid = r.astype(BF16)
    return hi, mid, (r - mid.astype(F32)).astype(BF16)


def _dot_exact_rhs(sel, a):
    return sum(_dot(sel, p) for p in _split3(a))


def _row_matrix(col, n_rows):
    l = col.shape[0]
    lane = lax.broadcasted_iota(jnp.int32, (l, SUBLANES), 1)
    padded = jnp.where(lane == 0, col, 0.0)
    ones = jnp.ones((n_rows, SUBLANES), BF16)
    return sum(_dot_nt(ones, p) for p in _split3(padded))


def _transpose_bf16(x_bf):
    d = x_bf.shape[1]
    eye = (lax.broadcasted_iota(jnp.int32, (d, d), 0) == lax.broadcasted_iota(jnp.int32, (d, d), 1)).astype(BF16)
    return _dot_nt(eye, x_bf).astype(BF16)


def _log_sigmoid(x):
    return jnp.minimum(x, 0.0) - jnp.log1p(jnp.exp(-jnp.abs(x)))


def _softplus(x):
    return jnp.maximum(x, 0.0) + jnp.log1p(jnp.exp(-jnp.abs(x)))


def _with_ones_col(v):
    l, d = v.shape
    lane = lax.broadcasted_iota(jnp.int32, (l, LANE - d), 1)
    return jnp.concatenate([v, jnp.where(lane == 0, 1.0, 0.0)], axis=1)


def _bf(x):
    return x.astype(BF16)


def _mlstm_kernel(q_ref, k_ref, v_ref, gate_ref, og_ref, bias_ref, ng_ref, o_ref, st_scr, m_scr):
    nb = q_ref.shape[0]
    H, D, L = GROUP_HEADS, HEAD_DIM, CHUNK
    chains = [(b, h) for b in range(nb) for h in range(H)]
    cols = lambda h: slice(h * D, (h + 1) * D)
    each = lambda fn: [fn(c, b, h) for c, (b, h) in enumerate(chains)]

    @pl.when(pl.program_id(0) == 0)
    def _():
        st_scr[...] = jnp.zeros(st_scr.shape, F32)
        m_scr[...] = jnp.zeros(m_scr.shape, F32)

    row = lax.broadcasted_iota(jnp.int32, (L, L), 0)
    col = lax.broadcasted_iota(jnp.int32, (L, L), 1)
    tri = row >= col
    tri_bf = tri.astype(BF16)
    gates = [gate_ref[b] + bias_ref[...] for b in range(nb)]
    bcum = [_dot_exact_rhs(tri_bf, _log_sigmoid(g[:, H:])) for g in gates]
    q = each(lambda c, b, h: _bf(q_ref[b, :, cols(h)]))
    kt = each(lambda c, b, h: _transpose_bf16(_bf(k_ref[b, :, cols(h)] * (D ** -0.5))))
    v_aug = each(lambda c, b, h: _with_ones_col(v_ref[b, :, cols(h)]))
    bc = each(lambda c, b, h: bcum[b][:, h:h + 1])
    ic = each(lambda c, b, h: gates[b][:, h:h + 1])
    m_st = each(lambda c, b, h: m_scr[c, 0:1, 0:1])
    st = each(lambda c, b, h: st_scr[c])
    log_d = each(lambda c, b, h: jnp.where(tri, bc[c] + _row_matrix(ic[c] - bc[c], L), -jnp.inf))
    m_t = each(lambda c, b, h: jnp.maximum(jnp.max(log_d[c], axis=-1, keepdims=True), bc[c] + m_st[c]))
    sqk = each(lambda c, b, h: _dot(q[c], kt[c]) * jnp.exp(log_d[c] - m_t[c]))
    inter = each(lambda c, b, h: jnp.exp(bc[c] + m_st[c] - m_t[c]))
    tot = each(lambda c, b, h: _dot(_bf(sqk[c]), _bf(v_aug[c])) + inter[c] * _dot(q[c], _bf(st[c])))
    h_t = each(lambda c, b, h: tot[c][:, :D] * (1.0 / jnp.maximum(jnp.abs(tot[c][:, D:D + 1]), jnp.exp(-m_t[c]))))
    b_end = each(lambda c, b, h: bc[c][L - 1:L])
    log_w = each(lambda c, b, h: b_end[c] - bc[c] + ic[c])
    m_new = each(lambda c, b, h: jnp.maximum(b_end[c] + m_st[c], jnp.max(log_w[c], axis=0, keepdims=True)))
    upd = each(lambda c, b, h: _dot(kt[c], _bf(jnp.exp(log_w[c] - m_new[c]) * v_aug[c])))
    for c in range(len(chains)):
        st_scr[c] = jnp.exp(b_end[c] + m_st[c] - m_new[c]) * st[c] + upd[c]
        m_scr[c] = jnp.broadcast_to(m_new[c], m_scr.shape[1:])
    for b in range(nb):
        outs = []
        for h in range(H):
            o = jax.nn.sigmoid(og_ref[b, :, cols(h)]) * h_t[b * H + h]
            outs.append(o * lax.rsqrt(jnp.mean(o * o, axis=-1, keepdims=True) + NORM_EPS) * ng_ref[:, cols(h)])
        o_ref[b] = jnp.concatenate(outs, axis=1)


def mlstm_group(h, i_bias, f_bias, norm_g):
    b, s, _ = h.shape
    W, H = GROUP_W, GROUP_HEADS
    q, k, v = h[..., :W], h[..., W:2 * W], h[..., 2 * W:3 * W]
    gates = h[..., 3 * W:3 * W + 2 * H]
    og = h[..., 3 * W + 2 * H:]
    bias = jnp.concatenate([i_bias, f_bias]).reshape(1, 2 * H)
    tok = lambda n: pl.BlockSpec((b, CHUNK, n), lambda i: (0, i, 0))
    fixed = lambda n: pl.BlockSpec((1, n), lambda i: (0, 0))
    return pl.pallas_call(
        _mlstm_kernel,
        grid=(s // CHUNK,),
        in_specs=[tok(W), tok(W), tok(W), tok(2 * H), tok(W), fixed(2 * H), fixed(W)],
        out_specs=tok(W),
        out_shape=jax.ShapeDtypeStruct((b, s, W), F32),
        scratch_shapes=[pltpu.VMEM((b * H, HEAD_DIM, LANE), F32), pltpu.VMEM((b * H, SUBLANES, LANE), F32)],
        compiler_params=pltpu.CompilerParams(dimension_semantics=("arbitrary",)),
        name="mlstm",
    )(q, k, v, gates, og, bias, norm_g.reshape(1, W))


def _gdn_kernel(x_ref, xprev_ref, cw_ref, gate_ref, z_ref, par_ref, ng_ref, o_ref, st_scr):
    nb = x_ref.shape[0]
    H, D, L = GROUP_HEADS, HEAD_DIM, CHUNK
    W = H * D
    chains = [(b, h) for b in range(nb) for h in range(H)]
    each = lambda fn: [fn(c, b, h) for c, (b, h) in enumerate(chains)]
    first = pl.program_id(0) == 0

    @pl.when(first)
    def _():
        st_scr[...] = jnp.zeros(st_scr.shape, F32)

    row = lax.broadcasted_iota(jnp.int32, (L, L), 0)
    col = lax.broadcasted_iota(jnp.int32, (L, L), 1)
    tri = row >= col
    tri_bf = tri.astype(BF16)
    eye = (row == col).astype(F32)
    par = par_ref[...]
    qkv, beta4, gc4 = [], [], []
    for b in range(nb):
        prev = jnp.where(first, 0.0, xprev_ref[b])
        full = jnp.concatenate([prev, x_ref[b]], axis=0)
        acc = full[SUBLANES:] * cw_ref[CONV_W - 1:CONV_W, :]
        for j in range(CONV_W - 1):
            shifted = pltpu.roll(full, CONV_W - 1 - j, axis=0)[SUBLANES:]
            acc = acc + shifted * cw_ref[j:j + 1, :]
        qkv.append(acc * jax.nn.sigmoid(acc))
        gates = gate_ref[b]
        beta4.append(jax.nn.sigmoid(gates[:, :H]))
        gc4.append(_dot_exact_rhs(tri_bf, -jnp.exp(par[:, :H]) * _softplus(gates[:, H:] + par[:, H:])))

    def l2n(t):
        return t * lax.rsqrt(jnp.sum(t * t, axis=-1, keepdims=True) + NORM_EPS)

    q = each(lambda c, b, h: l2n(qkv[b][:, h * D:(h + 1) * D]) * (D ** -0.5))
    k = each(lambda c, b, h: l2n(qkv[b][:, W + h * D:W + (h + 1) * D]))
    v = each(lambda c, b, h: qkv[b][:, 2 * W + h * D:2 * W + (h + 1) * D])
    beta = each(lambda c, b, h: beta4[b][:, h:h + 1])
    gc = each(lambda c, b, h: gc4[b][:, h:h + 1])
    gc_row = each(lambda c, b, h: _row_matrix(gc[c], L))
    decay = each(lambda c, b, h: jnp.where(tri, jnp.exp(gc[c] - gc_row[c]), 0.0))
    kt = each(lambda c, b, h: _transpose_bf16(_bf(k[c])))
    kb = each(lambda c, b, h: k[c] * beta[c])
    x = each(lambda c, b, h: -jnp.where(row > col, _dot(_bf(kb[c]), kt[c]) * decay[c], 0.0))
    t_inv = [eye + xc for xc in x]
    for _ in range(5):
        x = [_dot(_bf(xc), _bf(xc)) for xc in x]
        t_inv = [tc + _dot(_bf(tc), _bf(xc)) for tc, xc in zip(t_inv, x)]
    u = each(lambda c, b, h: _dot(_bf(t_inv[c]), _bf(v[c] * beta[c])))
    wm = each(lambda c, b, h: _dot(_bf(t_inv[c]), _bf(kb[c] * jnp.exp(gc[c]))))
    attn = each(lambda c, b, h: _dot(_bf(q[c]), kt[c]) * decay[c])
    st = each(lambda c, b, h: st_scr[c])
    v_new = each(lambda c, b, h: u[c] - _dot(_bf(wm[c]), _bf(st[c])))
    o = each(lambda c, b, h: _dot(_bf(q[c] * jnp.exp(gc[c])), _bf(st[c])) + _dot(_bf(attn[c]), _bf(v_new[c])))
    g_end = each(lambda c, b, h: gc[c][L - 1:L])
    upd = each(lambda c, b, h: _dot(_bf(kt[c].astype(F32) * jnp.exp(g_end[c] - gc_row[c])), _bf(v_new[c])))
    for c in range(len(chains)):
        st_scr[c] = st[c] * jnp.exp(g_end[c]) + upd[c]
    for b in range(nb):
        outs = []
        for h in range(H):
            oc = o[b * H + h]
            z = z_ref[b, :, h * D:(h + 1) * D]
            y = oc * lax.rsqrt(jnp.mean(oc * oc, axis=-1, keepdims=True) + NORM_EPS) * ng_ref[...]
            outs.append(y * (z * jax.nn.sigmoid(z)))
        o_ref[b] = jnp.concatenate(outs, axis=1)


def gdn_group(h, conv_w, a_log, dt_bias, norm_g):
    b, s, _ = h.shape
    W, H = GROUP_W, GROUP_HEADS
    x = h[..., :3 * W]
    gates = h[..., 3 * W:3 * W + 2 * H]
    z = h[..., 3 * W + 2 * H:]
    par = jnp.concatenate([a_log, dt_bias]).reshape(1, 2 * H)
    tok = lambda n: pl.BlockSpec((b, CHUNK, n), lambda i: (0, i, 0))
    per_chunk = CHUNK // SUBLANES
    prev = pl.BlockSpec((b, SUBLANES, 3 * W), lambda i: (0, jnp.maximum(i * per_chunk - 1, 0), 0))
    fixed = lambda r, n: pl.BlockSpec((r, n), lambda i: (0, 0))
    return pl.pallas_call(
        _gdn_kernel,
        grid=(s // CHUNK,),
        in_specs=[tok(3 * W), prev, fixed(CONV_W, 3 * W), tok(2 * H), tok(W), fixed(1, 2 * H), fixed(1, HEAD_DIM)],
        out_specs=tok(W),
        out_shape=jax.ShapeDtypeStruct((b, s, W), F32),
        scratch_shapes=[pltpu.VMEM((b * H, HEAD_DIM, HEAD_DIM), F32)],
        compiler_params=pltpu.CompilerParams(dimension_semantics=("arbitrary",)),
        name="gdn",
    )(x, x, conv_w, gates, z, par, norm_g.reshape(1, HEAD_DIM))


NSA_IN = GROUP_W + 6 * HEAD_DIM + 3 * GROUP_HEADS
MLSTM_IN = 4 * GROUP_W + 2 * GROUP_HEADS
GDN_IN = 4 * GROUP_W + 2 * GROUP_HEADS
MLA_IN = Q_LORA + KV_LORA + QK_ROPE


def kernel(x, positions, t5_table, norm1_g, norm2_g, final_norm_g, w_in, w_out, nsa_w_cmp_k, nsa_w_cmp_v, nsa_cmp_pos, mlstm_i_bias, mlstm_f_bias, mlstm_norm_g, gdn_conv_w, gdn_a_log, gdn_dt_bias, gdn_norm_g, mla_q_norm_g, mla_kv_norm_g, mla_w_uq, mla_w_ukv, peer_w_q, peer_sub_keys, peer_u, peer_v):
    b, s, d = x.shape
    t = b * s
    h = x.reshape(t, d)
    depth = w_in.shape[0]
    d_in = w_in.shape[2]
    d_in_pad = _round_up(d_in, LANE)
    bias_tiles = nsa_bias_tiles(t5_table)
    for l in range(depth):
        w_in_l = jnp.pad(w_in[l], ((0, 0), (0, d_in_pad - d_in))).astype(BF16)
        hp = norm_matmul(h, norm1_g[l], w_in_l)[:, :d_in].reshape(b, s, d_in)
        c0, c1, c2 = NSA_IN, NSA_IN + MLSTM_IN, NSA_IN + MLSTM_IN + GDN_IN
        h_nsa, h_ml, h_gdn, h_mla = hp[..., :c0], hp[..., c0:c1], hp[..., c1:c2], hp[..., c2:]
        o = jnp.concatenate([
            nsa_group(h_nsa, nsa_w_cmp_k[l], nsa_w_cmp_v[l], nsa_cmp_pos[l], t5_table, bias_tiles),
            mlstm_group(h_ml, mlstm_i_bias[l], mlstm_f_bias[l], mlstm_norm_g[l]),
            gdn_group(h_gdn, gdn_conv_w[l], gdn_a_log[l], gdn_dt_bias[l], gdn_norm_g[l]),
            mla_group(h_mla, positions, mla_q_norm_g[l], mla_kv_norm_g[l], mla_w_uq[l], mla_w_ukv[l]),
        ], axis=-1)
        h = matmul_residual(o.reshape(t, -1), w_out[l].astype(BF16), h)
        h = peer_ffn_residual(h, norm2_g[l], peer_w_q[l], peer_sub_keys[l], peer_u[l], peer_v[l])
    return rmsnorm(h, final_norm_g).reshape(b, s, d)
```

```python
import functools
import math

import jax
import jax.numpy as jnp
import numpy as np
from jax import lax
from jax.experimental import pallas as pl
from jax.experimental.pallas import tpu as pltpu

F32 = jnp.float32
BF16 = jnp.bfloat16
NORM_EPS = 1e-6
NEG = -1e30
LANE = 128
ROW_TILE = 512
VMEM_LIMIT = 56 * 1024 * 1024

HEAD_DIM = 64
GROUP_HEADS = 4
GROUP_W = GROUP_HEADS * HEAD_DIM
Q_BLOCK = 128
CMP_LEN = 32
CMP_STRIDE = 16
SEL_LEN = 64
SEL_TOPN = 16
WINDOW = 512
SEL_FORCE = 1e3
N_BUCKETS = 32
MAX_DISTANCE = 128


def _round_up(n, m):
    return (n + m - 1) // m * m


def _dot_nt(a, b):
    return lax.dot_general(a, b, (((1,), (1,)), ((), ())), preferred_element_type=F32)


def _dot(a, b):
    return jnp.dot(a, b, preferred_element_type=F32)


def _norm_matmul_kernel(x_ref, g_ref, w_ref, o_ref):
    x = x_ref[...]
    y = x * lax.rsqrt(jnp.mean(x * x, axis=-1, keepdims=True) + NORM_EPS) * g_ref[...]
    o_ref[...] = _dot(y.astype(BF16), w_ref[...])


def norm_matmul(x, g, w):
    t, d = x.shape
    n = w.shape[1]
    return pl.pallas_call(
        _norm_matmul_kernel,
        grid=(t // ROW_TILE,),
        in_specs=[pl.BlockSpec((ROW_TILE, d), lambda i: (i, 0)),
                  pl.BlockSpec((1, d), lambda i: (0, 0)),
                  pl.BlockSpec((d, n), lambda i: (0, 0))],
        out_specs=pl.BlockSpec((ROW_TILE, n), lambda i: (i, 0)),
        out_shape=jax.ShapeDtypeStruct((t, n), F32),
        compiler_params=pltpu.CompilerParams(dimension_semantics=("arbitrary",),
                                             vmem_limit_bytes=VMEM_LIMIT),
        name="norm_matmul",
    )(x, g.reshape(1, d), w)


def _matmul_residual_kernel(o_ref, w_ref, r_ref, out_ref):
    out_ref[...] = r_ref[...] + _dot(o_ref[...].astype(BF16), w_ref[...])


def matmul_residual(o, w, res):
    t, d = o.shape
    n = w.shape[1]
    return pl.pallas_call(
        _matmul_residual_kernel,
        grid=(t // ROW_TILE,),
        in_specs=[pl.BlockSpec((ROW_TILE, d), lambda i: (i, 0)),
                  pl.BlockSpec((d, n), lambda i: (0, 0)),
                  pl.BlockSpec((ROW_TILE, n), lambda i: (i, 0))],
        out_specs=pl.BlockSpec((ROW_TILE, n), lambda i: (i, 0)),
        out_shape=jax.ShapeDtypeStruct((t, n), F32),
        compiler_params=pltpu.CompilerParams(dimension_semantics=("arbitrary",),
                                             vmem_limit_bytes=VMEM_LIMIT),
        name="matmul_residual",
    )(o, w, res)


def _rmsnorm_kernel(x_ref, g_ref, o_ref):
    x = x_ref[...]
    o_ref[...] = x * lax.rsqrt(jnp.mean(x * x, axis=-1, keepdims=True) + NORM_EPS) * g_ref[...]


def rmsnorm(x, g):
    t, d = x.shape
    return pl.pallas_call(
        _rmsnorm_kernel,
        grid=(t // ROW_TILE,),
        in_specs=[pl.BlockSpec((ROW_TILE, d), lambda i: (i, 0)),
                  pl.BlockSpec((1, d), lambda i: (0, 0))],
        out_specs=pl.BlockSpec((ROW_TILE, d), lambda i: (i, 0)),
        out_shape=jax.ShapeDtypeStruct((t, d), F32),
        compiler_params=pltpu.CompilerParams(dimension_semantics=("arbitrary",)),
        name="rmsnorm",
    )(x, g.reshape(1, d))


def _t5_bucket_starts():
    d = np.arange(0, 2 * MAX_DISTANCE)
    max_exact = N_BUCKETS // 2
    val = (np.log(np.maximum(d, 1).astype(np.float32) / np.float32(max_exact))
           / np.float32(math.log(MAX_DISTANCE / max_exact)) * np.float32(N_BUCKETS - max_exact))
    large = np.minimum(max_exact + val.astype(np.int32), N_BUCKETS - 1)
    bucket = np.where(d < max_exact, d, large)
    assert np.all(np.diff(bucket) >= 0)
    return [(int(k), int(d[bucket == k].min())) for k in range(N_BUCKETS) if np.any(bucket == k)]


T5_STARTS = _t5_bucket_starts()


def _t5_bias_heads(dist, tab_ref):
    accs = [jnp.full(dist.shape, tab_ref[0, h], F32) for h in range(GROUP_HEADS)]
    for bucket, start in T5_STARTS[1:]:
        m = dist >= start
        accs = [jnp.where(m, tab_ref[bucket, h], accs[h]) for h in range(GROUP_HEADS)]
    return jnp.stack(accs)


def _nsa_compress_kernel(x_ref, pos_ref, w_ref, o_ref):
    x = x_ref[0, 0]
    n = x.shape[0]
    lo = _dot((x + pos_ref[0, 0:1, :]).astype(BF16), w_ref[0, 0])
    hi = _dot((x + pos_ref[0, 1:2, :]).astype(BF16), w_ref[0, 1])
    o_ref[0, 0] = (lo + pltpu.roll(hi, n - 1, axis=0)).astype(o_ref.dtype)


def nsa_compress(x, pos, w):
    two, b, n, c = x.shape
    return pl.pallas_call(
        _nsa_compress_kernel,
        grid=(two, b),
        in_specs=[pl.BlockSpec((1, 1, n, c), lambda k, i: (k, i, 0, 0)),
                  pl.BlockSpec((1, 2, c), lambda k, i: (0, 0, 0)),
                  pl.BlockSpec((1, 2, c, HEAD_DIM), lambda k, i: (k, 0, 0, 0))],
        out_specs=pl.BlockSpec((1, 1, n, HEAD_DIM), lambda k, i: (k, i, 0, 0)),
        out_shape=jax.ShapeDtypeStruct((two, b, n, HEAD_DIM), BF16),
        compiler_params=pltpu.CompilerParams(dimension_semantics=("arbitrary", "arbitrary")),
        name="nsa_compress",
    )(x, pos, w)


FAR_TILE = 512
NEAR_KEYS = 768
WIN_KEYS = WINDOW + Q_BLOCK
NEAR_PAD = NEAR_KEYS - Q_BLOCK
CMP_TILE = LANE
CMP_TILES_BACK = CMP_TILE * CMP_STRIDE // Q_BLOCK
N_CMP_BIAS = CMP_TILES_BACK + 3


def _masked_bias(dist, valid, tab_ref):
    return jnp.where(valid, _t5_bias_heads(dist, tab_ref), NEG)


def _nsa_bias_kernel(tab_ref, near_ref, win_ref):
    q = lax.broadcasted_iota(jnp.int32, (Q_BLOCK, NEAR_KEYS), 0)
    k = lax.broadcasted_iota(jnp.int32, (Q_BLOCK, NEAR_KEYS), 1)
    dist = NEAR_PAD + q - k
    near_ref[...] = _masked_bias(dist, dist >= 0, tab_ref)
    q = lax.broadcasted_iota(jnp.int32, (Q_BLOCK, WIN_KEYS), 0)
    k = lax.broadcasted_iota(jnp.int32, (Q_BLOCK, WIN_KEYS), 1)
    dist = WINDOW + q - k
    win_ref[...] = _masked_bias(dist, (dist >= 0) & (dist < WINDOW), tab_ref)


def _nsa_cmp_bias_kernel(tab_ref, o_ref):
    e = pl.program_id(0)
    q = lax.broadcasted_iota(jnp.int32, (Q_BLOCK, CMP_TILE), 0)
    n = lax.broadcasted_iota(jnp.int32, (Q_BLOCK, CMP_TILE), 1)
    dist = (e - 1) * Q_BLOCK + q - (n * CMP_STRIDE + CMP_LEN - 1)
    o_ref[0] = _masked_bias(dist, dist >= 0, tab_ref)


def nsa_bias_tiles(t5_table):
    H, Q = GROUP_HEADS, Q_BLOCK
    smem = pl.BlockSpec(memory_space=pltpu.SMEM)
    near, win = pl.pallas_call(
        _nsa_bias_kernel, grid=(1,), in_specs=[smem],
        out_specs=[pl.BlockSpec((H, Q, NEAR_KEYS), lambda i: (0, 0, 0)),
                   pl.BlockSpec((H, Q, WIN_KEYS), lambda i: (0, 0, 0))],
        out_shape=[jax.ShapeDtypeStruct((H, Q, NEAR_KEYS), F32), jax.ShapeDtypeStruct((H, Q, WIN_KEYS), F32)],
        name="nsa_bias",
    )(t5_table)
    cmp = pl.pallas_call(
        _nsa_cmp_bias_kernel, grid=(N_CMP_BIAS,), in_specs=[smem],
        out_specs=pl.BlockSpec((1, H, Q, CMP_TILE), lambda e: (e, 0, 0, 0)),
        out_shape=jax.ShapeDtypeStruct((N_CMP_BIAS, H, Q, CMP_TILE), F32),
        name="nsa_cmp_bias",
    )(t5_table)
    return near, win, cmp


def _nsa_attn_kernel(tab_ref, q_ref, g_ref, kc_ref, vc_ref, ks_ref, vs_ref, kw_ref, vw_ref,
                     bnear_ref, bwin_ref, bcmp_ref, o_ref, m_scr, acc_scr):
    H, Q, D = GROUP_HEADS, Q_BLOCK, HEAD_DIM
    n_cmp = kc_ref.shape[1]
    n_sel = (ks_ref.shape[1] - NEAR_PAD) // SEL_LEN
    i = pl.program_id(1)
    t0 = i * Q

    q = q_ref[0] * (D ** -0.5)
    q4 = jnp.concatenate([q[:, h * D:(h + 1) * D] for h in range(H)], axis=0).astype(BF16)

    s = _dot_nt(q4, kc_ref[0]).reshape(H, Q, n_cmp)
    tiles = []
    for g in range(n_cmp // CMP_TILE):
        e = jnp.clip(i - g * CMP_TILES_BACK, -1, CMP_TILES_BACK + 1) + 1
        tiles.append(s[:, :, g * CMP_TILE:(g + 1) * CMP_TILE] + bcmp_ref[e])
    s = jnp.concatenate(tiles, axis=-1)
    t_c = t0 + lax.broadcasted_iota(jnp.int32, (Q, n_cmp), 0)
    valid_c = t_c >= lax.broadcasted_iota(jnp.int32, (Q, n_cmp), 1) * CMP_STRIDE + CMP_LEN - 1
    p = jnp.where(valid_c, jnp.exp(s - jnp.max(s, axis=-1, keepdims=True)), 0.0)
    l = jnp.sum(p, axis=-1, keepdims=True)
    p = p * (1.0 / jnp.where(l > 0.0, l, 1.0))
    o_c = _dot(p.reshape(H * Q, n_cmp).astype(BF16), vc_ref[0])

    psum = p[0] + p[1] + p[2] + p[3]
    c_idx = lax.broadcasted_iota(jnp.int32, (n_cmp, n_sel), 0)
    j_idx = lax.broadcasted_iota(jnp.int32, (n_cmp, n_sel), 1)
    ratio = SEL_LEN // CMP_STRIDE
    n_in_sel = (SEL_LEN - CMP_LEN) // CMP_STRIDE + 1
    pool = ((c_idx // ratio == j_idx) & (c_idx % ratio < n_in_sel)).astype(BF16)
    p_hi = psum.astype(BF16)
    p_mid = (psum - p_hi.astype(F32)).astype(BF16)
    p_lo = (psum - p_hi.astype(F32) - p_mid.astype(F32)).astype(BF16)
    imp = _dot(p_hi, pool) + _dot(p_mid, pool) + _dot(p_lo, pool)

    jb = lax.broadcasted_iota(jnp.int32, (Q, n_sel), 1)
    cur = (t0 + lax.broadcasted_iota(jnp.int32, (Q, n_sel), 0)) // SEL_LEN
    forced = (jb == 0) | (jb == cur) | (jb == cur - 1)
    work = jnp.where(jb <= cur, imp + jnp.where(forced, SEL_FORCE, 0.0), NEG)
    sel = jnp.zeros((Q, n_sel), F32)
    jb_f = jb.astype(F32)
    for _ in range(min(SEL_TOPN, n_sel)):
        mx = jnp.max(work, axis=-1, keepdims=True)
        first = jnp.min(jnp.where(work == mx, jb_f, jnp.inf), axis=-1, keepdims=True)
        pick = jb_f == first
        sel = jnp.where(pick & (mx > 0.5 * NEG), 1.0, sel)
        work = jnp.where(pick, -jnp.inf, work)
    sel_bf = sel.astype(BF16)

    far_bias = jnp.stack([jnp.full((Q, 1), tab_ref[N_BUCKETS - 1, h], F32) for h in range(H)])
    m_scr[...] = jnp.full(m_scr.shape, NEG, F32)
    acc_scr[...] = jnp.zeros(acc_scr.shape, F32)

    def sel_step(row0, n_keys, bias, first_key):
        k = ks_ref[0, pl.ds(row0, n_keys), :]
        v = vs_ref[0, pl.ds(row0, n_keys), :]
        s = _dot_nt(q4, k).reshape(H, Q, n_keys) + bias
        pos = row0 - NEAR_PAD + lax.broadcasted_iota(jnp.int32, (n_sel, n_keys), 1)
        blk = jnp.where(pos >= first_key, pos // SEL_LEN, -1)
        expand = (lax.broadcasted_iota(jnp.int32, (n_sel, n_keys), 0) == blk).astype(BF16)
        mask = _dot(sel_bf, expand) > 0.5
        s = jnp.where(mask, s, NEG)
        m_old = m_scr[...].reshape(H, Q, 1)
        m_new = jnp.maximum(m_old, jnp.max(s, axis=-1, keepdims=True))
        p = jnp.where(mask, jnp.exp(s - m_new), 0.0).reshape(H * Q, n_keys)
        alpha = jnp.exp(m_old - m_new).reshape(H * Q, 1)
        acc_scr[...] = alpha * acc_scr[...] + _dot(p.astype(BF16), v)
        m_scr[...] = m_new.reshape(H * Q, 1)

    n_far = jnp.maximum(i - 1, 0) // (FAR_TILE // Q)

    def far_body(kp, carry):
        sel_step(pl.multiple_of(NEAR_PAD + kp * (2 * FAR_TILE), Q), 2 * FAR_TILE, far_bias, 0)
        return carry

    lax.fori_loop(0, n_far // 2, far_body, 0)

    @pl.when(n_far % 2 == 1)
    def _():
        sel_step(pl.multiple_of(NEAR_PAD + (n_far - 1) * FAR_TILE, Q), FAR_TILE, far_bias, 0)
    row0 = pl.multiple_of(t0, Q)
    sel_step(row0, NEAR_KEYS, bnear_ref[...], n_far * FAR_TILE)
    acc = acc_scr[...]
    o_s = acc[:, :D] * (1.0 / acc[:, D:D + 1])

    kw = kw_ref[0, pl.ds(row0, WIN_KEYS), :]
    vw = vw_ref[0, pl.ds(row0, WIN_KEYS), :]
    in_seq = lax.broadcasted_iota(jnp.int32, (1, WIN_KEYS), 1) >= WINDOW - t0
    s = _dot_nt(q4, kw).reshape(H, Q, WIN_KEYS) + bwin_ref[...] + jnp.where(in_seq, 0.0, NEG)
    p = jnp.exp(s - jnp.max(s, axis=-1, keepdims=True))
    acc_w = _dot(p.reshape(H * Q, WIN_KEYS).astype(BF16), vw)
    o_w = acc_w[:, :D] * (1.0 / acc_w[:, D:D + 1])

    g = jax.nn.sigmoid(g_ref[0])
    outs = []
    for h in range(H):
        rows = slice(h * Q, (h + 1) * Q)
        outs.append(g[:, h:h + 1] * o_c[rows] + g[:, H + h:H + h + 1] * o_s[rows]
                    + g[:, 2 * H + h:2 * H + h + 1] * o_w[rows])
    o_ref[0] = jnp.concatenate(outs, axis=1)


def nsa_attention(t5_table, bias_tiles, q, gates, k_cmp, v_cmp, ks, vs_aug, kw, vw_aug):
    b, s, _ = q.shape
    bnear, bwin, bcmp = bias_tiles
    full = lambda a: pl.BlockSpec((1,) + a.shape[1:], lambda bi, i: (bi,) + (0,) * (a.ndim - 1))
    fixed = lambda a: pl.BlockSpec(a.shape, lambda bi, i: (0,) * a.ndim)
    return pl.pallas_call(
        _nsa_attn_kernel,
        grid=(b, s // Q_BLOCK),
        in_specs=[pl.BlockSpec(memory_space=pltpu.SMEM),
                  pl.BlockSpec((1, Q_BLOCK, GROUP_W), lambda bi, i: (bi, i, 0)),
                  pl.BlockSpec((1, Q_BLOCK, 3 * GROUP_HEADS), lambda bi, i: (bi, i, 0)),
                  full(k_cmp), full(v_cmp), full(ks), full(vs_aug), full(kw), full(vw_aug),
                  fixed(bnear), fixed(bwin), fixed(bcmp)],
        out_specs=pl.BlockSpec((1, Q_BLOCK, GROUP_W), lambda bi, i: (bi, i, 0)),
        out_shape=jax.ShapeDtypeStruct((b, s, GROUP_W), F32),
        scratch_shapes=[pltpu.VMEM((GROUP_HEADS * Q_BLOCK, 1), F32),
                        pltpu.VMEM((GROUP_HEADS * Q_BLOCK, 2 * HEAD_DIM), F32)],
        compiler_params=pltpu.CompilerParams(dimension_semantics=("arbitrary", "arbitrary"),
                                             vmem_limit_bytes=VMEM_LIMIT),
        name="nsa_attention",
    )(t5_table, q, gates, k_cmp, v_cmp, ks, vs_aug, kw, vw_aug, bnear, bwin, bcmp)


def _front_pad(x, rows):
    return jnp.pad(x, ((0, 0), (rows, 0), (0, 0)))


def _with_ones(v):
    b, s, d = v.shape
    pad = jnp.zeros((b, s, d), BF16).at[:, :, 0].set(1.0)
    return jnp.concatenate([v.astype(BF16), pad], axis=-1)


def nsa_group(h, w_cmp_k, w_cmp_v, cmp_pos, t5_table, bias_tiles=None):
    if bias_tiles is None:
        bias_tiles = nsa_bias_tiles(t5_table)
    b, s, _ = h.shape
    d = HEAD_DIM
    q = h[..., :GROUP_W]
    kc, vc, ks, vs, kw, vw = [h[..., GROUP_W + j * d: GROUP_W + (j + 1) * d] for j in range(6)]
    gates = h[..., GROUP_W + 6 * d:]
    n = s // CMP_STRIDE
    x = jnp.stack([kc, vc]).reshape(2, b, n, CMP_STRIDE * d)
    pos = cmp_pos.reshape(1, 2, CMP_STRIDE * d)
    w = jnp.stack([w_cmp_k, w_cmp_v]).reshape(2, 2, CMP_STRIDE * d, d).astype(BF16)
    cmp = nsa_compress(x, pos, w)
    return nsa_attention(t5_table, bias_tiles, q, gates, cmp[0], cmp[1],
                         _front_pad(ks.astype(BF16), NEAR_PAD), _front_pad(_with_ones(vs), NEAR_PAD),
                         _front_pad(kw.astype(BF16), WINDOW), _front_pad(_with_ones(vw), WINDOW))


PEER_HEADS = 8
PEER_DK = 256
N_KEYS = 128
PEER_TOPK = 16
PEER_SLOTS = PEER_HEADS * PEER_TOPK
ROUTE_TILE = 128
GATHER_TILE = 16


def _top_rows(work, order, n_top, payload=None):
    vals, args = [], []
    order_f = order.astype(F32)
    for _ in range(n_top):
        mx = jnp.max(work, axis=0, keepdims=True)
        first = jnp.min(jnp.where(work == mx, order_f, jnp.inf), axis=0, keepdims=True)
        pick = order_f == first
        vals.append(mx)
        if payload is None:
            args.append(first.astype(jnp.int32))
        else:
            args.append(jnp.sum(jnp.where(pick, payload, 0), axis=0, keepdims=True))
        work = jnp.where(pick, -jnp.inf, work)
    return jnp.concatenate(vals, axis=0), jnp.concatenate(args, axis=0)


def _pair_rows(s1, s2, combine):
    k = PEER_TOPK
    rows = [combine(s1[0:1], s2)]
    for a in range(1, k // 2):
        rows.append(combine(s1[a:a + 1], s2[0:k // 2]))
    rows.append(combine(s1[k // 2:], s2[0:1]))
    return jnp.concatenate(rows, axis=0)


def _peer_route_kernel(h_ref, g_ref, wq_ref, keys_ref, xn_ref, eidx_ref, gate_ref):
    x = h_ref[...]
    xn = x * lax.rsqrt(jnp.mean(x * x, axis=-1, keepdims=True) + NORM_EPS) * g_ref[...]
    xn_ref[...] = xn
    q = _dot(xn.astype(BF16), wq_ref[...]).astype(BF16)
    tq = x.shape[0]
    half = PEER_DK // 2
    n_iota = lax.broadcasted_iota(jnp.int32, (N_KEYS, tq), 0)
    k_iota = lax.broadcasted_iota(jnp.int32, (PEER_TOPK, tq), 0)
    order = _pair_rows(k_iota, k_iota, lambda a, b: a * PEER_TOPK + b)
    reachable = (order // PEER_TOPK + 1) * (order % PEER_TOPK + 1) <= PEER_TOPK
    for p in range(PEER_HEADS):
        tops = []
        for c in range(2):
            g = 2 * p + c
            sc = _dot_nt(keys_ref[g], q[:, g * half:(g + 1) * half])
            tops.append(_top_rows(sc, n_iota, PEER_TOPK))
        (s1, i1), (s2, i2) = tops
        cand = jnp.where(reachable, _pair_rows(s1, s2, lambda a, b: a + b), -jnp.inf)
        cidx = _pair_rows(i1, i2, lambda a, b: a * N_KEYS + b)
        top, eidx = _top_rows(cand, order, PEER_TOPK, payload=cidx)
        e = jnp.exp(top - top[0:1])
        eidx_ref[p * PEER_TOPK:(p + 1) * PEER_TOPK, :] = eidx
        gate_ref[p * PEER_TOPK:(p + 1) * PEER_TOPK, :] = e * (1.0 / jnp.sum(e, axis=0, keepdims=True))


def peer_route(h, g, w_q, keys):
    t, d = h.shape
    nq = w_q.shape[1]
    return pl.pallas_call(
        _peer_route_kernel,
        grid=(t // ROUTE_TILE,),
        in_specs=[pl.BlockSpec((ROUTE_TILE, d), lambda i: (i, 0)),
                  pl.BlockSpec((1, d), lambda i: (0, 0)),
                  pl.BlockSpec((d, nq), lambda i: (0, 0)),
                  pl.BlockSpec(keys.shape, lambda i: (0, 0, 0))],
        out_specs=[pl.BlockSpec((ROUTE_TILE, d), lambda i: (i, 0)),
                   pl.BlockSpec((PEER_SLOTS, ROUTE_TILE), lambda i: (0, i)),
                   pl.BlockSpec((PEER_SLOTS, ROUTE_TILE), lambda i: (0, i))],
        out_shape=[jax.ShapeDtypeStruct((t, d), F32),
                   jax.ShapeDtypeStruct((PEER_SLOTS, t), jnp.int32),
                   jax.ShapeDtypeStruct((PEER_SLOTS, t), F32)],
        compiler_params=pltpu.CompilerParams(dimension_semantics=("arbitrary",),
                                             vmem_limit_bytes=VMEM_LIMIT),
        name="peer_route",
    )(h, g.reshape(1, d), w_q, keys)


SUBLANES = 8
SLAB = 2 * SUBLANES
ISSUE_UNROLL = 8


def _gelu(x):
    return 0.5 * x * (1.0 + lax.erf(x * (2.0 ** -0.5)))


BITREV8 = (0, 4, 2, 6, 1, 5, 3, 7)


def _fold_pair(a, b, half, sub):
    low = (sub % (2 * half)) < half
    partner = jnp.where(low, pltpu.roll(a, SUBLANES - half, axis=0), pltpu.roll(b, half, axis=0))
    return jnp.where(low, a, b) + partner


GATHER_DEPTH = 3
DMA_QUEUES = 2


def _peer_gather_kernel(idx0_ref, idx1_ref, idx2_ref, xn_ref, h_ref, gate_ref, uv_hbm, o_ref,
                        buf_a, buf_b, buf_c, w_scr, sem):
    i = pl.program_id(0)
    n = pl.num_programs(0)
    tt = xn_ref.shape[0]
    n_slabs = tt * PEER_SLOTS
    bufs = (buf_a, buf_b, buf_c)

    def request(ref, k, r, queue):
        src = pl.multiple_of(ref[0, 0, r] * SLAB, SLAB)
        pltpu.make_async_copy(uv_hbm.at[pl.ds(src, SLAB)], bufs[k].at[pl.ds(r * SLAB, SLAB)],
                              sem.at[k]).start(priority=queue)

    def request_tile(ref, k):
        def body(c, carry):
            for j in range(ISSUE_UNROLL):
                request(ref, k, pl.multiple_of(c * ISSUE_UNROLL, ISSUE_UNROLL) + j, j % DMA_QUEUES)
            return carry
        lax.fori_loop(0, n_slabs // ISSUE_UNROLL, body, 0)

    def wait_tile(k):
        pltpu.make_async_copy(uv_hbm.at[pl.ds(0, n_slabs * SLAB)], bufs[k], sem.at[k]).wait()

    @pl.when(i == 0)
    def _():
        request_tile(idx0_ref, 0)
        request_tile(idx1_ref, 1)

    def step(cur, ahead):
        wait_tile(cur)
        gates = gate_ref[0]
        sub = lax.broadcasted_iota(jnp.int32, (SUBLANES, LANE), 0)
        for t in range(tt):
            for j in range(PEER_SLOTS):
                request(idx2_ref, ahead, t * PEER_SLOTS + j, j % DMA_QUEUES)
            x = xn_ref[t]

            def slab(pos, half):
                return bufs[cur][pl.ds((t * PEER_SLOTS + pos) * SLAB + half * SUBLANES, SUBLANES), :]

            groups = []
            for g in range(PEER_SLOTS // SUBLANES):
                p = [slab(g * SUBLANES + k, 0) * x for k in range(SUBLANES)]
                c = [_fold_pair(p[2 * k], p[2 * k + 1], 4, sub) for k in range(4)]
                d = [_fold_pair(c[0], c[1], 2, sub), _fold_pair(c[2], c[3], 2, sub)]
                groups.append(_fold_pair(d[0], d[1], 1, sub))
            act = jnp.sum(jnp.concatenate(groups, axis=0), axis=-1, keepdims=True)
            w_scr[...] = jnp.broadcast_to(gates[:, t:t + 1] * _gelu(act), (PEER_SLOTS, LANE))
            acc = h_ref[t]
            for g in range(PEER_SLOTS // SUBLANES):
                for k in range(SUBLANES):
                    w_row = w_scr[pl.ds(g * SUBLANES + BITREV8[k], 1), :]
                    acc = acc + slab(g * SUBLANES + k, 1) * w_row
            o_ref[t] = acc

    for k in range(GATHER_DEPTH):
        @pl.when(i % GATHER_DEPTH == k)
        def _(k=k):
            step(k, (k + GATHER_DEPTH - 1) % GATHER_DEPTH)

    @pl.when(i == n - 1)
    def _():
        for k in range(GATHER_DEPTH):
            @pl.when(i % GATHER_DEPTH == k)
            def _(k=k):
                wait_tile((k + 1) % GATHER_DEPTH)
                wait_tile((k + 2) % GATHER_DEPTH)


def peer_gather(idx, xn, h, gate_b, uv):
    t = xn.shape[0]
    nt = t // GATHER_TILE
    n_slabs = GATHER_TILE * PEER_SLOTS
    tok = pl.BlockSpec((GATHER_TILE, SUBLANES, LANE), lambda i: (i, 0, 0))
    idx_spec = lambda k: pl.BlockSpec((1, 1, n_slabs), lambda i: (jnp.minimum(i + k, nt - 1), 0, 0),
                                      memory_space=pltpu.SMEM)
    slab_buf = pltpu.VMEM((n_slabs * SLAB, LANE), F32)
    return pl.pallas_call(
        _peer_gather_kernel,
        grid=(nt,),
        in_specs=[idx_spec(0), idx_spec(1), idx_spec(2), tok, tok,
                  pl.BlockSpec((1, PEER_SLOTS, GATHER_TILE), lambda i: (i, 0, 0)),
                  pl.BlockSpec(memory_space=pl.ANY)],
        out_specs=tok,
        out_shape=jax.ShapeDtypeStruct((t, SUBLANES, LANE), F32),
        scratch_shapes=[slab_buf, slab_buf, slab_buf, pltpu.VMEM((PEER_SLOTS, LANE), F32),
                        pltpu.SemaphoreType.DMA((GATHER_DEPTH,))],
        compiler_params=pltpu.CompilerParams(dimension_semantics=("arbitrary",),
                                             vmem_limit_bytes=VMEM_LIMIT),
        name="peer_gather",
    )(idx, idx, idx, xn, h, gate_b, uv)


def peer_ffn_residual(h, g, w_q, sub_keys, u_tab, v_tab):
    t, d = h.shape
    e = u_tab.shape[0]
    keys = sub_keys.reshape(PEER_HEADS * 2, N_KEYS, PEER_DK // 2).astype(BF16)
    xn, eidx_t, gate_t = peer_route(h, g, w_q.astype(BF16), keys)
    nt = t // GATHER_TILE
    order = np.arange(PEER_SLOTS).reshape(-1, SUBLANES)[:, list(BITREV8)].reshape(-1)
    idx = eidx_t[order].T.reshape(nt, 1, GATHER_TILE * PEER_SLOTS)
    gate_b = gate_t.reshape(PEER_SLOTS, nt, GATHER_TILE).transpose(1, 0, 2)
    uv = jnp.concatenate([u_tab.reshape(e, SUBLANES, LANE), v_tab.reshape(e, SUBLANES, LANE)],
                         axis=1).reshape(e * SLAB, LANE)
    out = peer_gather(idx, xn.reshape(t, SUBLANES, LANE), h.reshape(t, SUBLANES, LANE), gate_b, uv)
    return out.reshape(t, d)


Q_LORA = 256
KV_LORA = 128
QK_NOPE = 64
QK_ROPE = 32
V_HEAD = 64
ROPE_THETA = 10000.0
MLA_TQ = 256
MLA_TK = 512


def _rms(x, g):
    return x * lax.rsqrt(jnp.mean(x * x, axis=-1, keepdims=True) + NORM_EPS) * g


def _mla_prep_kernel(pos_ref, freq_ref, cq_ref, ckv_ref, kr_ref, gq_ref, gkv_ref, wq_ref, wk_ref, wv_ref,
                     q_out, k_out, v_out):
    rows = cq_ref.shape[0]
    half = QK_ROPE // 2
    cq = _rms(cq_ref[...], gq_ref[...]).astype(BF16)
    ckv = _rms(ckv_ref[...], gkv_ref[...]).astype(BF16)
    qf = _dot(cq, wq_ref[...])
    kf = _dot(ckv, wk_ref[...])
    vf = _dot(ckv, wv_ref[...])
    lane = lax.broadcasted_iota(jnp.int32, (rows, LANE), 1)
    ang = pos_ref[...].astype(F32) * freq_ref[...]
    cos, sin = jnp.cos(ang), jnp.sin(ang)
    in_rope = (lane >= QK_NOPE) & (lane < QK_NOPE + QK_ROPE)
    first = lane < QK_NOPE + half
    c_mul = jnp.where(lane < QK_NOPE, 1.0, jnp.where(in_rope, cos, 0.0))
    s_mul = jnp.where(in_rope, jnp.where(first, -sin, sin), 0.0)

    def rope(x):
        partner = jnp.where(first, pltpu.roll(x, LANE - half, axis=1), pltpu.roll(x, half, axis=1))
        return x * c_mul + partner * s_mul

    scale = (QK_NOPE + QK_ROPE) ** -0.5
    kr = rope(jnp.concatenate([jnp.zeros((rows, QK_NOPE), F32), kr_ref[...],
                               jnp.zeros((rows, LANE - QK_NOPE - QK_ROPE), F32)], axis=1))
    ones_col = jnp.where(lane == V_HEAD, 1.0, 0.0)
    for h in range(GROUP_HEADS):
        cols = slice(h * LANE, (h + 1) * LANE)
        q_out[:, cols] = (rope(qf[:, cols]) * scale).astype(BF16)
        k_out[:, cols] = (kf[:, cols] + kr).astype(BF16)
        v_out[:, cols] = (vf[:, cols] + ones_col).astype(BF16)


def mla_prep(pos, c_q, c_kv, k_r, gq, gkv, w_uq, w_ukv):
    t = c_q.shape[0]
    H = GROUP_HEADS
    wq = jnp.pad(w_uq.reshape(Q_LORA, H, QK_NOPE + QK_ROPE), ((0, 0), (0, 0), (0, LANE - QK_NOPE - QK_ROPE)))
    wkv = w_ukv.reshape(KV_LORA, H, QK_NOPE + V_HEAD)
    wk = jnp.pad(wkv[:, :, :QK_NOPE], ((0, 0), (0, 0), (0, LANE - QK_NOPE)))
    wv = jnp.pad(wkv[:, :, QK_NOPE:], ((0, 0), (0, 0), (0, LANE - V_HEAD)))
    wq, wk, wv = [w.reshape(w.shape[0], H * LANE).astype(BF16) for w in (wq, wk, wv)]
    inv_freq = ROPE_THETA ** (-np.arange(0, QK_ROPE, 2, dtype=np.float32) / QK_ROPE)
    freq = np.zeros((1, LANE), np.float32)
    freq[0, QK_NOPE:QK_NOPE + QK_ROPE] = np.tile(inv_freq, 2)
    row = lambda n: pl.BlockSpec((ROW_TILE, n), lambda i: (i, 0))
    fixed = lambda a: pl.BlockSpec(a.shape, lambda i: (0,) * a.ndim)
    gq2, gkv2, freq = gq.reshape(1, -1), gkv.reshape(1, -1), jnp.asarray(freq)
    out = jax.ShapeDtypeStruct((t, H * LANE), BF16)
    return pl.pallas_call(
        _mla_prep_kernel,
        grid=(t // ROW_TILE,),
        in_specs=[row(1), fixed(freq), row(Q_LORA), row(KV_LORA), row(QK_ROPE), fixed(gq2), fixed(gkv2),
                  fixed(wq), fixed(wk), fixed(wv)],
        out_specs=[row(H * LANE)] * 3,
        out_shape=[out] * 3,
        compiler_params=pltpu.CompilerParams(dimension_semantics=("arbitrary",)),
        name="mla_prep",
    )(pos, freq, c_q, c_kv, k_r, gq2, gkv2, wq, wk, wv)


def _mla_attn_kernel(q_ref, k_ref, v_ref, o_ref, m_scr, acc_scr):
    i = pl.program_id(1)
    tq = q_ref.shape[1]
    row = i * tq + lax.broadcasted_iota(jnp.int32, (tq, MLA_TK), 0)
    col = lax.broadcasted_iota(jnp.int32, (tq, MLA_TK), 1)
    n_full = (i * tq) // MLA_TK
    heads = range(GROUP_HEADS)
    cols = [slice(h * LANE, (h + 1) * LANE) for h in heads]
    q = [q_ref[0, :, cols[h]] for h in heads]
    m_scr[...] = jnp.full(m_scr.shape, NEG, F32)
    acc_scr[...] = jnp.zeros(acc_scr.shape, F32)

    def step(j, masked):
        start = pl.multiple_of(j * MLA_TK, MLA_TK)
        s = [_dot_nt(q[h], k_ref[0, pl.ds(start, MLA_TK), cols[h]]) for h in heads]
        if masked:
            s = [jnp.where(start + col <= row, sh, NEG) for sh in s]
        m_old = [m_scr[h] for h in heads]
        m_new = [jnp.maximum(m_old[h], jnp.max(s[h], axis=-1, keepdims=True)) for h in heads]
        p = [jnp.exp(s[h] - m_new[h]).astype(BF16) for h in heads]
        pv = [_dot(p[h], v_ref[0, pl.ds(start, MLA_TK), cols[h]]) for h in heads]
        for h in heads:
            acc_scr[h] = jnp.exp(m_old[h] - m_new[h]) * acc_scr[h] + pv[h]
            m_scr[h] = m_new[h]

    def body(j, carry):
        step(j, False)
        return carry

    lax.fori_loop(0, n_full, body, 0)
    step(n_full, True)
    outs = [acc_scr[h][:, :V_HEAD] * (1.0 / acc_scr[h][:, V_HEAD:V_HEAD + 1]) for h in heads]
    o_ref[0] = jnp.concatenate(outs, axis=1)


def mla_attention(q, k, v):
    b, s, w = q.shape
    return pl.pallas_call(
        _mla_attn_kernel,
        grid=(b, s // MLA_TQ),
        in_specs=[pl.BlockSpec((1, MLA_TQ, w), lambda bi, i: (bi, i, 0)),
                  pl.BlockSpec((1, s, w), lambda bi, i: (bi, 0, 0)),
                  pl.BlockSpec((1, s, w), lambda bi, i: (bi, 0, 0))],
        out_specs=pl.BlockSpec((1, MLA_TQ, GROUP_HEADS * V_HEAD), lambda bi, i: (bi, i, 0)),
        out_shape=jax.ShapeDtypeStruct((b, s, GROUP_HEADS * V_HEAD), F32),
        scratch_shapes=[pltpu.VMEM((GROUP_HEADS, MLA_TQ, 1), F32), pltpu.VMEM((GROUP_HEADS, MLA_TQ, LANE), F32)],
        compiler_params=pltpu.CompilerParams(dimension_semantics=("arbitrary", "arbitrary"),
                                             vmem_limit_bytes=VMEM_LIMIT),
        name="mla_attention",
    )(q, k, v)


def mla_group(h, positions, q_norm_g, kv_norm_g, w_uq, w_ukv):
    b, s, _ = h.shape
    t = b * s
    h2 = h.reshape(t, -1)
    c_q, c_kv, k_r = h2[:, :Q_LORA], h2[:, Q_LORA:Q_LORA + KV_LORA], h2[:, Q_LORA + KV_LORA:]
    q, k, v = mla_prep(positions.reshape(t, 1), c_q, c_kv, k_r, q_norm_g, kv_norm_g, w_uq, w_ukv)
    w = GROUP_HEADS * LANE
    return mla_attention(q.reshape(b, s, w), k.reshape(b, s, w), v.reshape(b, s, w))


CHUNK = 64
CONV_W = 4


def _split3(a):
    hi = a.astype(BF16)
    r = a - hi.astype(F32)
    mid = r.astype(BF16)
    return hi, mid, (r - mid.astype(F32)).astype(BF16)


def _dot_exact_rhs(sel, a):
    return sum(_dot(sel, p) for p in _split3(a))


def _row_matrix(col, n_rows):
    l = col.shape[0]
    lane = lax.broadcasted_iota(jnp.int32, (l, SUBLANES), 1)
    padded = jnp.where(lane == 0, col, 0.0)
    ones = jnp.ones((n_rows, SUBLANES), BF16)
    return sum(_dot_nt(ones, p) for p in _split3(padded))


def _transpose_bf16(x_bf):
    d = x_bf.shape[1]
    eye = (lax.broadcasted_iota(jnp.int32, (d, d), 0) == lax.broadcasted_iota(jnp.int32, (d, d), 1)).astype(BF16)
    return _dot_nt(eye, x_bf).astype(BF16)


def _log_sigmoid(x):
    return jnp.minimum(x, 0.0) - jnp.log1p(jnp.exp(-jnp.abs(x)))


def _softplus(x):
    return jnp.maximum(x, 0.0) + jnp.log1p(jnp.exp(-jnp.abs(x)))


def _with_ones_col(v):
    l, d = v.shape
    lane = lax.broadcasted_iota(jnp.int32, (l, LANE - d), 1)
    return jnp.concatenate([v, jnp.where(lane == 0, 1.0, 0.0)], axis=1)


def _bf(x):
    return x.astype(BF16)


def _mlstm_kernel(q_ref, k_ref, v_ref, gate_ref, og_ref, bias_ref, ng_ref, o_ref, st_scr, m_scr):
    nb = q_ref.shape[0]
    H, D, L = GROUP_HEADS, HEAD_DIM, CHUNK
    chains = [(b, h) for b in range(nb) for h in range(H)]
    cols = lambda h: slice(h * D, (h + 1) * D)
    each = lambda fn: [fn(c, b, h) for c, (b, h) in enumerate(chains)]

    @pl.when(pl.program_id(0) == 0)
    def _():
        st_scr[...] = jnp.zeros(st_scr.shape, F32)
        m_scr[...] = jnp.zeros(m_scr.shape, F32)

    row = lax.broadcasted_iota(jnp.int32, (L, L), 0)
    col = lax.broadcasted_iota(jnp.int32, (L, L), 1)
    tri = row >= col
    tri_bf = tri.astype(BF16)
    gates = [gate_ref[b] + bias_ref[...] for b in range(nb)]
    bcum = [_dot_exact_rhs(tri_bf, _log_sigmoid(g[:, H:])) for g in gates]
    q = each(lambda c, b, h: _bf(q_ref[b, :, cols(h)]))
    kt = each(lambda c, b, h: _transpose_bf16(_bf(k_ref[b, :, cols(h)] * (D ** -0.5))))
    v_aug = each(lambda c, b, h: _with_ones_col(v_ref[b, :, cols(h)]))
    bc = each(lambda c, b, h: bcum[b][:, h:h + 1])
    ic = each(lambda c, b, h: gates[b][:, h:h + 1])
    m_st = each(lambda c, b, h: m_scr[c, 0:1, 0:1])
    st = each(lambda c, b, h: st_scr[c])
    log_d = each(lambda c, b, h: jnp.where(tri, bc[c] + _row_matrix(ic[c] - bc[c], L), -jnp.inf))
    m_t = each(lambda c, b, h: jnp.maximum(jnp.max(log_d[c], axis=-1, keepdims=True), bc[c] + m_st[c]))
    sqk = each(lambda c, b, h: _dot(q[c], kt[c]) * jnp.exp(log_d[c] - m_t[c]))
    inter = each(lambda c, b, h: jnp.exp(bc[c] + m_st[c] - m_t[c]))
    tot = each(lambda c, b, h: _dot(_bf(sqk[c]), _bf(v_aug[c])) + inter[c] * _dot(q[c], _bf(st[c])))
    h_t = each(lambda c, b, h: tot[c][:, :D] * (1.0 / jnp.maximum(jnp.abs(tot[c][:, D:D + 1]), jnp.exp(-m_t[c]))))
    b_end = each(lambda c, b, h: bc[c][L - 1:L])
    log_w = each(lambda c, b, h: b_end[c] - bc[c] + ic[c])
    m_new = each(lambda c, b, h: jnp.maximum(b_end[c] + m_st[c], jnp.max(log_w[c], axis=0, keepdims=True)))
    upd = each(lambda c, b, h: _dot(kt[c], _bf(jnp.exp(log_w[c] - m_new[c]) * v_aug[c])))
    for c in range(len(chains)):
        st_scr[c] = jnp.exp(b_end[c] + m_st[c] - m_new[c]) * st[c] + upd[c]
        m_scr[c] = jnp.broadcast_to(m_new[c], m_scr.shape[1:])
    for b in range(nb):
        outs = []
        for h in range(H):
            o = jax.nn.sigmoid(og_ref[b, :, cols(h)]) * h_t[b * H + h]
            outs.append(o * lax.rsqrt(jnp.mean(o * o, axis=-1, keepdims=True) + NORM_EPS) * ng_ref[:, cols(h)])
        o_ref[b] = jnp.concatenate(outs, axis=1)


def mlstm_group(h, i_bias, f_bias, norm_g):
    b, s, _ = h.shape
    W, H = GROUP_W, GROUP_HEADS
    q, k, v = h[..., :W], h[..., W:2 * W], h[..., 2 * W:3 * W]
    gates = h[..., 3 * W:3 * W + 2 * H]
    og = h[..., 3 * W + 2 * H:]
    bias = jnp.concatenate([i_bias, f_bias]).reshape(1, 2 * H)
    tok = lambda n: pl.BlockSpec((b, CHUNK, n), lambda i: (0, i, 0))
    fixed = lambda n: pl.BlockSpec((1, n), lambda i: (0, 0))
    return pl.pallas_call(
        _mlstm_kernel,
        grid=(s // CHUNK,),
        in_specs=[tok(W), tok(W), tok(W), tok(2 * H), tok(W), fixed(2 * H), fixed(W)],
        out_specs=tok(W),
        out_shape=jax.ShapeDtypeStruct((b, s, W), F32),
        scratch_shapes=[pltpu.VMEM((b * H, HEAD_DIM, LANE), F32), pltpu.VMEM((b * H, SUBLANES, LANE), F32)],
        compiler_params=pltpu.CompilerParams(dimension_semantics=("arbitrary",)),
        name="mlstm",
    )(q, k, v, gates, og, bias, norm_g.reshape(1, W))


def _gdn_kernel(x_ref, xprev_ref, cw_ref, gate_ref, z_ref, par_ref, ng_ref, o_ref, st_scr):
    nb = x_ref.shape[0]
    H, D, L = GROUP_HEADS, HEAD_DIM, CHUNK
    W = H * D
    chains = [(b, h) for b in range(nb) for h in range(H)]
    each = lambda fn: [fn(c, b, h) for c, (b, h) in enumerate(chains)]
    first = pl.program_id(0) == 0

    @pl.when(first)
    def _():
        st_scr[...] = jnp.zeros(st_scr.shape, F32)

    row = lax.broadcasted_iota(jnp.int32, (L, L), 0)
    col = lax.broadcasted_iota(jnp.int32, (L, L), 1)
    tri = row >= col
    tri_bf = tri.astype(BF16)
    eye = (row == col).astype(F32)
    par = par_ref[...]
    qkv, beta4, gc4 = [], [], []
    for b in range(nb):
        prev = jnp.where(first, 0.0, xprev_ref[b])
        full = jnp.concatenate([prev, x_ref[b]], axis=0)
        acc = full[SUBLANES:] * cw_ref[CONV_W - 1:CONV_W, :]
        for j in range(CONV_W - 1):
            shifted = pltpu.roll(full, CONV_W - 1 - j, axis=0)[SUBLANES:]
            acc = acc + shifted * cw_ref[j:j + 1, :]
        qkv.append(acc * jax.nn.sigmoid(acc))
        gates = gate_ref[b]
        beta4.append(jax.nn.sigmoid(gates[:, :H]))
        gc4.append(_dot_exact_rhs(tri_bf, -jnp.exp(par[:, :H]) * _softplus(gates[:, H:] + par[:, H:])))

    def l2n(t):
        return t * lax.rsqrt(jnp.sum(t * t, axis=-1, keepdims=True) + NORM_EPS)

    q = each(lambda c, b, h: l2n(qkv[b][:, h * D:(h + 1) * D]) * (D ** -0.5))
    k = each(lambda c, b, h: l2n(qkv[b][:, W + h * D:W + (h + 1) * D]))
    v = each(lambda c, b, h: qkv[b][:, 2 * W + h * D:2 * W + (h + 1) * D])
    beta = each(lambda c, b, h: beta4[b][:, h:h + 1])
    gc = each(lambda c, b, h: gc4[b][:, h:h + 1])
    gc_row = each(lambda c, b, h: _row_matrix(gc[c], L))
    decay = each(lambda c, b, h: jnp.where(tri, jnp.exp(gc[c] - gc_row[c]), 0.0))
    kt = each(lambda c, b, h: _transpose_bf16(_bf(k[c])))
    kb = each(lambda c, b, h: k[c] * beta[c])
    x = each(lambda c, b, h: -jnp.where(row > col, _dot(_bf(kb[c]), kt[c]) * decay[c], 0.0))
    t_inv = [eye + xc for xc in x]
    for _ in range(5):
        x = [_dot(_bf(xc), _bf(xc)) for xc in x]
        t_inv = [tc + _dot(_bf(tc), _bf(xc)) for tc, xc in zip(t_inv, x)]
    u = each(lambda c, b, h: _dot(_bf(t_inv[c]), _bf(v[c] * beta[c])))
    wm = each(lambda c, b, h: _dot(_bf(t_inv[c]), _bf(kb[c] * jnp.exp(gc[c]))))
    attn = each(lambda c, b, h: _dot(_bf(q[c]), kt[c]) * decay[c])
    st = each(lambda c, b, h: st_scr[c])
    v_new = each(lambda c, b, h: u[c] - _dot(_bf(wm[c]), _bf(st[c])))
    o = each(lambda c, b, h: _dot(_bf(q[c] * jnp.exp(gc[c])), _bf(st[c])) + _dot(_bf(attn[c]), _bf(v_new[c])))
    g_end = each(lambda c, b, h: gc[c][L - 1:L])
    upd = each(lambda c, b, h: _dot(_bf(kt[c].astype(F32) * jnp.exp(g_end[c] - gc_row[c])), _bf(v_new[c])))
    for c in range(len(chains)):
        st_scr[c] = st[c] * jnp.exp(g_end[c]) + upd[c]
    for b in range(nb):
        outs = []
        for h in range(H):
            oc = o[b * H + h]
            z = z_ref[b, :, h * D:(h + 1) * D]
            y = oc * lax.rsqrt(jnp.mean(oc * oc, axis=-1, keepdims=True) + NORM_EPS) * ng_ref[...]
            outs.append(y * (z * jax.nn.sigmoid(z)))
        o_ref[b] = jnp.concatenate(outs, axis=1)


def gdn_group(h, conv_w, a_log, dt_bias, norm_g):
    b, s, _ = h.shape
    W, H = GROUP_W, GROUP_HEADS
    x = h[..., :3 * W]
    gates = h[..., 3 * W:3 * W + 2 * H]
    z = h[..., 3 * W + 2 * H:]
    par = jnp.concatenate([a_log, dt_bias]).reshape(1, 2 * H)
    tok = lambda n: pl.BlockSpec((b, CHUNK, n), lambda i: (0, i, 0))
    per_chunk = CHUNK // SUBLANES
    prev = pl.BlockSpec((b, SUBLANES, 3 * W), lambda i: (0, jnp.maximum(i * per_chunk - 1, 0), 0))
    fixed = lambda r, n: pl.BlockSpec((r, n), lambda i: (0, 0))
    return pl.pallas_call(
        _gdn_kernel,
        grid=(s // CHUNK,),
        in_specs=[tok(3 * W), prev, fixed(CONV_W, 3 * W), tok(2 * H), tok(W), fixed(1, 2 * H), fixed(1, HEAD_DIM)],
        out_specs=tok(W),
        out_shape=jax.ShapeDtypeStruct((b, s, W), F32),
        scratch_shapes=[pltpu.VMEM((b * H, HEAD_DIM, HEAD_DIM), F32)],
        compiler_params=pltpu.CompilerParams(dimension_semantics=("arbitrary",)),
        name="gdn",
    )(x, x, conv_w, gates, z, par, norm_g.reshape(1, HEAD_DIM))


NSA_IN = GROUP_W + 6 * HEAD_DIM + 3 * GROUP_HEADS
MLSTM_IN = 4 * GROUP_W + 2 * GROUP_HEADS
GDN_IN = 4 * GROUP_W + 2 * GROUP_HEADS
MLA_IN = Q_LORA + KV_LORA + QK_ROPE


def kernel(x, positions, t5_table, norm1_g, norm2_g, final_norm_g, w_in, w_out, nsa_w_cmp_k, nsa_w_cmp_v, nsa_cmp_pos, mlstm_i_bias, mlstm_f_bias, mlstm_norm_g, gdn_conv_w, gdn_a_log, gdn_dt_bias, gdn_norm_g, mla_q_norm_g, mla_kv_norm_g, mla_w_uq, mla_w_ukv, peer_w_q, peer_sub_keys, peer_u, peer_v):
    b, s, d = x.shape
    t = b * s
    h = x.reshape(t, d)
    depth = w_in.shape[0]
    d_in = w_in.shape[2]
    d_in_pad = _round_up(d_in, LANE)
    bias_tiles = nsa_bias_tiles(t5_table)
    for l in range(depth):
        w_in_l = jnp.pad(w_in[l], ((0, 0), (0, d_in_pad - d_in))).astype(BF16)
        hp = norm_matmul(h, norm1_g[l], w_in_l)[:, :d_in].reshape(b, s, d_in)
        c0, c1, c2 = NSA_IN, NSA_IN + MLSTM_IN, NSA_IN + MLSTM_IN + GDN_IN
        h_nsa, h_ml, h_gdn, h_mla = hp[..., :c0], hp[..., c0:c1], hp[..., c1:c2], hp[..., c2:]
        o = jnp.concatenate([
            nsa_group(h_nsa, nsa_w_cmp_k[l], nsa_w_cmp_v[l], nsa_cmp_pos[l], t5_table, bias_tiles),
            mlstm_group(h_ml, mlstm_i_bias[l], mlstm_f_bias[l], mlstm_norm_g[l]),
            gdn_group(h_gdn, gdn_conv_w[l], gdn_a_log[l], gdn_dt_bias[l], gdn_norm_g[l]),
            mla_group(h_mla, positions, mla_q_norm_g[l], mla_kv_norm_g[l], mla_w_uq[l], mla_w_ukv[l]),
        ], axis=-1)
        h = matmul_residual(o.reshape(t, -1), w_out[l].astype(BF16), h)
        h = peer_ffn_residual(h, norm2_g[l], peer_w_q[l], peer_sub_keys[l], peer_u[l], peer_v[l])
    return rmsnorm(h, final_norm_g).reshape(b, s, d)
```

```python
import functools
import math

import jax
import jax.numpy as jnp
import numpy as np
from jax import lax
from jax.experimental import pallas as pl
from jax.experimental.pallas import tpu as pltpu

F32 = jnp.float32
BF16 = jnp.bfloat16
NORM_EPS = 1e-6
NEG = -1e30
LANE = 128
ROW_TILE = 512
VMEM_LIMIT = 56 * 1024 * 1024

HEAD_DIM = 64
GROUP_HEADS = 4
GROUP_W = GROUP_HEADS * HEAD_DIM
Q_BLOCK = 128
CMP_LEN = 32
CMP_STRIDE = 16
SEL_LEN = 64
SEL_TOPN = 16
WINDOW = 512
SEL_FORCE = 1e3
N_BUCKETS = 32
MAX_DISTANCE = 128


def _round_up(n, m):
    return (n + m - 1) // m * m


def _dot_nt(a, b):
    return lax.dot_general(a, b, (((1,), (1,)), ((), ())), preferred_element_type=F32)


def _dot(a, b):
    return jnp.dot(a, b, preferred_element_type=F32)


def _norm_matmul_kernel(x_ref, g_ref, w_ref, o_ref):
    x = x_ref[...]
    y = x * lax.rsqrt(jnp.mean(x * x, axis=-1, keepdims=True) + NORM_EPS) * g_ref[...]
    o_ref[...] = _dot(y.astype(BF16), w_ref[...])


def norm_matmul(x, g, w):
    t, d = x.shape
    n = w.shape[1]
    return pl.pallas_call(
        _norm_matmul_kernel,
        grid=(t // ROW_TILE,),
        in_specs=[pl.BlockSpec((ROW_TILE, d), lambda i: (i, 0)),
                  pl.BlockSpec((1, d), lambda i: (0, 0)),
                  pl.BlockSpec((d, n), lambda i: (0, 0))],
        out_specs=pl.BlockSpec((ROW_TILE, n), lambda i: (i, 0)),
        out_shape=jax.ShapeDtypeStruct((t, n), F32),
        compiler_params=pltpu.CompilerParams(dimension_semantics=("arbitrary",),
                                             vmem_limit_bytes=VMEM_LIMIT),
        name="norm_matmul",
    )(x, g.reshape(1, d), w)


def _matmul_residual_kernel(o_ref, w_ref, r_ref, out_ref):
    out_ref[...] = r_ref[...] + _dot(o_ref[...].astype(BF16), w_ref[...])


def matmul_residual(o, w, res):
    t, d = o.shape
    n = w.shape[1]
    return pl.pallas_call(
        _matmul_residual_kernel,
        grid=(t // ROW_TILE,),
        in_specs=[pl.BlockSpec((ROW_TILE, d), lambda i: (i, 0)),
                  pl.BlockSpec((d, n), lambda i: (0, 0)),
                  pl.BlockSpec((ROW_TILE, n), lambda i: (i, 0))],
        out_specs=pl.BlockSpec((ROW_TILE, n), lambda i: (i, 0)),
        out_shape=jax.ShapeDtypeStruct((t, n), F32),
        compiler_params=pltpu.CompilerParams(dimension_semantics=("arbitrary",),
                                             vmem_limit_bytes=VMEM_LIMIT),
        name="matmul_residual",
    )(o, w, res)


def _rmsnorm_kernel(x_ref, g_ref, o_ref):
    x = x_ref[...]
    o_ref[...] = x * lax.rsqrt(jnp.mean(x * x, axis=-1, keepdims=True) + NORM_EPS) * g_ref[...]


def rmsnorm(x, g):
    t, d = x.shape
    return pl.pallas_call(
        _rmsnorm_kernel,
        grid=(t // ROW_TILE,),
        in_specs=[pl.BlockSpec((ROW_TILE, d), lambda i: (i, 0)),
                  pl.BlockSpec((1, d), lambda i: (0, 0))],
        out_specs=pl.BlockSpec((ROW_TILE, d), lambda i: (i, 0)),
        out_shape=jax.ShapeDtypeStruct((t, d), F32),
        compiler_params=pltpu.CompilerParams(dimension_semantics=("arbitrary",)),
        name="rmsnorm",
    )(x, g.reshape(1, d))


def _t5_bucket_starts():
    d = np.arange(0, 2 * MAX_DISTANCE)
    max_exact = N_BUCKETS // 2
    val = (np.log(np.maximum(d, 1).astype(np.float32) / np.float32(max_exact))
           / np.float32(math.log(MAX_DISTANCE / max_exact)) * np.float32(N_BUCKETS - max_exact))
    large = np.minimum(max_exact + val.astype(np.int32), N_BUCKETS - 1)
    bucket = np.where(d < max_exact, d, large)
    assert np.all(np.diff(bucket) >= 0)
    return [(int(k), int(d[bucket == k].min())) for k in range(N_BUCKETS) if np.any(bucket == k)]


T5_STARTS = _t5_bucket_starts()


def _t5_bias_heads(dist, tab_ref):
    accs = [jnp.full(dist.shape, tab_ref[0, h], F32) for h in range(GROUP_HEADS)]
    for bucket, start in T5_STARTS[1:]:
        m = dist >= start
        accs = [jnp.where(m, tab_ref[bucket, h], accs[h]) for h in range(GROUP_HEADS)]
    return jnp.stack(accs)


def _nsa_compress_kernel(x_ref, pos_ref, w_ref, o_ref):
    x = x_ref[0, 0]
    n = x.shape[0]
    lo = _dot((x + pos_ref[0, 0:1, :]).astype(BF16), w_ref[0, 0])
    hi = _dot((x + pos_ref[0, 1:2, :]).astype(BF16), w_ref[0, 1])
    o_ref[0, 0] = (lo + pltpu.roll(hi, n - 1, axis=0)).astype(o_ref.dtype)


def nsa_compress(x, pos, w):
    two, b, n, c = x.shape
    return pl.pallas_call(
        _nsa_compress_kernel,
        grid=(two, b),
        in_specs=[pl.BlockSpec((1, 1, n, c), lambda k, i: (k, i, 0, 0)),
                  pl.BlockSpec((1, 2, c), lambda k, i: (0, 0, 0)),
                  pl.BlockSpec((1, 2, c, HEAD_DIM), lambda k, i: (k, 0, 0, 0))],
        out_specs=pl.BlockSpec((1, 1, n, HEAD_DIM), lambda k, i: (k, i, 0, 0)),
        out_shape=jax.ShapeDtypeStruct((two, b, n, HEAD_DIM), BF16),
        compiler_params=pltpu.CompilerParams(dimension_semantics=("arbitrary", "arbitrary")),
        name="nsa_compress",
    )(x, pos, w)


FAR_TILE = 512
NEAR_KEYS = 768
WIN_KEYS = WINDOW + Q_BLOCK
NEAR_PAD = NEAR_KEYS - Q_BLOCK
CMP_TILE = LANE
CMP_TILES_BACK = CMP_TILE * CMP_STRIDE // Q_BLOCK
N_CMP_BIAS = CMP_TILES_BACK + 3


def _masked_bias(dist, valid, tab_ref):
    return jnp.where(valid, _t5_bias_heads(dist, tab_ref), NEG)


def _nsa_bias_kernel(tab_ref, near_ref, win_ref):
    q = lax.broadcasted_iota(jnp.int32, (Q_BLOCK, NEAR_KEYS), 0)
    k = lax.broadcasted_iota(jnp.int32, (Q_BLOCK, NEAR_KEYS), 1)
    dist = NEAR_PAD + q - k
    near_ref[...] = _masked_bias(dist, dist >= 0, tab_ref)
    q = lax.broadcasted_iota(jnp.int32, (Q_BLOCK, WIN_KEYS), 0)
    k = lax.broadcasted_iota(jnp.int32, (Q_BLOCK, WIN_KEYS), 1)
    dist = WINDOW + q - k
    win_ref[...] = _masked_bias(dist, (dist >= 0) & (dist < WINDOW), tab_ref)


def _nsa_cmp_bias_kernel(tab_ref, o_ref):
    e = pl.program_id(0)
    q = lax.broadcasted_iota(jnp.int32, (Q_BLOCK, CMP_TILE), 0)
    n = lax.broadcasted_iota(jnp.int32, (Q_BLOCK, CMP_TILE), 1)
    dist = (e - 1) * Q_BLOCK + q - (n * CMP_STRIDE + CMP_LEN - 1)
    o_ref[0] = _masked_bias(dist, dist >= 0, tab_ref)


def nsa_bias_tiles(t5_table):
    H, Q = GROUP_HEADS, Q_BLOCK
    smem = pl.BlockSpec(memory_space=pltpu.SMEM)
    near, win = pl.pallas_call(
        _nsa_bias_kernel, grid=(1,), in_specs=[smem],
        out_specs=[pl.BlockSpec((H, Q, NEAR_KEYS), lambda i: (0, 0, 0)),
                   pl.BlockSpec((H, Q, WIN_KEYS), lambda i: (0, 0, 0))],
        out_shape=[jax.ShapeDtypeStruct((H, Q, NEAR_KEYS), F32), jax.ShapeDtypeStruct((H, Q, WIN_KEYS), F32)],
        name="nsa_bias",
    )(t5_table)
    cmp = pl.pallas_call(
        _nsa_cmp_bias_kernel, grid=(N_CMP_BIAS,), in_specs=[smem],
        out_specs=pl.BlockSpec((1, H, Q, CMP_TILE), lambda e: (e, 0, 0, 0)),
        out_shape=jax.ShapeDtypeStruct((N_CMP_BIAS, H, Q, CMP_TILE), F32),
        name="nsa_cmp_bias",
    )(t5_table)
    return near, win, cmp


def _nsa_attn_kernel(tab_ref, q_ref, g_ref, kc_ref, vc_ref, ks_ref, vs_ref, kw_ref, vw_ref,
                     bnear_ref, bwin_ref, bcmp_ref, o_ref, m_scr, acc_scr):
    H, Q, D = GROUP_HEADS, Q_BLOCK, HEAD_DIM
    n_cmp = kc_ref.shape[1]
    n_sel = (ks_ref.shape[1] - NEAR_PAD) // SEL_LEN
    i = pl.program_id(1)
    t0 = i * Q

    q = q_ref[0] * (D ** -0.5)
    q4 = jnp.concatenate([q[:, h * D:(h + 1) * D] for h in range(H)], axis=0).astype(BF16)

    s = _dot_nt(q4, kc_ref[0]).reshape(H, Q, n_cmp)
    tiles = []
    for g in range(n_cmp // CMP_TILE):
        e = jnp.clip(i - g * CMP_TILES_BACK, -1, CMP_TILES_BACK + 1) + 1
        tiles.append(s[:, :, g * CMP_TILE:(g + 1) * CMP_TILE] + bcmp_ref[e])
    s = jnp.concatenate(tiles, axis=-1)
    t_c = t0 + lax.broadcasted_iota(jnp.int32, (Q, n_cmp), 0)
    valid_c = t_c >= lax.broadcasted_iota(jnp.int32, (Q, n_cmp), 1) * CMP_STRIDE + CMP_LEN - 1
    p = jnp.where(valid_c, jnp.exp(s - jnp.max(s, axis=-1, keepdims=True)), 0.0)
    l = jnp.sum(p, axis=-1, keepdims=True)
    p = p * (1.0 / jnp.where(l > 0.0, l, 1.0))
    o_c = _dot(p.reshape(H * Q, n_cmp).astype(BF16), vc_ref[0])

    psum = p[0] + p[1] + p[2] + p[3]
    c_idx = lax.broadcasted_iota(jnp.int32, (n_cmp, n_sel), 0)
    j_idx = lax.broadcasted_iota(jnp.int32, (n_cmp, n_sel), 1)
    ratio = SEL_LEN // CMP_STRIDE
    n_in_sel = (SEL_LEN - CMP_LEN) // CMP_STRIDE + 1
    pool = ((c_idx // ratio == j_idx) & (c_idx % ratio < n_in_sel)).astype(BF16)
    p_hi = psum.astype(BF16)
    p_mid = (psum - p_hi.astype(F32)).astype(BF16)
    p_lo = (psum - p_hi.astype(F32) - p_mid.astype(F32)).astype(BF16)
    imp = _dot(p_hi, pool) + _dot(p_mid, pool) + _dot(p_lo, pool)

    row0 = pl.multiple_of(t0, Q)
    kw = kw_ref[0, pl.ds(row0, WIN_KEYS), :]
    vw = vw_ref[0, pl.ds(row0, WIN_KEYS), :]
    in_seq = lax.broadcasted_iota(jnp.int32, (1, WIN_KEYS), 1) >= WINDOW - t0
    s = _dot_nt(q4, kw).reshape(H, Q, WIN_KEYS) + bwin_ref[...] + jnp.where(in_seq, 0.0, NEG)
    p_w = jnp.exp(s - jnp.max(s, axis=-1, keepdims=True))
    acc_w = _dot(p_w.reshape(H * Q, WIN_KEYS).astype(BF16), vw)
    o_w = acc_w[:, :D] * (1.0 / acc_w[:, D:D + 1])

    jb = lax.broadcasted_iota(jnp.int32, (Q, n_sel), 1)
    cur = (t0 + lax.broadcasted_iota(jnp.int32, (Q, n_sel), 0)) // SEL_LEN
    forced = (jb == 0) | (jb == cur) | (jb == cur - 1)
    work = jnp.where(jb <= cur, imp + jnp.where(forced, SEL_FORCE, 0.0), NEG)
    sel = jnp.zeros((Q, n_sel), F32)
    jb_f = jb.astype(F32)
    for _ in range(min(SEL_TOPN, n_sel)):
        mx = jnp.max(work, axis=-1, keepdims=True)
        first = jnp.min(jnp.where(work == mx, jb_f, jnp.inf), axis=-1, keepdims=True)
        pick = jb_f == first
        sel = jnp.where(pick & (mx > 0.5 * NEG), 1.0, sel)
        work = jnp.where(pick, -jnp.inf, work)
    sel_bf = sel.astype(BF16)

    far_bias = jnp.stack([jnp.full((Q, 1), tab_ref[N_BUCKETS - 1, h], F32) for h in range(H)])
    m_scr[...] = jnp.full(m_scr.shape, NEG, F32)
    acc_scr[...] = jnp.zeros(acc_scr.shape, F32)

    def sel_scores(row0, n_keys, bias, first_key):
        s = _dot_nt(q4, ks_ref[0, pl.ds(row0, n_keys), :]).reshape(H, Q, n_keys) + bias
        pos = row0 - NEAR_PAD + lax.broadcasted_iota(jnp.int32, (n_sel, n_keys), 1)
        blk = jnp.where(pos >= first_key, pos // SEL_LEN, -1)
        expand = (lax.broadcasted_iota(jnp.int32, (n_sel, n_keys), 0) == blk).astype(BF16)
        mask = _dot(sel_bf, expand) > 0.5
        return jnp.where(mask, s, NEG), mask

    def sel_absorb(row0, n_keys, s, mask):
        m_old = m_scr[...].reshape(H, Q, 1)
        m_new = jnp.maximum(m_old, jnp.max(s, axis=-1, keepdims=True))
        p = jnp.where(mask, jnp.exp(s - m_new), 0.0).reshape(H * Q, n_keys)
        alpha = jnp.exp(m_old - m_new).reshape(H * Q, 1)
        acc_scr[...] = alpha * acc_scr[...] + _dot(p.astype(BF16), vs_ref[0, pl.ds(row0, n_keys), :])
        m_scr[...] = m_new.reshape(H * Q, 1)

    def sel_step(row0, n_keys, bias, first_key):
        sel_absorb(row0, n_keys, *sel_scores(row0, n_keys, bias, first_key))

    n_far = jnp.maximum(i - 1, 0) // (FAR_TILE // Q)

    def far_body(kp, carry):
        row_a = pl.multiple_of(NEAR_PAD + kp * (2 * FAR_TILE), Q)
        row_b = pl.multiple_of(row_a + FAR_TILE, Q)
        sa = sel_scores(row_a, FAR_TILE, far_bias, 0)
        sb = sel_scores(row_b, FAR_TILE, far_bias, 0)
        sel_absorb(row_a, FAR_TILE, *sa)
        sel_absorb(row_b, FAR_TILE, *sb)
        return carry

    lax.fori_loop(0, n_far // 2, far_body, 0)

    @pl.when(n_far % 2 == 1)
    def _():
        sel_step(pl.multiple_of(NEAR_PAD + (n_far - 1) * FAR_TILE, Q), FAR_TILE, far_bias, 0)
    sel_step(row0, NEAR_KEYS, bnear_ref[...], n_far * FAR_TILE)
    acc = acc_scr[...]
    o_s = acc[:, :D] * (1.0 / acc[:, D:D + 1])

    g = jax.nn.sigmoid(g_ref[0])
    outs = []
    for h in range(H):
        rows = slice(h * Q, (h + 1) * Q)
        outs.append(g[:, h:h + 1] * o_c[rows] + g[:, H + h:H + h + 1] * o_s[rows]
                    + g[:, 2 * H + h:2 * H + h + 1] * o_w[rows])
    o_ref[0] = jnp.concatenate(outs, axis=1)


def nsa_attention(t5_table, bias_tiles, q, gates, k_cmp, v_cmp, ks, vs_aug, kw, vw_aug):
    b, s, _ = q.shape
    bnear, bwin, bcmp = bias_tiles
    full = lambda a: pl.BlockSpec((1,) + a.shape[1:], lambda bi, i: (bi,) + (0,) * (a.ndim - 1))
    fixed = lambda a: pl.BlockSpec(a.shape, lambda bi, i: (0,) * a.ndim)
    return pl.pallas_call(
        _nsa_attn_kernel,
        grid=(b, s // Q_BLOCK),
        in_specs=[pl.BlockSpec(memory_space=pltpu.SMEM),
                  pl.BlockSpec((1, Q_BLOCK, GROUP_W), lambda bi, i: (bi, i, 0)),
                  pl.BlockSpec((1, Q_BLOCK, 3 * GROUP_HEADS), lambda bi, i: (bi, i, 0)),
                  full(k_cmp), full(v_cmp), full(ks), full(vs_aug), full(kw), full(vw_aug),
                  fixed(bnear), fixed(bwin), fixed(bcmp)],
        out_specs=pl.BlockSpec((1, Q_BLOCK, GROUP_W), lambda bi, i: (bi, i, 0)),
        out_shape=jax.ShapeDtypeStruct((b, s, GROUP_W), F32),
        scratch_shapes=[pltpu.VMEM((GROUP_HEADS * Q_BLOCK, 1), F32),
                        pltpu.VMEM((GROUP_HEADS * Q_BLOCK, 2 * HEAD_DIM), F32)],
        compiler_params=pltpu.CompilerParams(dimension_semantics=("arbitrary", "arbitrary"),
                                             vmem_limit_bytes=VMEM_LIMIT),
        name="nsa_attention",
    )(t5_table, q, gates, k_cmp, v_cmp, ks, vs_aug, kw, vw_aug, bnear, bwin, bcmp)


def _front_pad(x, rows):
    return jnp.pad(x, ((0, 0), (rows, 0), (0, 0)))


def _with_ones(v):
    b, s, d = v.shape
    pad = jnp.zeros((b, s, d), BF16).at[:, :, 0].set(1.0)
    return jnp.concatenate([v.astype(BF16), pad], axis=-1)


def nsa_group(h, w_cmp_k, w_cmp_v, cmp_pos, t5_table, bias_tiles=None):
    if bias_tiles is None:
        bias_tiles = nsa_bias_tiles(t5_table)
    b, s, _ = h.shape
    d = HEAD_DIM
    q = h[..., :GROUP_W]
    kc, vc, ks, vs, kw, vw = [h[..., GROUP_W + j * d: GROUP_W + (j + 1) * d] for j in range(6)]
    gates = h[..., GROUP_W + 6 * d:]
    n = s // CMP_STRIDE
    x = jnp.stack([kc, vc]).reshape(2, b, n, CMP_STRIDE * d)
    pos = cmp_pos.reshape(1, 2, CMP_STRIDE * d)
    w = jnp.stack([w_cmp_k, w_cmp_v]).reshape(2, 2, CMP_STRIDE * d, d).astype(BF16)
    cmp = nsa_compress(x, pos, w)
    return nsa_attention(t5_table, bias_tiles, q, gates, cmp[0], cmp[1],
                         _front_pad(ks.astype(BF16), NEAR_PAD), _front_pad(_with_ones(vs), NEAR_PAD),
                         _front_pad(kw.astype(BF16), WINDOW), _front_pad(_with_ones(vw), WINDOW))


PEER_HEADS = 8
PEER_DK = 256
N_KEYS = 128
PEER_TOPK = 16
PEER_SLOTS = PEER_HEADS * PEER_TOPK
ROUTE_TILE = 128
GATHER_TILE = 8


def _top_rows(work, order, n_top, payload=None):
    vals, args = [], []
    order_f = order.astype(F32)
    for _ in range(n_top):
        mx = jnp.max(work, axis=0, keepdims=True)
        first = jnp.min(jnp.where(work == mx, order_f, jnp.inf), axis=0, keepdims=True)
        pick = order_f == first
        vals.append(mx)
        if payload is None:
            args.append(first.astype(jnp.int32))
        else:
            args.append(jnp.sum(jnp.where(pick, payload, 0), axis=0, keepdims=True))
        work = jnp.where(pick, -jnp.inf, work)
    return jnp.concatenate(vals, axis=0), jnp.concatenate(args, axis=0)


def _pair_rows(s1, s2, combine):
    k = PEER_TOPK
    rows = [combine(s1[0:1], s2)]
    for a in range(1, k // 2):
        rows.append(combine(s1[a:a + 1], s2[0:k // 2]))
    rows.append(combine(s1[k // 2:], s2[0:1]))
    return jnp.concatenate(rows, axis=0)


def _peer_route_kernel(h_ref, g_ref, wq_ref, keys_ref, xn_ref, eidx_ref, gate_ref):
    x = h_ref[...]
    xn = x * lax.rsqrt(jnp.mean(x * x, axis=-1, keepdims=True) + NORM_EPS) * g_ref[...]
    xn_ref[...] = xn
    q = _dot(xn.astype(BF16), wq_ref[...]).astype(BF16)
    tq = x.shape[0]
    half = PEER_DK // 2
    n_iota = lax.broadcasted_iota(jnp.int32, (N_KEYS, tq), 0)
    k_iota = lax.broadcasted_iota(jnp.int32, (PEER_TOPK, tq), 0)
    order = _pair_rows(k_iota, k_iota, lambda a, b: a * PEER_TOPK + b)
    reachable = (order // PEER_TOPK + 1) * (order % PEER_TOPK + 1) <= PEER_TOPK
    for p in range(PEER_HEADS):
        tops = []
        for c in range(2):
            g = 2 * p + c
            sc = _dot_nt(keys_ref[g], q[:, g * half:(g + 1) * half])
            tops.append(_top_rows(sc, n_iota, PEER_TOPK))
        (s1, i1), (s2, i2) = tops
        cand = jnp.where(reachable, _pair_rows(s1, s2, lambda a, b: a + b), -jnp.inf)
        cidx = _pair_rows(i1, i2, lambda a, b: a * N_KEYS + b)
        top, eidx = _top_rows(cand, order, PEER_TOPK, payload=cidx)
        e = jnp.exp(top - top[0:1])
        eidx_ref[p * PEER_TOPK:(p + 1) * PEER_TOPK, :] = eidx
        gate_ref[p * PEER_TOPK:(p + 1) * PEER_TOPK, :] = e * (1.0 / jnp.sum(e, axis=0, keepdims=True))


def peer_route(h, g, w_q, keys):
    t, d = h.shape
    nq = w_q.shape[1]
    return pl.pallas_call(
        _peer_route_kernel,
        grid=(t // ROUTE_TILE,),
        in_specs=[pl.BlockSpec((ROUTE_TILE, d), lambda i: (i, 0)),
                  pl.BlockSpec((1, d), lambda i: (0, 0)),
                  pl.BlockSpec((d, nq), lambda i: (0, 0)),
                  pl.BlockSpec(keys.shape, lambda i: (0, 0, 0))],
        out_specs=[pl.BlockSpec((ROUTE_TILE, d), lambda i: (i, 0)),
                   pl.BlockSpec((PEER_SLOTS, ROUTE_TILE), lambda i: (0, i)),
                   pl.BlockSpec((PEER_SLOTS, ROUTE_TILE), lambda i: (0, i))],
        out_shape=[jax.ShapeDtypeStruct((t, d), F32),
                   jax.ShapeDtypeStruct((PEER_SLOTS, t), jnp.int32),
                   jax.ShapeDtypeStruct((PEER_SLOTS, t), F32)],
        compiler_params=pltpu.CompilerParams(dimension_semantics=("arbitrary",),
                                             vmem_limit_bytes=VMEM_LIMIT),
        name="peer_route",
    )(h, g.reshape(1, d), w_q, keys)


SUBLANES = 8
SLAB = 2 * SUBLANES
ISSUE_UNROLL = 8


def _gelu(x):
    return 0.5 * x * (1.0 + lax.erf(x * (2.0 ** -0.5)))


BITREV8 = (0, 4, 2, 6, 1, 5, 3, 7)


def _fold_pair(a, b, half, sub):
    low = (sub % (2 * half)) < half
    partner = jnp.where(low, pltpu.roll(a, SUBLANES - half, axis=0), pltpu.roll(b, half, axis=0))
    return jnp.where(low, a, b) + partner


GATHER_DEPTH = 3
DMA_QUEUES = 2


def _peer_gather_kernel(idx0_ref, idx1_ref, idx2_ref, xn_ref, h_ref, gate_ref, uv_hbm, o_ref,
                        buf_a, buf_b, buf_c, w_scr, sem):
    i = pl.program_id(0)
    n = pl.num_programs(0)
    tt = xn_ref.shape[0]
    n_slabs = tt * PEER_SLOTS
    bufs = (buf_a, buf_b, buf_c)

    def request(ref, k, r, queue):
        src = pl.multiple_of(ref[0, 0, r] * SLAB, SLAB)
        pltpu.make_async_copy(uv_hbm.at[pl.ds(src, SLAB)], bufs[k].at[pl.ds(r * SLAB, SLAB)],
                              sem.at[k]).start(priority=queue)

    def request_tile(ref, k):
        def body(c, carry):
            for j in range(ISSUE_UNROLL):
                request(ref, k, pl.multiple_of(c * ISSUE_UNROLL, ISSUE_UNROLL) + j, j % DMA_QUEUES)
            return carry
        lax.fori_loop(0, n_slabs // ISSUE_UNROLL, body, 0)

    def wait_tile(k):
        pltpu.make_async_copy(uv_hbm.at[pl.ds(0, n_slabs * SLAB)], bufs[k], sem.at[k]).wait()

    @pl.when(i == 0)
    def _():
        request_tile(idx0_ref, 0)
        request_tile(idx1_ref, 1)

    def step(cur, ahead):
        wait_tile(cur)
        gates = gate_ref[0]
        sub = lax.broadcasted_iota(jnp.int32, (SUBLANES, LANE), 0)
        for t in range(tt):
            for j in range(PEER_SLOTS):
                request(idx2_ref, ahead, t * PEER_SLOTS + j, j % DMA_QUEUES)
            x = xn_ref[t]

            def slab(pos, half):
                return bufs[cur][pl.ds((t * PEER_SLOTS + pos) * SLAB + half * SUBLANES, SUBLANES), :]

            groups = []
            for g in range(PEER_SLOTS // SUBLANES):
                p = [slab(g * SUBLANES + k, 0) * x for k in range(SUBLANES)]
                c = [_fold_pair(p[2 * k], p[2 * k + 1], 4, sub) for k in range(4)]
                d = [_fold_pair(c[0], c[1], 2, sub), _fold_pair(c[2], c[3], 2, sub)]
                groups.append(_fold_pair(d[0], d[1], 1, sub))
            act = jnp.sum(jnp.concatenate(groups, axis=0), axis=-1, keepdims=True)
            w_scr[...] = jnp.broadcast_to(gates[:, t:t + 1] * _gelu(act), (PEER_SLOTS, LANE))
            acc = h_ref[t]
            for g in range(PEER_SLOTS // SUBLANES):
                for k in range(SUBLANES):
                    w_row = w_scr[pl.ds(g * SUBLANES + BITREV8[k], 1), :]
                    acc = acc + slab(g * SUBLANES + k, 1) * w_row
            o_ref[t] = acc

    for k in range(GATHER_DEPTH):
        @pl.when(i % GATHER_DEPTH == k)
        def _(k=k):
            step(k, (k + GATHER_DEPTH - 1) % GATHER_DEPTH)

    @pl.when(i == n - 1)
    def _():
        for k in range(GATHER_DEPTH):
            @pl.when(i % GATHER_DEPTH == k)
            def _(k=k):
                wait_tile((k + 1) % GATHER_DEPTH)
                wait_tile((k + 2) % GATHER_DEPTH)


def peer_gather(idx, xn, h, gate_b, uv):
    t = xn.shape[0]
    nt = t // GATHER_TILE
    n_slabs = GATHER_TILE * PEER_SLOTS
    tok = pl.BlockSpec((GATHER_TILE, SUBLANES, LANE), lambda i: (i, 0, 0))
    idx_spec = lambda k: pl.BlockSpec((1, 1, n_slabs), lambda i: (jnp.minimum(i + k, nt - 1), 0, 0),
                                      memory_space=pltpu.SMEM)
    slab_buf = pltpu.VMEM((n_slabs * SLAB, LANE), F32)
    return pl.pallas_call(
        _peer_gather_kernel,
        grid=(nt,),
        in_specs=[idx_spec(0), idx_spec(1), idx_spec(2), tok, tok,
                  pl.BlockSpec((1, PEER_SLOTS, GATHER_TILE), lambda i: (i, 0, 0)),
                  pl.BlockSpec(memory_space=pl.ANY)],
        out_specs=tok,
        out_shape=jax.ShapeDtypeStruct((t, SUBLANES, LANE), F32),
        scratch_shapes=[slab_buf, slab_buf, slab_buf, pltpu.VMEM((PEER_SLOTS, LANE), F32),
                        pltpu.SemaphoreType.DMA((GATHER_DEPTH,))],
        compiler_params=pltpu.CompilerParams(dimension_semantics=("arbitrary",),
                                             vmem_limit_bytes=VMEM_LIMIT),
        name="peer_gather",
    )(idx, idx, idx, xn, h, gate_b, uv)


def peer_ffn_residual(h, g, w_q, sub_keys, u_tab, v_tab):
    t, d = h.shape
    e = u_tab.shape[0]
    keys = sub_keys.reshape(PEER_HEADS * 2, N_KEYS, PEER_DK // 2).astype(BF16)
    xn, eidx_t, gate_t = peer_route(h, g, w_q.astype(BF16), keys)
    nt = t // GATHER_TILE
    order = np.arange(PEER_SLOTS).reshape(-1, SUBLANES)[:, list(BITREV8)].reshape(-1)
    idx = eidx_t[order].T.reshape(nt, 1, GATHER_TILE * PEER_SLOTS)
    gate_b = gate_t.reshape(PEER_SLOTS, nt, GATHER_TILE).transpose(1, 0, 2)
    uv = jnp.concatenate([u_tab.reshape(e, SUBLANES, LANE), v_tab.reshape(e, SUBLANES, LANE)],
                         axis=1).reshape(e * SLAB, LANE)
    out = peer_gather(idx, xn.reshape(t, SUBLANES, LANE), h.reshape(t, SUBLANES, LANE), gate_b, uv)
    return out.reshape(t, d)


Q_LORA = 256
KV_LORA = 128
QK_NOPE = 64
QK_ROPE = 32
V_HEAD = 64
ROPE_THETA = 10000.0
MLA_TQ = 256
MLA_TK = 512


def _rms(x, g):
    return x * lax.rsqrt(jnp.mean(x * x, axis=-1, keepdims=True) + NORM_EPS) * g


def _mla_prep_kernel(pos_ref, freq_ref, cq_ref, ckv_ref, kr_ref, gq_ref, gkv_ref, wq_ref, wk_ref, wv_ref,
                     q_out, k_out, v_out):
    rows = cq_ref.shape[0]
    half = QK_ROPE // 2
    cq = _rms(cq_ref[...], gq_ref[...]).astype(BF16)
    ckv = _rms(ckv_ref[...], gkv_ref[...]).astype(BF16)
    qf = _dot(cq, wq_ref[...])
    kf = _dot(ckv, wk_ref[...])
    vf = _dot(ckv, wv_ref[...])
    lane = lax.broadcasted_iota(jnp.int32, (rows, LANE), 1)
    ang = pos_ref[...].astype(F32) * freq_ref[...]
    cos, sin = jnp.cos(ang), jnp.sin(ang)
    in_rope = (lane >= QK_NOPE) & (lane < QK_NOPE + QK_ROPE)
    first = lane < QK_NOPE + half
    c_mul = jnp.where(lane < QK_NOPE, 1.0, jnp.where(in_rope, cos, 0.0))
    s_mul = jnp.where(in_rope, jnp.where(first, -sin, sin), 0.0)

    def rope(x):
        partner = jnp.where(first, pltpu.roll(x, LANE - half, axis=1), pltpu.roll(x, half, axis=1))
        return x * c_mul + partner * s_mul

    scale = (QK_NOPE + QK_ROPE) ** -0.5
    kr = rope(jnp.concatenate([jnp.zeros((rows, QK_NOPE), F32), kr_ref[...],
                               jnp.zeros((rows, LANE - QK_NOPE - QK_ROPE), F32)], axis=1))
    ones_col = jnp.where(lane == V_HEAD, 1.0, 0.0)
    for h in range(GROUP_HEADS):
        cols = slice(h * LANE, (h + 1) * LANE)
        q_out[:, cols] = (rope(qf[:, cols]) * scale).astype(BF16)
        k_out[:, cols] = (kf[:, cols] + kr).astype(BF16)
        v_out[:, cols] = (vf[:, cols] + ones_col).astype(BF16)


def mla_prep(pos, c_q, c_kv, k_r, gq, gkv, w_uq, w_ukv):
    t = c_q.shape[0]
    H = GROUP_HEADS
    wq = jnp.pad(w_uq.reshape(Q_LORA, H, QK_NOPE + QK_ROPE), ((0, 0), (0, 0), (0, LANE - QK_NOPE - QK_ROPE)))
    wkv = w_ukv.reshape(KV_LORA, H, QK_NOPE + V_HEAD)
    wk = jnp.pad(wkv[:, :, :QK_NOPE], ((0, 0), (0, 0), (0, LANE - QK_NOPE)))
    wv = jnp.pad(wkv[:, :, QK_NOPE:], ((0, 0), (0, 0), (0, LANE - V_HEAD)))
    wq, wk, wv = [w.reshape(w.shape[0], H * LANE).astype(BF16) for w in (wq, wk, wv)]
    inv_freq = ROPE_THETA ** (-np.arange(0, QK_ROPE, 2, dtype=np.float32) / QK_ROPE)
    freq = np.zeros((1, LANE), np.float32)
    freq[0, QK_NOPE:QK_NOPE + QK_ROPE] = np.tile(inv_freq, 2)
    row = lambda n: pl.BlockSpec((ROW_TILE, n), lambda i: (i, 0))
    fixed = lambda a: pl.BlockSpec(a.shape, lambda i: (0,) * a.ndim)
    gq2, gkv2, freq = gq.reshape(1, -1), gkv.reshape(1, -1), jnp.asarray(freq)
    out = jax.ShapeDtypeStruct((t, H * LANE), BF16)
    return pl.pallas_call(
        _mla_prep_kernel,
        grid=(t // ROW_TILE,),
        in_specs=[row(1), fixed(freq), row(Q_LORA), row(KV_LORA), row(QK_ROPE), fixed(gq2), fixed(gkv2),
                  fixed(wq), fixed(wk), fixed(wv)],
        out_specs=[row(H * LANE)] * 3,
        out_shape=[out] * 3,
        compiler_params=pltpu.CompilerParams(dimension_semantics=("arbitrary",)),
        name="mla_prep",
    )(pos, freq, c_q, c_kv, k_r, gq2, gkv2, wq, wk, wv)


def _mla_attn_kernel(q_ref, k_ref, v_ref, o_ref, m_scr, acc_scr):
    i = pl.program_id(1)
    tq = q_ref.shape[1]
    row = i * tq + lax.broadcasted_iota(jnp.int32, (tq, MLA_TK), 0)
    col = lax.broadcasted_iota(jnp.int32, (tq, MLA_TK), 1)
    n_full = (i * tq) // MLA_TK
    heads = range(GROUP_HEADS)
    cols = [slice(h * LANE, (h + 1) * LANE) for h in heads]
    q = [q_ref[0, :, cols[h]] for h in heads]
    m_scr[...] = jnp.full(m_scr.shape, NEG, F32)
    acc_scr[...] = jnp.zeros(acc_scr.shape, F32)

    def step(j, masked):
        start = pl.multiple_of(j * MLA_TK, MLA_TK)

        def qk(h):
            s = _dot_nt(q[h], k_ref[0, pl.ds(start, MLA_TK), cols[h]])
            return jnp.where(start + col <= row, s, NEG) if masked else s

        def absorb(h, s):
            m_old = m_scr[h]
            m_new = jnp.maximum(m_old, jnp.max(s, axis=-1, keepdims=True))
            p = jnp.exp(s - m_new).astype(BF16)
            acc_scr[h] = jnp.exp(m_old - m_new) * acc_scr[h] + _dot(p, v_ref[0, pl.ds(start, MLA_TK), cols[h]])
            m_scr[h] = m_new

        s_prev = qk(0)
        for h in range(1, GROUP_HEADS):
            s_next = qk(h)
            absorb(h - 1, s_prev)
            s_prev = s_next
        absorb(GROUP_HEADS - 1, s_prev)

    def body(j, carry):
        step(j, False)
        return carry

    lax.fori_loop(0, n_full, body, 0)
    step(n_full, True)
    outs = [acc_scr[h][:, :V_HEAD] * (1.0 / acc_scr[h][:, V_HEAD:V_HEAD + 1]) for h in heads]
    o_ref[0] = jnp.concatenate(outs, axis=1)


def mla_attention(q, k, v):
    b, s, w = q.shape
    return pl.pallas_call(
        _mla_attn_kernel,
        grid=(b, s // MLA_TQ),
        in_specs=[pl.BlockSpec((1, MLA_TQ, w), lambda bi, i: (bi, i, 0)),
                  pl.BlockSpec((1, s, w), lambda bi, i: (bi, 0, 0)),
                  pl.BlockSpec((1, s, w), lambda bi, i: (bi, 0, 0))],
        out_specs=pl.BlockSpec((1, MLA_TQ, GROUP_HEADS * V_HEAD), lambda bi, i: (bi, i, 0)),
        out_shape=jax.ShapeDtypeStruct((b, s, GROUP_HEADS * V_HEAD), F32),
        scratch_shapes=[pltpu.VMEM((GROUP_HEADS, MLA_TQ, 1), F32), pltpu.VMEM((GROUP_HEADS, MLA_TQ, LANE), F32)],
        compiler_params=pltpu.CompilerParams(dimension_semantics=("arbitrary", "arbitrary"),
                                             vmem_limit_bytes=VMEM_LIMIT),
        name="mla_attention",
    )(q, k, v)


def mla_group(h, positions, q_norm_g, kv_norm_g, w_uq, w_ukv):
    b, s, _ = h.shape
    t = b * s
    h2 = h.reshape(t, -1)
    c_q, c_kv, k_r = h2[:, :Q_LORA], h2[:, Q_LORA:Q_LORA + KV_LORA], h2[:, Q_LORA + KV_LORA:]
    q, k, v = mla_prep(positions.reshape(t, 1), c_q, c_kv, k_r, q_norm_g, kv_norm_g, w_uq, w_ukv)
    w = GROUP_HEADS * LANE
    return mla_attention(q.reshape(b, s, w), k.reshape(b, s, w), v.reshape(b, s, w))


CHUNK = 64
CONV_W = 4


def _split3(a):
    hi = a.astype(BF16)
    r = a - hi.astype(F32)
    mid = r.astype(BF16)
    return hi, mid, (r - mid.astype(F32)).astype(BF16)


def _dot_exact_rhs(sel, a):
    return sum(_dot(sel, p) for p in _split3(a))


def _row_matrix(col, n_rows):
    l = col.shape[0]
    lane = lax.broadcasted_iota(jnp.int32, (l, SUBLANES), 1)
    padded = jnp.where(lane == 0, col, 0.0)
    ones = jnp.ones((n_rows, SUBLANES), BF16)
    return sum(_dot_nt(ones, p) for p in _split3(padded))


def _transpose_bf16(x_bf):
    d = x_bf.shape[1]
    eye = (lax.broadcasted_iota(jnp.int32, (d, d), 0) == lax.broadcasted_iota(jnp.int32, (d, d), 1)).astype(BF16)
    return _dot_nt(eye, x_bf).astype(BF16)


def _log_sigmoid(x):
    return jnp.minimum(x, 0.0) - jnp.log1p(jnp.exp(-jnp.abs(x)))


def _softplus(x):
    return jnp.maximum(x, 0.0) + jnp.log1p(jnp.exp(-jnp.abs(x)))


def _with_ones_col(v):
    l, d = v.shape
    lane = lax.broadcasted_iota(jnp.int32, (l, LANE - d), 1)
    return jnp.concatenate([v, jnp.where(lane == 0, 1.0, 0.0)], axis=1)


def _bf(x):
    return x.astype(BF16)


def _mlstm_kernel(q_ref, k_ref, v_ref, gate_ref, og_ref, bias_ref, ng_ref, o_ref, st_scr, m_scr):
    nb = q_ref.shape[0]
    H, D, L = GROUP_HEADS, HEAD_DIM, CHUNK
    chains = [(b, h) for b in range(nb) for h in range(H)]
    cols = lambda h: slice(h * D, (h + 1) * D)
    each = lambda fn: [fn(c, b, h) for c, (b, h) in enumerate(chains)]

    @pl.when(pl.program_id(0) == 0)
    def _():
        st_scr[...] = jnp.zeros(st_scr.shape, F32)
        m_scr[...] = jnp.zeros(m_scr.shape, F32)

    row = lax.broadcasted_iota(jnp.int32, (L, L), 0)
    col = lax.broadcasted_iota(jnp.int32, (L, L), 1)
    tri = row >= col
    tri_bf = tri.astype(BF16)
    gates = [gate_ref[b] + bias_ref[...] for b in range(nb)]
    bcum = [_dot_exact_rhs(tri_bf, _log_sigmoid(g[:, H:])) for g in gates]
    q = each(lambda c, b, h: _bf(q_ref[b, :, cols(h)]))
    kt = each(lambda c, b, h: _transpose_bf16(_bf(k_ref[b, :, cols(h)] * (D ** -0.5))))
    v_aug = each(lambda c, b, h: _with_ones_col(v_ref[b, :, cols(h)]))
    bc = each(lambda c, b, h: bcum[b][:, h:h + 1])
    ic = each(lambda c, b, h: gates[b][:, h:h + 1])
    m_st = each(lambda c, b, h: m_scr[c, 0:1, 0:1])
    st = each(lambda c, b, h: st_scr[c])
    log_d = each(lambda c, b, h: jnp.where(tri, bc[c] + _row_matrix(ic[c] - bc[c], L), -jnp.inf))
    m_t = each(lambda c, b, h: jnp.maximum(jnp.max(log_d[c], axis=-1, keepdims=True), bc[c] + m_st[c]))
    sqk = each(lambda c, b, h: _dot(q[c], kt[c]) * jnp.exp(log_d[c] - m_t[c]))
    inter = each(lambda c, b, h: jnp.exp(bc[c] + m_st[c] - m_t[c]))
    tot = each(lambda c, b, h: _dot(_bf(sqk[c]), _bf(v_aug[c])) + inter[c] * _dot(q[c], _bf(st[c])))
    h_t = each(lambda c, b, h: tot[c][:, :D] * (1.0 / jnp.maximum(jnp.abs(tot[c][:, D:D + 1]), jnp.exp(-m_t[c]))))
    b_end = each(lambda c, b, h: bc[c][L - 1:L])
    log_w = each(lambda c, b, h: b_end[c] - bc[c] + ic[c])
    m_new = each(lambda c, b, h: jnp.maximum(b_end[c] + m_st[c], jnp.max(log_w[c], axis=0, keepdims=True)))
    upd = each(lambda c, b, h: _dot(kt[c], _bf(jnp.exp(log_w[c] - m_new[c]) * v_aug[c])))
    for c in range(len(chains)):
        st_scr[c] = jnp.exp(b_end[c] + m_st[c] - m_new[c]) * st[c] + upd[c]
        m_scr[c] = jnp.broadcast_to(m_new[c], m_scr.shape[1:])
    for b in range(nb):
        outs = []
        for h in range(H):
            o = jax.nn.sigmoid(og_ref[b, :, cols(h)]) * h_t[b * H + h]
            outs.append(o * lax.rsqrt(jnp.mean(o * o, axis=-1, keepdims=True) + NORM_EPS) * ng_ref[:, cols(h)])
        o_ref[b] = jnp.concatenate(outs, axis=1)


def mlstm_group(h, i_bias, f_bias, norm_g):
    b, s, _ = h.shape
    W, H = GROUP_W, GROUP_HEADS
    q, k, v = h[..., :W], h[..., W:2 * W], h[..., 2 * W:3 * W]
    gates = h[..., 3 * W:3 * W + 2 * H]
    og = h[..., 3 * W + 2 * H:]
    bias = jnp.concatenate([i_bias, f_bias]).reshape(1, 2 * H)
    tok = lambda n: pl.BlockSpec((b, CHUNK, n), lambda i: (0, i, 0))
    fixed = lambda n: pl.BlockSpec((1, n), lambda i: (0, 0))
    return pl.pallas_call(
        _mlstm_kernel,
        grid=(s // CHUNK,),
        in_specs=[tok(W), tok(W), tok(W), tok(2 * H), tok(W), fixed(2 * H), fixed(W)],
        out_specs=tok(W),
        out_shape=jax.ShapeDtypeStruct((b, s, W), F32),
        scratch_shapes=[pltpu.VMEM((b * H, HEAD_DIM, LANE), F32), pltpu.VMEM((b * H, SUBLANES, LANE), F32)],
        compiler_params=pltpu.CompilerParams(dimension_semantics=("arbitrary",)),
        name="mlstm",
    )(q, k, v, gates, og, bias, norm_g.reshape(1, W))


def _gdn_kernel(x_ref, xprev_ref, cw_ref, gate_ref, z_ref, par_ref, ng_ref, o_ref, st_scr):
    nb = x_ref.shape[0]
    H, D, L = GROUP_HEADS, HEAD_DIM, CHUNK
    W = H * D
    chains = [(b, h) for b in range(nb) for h in range(H)]
    each = lambda fn: [fn(c, b, h) for c, (b, h) in enumerate(chains)]
    first = pl.program_id(0) == 0

    @pl.when(first)
    def _():
        st_scr[...] = jnp.zeros(st_scr.shape, F32)

    row = lax.broadcasted_iota(jnp.int32, (L, L), 0)
    col = lax.broadcasted_iota(jnp.int32, (L, L), 1)
    tri = row >= col
    tri_bf = tri.astype(BF16)
    eye = (row == col).astype(F32)
    par = par_ref[...]
    qkv, beta4, gc4 = [], [], []
    for b in range(nb):
        prev = jnp.where(first, 0.0, xprev_ref[b])
        full = jnp.concatenate([prev, x_ref[b]], axis=0)
        acc = full[SUBLANES:] * cw_ref[CONV_W - 1:CONV_W, :]
        for j in range(CONV_W - 1):
            shifted = pltpu.roll(full, CONV_W - 1 - j, axis=0)[SUBLANES:]
            acc = acc + shifted * cw_ref[j:j + 1, :]
        qkv.append(acc * jax.nn.sigmoid(acc))
        gates = gate_ref[b]
        beta4.append(jax.nn.sigmoid(gates[:, :H]))
        gc4.append(_dot_exact_rhs(tri_bf, -jnp.exp(par[:, :H]) * _softplus(gates[:, H:] + par[:, H:])))

    def l2n(t):
        return t * lax.rsqrt(jnp.sum(t * t, axis=-1, keepdims=True) + NORM_EPS)

    q = each(lambda c, b, h: l2n(qkv[b][:, h * D:(h + 1) * D]) * (D ** -0.5))
    k = each(lambda c, b, h: l2n(qkv[b][:, W + h * D:W + (h + 1) * D]))
    v = each(lambda c, b, h: qkv[b][:, 2 * W + h * D:2 * W + (h + 1) * D])
    beta = each(lambda c, b, h: beta4[b][:, h:h + 1])
    gc = each(lambda c, b, h: gc4[b][:, h:h + 1])
    gc_row = each(lambda c, b, h: _row_matrix(gc[c], L))
    decay = each(lambda c, b, h: jnp.where(tri, jnp.exp(gc[c] - gc_row[c]), 0.0))
    kt = each(lambda c, b, h: _transpose_bf16(_bf(k[c])))
    kb = each(lambda c, b, h: k[c] * beta[c])
    x = each(lambda c, b, h: -jnp.where(row > col, _dot(_bf(kb[c]), kt[c]) * decay[c], 0.0))
    t_inv = [eye + xc for xc in x]
    for _ in range(5):
        x = [_dot(_bf(xc), _bf(xc)) for xc in x]
        t_inv = [tc + _dot(_bf(tc), _bf(xc)) for tc, xc in zip(t_inv, x)]
    u = each(lambda c, b, h: _dot(_bf(t_inv[c]), _bf(v[c] * beta[c])))
    wm = each(lambda c, b, h: _dot(_bf(t_inv[c]), _bf(kb[c] * jnp.exp(gc[c]))))
    attn = each(lambda c, b, h: _dot(_bf(q[c]), kt[c]) * decay[c])
    st = each(lambda c, b, h: st_scr[c])
    v_new = each(lambda c, b, h: u[c] - _dot(_bf(wm[c]), _bf(st[c])))
    o = each(lambda c, b, h: _dot(_bf(q[c] * jnp.exp(gc[c])), _bf(st[c])) + _dot(_bf(attn[c]), _bf(v_new[c])))
    g_end = each(lambda c, b, h: gc[c][L - 1:L])
    upd = each(lambda c, b, h: _dot(_bf(kt[c].astype(F32) * jnp.exp(g_end[c] - gc_row[c])), _bf(v_new[c])))
    for c in range(len(chains)):
        st_scr[c] = st[c] * jnp.exp(g_end[c]) + upd[c]
    for b in range(nb):
        outs = []
        for h in range(H):
            oc = o[b * H + h]
            z = z_ref[b, :, h * D:(h + 1) * D]
            y = oc * lax.rsqrt(jnp.mean(oc * oc, axis=-1, keepdims=True) + NORM_EPS) * ng_ref[...]
            outs.append(y * (z * jax.nn.sigmoid(z)))
        o_ref[b] = jnp.concatenate(outs, axis=1)


def gdn_group(h, conv_w, a_log, dt_bias, norm_g):
    b, s, _ = h.shape
    W, H = GROUP_W, GROUP_HEADS
    x = h[..., :3 * W]
    gates = h[..., 3 * W:3 * W + 2 * H]
    z = h[..., 3 * W + 2 * H:]
    par = jnp.concatenate([a_log, dt_bias]).reshape(1, 2 * H)
    tok = lambda n: pl.BlockSpec((b, CHUNK, n), lambda i: (0, i, 0))
    per_chunk = CHUNK // SUBLANES
    prev = pl.BlockSpec((b, SUBLANES, 3 * W), lambda i: (0, jnp.maximum(i * per_chunk - 1, 0), 0))
    fixed = lambda r, n: pl.BlockSpec((r, n), lambda i: (0, 0))
    return pl.pallas_call(
        _gdn_kernel,
        grid=(s // CHUNK,),
        in_specs=[tok(3 * W), prev, fixed(CONV_W, 3 * W), tok(2 * H), tok(W), fixed(1, 2 * H), fixed(1, HEAD_DIM)],
        out_specs=tok(W),
        out_shape=jax.ShapeDtypeStruct((b, s, W), F32),
        scratch_shapes=[pltpu.VMEM((b * H, HEAD_DIM, HEAD_DIM), F32)],
        compiler_params=pltpu.CompilerParams(dimension_semantics=("arbitrary",)),
        name="gdn",
    )(x, x, conv_w, gates, z, par, norm_g.reshape(1, HEAD_DIM))


NSA_IN = GROUP_W + 6 * HEAD_DIM + 3 * GROUP_HEADS
MLSTM_IN = 4 * GROUP_W + 2 * GROUP_HEADS
GDN_IN = 4 * GROUP_W + 2 * GROUP_HEADS
MLA_IN = Q_LORA + KV_LORA + QK_ROPE


def kernel(x, positions, t5_table, norm1_g, norm2_g, final_norm_g, w_in, w_out, nsa_w_cmp_k, nsa_w_cmp_v, nsa_cmp_pos, mlstm_i_bias, mlstm_f_bias, mlstm_norm_g, gdn_conv_w, gdn_a_log, gdn_dt_bias, gdn_norm_g, mla_q_norm_g, mla_kv_norm_g, mla_w_uq, mla_w_ukv, peer_w_q, peer_sub_keys, peer_u, peer_v):
    b, s, d = x.shape
    t = b * s
    h = x.reshape(t, d)
    depth = w_in.shape[0]
    d_in = w_in.shape[2]
    d_in_pad = _round_up(d_in, LANE)
    bias_tiles = nsa_bias_tiles(t5_table)
    for l in range(depth):
        w_in_l = jnp.pad(w_in[l], ((0, 0), (0, d_in_pad - d_in))).astype(BF16)
        hp = norm_matmul(h, norm1_g[l], w_in_l)[:, :d_in].reshape(b, s, d_in)
        c0, c1, c2 = NSA_IN, NSA_IN + MLSTM_IN, NSA_IN + MLSTM_IN + GDN_IN
        h_nsa, h_ml, h_gdn, h_mla = hp[..., :c0], hp[..., c0:c1], hp[..., c1:c2], hp[..., c2:]
        o = jnp.concatenate([
            nsa_group(h_nsa, nsa_w_cmp_k[l], nsa_w_cmp_v[l], nsa_cmp_pos[l], t5_table, bias_tiles),
            mlstm_group(h_ml, mlstm_i_bias[l], mlstm_f_bias[l], mlstm_norm_g[l]),
            gdn_group(h_gdn, gdn_conv_w[l], gdn_a_log[l], gdn_dt_bias[l], gdn_norm_g[l]),
            mla_group(h_mla, positions, mla_q_norm_g[l], mla_kv_norm_g[l], mla_w_uq[l], mla_w_ukv[l]),
        ], axis=-1)
        h = matmul_residual(o.reshape(t, -1), w_out[l].astype(BF16), h)
        h = peer_ffn_residual(h, norm2_g[l], peer_w_q[l], peer_sub_keys[l], peer_u[l], peer_v[l])
    return rmsnorm(h, final_norm_g).reshape(b, s, d)
```

```python
import functools
import math

import jax
import jax.numpy as jnp
import numpy as np
from jax import lax
from jax.experimental import pallas as pl
from jax.experimental.pallas import tpu as pltpu

F32 = jnp.float32
BF16 = jnp.bfloat16
NORM_EPS = 1e-6
NEG = -1e30
LANE = 128
ROW_TILE = 512
VMEM_LIMIT = 56 * 1024 * 1024

HEAD_DIM = 64
GROUP_HEADS = 4
GROUP_W = GROUP_HEADS * HEAD_DIM
Q_BLOCK = 128
CMP_LEN = 32
CMP_STRIDE = 16
SEL_LEN = 64
SEL_TOPN = 16
WINDOW = 512
SEL_FORCE = 1e3
N_BUCKETS = 32
MAX_DISTANCE = 128


def _round_up(n, m):
    return (n + m - 1) // m * m


def _dot_nt(a, b):
    return lax.dot_general(a, b, (((1,), (1,)), ((), ())), preferred_element_type=F32)


def _dot(a, b):
    return jnp.dot(a, b, preferred_element_type=F32)


def _norm_matmul_split_kernel(*refs, ones_cols):
    n = len(ones_cols)
    x_ref, g_ref = refs[:2]
    w_refs, o_refs = refs[2:2 + n], refs[2 + n:]
    x = x_ref[...]
    y = (x * lax.rsqrt(jnp.mean(x * x, axis=-1, keepdims=True) + NORM_EPS) * g_ref[...]).astype(BF16)
    for w_ref, o_ref, col in zip(w_refs, o_refs, ones_cols):
        r = _dot(y, w_ref[...])
        if col is not None:
            r = r + jnp.where(lax.broadcasted_iota(jnp.int32, r.shape, 1) == col, 1.0, 0.0)
        o_ref[...] = r.astype(o_ref.dtype)


def norm_matmul_split(x, g, weights, dtypes, ones_cols):
    t, d = x.shape
    row = lambda n: pl.BlockSpec((ROW_TILE, n), lambda i: (i, 0))
    return pl.pallas_call(
        functools.partial(_norm_matmul_split_kernel, ones_cols=tuple(ones_cols)),
        grid=(t // ROW_TILE,),
        in_specs=[row(d), pl.BlockSpec((1, d), lambda i: (0, 0))]
                 + [pl.BlockSpec(w.shape, lambda i: (0, 0)) for w in weights],
        out_specs=[row(w.shape[1]) for w in weights],
        out_shape=[jax.ShapeDtypeStruct((t, w.shape[1]), dt) for w, dt in zip(weights, dtypes)],
        compiler_params=pltpu.CompilerParams(dimension_semantics=("arbitrary",),
                                             vmem_limit_bytes=VMEM_LIMIT),
        name="norm_matmul",
    )(x, g.reshape(1, d), *weights)


def _matmul_residual_kernel(*refs):
    n = (len(refs) - 2) // 2
    o_refs, w_refs, r_ref, out_ref = refs[:n], refs[n:2 * n], refs[2 * n], refs[2 * n + 1]
    acc = r_ref[...]
    for o_ref, w_ref in zip(o_refs, w_refs):
        acc = acc + _dot(o_ref[...].astype(BF16), w_ref[...])
    out_ref[...] = acc


def matmul_residual(parts, weights, res):
    t, n = res.shape
    row = lambda w: pl.BlockSpec((ROW_TILE, w), lambda i: (i, 0))
    return pl.pallas_call(
        _matmul_residual_kernel,
        grid=(t // ROW_TILE,),
        in_specs=[row(o.shape[1]) for o in parts]
                 + [pl.BlockSpec(w.shape, lambda i: (0, 0)) for w in weights] + [row(n)],
        out_specs=row(n),
        out_shape=jax.ShapeDtypeStruct((t, n), F32),
        compiler_params=pltpu.CompilerParams(dimension_semantics=("arbitrary",),
                                             vmem_limit_bytes=VMEM_LIMIT),
        name="matmul_residual",
    )(*parts, *weights, res)


def _rmsnorm_kernel(x_ref, g_ref, o_ref):
    x = x_ref[...]
    o_ref[...] = x * lax.rsqrt(jnp.mean(x * x, axis=-1, keepdims=True) + NORM_EPS) * g_ref[...]


def rmsnorm(x, g):
    t, d = x.shape
    return pl.pallas_call(
        _rmsnorm_kernel,
        grid=(t // ROW_TILE,),
        in_specs=[pl.BlockSpec((ROW_TILE, d), lambda i: (i, 0)),
                  pl.BlockSpec((1, d), lambda i: (0, 0))],
        out_specs=pl.BlockSpec((ROW_TILE, d), lambda i: (i, 0)),
        out_shape=jax.ShapeDtypeStruct((t, d), F32),
        compiler_params=pltpu.CompilerParams(dimension_semantics=("arbitrary",)),
        name="rmsnorm",
    )(x, g.reshape(1, d))


def _t5_bucket_starts():
    d = np.arange(0, 2 * MAX_DISTANCE)
    max_exact = N_BUCKETS // 2
    val = (np.log(np.maximum(d, 1).astype(np.float32) / np.float32(max_exact))
           / np.float32(math.log(MAX_DISTANCE / max_exact)) * np.float32(N_BUCKETS - max_exact))
    large = np.minimum(max_exact + val.astype(np.int32), N_BUCKETS - 1)
    bucket = np.where(d < max_exact, d, large)
    assert np.all(np.diff(bucket) >= 0)
    return [(int(k), int(d[bucket == k].min())) for k in range(N_BUCKETS) if np.any(bucket == k)]


T5_STARTS = _t5_bucket_starts()


def _t5_bias_heads(dist, tab_ref):
    accs = [jnp.full(dist.shape, tab_ref[0, h], F32) for h in range(GROUP_HEADS)]
    for bucket, start in T5_STARTS[1:]:
        m = dist >= start
        accs = [jnp.where(m, tab_ref[bucket, h], accs[h]) for h in range(GROUP_HEADS)]
    return jnp.stack(accs)


def _nsa_compress_kernel(x_ref, pos_ref, w_ref, o_ref):
    x = x_ref[0, 0]
    n = x.shape[0]
    lo = _dot((x + pos_ref[0, 0:1, :]).astype(BF16), w_ref[0, 0])
    hi = _dot((x + pos_ref[0, 1:2, :]).astype(BF16), w_ref[0, 1])
    o_ref[0, 0] = (lo + pltpu.roll(hi, n - 1, axis=0)).astype(o_ref.dtype)


def nsa_compress(x, pos, w):
    two, b, n, c = x.shape
    return pl.pallas_call(
        _nsa_compress_kernel,
        grid=(two, b),
        in_specs=[pl.BlockSpec((1, 1, n, c), lambda k, i: (k, i, 0, 0)),
                  pl.BlockSpec((1, 2, c), lambda k, i: (0, 0, 0)),
                  pl.BlockSpec((1, 2, c, HEAD_DIM), lambda k, i: (k, 0, 0, 0))],
        out_specs=pl.BlockSpec((1, 1, n, HEAD_DIM), lambda k, i: (k, i, 0, 0)),
        out_shape=jax.ShapeDtypeStruct((two, b, n, HEAD_DIM), BF16),
        compiler_params=pltpu.CompilerParams(dimension_semantics=("arbitrary", "arbitrary")),
        name="nsa_compress",
    )(x, pos, w)


FAR_TILE = 512
NEAR_KEYS = FAR_TILE + Q_BLOCK
WIN_KEYS = WINDOW + Q_BLOCK
NEAR_PAD = NEAR_KEYS - Q_BLOCK
CMP_TILE = LANE
CMP_TILES_BACK = CMP_TILE * CMP_STRIDE // Q_BLOCK
N_CMP_BIAS = CMP_TILES_BACK + 3


def _masked_bias(dist, valid, tab_ref):
    return jnp.where(valid, _t5_bias_heads(dist, tab_ref), NEG)


def _nsa_bias_kernel(tab_ref, near_ref, win_ref):
    q = lax.broadcasted_iota(jnp.int32, (Q_BLOCK, NEAR_KEYS), 0)
    k = lax.broadcasted_iota(jnp.int32, (Q_BLOCK, NEAR_KEYS), 1)
    dist = NEAR_PAD + q - k
    near_ref[...] = _masked_bias(dist, dist >= 0, tab_ref)
    q = lax.broadcasted_iota(jnp.int32, (Q_BLOCK, WIN_KEYS), 0)
    k = lax.broadcasted_iota(jnp.int32, (Q_BLOCK, WIN_KEYS), 1)
    dist = WINDOW + q - k
    win_ref[...] = _masked_bias(dist, (dist >= 0) & (dist < WINDOW), tab_ref)


def _nsa_cmp_bias_kernel(tab_ref, o_ref):
    e = pl.program_id(0)
    q = lax.broadcasted_iota(jnp.int32, (Q_BLOCK, CMP_TILE), 0)
    n = lax.broadcasted_iota(jnp.int32, (Q_BLOCK, CMP_TILE), 1)
    dist = (e - 1) * Q_BLOCK + q - (n * CMP_STRIDE + CMP_LEN - 1)
    o_ref[0] = _masked_bias(dist, dist >= 0, tab_ref)


def nsa_bias_tiles(t5_table):
    H, Q = GROUP_HEADS, Q_BLOCK
    smem = pl.BlockSpec(memory_space=pltpu.SMEM)
    near, win = pl.pallas_call(
        _nsa_bias_kernel, grid=(1,), in_specs=[smem],
        out_specs=[pl.BlockSpec((H, Q, NEAR_KEYS), lambda i: (0, 0, 0)),
                   pl.BlockSpec((H, Q, WIN_KEYS), lambda i: (0, 0, 0))],
        out_shape=[jax.ShapeDtypeStruct((H, Q, NEAR_KEYS), F32), jax.ShapeDtypeStruct((H, Q, WIN_KEYS), F32)],
        name="nsa_bias",
    )(t5_table)
    cmp = pl.pallas_call(
        _nsa_cmp_bias_kernel, grid=(N_CMP_BIAS,), in_specs=[smem],
        out_specs=pl.BlockSpec((1, H, Q, CMP_TILE), lambda e: (e, 0, 0, 0)),
        out_shape=jax.ShapeDtypeStruct((N_CMP_BIAS, H, Q, CMP_TILE), F32),
        name="nsa_cmp_bias",
    )(t5_table)
    return near, win, cmp


def _nsa_attn_kernel(tab_ref, q_ref, g_ref, kc_ref, vc_ref, ks_ref, vs_ref, kw_ref, vw_ref,
                     bnear_ref, bwin_ref, bcmp_ref, o_ref, m_scr, acc_scr):
    H, Q, D = GROUP_HEADS, Q_BLOCK, HEAD_DIM
    n_cmp = kc_ref.shape[1]
    n_sel = (ks_ref.shape[1] - NEAR_PAD) // SEL_LEN
    i = pl.program_id(1)
    t0 = i * Q

    q = q_ref[0] * (D ** -0.5)
    q4 = jnp.concatenate([q[:, h * D:(h + 1) * D] for h in range(H)], axis=0).astype(BF16)

    s = _dot_nt(q4, kc_ref[0]).reshape(H, Q, n_cmp)
    tiles = []
    for g in range(n_cmp // CMP_TILE):
        e = jnp.clip(i - g * CMP_TILES_BACK, -1, CMP_TILES_BACK + 1) + 1
        tiles.append(s[:, :, g * CMP_TILE:(g + 1) * CMP_TILE] + bcmp_ref[e])
    s = jnp.concatenate(tiles, axis=-1)
    t_c = t0 + lax.broadcasted_iota(jnp.int32, (Q, n_cmp), 0)
    valid_c = t_c >= lax.broadcasted_iota(jnp.int32, (Q, n_cmp), 1) * CMP_STRIDE + CMP_LEN - 1
    p = jnp.where(valid_c, jnp.exp(s - jnp.max(s, axis=-1, keepdims=True)), 0.0)
    l = jnp.sum(p, axis=-1, keepdims=True)
    p = p * (1.0 / jnp.where(l > 0.0, l, 1.0))
    o_c = _dot(p.reshape(H * Q, n_cmp).astype(BF16), vc_ref[0])

    psum = p[0] + p[1] + p[2] + p[3]
    c_idx = lax.broadcasted_iota(jnp.int32, (n_cmp, n_sel), 0)
    j_idx = lax.broadcasted_iota(jnp.int32, (n_cmp, n_sel), 1)
    ratio = SEL_LEN // CMP_STRIDE
    n_in_sel = (SEL_LEN - CMP_LEN) // CMP_STRIDE + 1
    pool = ((c_idx // ratio == j_idx) & (c_idx % ratio < n_in_sel)).astype(BF16)
    p_hi = psum.astype(BF16)
    p_mid = (psum - p_hi.astype(F32)).astype(BF16)
    p_lo = (psum - p_hi.astype(F32) - p_mid.astype(F32)).astype(BF16)
    imp = _dot(p_hi, pool) + _dot(p_mid, pool) + _dot(p_lo, pool)

    row0 = pl.multiple_of(t0, Q)
    kw = kw_ref[0, pl.ds(row0, WIN_KEYS), :]
    vw = vw_ref[0, pl.ds(row0, WIN_KEYS), :]
    in_seq = lax.broadcasted_iota(jnp.int32, (1, WIN_KEYS), 1) >= WINDOW - t0
    s = _dot_nt(q4, kw).reshape(H, Q, WIN_KEYS) + bwin_ref[...] + jnp.where(in_seq, 0.0, NEG)
    p_w = jnp.exp(s - jnp.max(s, axis=-1, keepdims=True))
    acc_w = _dot(p_w.reshape(H * Q, WIN_KEYS).astype(BF16), vw)
    o_w = acc_w[:, :D] * (1.0 / acc_w[:, D:D + 1])

    jb = lax.broadcasted_iota(jnp.int32, (Q, n_sel), 1)
    cur = (t0 + lax.broadcasted_iota(jnp.int32, (Q, n_sel), 0)) // SEL_LEN
    forced = (jb == 0) | (jb == cur) | (jb == cur - 1)
    work = jnp.where(jb <= cur, imp + jnp.where(forced, SEL_FORCE, 0.0), NEG)
    sel = jnp.zeros((Q, n_sel), F32)
    jb_f = jb.astype(F32)
    for _ in range(min(SEL_TOPN, n_sel)):
        mx = jnp.max(work, axis=-1, keepdims=True)
        first = jnp.min(jnp.where(work == mx, jb_f, jnp.inf), axis=-1, keepdims=True)
        pick = jb_f == first
        sel = jnp.where(pick & (mx > 0.5 * NEG), 1.0, sel)
        work = jnp.where(pick, -jnp.inf, work)
    sel_bf = sel.astype(BF16)

    far_bias = jnp.stack([jnp.full((Q, 1), tab_ref[N_BUCKETS - 1, h], F32) for h in range(H)])
    m_scr[...] = jnp.full(m_scr.shape, NEG, F32)
    acc_scr[...] = jnp.zeros(acc_scr.shape, F32)

    def sel_scores(row0, n_keys, bias, first_key):
        s = _dot_nt(q4, ks_ref[0, pl.ds(row0, n_keys), :]).reshape(H, Q, n_keys) + bias
        pos = row0 - NEAR_PAD + lax.broadcasted_iota(jnp.int32, (n_sel, n_keys), 1)
        blk = jnp.where(pos >= first_key, pos // SEL_LEN, -1)
        expand = (lax.broadcasted_iota(jnp.int32, (n_sel, n_keys), 0) == blk).astype(BF16)
        mask = _dot(sel_bf, expand) > 0.5
        return jnp.where(mask, s, NEG), mask

    def sel_absorb(row0, n_keys, s, mask):
        m_old = m_scr[...].reshape(H, Q, 1)
        m_new = jnp.maximum(m_old, jnp.max(s, axis=-1, keepdims=True))
        p = jnp.where(mask, jnp.exp(s - m_new), 0.0).reshape(H * Q, n_keys)
        alpha = jnp.exp(m_old - m_new).reshape(H * Q, 1)
        acc_scr[...] = alpha * acc_scr[...] + _dot(p.astype(BF16), vs_ref[0, pl.ds(row0, n_keys), :])
        m_scr[...] = m_new.reshape(H * Q, 1)

    def sel_step(row0, n_keys, bias, first_key):
        sel_absorb(row0, n_keys, *sel_scores(row0, n_keys, bias, first_key))

    n_far = jnp.maximum(i - 1, 0) // (FAR_TILE // Q)

    def far_body(kp, carry):
        row_a = pl.multiple_of(NEAR_PAD + kp * (2 * FAR_TILE), Q)
        row_b = pl.multiple_of(row_a + FAR_TILE, Q)
        sa = sel_scores(row_a, FAR_TILE, far_bias, 0)
        sb = sel_scores(row_b, FAR_TILE, far_bias, 0)
        sel_absorb(row_a, FAR_TILE, *sa)
        sel_absorb(row_b, FAR_TILE, *sb)
        return carry

    lax.fori_loop(0, n_far // 2, far_body, 0)

    @pl.when(n_far % 2 == 1)
    def _():
        sel_step(pl.multiple_of(NEAR_PAD + (n_far - 1) * FAR_TILE, Q), FAR_TILE, far_bias, 0)
    sel_step(row0, NEAR_KEYS, bnear_ref[...], n_far * FAR_TILE)
    acc = acc_scr[...]
    o_s = acc[:, :D] * (1.0 / acc[:, D:D + 1])

    g = jax.nn.sigmoid(g_ref[0])
    outs = []
    for h in range(H):
        rows = slice(h * Q, (h + 1) * Q)
        outs.append(g[:, h:h + 1] * o_c[rows] + g[:, H + h:H + h + 1] * o_s[rows]
                    + g[:, 2 * H + h:2 * H + h + 1] * o_w[rows])
    o_ref[0] = jnp.concatenate(outs, axis=1)


def nsa_attention(t5_table, bias_tiles, q, gates, k_cmp, v_cmp, ks, vs_aug, kw, vw_aug):
    b, s, _ = q.shape
    bnear, bwin, bcmp = bias_tiles
    full = lambda a: pl.BlockSpec((1,) + a.shape[1:], lambda bi, i: (bi,) + (0,) * (a.ndim - 1))
    fixed = lambda a: pl.BlockSpec(a.shape, lambda bi, i: (0,) * a.ndim)
    return pl.pallas_call(
        _nsa_attn_kernel,
        grid=(b, s // Q_BLOCK),
        in_specs=[pl.BlockSpec(memory_space=pltpu.SMEM),
                  pl.BlockSpec((1, Q_BLOCK, GROUP_W), lambda bi, i: (bi, i, 0)),
                  pl.BlockSpec((1, Q_BLOCK, 3 * GROUP_HEADS), lambda bi, i: (bi, i, 0)),
                  full(k_cmp), full(v_cmp), full(ks), full(vs_aug), full(kw), full(vw_aug),
                  fixed(bnear), fixed(bwin), fixed(bcmp)],
        out_specs=pl.BlockSpec((1, Q_BLOCK, GROUP_W), lambda bi, i: (bi, i, 0)),
        out_shape=jax.ShapeDtypeStruct((b, s, GROUP_W), F32),
        scratch_shapes=[pltpu.VMEM((GROUP_HEADS * Q_BLOCK, 1), F32),
                        pltpu.VMEM((GROUP_HEADS * Q_BLOCK, 2 * HEAD_DIM), F32)],
        compiler_params=pltpu.CompilerParams(dimension_semantics=("arbitrary", "arbitrary"),
                                             vmem_limit_bytes=VMEM_LIMIT),
        name="nsa_attention",
    )(t5_table, q, gates, k_cmp, v_cmp, ks, vs_aug, kw, vw_aug, bnear, bwin, bcmp)


def _front_pad(x, rows):
    return jnp.pad(x, ((0, 0), (rows, 0), (0, 0)))


def nsa_group(q, kc, vc, ks, vs_aug, kw, vw_aug, gates, w_cmp_k, w_cmp_v, cmp_pos, t5_table, bias_tiles):
    b, s, _ = q.shape
    d = HEAD_DIM
    n = s // CMP_STRIDE
    x = jnp.stack([kc, vc]).reshape(2, b, n, CMP_STRIDE * d)
    pos = cmp_pos.reshape(1, 2, CMP_STRIDE * d)
    w = jnp.stack([w_cmp_k, w_cmp_v]).reshape(2, 2, CMP_STRIDE * d, d).astype(BF16)
    cmp = nsa_compress(x, pos, w)
    return nsa_attention(t5_table, bias_tiles, q, gates, cmp[0], cmp[1],
                         _front_pad(ks, NEAR_PAD), _front_pad(vs_aug, NEAR_PAD),
                         _front_pad(kw, WINDOW), _front_pad(vw_aug, WINDOW))


PEER_HEADS = 8
PEER_DK = 256
N_KEYS = 128
PEER_TOPK = 16
PEER_SLOTS = PEER_HEADS * PEER_TOPK
ROUTE_TILE = 128
GATHER_TILE = 8


def _top_rows(work, order, n_top, payload=None):
    vals, args = [], []
    order_f = order.astype(F32)
    for _ in range(n_top):
        mx = jnp.max(work, axis=0, keepdims=True)
        first = jnp.min(jnp.where(work == mx, order_f, jnp.inf), axis=0, keepdims=True)
        pick = order_f == first
        vals.append(mx)
        if payload is None:
            args.append(first.astype(jnp.int32))
        else:
            args.append(jnp.sum(jnp.where(pick, payload, 0), axis=0, keepdims=True))
        work = jnp.where(pick, -jnp.inf, work)
    return jnp.concatenate(vals, axis=0), jnp.concatenate(args, axis=0)


def _pair_rows(s1, s2, combine):
    k = PEER_TOPK
    rows = [combine(s1[0:1], s2)]
    for a in range(1, k // 2):
        rows.append(combine(s1[a:a + 1], s2[0:k // 2]))
    rows.append(combine(s1[k // 2:], s2[0:1]))
    return jnp.concatenate(rows, axis=0)


def _peer_route_kernel(h_ref, g_ref, wq_ref, keys_ref, xn_ref, eidx_ref, gate_ref):
    x = h_ref[...]
    xn = x * lax.rsqrt(jnp.mean(x * x, axis=-1, keepdims=True) + NORM_EPS) * g_ref[...]
    xn_ref[...] = xn
    q = _dot(xn.astype(BF16), wq_ref[...]).astype(BF16)
    tq = x.shape[0]
    half = PEER_DK // 2
    n_iota = lax.broadcasted_iota(jnp.int32, (N_KEYS, tq), 0)
    k_iota = lax.broadcasted_iota(jnp.int32, (PEER_TOPK, tq), 0)
    order = _pair_rows(k_iota, k_iota, lambda a, b: a * PEER_TOPK + b)
    reachable = (order // PEER_TOPK + 1) * (order % PEER_TOPK + 1) <= PEER_TOPK
    for p in range(PEER_HEADS):
        tops = []
        for c in range(2):
            g = 2 * p + c
            sc = _dot_nt(keys_ref[g], q[:, g * half:(g + 1) * half])
            tops.append(_top_rows(sc, n_iota, PEER_TOPK))
        (s1, i1), (s2, i2) = tops
        cand = jnp.where(reachable, _pair_rows(s1, s2, lambda a, b: a + b), -jnp.inf)
        cidx = _pair_rows(i1, i2, lambda a, b: a * N_KEYS + b)
        top, eidx = _top_rows(cand, order, PEER_TOPK, payload=cidx)
        e = jnp.exp(top - top[0:1])
        eidx_ref[p * PEER_TOPK:(p + 1) * PEER_TOPK, :] = eidx
        gate_ref[p * PEER_TOPK:(p + 1) * PEER_TOPK, :] = e * (1.0 / jnp.sum(e, axis=0, keepdims=True))


def peer_route(h, g, w_q, keys):
    t, d = h.shape
    nq = w_q.shape[1]
    return pl.pallas_call(
        _peer_route_kernel,
        grid=(t // ROUTE_TILE,),
        in_specs=[pl.BlockSpec((ROUTE_TILE, d), lambda i: (i, 0)),
                  pl.BlockSpec((1, d), lambda i: (0, 0)),
                  pl.BlockSpec((d, nq), lambda i: (0, 0)),
                  pl.BlockSpec(keys.shape, lambda i: (0, 0, 0))],
        out_specs=[pl.BlockSpec((ROUTE_TILE, d), lambda i: (i, 0)),
                   pl.BlockSpec((PEER_SLOTS, ROUTE_TILE), lambda i: (0, i)),
                   pl.BlockSpec((PEER_SLOTS, ROUTE_TILE), lambda i: (0, i))],
        out_shape=[jax.ShapeDtypeStruct((t, d), F32),
                   jax.ShapeDtypeStruct((PEER_SLOTS, t), jnp.int32),
                   jax.ShapeDtypeStruct((PEER_SLOTS, t), F32)],
        compiler_params=pltpu.CompilerParams(dimension_semantics=("arbitrary",),
                                             vmem_limit_bytes=VMEM_LIMIT),
        name="peer_route",
    )(h, g.reshape(1, d), w_q, keys)


SUBLANES = 8
SLAB = 2 * SUBLANES
ISSUE_UNROLL = 8


def _gelu(x):
    return 0.5 * x * (1.0 + lax.erf(x * (2.0 ** -0.5)))


BITREV8 = (0, 4, 2, 6, 1, 5, 3, 7)


def _fold_pair(a, b, half, sub):
    low = (sub % (2 * half)) < half
    partner = jnp.where(low, pltpu.roll(a, SUBLANES - half, axis=0), pltpu.roll(b, half, axis=0))
    return jnp.where(low, a, b) + partner


GATHER_DEPTH = 3
DMA_QUEUES = 2


def _peer_gather_kernel(idx0_ref, idx1_ref, idx2_ref, xn_ref, h_ref, gate_ref, uv_hbm, o_ref,
                        buf_a, buf_b, buf_c, w_scr, sem):
    i = pl.program_id(0)
    n = pl.num_programs(0)
    tt = xn_ref.shape[0]
    n_slabs = tt * PEER_SLOTS
    bufs = (buf_a, buf_b, buf_c)

    def request(ref, k, r, queue):
        src = pl.multiple_of(ref[0, 0, r] * SLAB, SLAB)
        pltpu.make_async_copy(uv_hbm.at[pl.ds(src, SLAB)], bufs[k].at[pl.ds(r * SLAB, SLAB)],
                              sem.at[k]).start(priority=queue)

    def request_tile(ref, k):
        def body(c, carry):
            for j in range(ISSUE_UNROLL):
                request(ref, k, pl.multiple_of(c * ISSUE_UNROLL, ISSUE_UNROLL) + j, j % DMA_QUEUES)
            return carry
        lax.fori_loop(0, n_slabs // ISSUE_UNROLL, body, 0)

    def wait_tile(k):
        pltpu.make_async_copy(uv_hbm.at[pl.ds(0, n_slabs * SLAB)], bufs[k], sem.at[k]).wait()

    @pl.when(i == 0)
    def _():
        request_tile(idx0_ref, 0)
        request_tile(idx1_ref, 1)

    def step(cur, ahead):
        wait_tile(cur)
        gates = gate_ref[0]
        sub = lax.broadcasted_iota(jnp.int32, (SUBLANES, LANE), 0)
        for t in range(tt):
            for j in range(PEER_SLOTS):
                request(idx2_ref, ahead, t * PEER_SLOTS + j, j % DMA_QUEUES)
            x = xn_ref[t]

            def slab(pos, half):
                return bufs[cur][pl.ds((t * PEER_SLOTS + pos) * SLAB + half * SUBLANES, SUBLANES), :]

            groups = []
            for g in range(PEER_SLOTS // SUBLANES):
                p = [slab(g * SUBLANES + k, 0) * x for k in range(SUBLANES)]
                c = [_fold_pair(p[2 * k], p[2 * k + 1], 4, sub) for k in range(4)]
                d = [_fold_pair(c[0], c[1], 2, sub), _fold_pair(c[2], c[3], 2, sub)]
                groups.append(_fold_pair(d[0], d[1], 1, sub))
            act = jnp.sum(jnp.concatenate(groups, axis=0), axis=-1, keepdims=True)
            w_scr[...] = jnp.broadcast_to(gates[:, t:t + 1] * _gelu(act), (PEER_SLOTS, LANE))
            acc = h_ref[t]
            for g in range(PEER_SLOTS // SUBLANES):
                for k in range(SUBLANES):
                    w_row = w_scr[pl.ds(g * SUBLANES + BITREV8[k], 1), :]
                    acc = acc + slab(g * SUBLANES + k, 1) * w_row
            o_ref[t] = acc

    for k in range(GATHER_DEPTH):
        @pl.when(i % GATHER_DEPTH == k)
        def _(k=k):
            step(k, (k + GATHER_DEPTH - 1) % GATHER_DEPTH)

    @pl.when(i == n - 1)
    def _():
        for k in range(GATHER_DEPTH):
            @pl.when(i % GATHER_DEPTH == k)
            def _(k=k):
                wait_tile((k + 1) % GATHER_DEPTH)
                wait_tile((k + 2) % GATHER_DEPTH)


def peer_gather(idx, xn, h, gate_b, uv):
    t = xn.shape[0]
    nt = t // GATHER_TILE
    n_slabs = GATHER_TILE * PEER_SLOTS
    tok = pl.BlockSpec((GATHER_TILE, SUBLANES, LANE), lambda i: (i, 0, 0))
    idx_spec = lambda k: pl.BlockSpec((1, 1, n_slabs), lambda i: (jnp.minimum(i + k, nt - 1), 0, 0),
                                      memory_space=pltpu.SMEM)
    slab_buf = pltpu.VMEM((n_slabs * SLAB, LANE), F32)
    return pl.pallas_call(
        _peer_gather_kernel,
        grid=(nt,),
        in_specs=[idx_spec(0), idx_spec(1), idx_spec(2), tok, tok,
                  pl.BlockSpec((1, PEER_SLOTS, GATHER_TILE), lambda i: (i, 0, 0)),
                  pl.BlockSpec(memory_space=pl.ANY)],
        out_specs=tok,
        out_shape=jax.ShapeDtypeStruct((t, SUBLANES, LANE), F32),
        scratch_shapes=[slab_buf, slab_buf, slab_buf, pltpu.VMEM((PEER_SLOTS, LANE), F32),
                        pltpu.SemaphoreType.DMA((GATHER_DEPTH,))],
        compiler_params=pltpu.CompilerParams(dimension_semantics=("arbitrary",),
                                             vmem_limit_bytes=VMEM_LIMIT),
        name="peer_gather",
    )(idx, idx, idx, xn, h, gate_b, uv)


def peer_ffn_residual(h, g, w_q, sub_keys, u_tab, v_tab):
    t, d = h.shape
    e = u_tab.shape[0]
    keys = sub_keys.reshape(PEER_HEADS * 2, N_KEYS, PEER_DK // 2).astype(BF16)
    xn, eidx_t, gate_t = peer_route(h, g, w_q.astype(BF16), keys)
    nt = t // GATHER_TILE
    order = np.arange(PEER_SLOTS).reshape(-1, SUBLANES)[:, list(BITREV8)].reshape(-1)
    idx = eidx_t[order].T.reshape(nt, 1, GATHER_TILE * PEER_SLOTS)
    gate_b = gate_t.reshape(PEER_SLOTS, nt, GATHER_TILE).transpose(1, 0, 2)
    uv = jnp.concatenate([u_tab.reshape(e, SUBLANES, LANE), v_tab.reshape(e, SUBLANES, LANE)],
                         axis=1).reshape(e * SLAB, LANE)
    out = peer_gather(idx, xn.reshape(t, SUBLANES, LANE), h.reshape(t, SUBLANES, LANE), gate_b, uv)
    return out.reshape(t, d)


Q_LORA = 256
KV_LORA = 128
QK_NOPE = 64
QK_ROPE = 32
V_HEAD = 64
ROPE_THETA = 10000.0
MLA_TQ = 256
MLA_TK = 512


def _rms(x, g):
    return x * lax.rsqrt(jnp.mean(x * x, axis=-1, keepdims=True) + NORM_EPS) * g


def _mla_prep_kernel(pos_ref, freq_ref, cq_ref, ckv_ref, kr_ref, gq_ref, gkv_ref, wq_ref, wk_ref, wv_ref,
                     q_out, k_out, v_out):
    rows = cq_ref.shape[0]
    half = QK_ROPE // 2
    cq = _rms(cq_ref[...], gq_ref[...]).astype(BF16)
    ckv = _rms(ckv_ref[...], gkv_ref[...]).astype(BF16)
    qf = _dot(cq, wq_ref[...])
    kf = _dot(ckv, wk_ref[...])
    vf = _dot(ckv, wv_ref[...])
    lane = lax.broadcasted_iota(jnp.int32, (rows, LANE), 1)
    ang = pos_ref[...].astype(F32) * freq_ref[...]
    cos, sin = jnp.cos(ang), jnp.sin(ang)
    in_rope = (lane >= QK_NOPE) & (lane < QK_NOPE + QK_ROPE)
    first = lane < QK_NOPE + half
    c_mul = jnp.where(lane < QK_NOPE, 1.0, jnp.where(in_rope, cos, 0.0))
    s_mul = jnp.where(in_rope, jnp.where(first, -sin, sin), 0.0)

    def rope(x):
        partner = jnp.where(first, pltpu.roll(x, LANE - half, axis=1), pltpu.roll(x, half, axis=1))
        return x * c_mul + partner * s_mul

    scale = (QK_NOPE + QK_ROPE) ** -0.5
    kr = rope(jnp.concatenate([jnp.zeros((rows, QK_NOPE), F32), kr_ref[...],
                               jnp.zeros((rows, LANE - QK_NOPE - QK_ROPE), F32)], axis=1))
    ones_col = jnp.where(lane == V_HEAD, 1.0, 0.0)
    for h in range(GROUP_HEADS):
        cols = slice(h * LANE, (h + 1) * LANE)
        q_out[:, cols] = (rope(qf[:, cols]) * scale).astype(BF16)
        k_out[:, cols] = (kf[:, cols] + kr).astype(BF16)
        v_out[:, cols] = (vf[:, cols] + ones_col).astype(BF16)


def mla_prep(pos, c_q, c_kv, k_r, gq, gkv, w_uq, w_ukv):
    t = c_q.shape[0]
    H = GROUP_HEADS
    wq = jnp.pad(w_uq.reshape(Q_LORA, H, QK_NOPE + QK_ROPE), ((0, 0), (0, 0), (0, LANE - QK_NOPE - QK_ROPE)))
    wkv = w_ukv.reshape(KV_LORA, H, QK_NOPE + V_HEAD)
    wk = jnp.pad(wkv[:, :, :QK_NOPE], ((0, 0), (0, 0), (0, LANE - QK_NOPE)))
    wv = jnp.pad(wkv[:, :, QK_NOPE:], ((0, 0), (0, 0), (0, LANE - V_HEAD)))
    wq, wk, wv = [w.reshape(w.shape[0], H * LANE).astype(BF16) for w in (wq, wk, wv)]
    inv_freq = ROPE_THETA ** (-np.arange(0, QK_ROPE, 2, dtype=np.float32) / QK_ROPE)
    freq = np.zeros((1, LANE), np.float32)
    freq[0, QK_NOPE:QK_NOPE + QK_ROPE] = np.tile(inv_freq, 2)
    row = lambda n: pl.BlockSpec((ROW_TILE, n), lambda i: (i, 0))
    fixed = lambda a: pl.BlockSpec(a.shape, lambda i: (0,) * a.ndim)
    gq2, gkv2, freq = gq.reshape(1, -1), gkv.reshape(1, -1), jnp.asarray(freq)
    out = jax.ShapeDtypeStruct((t, H * LANE), BF16)
    return pl.pallas_call(
        _mla_prep_kernel,
        grid=(t // ROW_TILE,),
        in_specs=[row(1), fixed(freq), row(Q_LORA), row(KV_LORA), row(QK_ROPE), fixed(gq2), fixed(gkv2),
                  fixed(wq), fixed(wk), fixed(wv)],
        out_specs=[row(H * LANE)] * 3,
        out_shape=[out] * 3,
        compiler_params=pltpu.CompilerParams(dimension_semantics=("arbitrary",)),
        name="mla_prep",
    )(pos, freq, c_q, c_kv, k_r, gq2, gkv2, wq, wk, wv)


def _mla_attn_kernel(q_ref, k_ref, v_ref, o_ref, m_scr, acc_scr):
    i = pl.program_id(1)
    tq = q_ref.shape[1]
    row = i * tq + lax.broadcasted_iota(jnp.int32, (tq, MLA_TK), 0)
    col = lax.broadcasted_iota(jnp.int32, (tq, MLA_TK), 1)
    n_full = (i * tq) // MLA_TK
    heads = range(GROUP_HEADS)
    cols = [slice(h * LANE, (h + 1) * LANE) for h in heads]
    q = [q_ref[0, :, cols[h]] for h in heads]
    m_scr[...] = jnp.full(m_scr.shape, NEG, F32)
    acc_scr[...] = jnp.zeros(acc_scr.shape, F32)

    def step(j, masked):
        start = pl.multiple_of(j * MLA_TK, MLA_TK)

        def qk(h):
            s = _dot_nt(q[h], k_ref[0, pl.ds(start, MLA_TK), cols[h]])
            return jnp.where(start + col <= row, s, NEG) if masked else s

        def absorb(h, s):
            m_old = m_scr[h]
            m_new = jnp.maximum(m_old, jnp.max(s, axis=-1, keepdims=True))
            p = jnp.exp(s - m_new).astype(BF16)
            acc_scr[h] = jnp.exp(m_old - m_new) * acc_scr[h] + _dot(p, v_ref[0, pl.ds(start, MLA_TK), cols[h]])
            m_scr[h] = m_new

        s_prev = qk(0)
        for h in range(1, GROUP_HEADS):
            s_next = qk(h)
            absorb(h - 1, s_prev)
            s_prev = s_next
        absorb(GROUP_HEADS - 1, s_prev)

    def body(j, carry):
        step(j, False)
        return carry

    lax.fori_loop(0, n_full, body, 0)
    step(n_full, True)
    outs = [acc_scr[h][:, :V_HEAD] * (1.0 / acc_scr[h][:, V_HEAD:V_HEAD + 1]) for h in heads]
    o_ref[0] = jnp.concatenate(outs, axis=1)


def mla_attention(q, k, v):
    b, s, w = q.shape
    return pl.pallas_call(
        _mla_attn_kernel,
        grid=(b, s // MLA_TQ),
        in_specs=[pl.BlockSpec((1, MLA_TQ, w), lambda bi, i: (bi, i, 0)),
                  pl.BlockSpec((1, s, w), lambda bi, i: (bi, 0, 0)),
                  pl.BlockSpec((1, s, w), lambda bi, i: (bi, 0, 0))],
        out_specs=pl.BlockSpec((1, MLA_TQ, GROUP_HEADS * V_HEAD), lambda bi, i: (bi, i, 0)),
        out_shape=jax.ShapeDtypeStruct((b, s, GROUP_HEADS * V_HEAD), F32),
        scratch_shapes=[pltpu.VMEM((GROUP_HEADS, MLA_TQ, 1), F32), pltpu.VMEM((GROUP_HEADS, MLA_TQ, LANE), F32)],
        compiler_params=pltpu.CompilerParams(dimension_semantics=("arbitrary", "arbitrary"),
                                             vmem_limit_bytes=VMEM_LIMIT),
        name="mla_attention",
    )(q, k, v)


def mla_group(c_q, c_kv, k_r, positions, q_norm_g, kv_norm_g, w_uq, w_ukv):
    b, s = positions.shape
    t = b * s
    q, k, v = mla_prep(positions.reshape(t, 1), c_q, c_kv, k_r, q_norm_g, kv_norm_g, w_uq, w_ukv)
    w = GROUP_HEADS * LANE
    return mla_attention(q.reshape(b, s, w), k.reshape(b, s, w), v.reshape(b, s, w))


CHUNK = 64
CONV_W = 4


def _split3(a):
    hi = a.astype(BF16)
    r = a - hi.astype(F32)
    mid = r.astype(BF16)
    return hi, mid, (r - mid.astype(F32)).astype(BF16)


def _dot_exact_rhs(sel, a):
    return sum(_dot(sel, p) for p in _split3(a))


def _row_matrix(col, n_rows):
    l = col.shape[0]
    lane = lax.broadcasted_iota(jnp.int32, (l, SUBLANES), 1)
    padded = jnp.where(lane == 0, col, 0.0)
    ones = jnp.ones((n_rows, SUBLANES), BF16)
    return sum(_dot_nt(ones, p) for p in _split3(padded))


def _transpose_bf16(x_bf):
    d = x_bf.shape[1]
    eye = (lax.broadcasted_iota(jnp.int32, (d, d), 0) == lax.broadcasted_iota(jnp.int32, (d, d), 1)).astype(BF16)
    return _dot_nt(eye, x_bf).astype(BF16)


def _log_sigmoid(x):
    return jnp.minimum(x, 0.0) - jnp.log1p(jnp.exp(-jnp.abs(x)))


def _softplus(x):
    return jnp.maximum(x, 0.0) + jnp.log1p(jnp.exp(-jnp.abs(x)))


def _with_ones_col(v):
    l, d = v.shape
    lane = lax.broadcasted_iota(jnp.int32, (l, LANE - d), 1)
    return jnp.concatenate([v, jnp.where(lane == 0, 1.0, 0.0)], axis=1)


def _bf(x):
    return x.astype(BF16)


def _mlstm_kernel(q_ref, k_ref, v_ref, gate_ref, og_ref, bias_ref, ng_ref, o_ref, st_scr, m_scr):
    nb = q_ref.shape[0]
    H, D, L = GROUP_HEADS, HEAD_DIM, CHUNK
    chains = [(b, h) for b in range(nb) for h in range(H)]
    cols = lambda h: slice(h * D, (h + 1) * D)
    each = lambda fn: [fn(c, b, h) for c, (b, h) in enumerate(chains)]

    @pl.when(pl.program_id(0) == 0)
    def _():
        st_scr[...] = jnp.zeros(st_scr.shape, F32)
        m_scr[...] = jnp.zeros(m_scr.shape, F32)

    row = lax.broadcasted_iota(jnp.int32, (L, L), 0)
    col = lax.broadcasted_iota(jnp.int32, (L, L), 1)
    tri = row >= col
    tri_bf = tri.astype(BF16)
    gates = [gate_ref[b] + bias_ref[...] for b in range(nb)]
    bcum = [_dot_exact_rhs(tri_bf, _log_sigmoid(g[:, H:])) for g in gates]
    q = each(lambda c, b, h: _bf(q_ref[b, :, cols(h)]))
    kt = each(lambda c, b, h: _transpose_bf16(_bf(k_ref[b, :, cols(h)] * (D ** -0.5))))
    v_aug = each(lambda c, b, h: _with_ones_col(v_ref[b, :, cols(h)]))
    bc = each(lambda c, b, h: bcum[b][:, h:h + 1])
    ic = each(lambda c, b, h: gates[b][:, h:h + 1])
    m_st = each(lambda c, b, h: m_scr[c, 0:1, 0:1])
    st = each(lambda c, b, h: st_scr[c])
    log_d = each(lambda c, b, h: jnp.where(tri, bc[c] + _row_matrix(ic[c] - bc[c], L), -jnp.inf))
    m_t = each(lambda c, b, h: jnp.maximum(jnp.max(log_d[c], axis=-1, keepdims=True), bc[c] + m_st[c]))
    sqk = each(lambda c, b, h: _dot(q[c], kt[c]) * jnp.exp(log_d[c] - m_t[c]))
    inter = each(lambda c, b, h: jnp.exp(bc[c] + m_st[c] - m_t[c]))
    tot = each(lambda c, b, h: _dot(_bf(sqk[c]), _bf(v_aug[c])) + inter[c] * _dot(q[c], _bf(st[c])))
    h_t = each(lambda c, b, h: tot[c][:, :D] * (1.0 / jnp.maximum(jnp.abs(tot[c][:, D:D + 1]), jnp.exp(-m_t[c]))))
    b_end = each(lambda c, b, h: bc[c][L - 1:L])
    log_w = each(lambda c, b, h: b_end[c] - bc[c] + ic[c])
    m_new = each(lambda c, b, h: jnp.maximum(b_end[c] + m_st[c], jnp.max(log_w[c], axis=0, keepdims=True)))
    upd = each(lambda c, b, h: _dot(kt[c], _bf(jnp.exp(log_w[c] - m_new[c]) * v_aug[c])))
    for c in range(len(chains)):
        st_scr[c] = jnp.exp(b_end[c] + m_st[c] - m_new[c]) * st[c] + upd[c]
        m_scr[c] = jnp.broadcast_to(m_new[c], m_scr.shape[1:])
    for b in range(nb):
        outs = []
        for h in range(H):
            o = jax.nn.sigmoid(og_ref[b, :, cols(h)]) * h_t[b * H + h]
            outs.append(o * lax.rsqrt(jnp.mean(o * o, axis=-1, keepdims=True) + NORM_EPS) * ng_ref[:, cols(h)])
        o_ref[b] = jnp.concatenate(outs, axis=1)


def mlstm_group(q, k, v, gates, og, i_bias, f_bias, norm_g):
    b, s, _ = q.shape
    W, H = GROUP_W, GROUP_HEADS
    bias = jnp.concatenate([i_bias, f_bias]).reshape(1, 2 * H)
    tok = lambda n: pl.BlockSpec((b, CHUNK, n), lambda i: (0, i, 0))
    fixed = lambda n: pl.BlockSpec((1, n), lambda i: (0, 0))
    return pl.pallas_call(
        _mlstm_kernel,
        grid=(s // CHUNK,),
        in_specs=[tok(W), tok(W), tok(W), tok(2 * H), tok(W), fixed(2 * H), fixed(W)],
        out_specs=tok(W),
        out_shape=jax.ShapeDtypeStruct((b, s, W), F32),
        scratch_shapes=[pltpu.VMEM((b * H, HEAD_DIM, LANE), F32), pltpu.VMEM((b * H, SUBLANES, LANE), F32)],
        compiler_params=pltpu.CompilerParams(dimension_semantics=("arbitrary",)),
        name="mlstm",
    )(q, k, v, gates, og, bias, norm_g.reshape(1, W))


def _gdn_kernel(x_ref, xprev_ref, cw_ref, gate_ref, z_ref, par_ref, ng_ref, o_ref, st_scr):
    nb = x_ref.shape[0]
    H, D, L = GROUP_HEADS, HEAD_DIM, CHUNK
    W = H * D
    chains = [(b, h) for b in range(nb) for h in range(H)]
    each = lambda fn: [fn(c, b, h) for c, (b, h) in enumerate(chains)]
    first = pl.program_id(0) == 0

    @pl.when(first)
    def _():
        st_scr[...] = jnp.zeros(st_scr.shape, F32)

    row = lax.broadcasted_iota(jnp.int32, (L, L), 0)
    col = lax.broadcasted_iota(jnp.int32, (L, L), 1)
    tri = row >= col
    tri_bf = tri.astype(BF16)
    eye = (row == col).astype(F32)
    par = par_ref[...]
    qkv, beta4, gc4 = [], [], []
    for b in range(nb):
        prev = jnp.where(first, 0.0, xprev_ref[b])
        full = jnp.concatenate([prev, x_ref[b]], axis=0)
        acc = full[SUBLANES:] * cw_ref[CONV_W - 1:CONV_W, :]
        for j in range(CONV_W - 1):
            shifted = pltpu.roll(full, CONV_W - 1 - j, axis=0)[SUBLANES:]
            acc = acc + shifted * cw_ref[j:j + 1, :]
        qkv.append(acc * jax.nn.sigmoid(acc))
        gates = gate_ref[b]
        beta4.append(jax.nn.sigmoid(gates[:, :H]))
        gc4.append(_dot_exact_rhs(tri_bf, -jnp.exp(par[:, :H]) * _softplus(gates[:, H:] + par[:, H:])))

    def l2n(t):
        return t * lax.rsqrt(jnp.sum(t * t, axis=-1, keepdims=True) + NORM_EPS)

    q = each(lambda c, b, h: l2n(qkv[b][:, h * D:(h + 1) * D]) * (D ** -0.5))
    k = each(lambda c, b, h: l2n(qkv[b][:, W + h * D:W + (h + 1) * D]))
    v = each(lambda c, b, h: qkv[b][:, 2 * W + h * D:2 * W + (h + 1) * D])
    beta = each(lambda c, b, h: beta4[b][:, h:h + 1])
    gc = each(lambda c, b, h: gc4[b][:, h:h + 1])
    gc_row = each(lambda c, b, h: _row_matrix(gc[c], L))
    decay = each(lambda c, b, h: jnp.where(tri, jnp.exp(gc[c] - gc_row[c]), 0.0))
    kt = each(lambda c, b, h: _transpose_bf16(_bf(k[c])))
    kb = each(lambda c, b, h: k[c] * beta[c])
    x = each(lambda c, b, h: -jnp.where(row > col, _dot(_bf(kb[c]), kt[c]) * decay[c], 0.0))
    t_inv = [eye + xc for xc in x]
    for _ in range(5):
        x = [_dot(_bf(xc), _bf(xc)) for xc in x]
        t_inv = [tc + _dot(_bf(tc), _bf(xc)) for tc, xc in zip(t_inv, x)]
    u = each(lambda c, b, h: _dot(_bf(t_inv[c]), _bf(v[c] * beta[c])))
    wm = each(lambda c, b, h: _dot(_bf(t_inv[c]), _bf(kb[c] * jnp.exp(gc[c]))))
    attn = each(lambda c, b, h: _dot(_bf(q[c]), kt[c]) * decay[c])
    st = each(lambda c, b, h: st_scr[c])
    v_new = each(lambda c, b, h: u[c] - _dot(_bf(wm[c]), _bf(st[c])))
    o = each(lambda c, b, h: _dot(_bf(q[c] * jnp.exp(gc[c])), _bf(st[c])) + _dot(_bf(attn[c]), _bf(v_new[c])))
    g_end = each(lambda c, b, h: gc[c][L - 1:L])
    upd = each(lambda c, b, h: _dot(_bf(kt[c].astype(F32) * jnp.exp(g_end[c] - gc_row[c])), _bf(v_new[c])))
    for c in range(len(chains)):
        st_scr[c] = st[c] * jnp.exp(g_end[c]) + upd[c]
    for b in range(nb):
        outs = []
        for h in range(H):
            oc = o[b * H + h]
            z = z_ref[b, :, h * D:(h + 1) * D]
            y = oc * lax.rsqrt(jnp.mean(oc * oc, axis=-1, keepdims=True) + NORM_EPS) * ng_ref[...]
            outs.append(y * (z * jax.nn.sigmoid(z)))
        o_ref[b] = jnp.concatenate(outs, axis=1)


def gdn_group(x, gates, z, conv_w, a_log, dt_bias, norm_g):
    b, s, _ = x.shape
    W, H = GROUP_W, GROUP_HEADS
    par = jnp.concatenate([a_log, dt_bias]).reshape(1, 2 * H)
    tok = lambda n: pl.BlockSpec((b, CHUNK, n), lambda i: (0, i, 0))
    per_chunk = CHUNK // SUBLANES
    prev = pl.BlockSpec((b, SUBLANES, 3 * W), lambda i: (0, jnp.maximum(i * per_chunk - 1, 0), 0))
    fixed = lambda r, n: pl.BlockSpec((r, n), lambda i: (0, 0))
    return pl.pallas_call(
        _gdn_kernel,
        grid=(s // CHUNK,),
        in_specs=[tok(3 * W), prev, fixed(CONV_W, 3 * W), tok(2 * H), tok(W), fixed(1, 2 * H), fixed(1, HEAD_DIM)],
        out_specs=tok(W),
        out_shape=jax.ShapeDtypeStruct((b, s, W), F32),
        scratch_shapes=[pltpu.VMEM((b * H, HEAD_DIM, HEAD_DIM), F32)],
        compiler_params=pltpu.CompilerParams(dimension_semantics=("arbitrary",)),
        name="gdn",
    )(x, x, conv_w, gates, z, par, norm_g.reshape(1, HEAD_DIM))


NSA_IN = GROUP_W + 6 * HEAD_DIM + 3 * GROUP_HEADS
MLSTM_IN = 4 * GROUP_W + 2 * GROUP_HEADS
GDN_IN = 4 * GROUP_W + 2 * GROUP_HEADS
MLA_IN = Q_LORA + KV_LORA + QK_ROPE


def kernel(x, positions, t5_table, norm1_g, norm2_g, final_norm_g, w_in, w_out, nsa_w_cmp_k, nsa_w_cmp_v, nsa_cmp_pos, mlstm_i_bias, mlstm_f_bias, mlstm_norm_g, gdn_conv_w, gdn_a_log, gdn_dt_bias, gdn_norm_g, mla_q_norm_g, mla_kv_norm_g, mla_w_uq, mla_w_ukv, peer_w_q, peer_sub_keys, peer_u, peer_v):
    b, s, d = x.shape
    t = b * s
    h = x.reshape(t, d)
    depth = w_in.shape[0]
    W, D, H = GROUP_W, HEAD_DIM, GROUP_HEADS
    bias_tiles = nsa_bias_tiles(t5_table)
    c_ml, c_gdn, c_mla = NSA_IN, NSA_IN + MLSTM_IN, NSA_IN + MLSTM_IN + GDN_IN
    small = [(W + 6 * D, 3 * H), (c_ml + 3 * W, 2 * H), (c_gdn + 3 * W, 2 * H),
             (c_mla + Q_LORA + KV_LORA, QK_ROPE)]
    groups = [(0, W, W, F32, None),
              (W, D, D, F32, None), (W + D, D, D, F32, None),
              (W + 2 * D, D, D, BF16, None), (W + 3 * D, D, 2 * D, BF16, D),
              (W + 4 * D, D, D, BF16, None), (W + 5 * D, D, 2 * D, BF16, D),
              (c_ml, W, W, F32, None), (c_ml + W, W, W, F32, None), (c_ml + 2 * W, W, W, F32, None),
              (c_ml + 3 * W + 2 * H, W, W, F32, None),
              (c_gdn, 3 * W, 3 * W, F32, None), (c_gdn + 3 * W + 2 * H, W, W, F32, None),
              (c_mla, Q_LORA, Q_LORA, F32, None), (c_mla + Q_LORA, KV_LORA, KV_LORA, F32, None)]
    for l in range(depth):
        w_l = w_in[l].astype(BF16)
        weights = [jnp.pad(w_l[:, c:c + n], ((0, 0), (0, wide - n))) for c, n, wide, _, _ in groups]
        w_small = jnp.concatenate([w_l[:, c:c + n] for c, n in small], axis=1)
        weights.append(jnp.pad(w_small, ((0, 0), (0, LANE - w_small.shape[1]))))
        outs = norm_matmul_split(h, norm1_g[l], weights, [g[3] for g in groups] + [F32],
                                 [g[4] for g in groups] + [None])
        (nsa_q, kc, vc, ks, vs_aug, kw, vw_aug, ml_q, ml_k, ml_v, ml_og, gdn_x, gdn_z, c_q, c_kv, misc) = outs
        offs = np.cumsum([0] + [n for _, n in small])
        nsa_gates, ml_gates, gdn_gates, k_r = [misc[:, offs[j]:offs[j + 1]] for j in range(len(small))]
        seq = lambda a: a.reshape(b, s, a.shape[-1])
        parts = [
            nsa_group(seq(nsa_q), seq(kc), seq(vc), seq(ks), seq(vs_aug), seq(kw), seq(vw_aug), seq(nsa_gates),
                      nsa_w_cmp_k[l], nsa_w_cmp_v[l], nsa_cmp_pos[l], t5_table, bias_tiles),
            mlstm_group(seq(ml_q), seq(ml_k), seq(ml_v), seq(ml_gates), seq(ml_og),
                        mlstm_i_bias[l], mlstm_f_bias[l], mlstm_norm_g[l]),
            gdn_group(seq(gdn_x), seq(gdn_gates), seq(gdn_z), gdn_conv_w[l], gdn_a_log[l], gdn_dt_bias[l],
                      gdn_norm_g[l]),
            mla_group(c_q, c_kv, k_r, positions, mla_q_norm_g[l], mla_kv_norm_g[l], mla_w_uq[l], mla_w_ukv[l]),
        ]
        w_o = w_out[l].astype(BF16)
        h = matmul_residual([p.reshape(t, W) for p in parts], [w_o[j * W:(j + 1) * W] for j in range(len(parts))], h)
        h = peer_ffn_residual(h, norm2_g[l], peer_w_q[l], peer_sub_keys[l], peer_u[l], peer_v[l])
    return rmsnorm(h, final_norm_g).reshape(b, s, d)
```

```python
import functools
import math

import jax
import jax.numpy as jnp
import numpy as np
from jax import lax
from jax.experimental import pallas as pl
from jax.experimental.pallas import tpu as pltpu

F32 = jnp.float32
BF16 = jnp.bfloat16
NORM_EPS = 1e-6
NEG = -1e30
LANE = 128
ROW_TILE = 512
VMEM_LIMIT = 56 * 1024 * 1024

HEAD_DIM = 64
GROUP_HEADS = 4
GROUP_W = GROUP_HEADS * HEAD_DIM
Q_BLOCK = 128
CMP_LEN = 32
CMP_STRIDE = 16
SEL_LEN = 64
SEL_TOPN = 16
WINDOW = 512
SEL_FORCE = 1e3
N_BUCKETS = 32
MAX_DISTANCE = 128


def _dot_nt(a, b):
    return lax.dot_general(a, b, (((1,), (1,)), ((), ())), preferred_element_type=F32)


def _dot(a, b):
    return jnp.dot(a, b, preferred_element_type=F32)


def _norm_matmul_split_kernel(*refs, ones_cols):
    n = len(ones_cols)
    x_ref, g_ref = refs[:2]
    w_refs, o_refs = refs[2:2 + n], refs[2 + n:]
    x = x_ref[...]
    y = (x * lax.rsqrt(jnp.mean(x * x, axis=-1, keepdims=True) + NORM_EPS) * g_ref[...]).astype(BF16)
    for w_ref, o_ref, col in zip(w_refs, o_refs, ones_cols):
        r = _dot(y, w_ref[...])
        if col is not None:
            r = r + jnp.where(lax.broadcasted_iota(jnp.int32, r.shape, 1) == col, 1.0, 0.0)
        o_ref[...] = r.astype(o_ref.dtype)


def norm_matmul_split(x, g, weights, dtypes, ones_cols):
    t, d = x.shape
    assert t % ROW_TILE == 0
    row = lambda n: pl.BlockSpec((ROW_TILE, n), lambda i: (i, 0))
    return pl.pallas_call(
        functools.partial(_norm_matmul_split_kernel, ones_cols=tuple(ones_cols)),
        grid=(t // ROW_TILE,),
        in_specs=[row(d), pl.BlockSpec((1, d), lambda i: (0, 0))]
                 + [pl.BlockSpec(w.shape, lambda i: (0, 0)) for w in weights],
        out_specs=[row(w.shape[1]) for w in weights],
        out_shape=[jax.ShapeDtypeStruct((t, w.shape[1]), dt) for w, dt in zip(weights, dtypes)],
        compiler_params=pltpu.CompilerParams(dimension_semantics=("arbitrary",),
                                             vmem_limit_bytes=VMEM_LIMIT),
        name="norm_matmul",
    )(x, g.reshape(1, d), *weights)


def _matmul_residual_kernel(*refs):
    n = (len(refs) - 2) // 2
    o_refs, w_refs, r_ref, out_ref = refs[:n], refs[n:2 * n], refs[2 * n], refs[2 * n + 1]
    acc = r_ref[...]
    for o_ref, w_ref in zip(o_refs, w_refs):
        acc = acc + _dot(o_ref[...].astype(BF16), w_ref[...])
    out_ref[...] = acc


def matmul_residual(parts, weights, res):
    t, n = res.shape
    row = lambda w: pl.BlockSpec((ROW_TILE, w), lambda i: (i, 0))
    return pl.pallas_call(
        _matmul_residual_kernel,
        grid=(t // ROW_TILE,),
        in_specs=[row(o.shape[1]) for o in parts]
                 + [pl.BlockSpec(w.shape, lambda i: (0, 0)) for w in weights] + [row(n)],
        out_specs=row(n),
        out_shape=jax.ShapeDtypeStruct((t, n), F32),
        compiler_params=pltpu.CompilerParams(dimension_semantics=("arbitrary",),
                                             vmem_limit_bytes=VMEM_LIMIT),
        name="matmul_residual",
    )(*parts, *weights, res)


def _rmsnorm_kernel(x_ref, g_ref, o_ref):
    x = x_ref[...]
    o_ref[...] = x * lax.rsqrt(jnp.mean(x * x, axis=-1, keepdims=True) + NORM_EPS) * g_ref[...]


def rmsnorm(x, g):
    t, d = x.shape
    return pl.pallas_call(
        _rmsnorm_kernel,
        grid=(t // ROW_TILE,),
        in_specs=[pl.BlockSpec((ROW_TILE, d), lambda i: (i, 0)),
                  pl.BlockSpec((1, d), lambda i: (0, 0))],
        out_specs=pl.BlockSpec((ROW_TILE, d), lambda i: (i, 0)),
        out_shape=jax.ShapeDtypeStruct((t, d), F32),
        compiler_params=pltpu.CompilerParams(dimension_semantics=("arbitrary",)),
        name="rmsnorm",
    )(x, g.reshape(1, d))


def _t5_bucket_starts():
    d = np.arange(0, 2 * MAX_DISTANCE)
    max_exact = N_BUCKETS // 2
    val = (np.log(np.maximum(d, 1).astype(np.float32) / np.float32(max_exact))
           / np.float32(math.log(MAX_DISTANCE / max_exact)) * np.float32(N_BUCKETS - max_exact))
    large = np.minimum(max_exact + val.astype(np.int32), N_BUCKETS - 1)
    bucket = np.where(d < max_exact, d, large)
    assert np.all(np.diff(bucket) >= 0)
    return [(int(k), int(d[bucket == k].min())) for k in range(N_BUCKETS) if np.any(bucket == k)]


T5_STARTS = _t5_bucket_starts()


def _t5_bias_heads(dist, tab_ref):
    accs = [jnp.full(dist.shape, tab_ref[0, h], F32) for h in range(GROUP_HEADS)]
    for bucket, start in T5_STARTS[1:]:
        m = dist >= start
        accs = [jnp.where(m, tab_ref[bucket, h], accs[h]) for h in range(GROUP_HEADS)]
    return jnp.stack(accs)


def _nsa_compress_kernel(x_ref, pos_ref, w_ref, o_ref):
    x = x_ref[0, 0]
    n = x.shape[0]
    lo = _dot((x + pos_ref[0, 0:1, :]).astype(BF16), w_ref[0, 0])
    hi = _dot((x + pos_ref[0, 1:2, :]).astype(BF16), w_ref[0, 1])
    o_ref[0, 0] = (lo + pltpu.roll(hi, n - 1, axis=0)).astype(o_ref.dtype)


def nsa_compress(x, pos, w):
    two, b, n, c = x.shape
    return pl.pallas_call(
        _nsa_compress_kernel,
        grid=(two, b),
        in_specs=[pl.BlockSpec((1, 1, n, c), lambda k, i: (k, i, 0, 0)),
                  pl.BlockSpec((1, 2, c), lambda k, i: (0, 0, 0)),
                  pl.BlockSpec((1, 2, c, HEAD_DIM), lambda k, i: (k, 0, 0, 0))],
        out_specs=pl.BlockSpec((1, 1, n, HEAD_DIM), lambda k, i: (k, i, 0, 0)),
        out_shape=jax.ShapeDtypeStruct((two, b, n, HEAD_DIM), BF16),
        compiler_params=pltpu.CompilerParams(dimension_semantics=("arbitrary", "arbitrary")),
        name="nsa_compress",
    )(x, pos, w)


FAR_TILE = 512
NEAR_KEYS = FAR_TILE + Q_BLOCK
WIN_KEYS = WINDOW + Q_BLOCK
NEAR_PAD = NEAR_KEYS - Q_BLOCK
CMP_TILE = LANE
CMP_TILES_BACK = CMP_TILE * CMP_STRIDE // Q_BLOCK
N_CMP_BIAS = CMP_TILES_BACK + 3


def _masked_bias(dist, valid, tab_ref):
    return jnp.where(valid, _t5_bias_heads(dist, tab_ref), NEG)


def _nsa_bias_kernel(tab_ref, near_ref, win_ref):
    q = lax.broadcasted_iota(jnp.int32, (Q_BLOCK, NEAR_KEYS), 0)
    k = lax.broadcasted_iota(jnp.int32, (Q_BLOCK, NEAR_KEYS), 1)
    dist = NEAR_PAD + q - k
    near_ref[...] = _masked_bias(dist, dist >= 0, tab_ref)
    q = lax.broadcasted_iota(jnp.int32, (Q_BLOCK, WIN_KEYS), 0)
    k = lax.broadcasted_iota(jnp.int32, (Q_BLOCK, WIN_KEYS), 1)
    dist = WINDOW + q - k
    win_ref[...] = _masked_bias(dist, (dist >= 0) & (dist < WINDOW), tab_ref)


def _nsa_cmp_bias_kernel(tab_ref, o_ref):
    e = pl.program_id(0)
    q = lax.broadcasted_iota(jnp.int32, (Q_BLOCK, CMP_TILE), 0)
    n = lax.broadcasted_iota(jnp.int32, (Q_BLOCK, CMP_TILE), 1)
    dist = (e - 1) * Q_BLOCK + q - (n * CMP_STRIDE + CMP_LEN - 1)
    o_ref[0] = _masked_bias(dist, dist >= 0, tab_ref)


def nsa_bias_tiles(t5_table):
    H, Q = GROUP_HEADS, Q_BLOCK
    smem = pl.BlockSpec(memory_space=pltpu.SMEM)
    near, win = pl.pallas_call(
        _nsa_bias_kernel, grid=(1,), in_specs=[smem],
        out_specs=[pl.BlockSpec((H, Q, NEAR_KEYS), lambda i: (0, 0, 0)),
                   pl.BlockSpec((H, Q, WIN_KEYS), lambda i: (0, 0, 0))],
        out_shape=[jax.ShapeDtypeStruct((H, Q, NEAR_KEYS), F32), jax.ShapeDtypeStruct((H, Q, WIN_KEYS), F32)],
        name="nsa_bias",
    )(t5_table)
    cmp = pl.pallas_call(
        _nsa_cmp_bias_kernel, grid=(N_CMP_BIAS,), in_specs=[smem],
        out_specs=pl.BlockSpec((1, H, Q, CMP_TILE), lambda e: (e, 0, 0, 0)),
        out_shape=jax.ShapeDtypeStruct((N_CMP_BIAS, H, Q, CMP_TILE), F32),
        name="nsa_cmp_bias",
    )(t5_table)
    return near, win, cmp


def _nsa_attn_kernel(tab_ref, q_ref, g_ref, kc_ref, vc_ref, ks_ref, vs_ref, kw_ref, vw_ref,
                     bnear_ref, bwin_ref, bcmp_ref, o_ref, m_scr, acc_scr):
    H, Q, D = GROUP_HEADS, Q_BLOCK, HEAD_DIM
    n_cmp = kc_ref.shape[1]
    n_sel = (ks_ref.shape[1] - NEAR_PAD) // SEL_LEN
    i = pl.program_id(1)
    t0 = i * Q

    q = q_ref[0] * (D ** -0.5)
    q4 = jnp.concatenate([q[:, h * D:(h + 1) * D] for h in range(H)], axis=0).astype(BF16)

    s = _dot_nt(q4, kc_ref[0]).reshape(H, Q, n_cmp)
    tiles = []
    for g in range(n_cmp // CMP_TILE):
        e = jnp.clip(i - g * CMP_TILES_BACK, -1, CMP_TILES_BACK + 1) + 1
        tiles.append(s[:, :, g * CMP_TILE:(g + 1) * CMP_TILE] + bcmp_ref[e])
    s = jnp.concatenate(tiles, axis=-1)
    t_c = t0 + lax.broadcasted_iota(jnp.int32, (Q, n_cmp), 0)
    valid_c = t_c >= lax.broadcasted_iota(jnp.int32, (Q, n_cmp), 1) * CMP_STRIDE + CMP_LEN - 1
    p = jnp.where(valid_c, jnp.exp(s - jnp.max(s, axis=-1, keepdims=True)), 0.0)
    l = jnp.sum(p, axis=-1, keepdims=True)
    p = p * (1.0 / jnp.where(l > 0.0, l, 1.0))
    o_c = _dot(p.reshape(H * Q, n_cmp).astype(BF16), vc_ref[0])

    psum = p[0] + p[1] + p[2] + p[3]
    c_idx = lax.broadcasted_iota(jnp.int32, (n_cmp, n_sel), 0)
    j_idx = lax.broadcasted_iota(jnp.int32, (n_cmp, n_sel), 1)
    ratio = SEL_LEN // CMP_STRIDE
    n_in_sel = (SEL_LEN - CMP_LEN) // CMP_STRIDE + 1
    pool = ((c_idx // ratio == j_idx) & (c_idx % ratio < n_in_sel)).astype(BF16)
    p_hi = psum.astype(BF16)
    p_mid = (psum - p_hi.astype(F32)).astype(BF16)
    p_lo = (psum - p_hi.astype(F32) - p_mid.astype(F32)).astype(BF16)
    imp = _dot(p_hi, pool) + _dot(p_mid, pool) + _dot(p_lo, pool)

    row0 = pl.multiple_of(t0, Q)
    kw = kw_ref[0, pl.ds(row0, WIN_KEYS), :]
    vw = vw_ref[0, pl.ds(row0, WIN_KEYS), :]
    in_seq = lax.broadcasted_iota(jnp.int32, (1, WIN_KEYS), 1) >= WINDOW - t0
    s = _dot_nt(q4, kw).reshape(H, Q, WIN_KEYS) + bwin_ref[...] + jnp.where(in_seq, 0.0, NEG)
    p_w = jnp.exp(s - jnp.max(s, axis=-1, keepdims=True))
    acc_w = _dot(p_w.reshape(H * Q, WIN_KEYS).astype(BF16), vw)
    o_w = acc_w[:, :D] * (1.0 / acc_w[:, D:D + 1])

    jb = lax.broadcasted_iota(jnp.int32, (Q, n_sel), 1)
    cur = (t0 + lax.broadcasted_iota(jnp.int32, (Q, n_sel), 0)) // SEL_LEN
    forced = (jb == 0) | (jb == cur) | (jb == cur - 1)
    work = jnp.where(jb <= cur, imp + jnp.where(forced, SEL_FORCE, 0.0), NEG)
    sel = jnp.zeros((Q, n_sel), F32)
    jb_f = jb.astype(F32)
    for _ in range(min(SEL_TOPN, n_sel)):
        mx = jnp.max(work, axis=-1, keepdims=True)
        first = jnp.min(jnp.where(work == mx, jb_f, jnp.inf), axis=-1, keepdims=True)
        pick = jb_f == first
        sel = jnp.where(pick & (mx > 0.5 * NEG), 1.0, sel)
        work = jnp.where(pick, -jnp.inf, work)
    sel_bf = sel.astype(BF16)

    far_bias = jnp.stack([jnp.full((Q, 1), tab_ref[N_BUCKETS - 1, h], F32) for h in range(H)])
    m_scr[...] = jnp.full(m_scr.shape, NEG, F32)
    acc_scr[...] = jnp.zeros(acc_scr.shape, F32)

    def sel_scores(row0, n_keys, bias, first_key):
        s = _dot_nt(q4, ks_ref[0, pl.ds(row0, n_keys), :]).reshape(H, Q, n_keys) + bias
        pos = row0 - NEAR_PAD + lax.broadcasted_iota(jnp.int32, (n_sel, n_keys), 1)
        blk = jnp.where(pos >= first_key, pos // SEL_LEN, -1)
        expand = (lax.broadcasted_iota(jnp.int32, (n_sel, n_keys), 0) == blk).astype(BF16)
        mask = _dot(sel_bf, expand) > 0.5
        return jnp.where(mask, s, NEG), mask

    def sel_absorb(row0, n_keys, s, mask):
        m_old = m_scr[...].reshape(H, Q, 1)
        m_new = jnp.maximum(m_old, jnp.max(s, axis=-1, keepdims=True))
        p = jnp.where(mask, jnp.exp(s - m_new), 0.0).reshape(H * Q, n_keys)
        alpha = jnp.exp(m_old - m_new).reshape(H * Q, 1)
        acc_scr[...] = alpha * acc_scr[...] + _dot(p.astype(BF16), vs_ref[0, pl.ds(row0, n_keys), :])
        m_scr[...] = m_new.reshape(H * Q, 1)

    def sel_step(row0, n_keys, bias, first_key):
        sel_absorb(row0, n_keys, *sel_scores(row0, n_keys, bias, first_key))

    n_far = jnp.maximum(i - 1, 0) // (FAR_TILE // Q)

    def far_body(kp, carry):
        row_a = pl.multiple_of(NEAR_PAD + kp * (2 * FAR_TILE), Q)
        row_b = pl.multiple_of(row_a + FAR_TILE, Q)
        sa = sel_scores(row_a, FAR_TILE, far_bias, 0)
        sb = sel_scores(row_b, FAR_TILE, far_bias, 0)
        sel_absorb(row_a, FAR_TILE, *sa)
        sel_absorb(row_b, FAR_TILE, *sb)
        return carry

    lax.fori_loop(0, n_far // 2, far_body, 0)

    @pl.when(n_far % 2 == 1)
    def _():
        sel_step(pl.multiple_of(NEAR_PAD + (n_far - 1) * FAR_TILE, Q), FAR_TILE, far_bias, 0)
    sel_step(row0, NEAR_KEYS, bnear_ref[...], n_far * FAR_TILE)
    acc = acc_scr[...]
    o_s = acc[:, :D] * (1.0 / acc[:, D:D + 1])

    g = jax.nn.sigmoid(g_ref[0])
    outs = []
    for h in range(H):
        rows = slice(h * Q, (h + 1) * Q)
        outs.append(g[:, h:h + 1] * o_c[rows] + g[:, H + h:H + h + 1] * o_s[rows]
                    + g[:, 2 * H + h:2 * H + h + 1] * o_w[rows])
    o_ref[0] = jnp.concatenate(outs, axis=1)


def nsa_attention(t5_table, bias_tiles, q, gates, k_cmp, v_cmp, ks, vs_aug, kw, vw_aug):
    b, s, _ = q.shape
    assert s % (CMP_TILE * CMP_STRIDE) == 0 and s >= NEAR_KEYS
    bnear, bwin, bcmp = bias_tiles
    full = lambda a: pl.BlockSpec((1,) + a.shape[1:], lambda bi, i: (bi,) + (0,) * (a.ndim - 1))
    fixed = lambda a: pl.BlockSpec(a.shape, lambda bi, i: (0,) * a.ndim)
    return pl.pallas_call(
        _nsa_attn_kernel,
        grid=(b, s // Q_BLOCK),
        in_specs=[pl.BlockSpec(memory_space=pltpu.SMEM),
                  pl.BlockSpec((1, Q_BLOCK, GROUP_W), lambda bi, i: (bi, i, 0)),
                  pl.BlockSpec((1, Q_BLOCK, 3 * GROUP_HEADS), lambda bi, i: (bi, i, 0)),
                  full(k_cmp), full(v_cmp), full(ks), full(vs_aug), full(kw), full(vw_aug),
                  fixed(bnear), fixed(bwin), fixed(bcmp)],
        out_specs=pl.BlockSpec((1, Q_BLOCK, GROUP_W), lambda bi, i: (bi, i, 0)),
        out_shape=jax.ShapeDtypeStruct((b, s, GROUP_W), F32),
        scratch_shapes=[pltpu.VMEM((GROUP_HEADS * Q_BLOCK, 1), F32),
                        pltpu.VMEM((GROUP_HEADS * Q_BLOCK, 2 * HEAD_DIM), F32)],
        compiler_params=pltpu.CompilerParams(dimension_semantics=("arbitrary", "arbitrary"),
                                             vmem_limit_bytes=VMEM_LIMIT),
        name="nsa_attention",
    )(t5_table, q, gates, k_cmp, v_cmp, ks, vs_aug, kw, vw_aug, bnear, bwin, bcmp)


def _front_pad(x, rows):
    return jnp.pad(x, ((0, 0), (rows, 0), (0, 0)))


def nsa_group(q, kc, vc, ks, vs_aug, kw, vw_aug, gates, w_cmp_k, w_cmp_v, cmp_pos, t5_table, bias_tiles):
    b, s, _ = q.shape
    d = HEAD_DIM
    n = s // CMP_STRIDE
    x = jnp.stack([kc, vc]).reshape(2, b, n, CMP_STRIDE * d)
    pos = cmp_pos.reshape(1, 2, CMP_STRIDE * d)
    w = jnp.stack([w_cmp_k, w_cmp_v]).reshape(2, 2, CMP_STRIDE * d, d).astype(BF16)
    cmp = nsa_compress(x, pos, w)
    return nsa_attention(t5_table, bias_tiles, q, gates, cmp[0], cmp[1],
                         _front_pad(ks, NEAR_PAD), _front_pad(vs_aug, NEAR_PAD),
                         _front_pad(kw, WINDOW), _front_pad(vw_aug, WINDOW))


PEER_HEADS = 8
PEER_DK = 256
N_KEYS = 128
PEER_TOPK = 16
PEER_SLOTS = PEER_HEADS * PEER_TOPK
ROUTE_TILE = 128
GATHER_TILE = 8


def _top_rows(work, order, n_top, payload=None):
    vals, args = [], []
    order_f = order.astype(F32)
    for _ in range(n_top):
        mx = jnp.max(work, axis=0, keepdims=True)
        first = jnp.min(jnp.where(work == mx, order_f, jnp.inf), axis=0, keepdims=True)
        pick = order_f == first
        vals.append(mx)
        if payload is None:
            args.append(first.astype(jnp.int32))
        else:
            args.append(jnp.sum(jnp.where(pick, payload, 0), axis=0, keepdims=True))
        work = jnp.where(pick, -jnp.inf, work)
    return jnp.concatenate(vals, axis=0), jnp.concatenate(args, axis=0)


def _pair_rows(s1, s2, combine):
    k = PEER_TOPK
    rows = [combine(s1[0:1], s2)]
    for a in range(1, k // 2):
        rows.append(combine(s1[a:a + 1], s2[0:k // 2]))
    rows.append(combine(s1[k // 2:], s2[0:1]))
    return jnp.concatenate(rows, axis=0)


def _peer_route_kernel(h_ref, g_ref, wq_ref, keys_ref, xn_ref, eidx_ref, gate_ref):
    x = h_ref[...]
    xn = x * lax.rsqrt(jnp.mean(x * x, axis=-1, keepdims=True) + NORM_EPS) * g_ref[...]
    xn_ref[...] = xn
    q = _dot(xn.astype(BF16), wq_ref[...]).astype(BF16)
    tq = x.shape[0]
    half = PEER_DK // 2
    n_iota = lax.broadcasted_iota(jnp.int32, (N_KEYS, tq), 0)
    k_iota = lax.broadcasted_iota(jnp.int32, (PEER_TOPK, tq), 0)
    order = _pair_rows(k_iota, k_iota, lambda a, b: a * PEER_TOPK + b)
    reachable = (order // PEER_TOPK + 1) * (order % PEER_TOPK + 1) <= PEER_TOPK
    for p in range(PEER_HEADS):
        tops = []
        for c in range(2):
            g = 2 * p + c
            sc = _dot_nt(keys_ref[g], q[:, g * half:(g + 1) * half])
            tops.append(_top_rows(sc, n_iota, PEER_TOPK))
        (s1, i1), (s2, i2) = tops
        cand = jnp.where(reachable, _pair_rows(s1, s2, lambda a, b: a + b), -jnp.inf)
        cidx = _pair_rows(i1, i2, lambda a, b: a * N_KEYS + b)
        top, eidx = _top_rows(cand, order, PEER_TOPK, payload=cidx)
        e = jnp.exp(top - top[0:1])
        eidx_ref[p * PEER_TOPK:(p + 1) * PEER_TOPK, :] = eidx
        gate_ref[p * PEER_TOPK:(p + 1) * PEER_TOPK, :] = e * (1.0 / jnp.sum(e, axis=0, keepdims=True))


def peer_route(h, g, w_q, keys):
    t, d = h.shape
    assert t % ROUTE_TILE == 0
    nq = w_q.shape[1]
    return pl.pallas_call(
        _peer_route_kernel,
        grid=(t // ROUTE_TILE,),
        in_specs=[pl.BlockSpec((ROUTE_TILE, d), lambda i: (i, 0)),
                  pl.BlockSpec((1, d), lambda i: (0, 0)),
                  pl.BlockSpec((d, nq), lambda i: (0, 0)),
                  pl.BlockSpec(keys.shape, lambda i: (0, 0, 0))],
        out_specs=[pl.BlockSpec((ROUTE_TILE, d), lambda i: (i, 0)),
                   pl.BlockSpec((PEER_SLOTS, ROUTE_TILE), lambda i: (0, i)),
                   pl.BlockSpec((PEER_SLOTS, ROUTE_TILE), lambda i: (0, i))],
        out_shape=[jax.ShapeDtypeStruct((t, d), F32),
                   jax.ShapeDtypeStruct((PEER_SLOTS, t), jnp.int32),
                   jax.ShapeDtypeStruct((PEER_SLOTS, t), F32)],
        compiler_params=pltpu.CompilerParams(dimension_semantics=("arbitrary",),
                                             vmem_limit_bytes=VMEM_LIMIT),
        name="peer_route",
    )(h, g.reshape(1, d), w_q, keys)


SUBLANES = 8
SLAB = 2 * SUBLANES
ISSUE_UNROLL = 8


def _gelu(x):
    return 0.5 * x * (1.0 + lax.erf(x * (2.0 ** -0.5)))


BITREV8 = (0, 4, 2, 6, 1, 5, 3, 7)


def _fold_pair(a, b, half, sub):
    low = (sub % (2 * half)) < half
    partner = jnp.where(low, pltpu.roll(a, SUBLANES - half, axis=0), pltpu.roll(b, half, axis=0))
    return jnp.where(low, a, b) + partner


GATHER_DEPTH = 3
DMA_QUEUES = 2


def _peer_gather_kernel(idx0_ref, idx1_ref, idx2_ref, xn_ref, h_ref, gate_ref, uv_hbm, o_ref,
                        buf_a, buf_b, buf_c, w_scr, sem):
    i = pl.program_id(0)
    n = pl.num_programs(0)
    tt = xn_ref.shape[0]
    n_slabs = tt * PEER_SLOTS
    bufs = (buf_a, buf_b, buf_c)

    def request(ref, k, r, queue):
        src = pl.multiple_of(ref[0, 0, r] * SLAB, SLAB)
        pltpu.make_async_copy(uv_hbm.at[pl.ds(src, SLAB)], bufs[k].at[pl.ds(r * SLAB, SLAB)],
                              sem.at[k]).start(priority=queue)

    def request_tile(ref, k):
        def body(c, carry):
            for j in range(ISSUE_UNROLL):
                request(ref, k, pl.multiple_of(c * ISSUE_UNROLL, ISSUE_UNROLL) + j, j % DMA_QUEUES)
            return carry
        lax.fori_loop(0, n_slabs // ISSUE_UNROLL, body, 0)

    def wait_tile(k):
        pltpu.make_async_copy(uv_hbm.at[pl.ds(0, n_slabs * SLAB)], bufs[k], sem.at[k]).wait()

    @pl.when(i == 0)
    def _():
        request_tile(idx0_ref, 0)
        request_tile(idx1_ref, 1)

    def step(cur, ahead):
        wait_tile(cur)
        gates = gate_ref[0]
        sub = lax.broadcasted_iota(jnp.int32, (SUBLANES, LANE), 0)
        for t in range(tt):
            for j in range(PEER_SLOTS):
                request(idx2_ref, ahead, t * PEER_SLOTS + j, j % DMA_QUEUES)
            x = xn_ref[t]

            def slab(pos, half):
                return bufs[cur][pl.ds((t * PEER_SLOTS + pos) * SLAB + half * SUBLANES, SUBLANES), :]

            groups = []
            for g in range(PEER_SLOTS // SUBLANES):
                p = [slab(g * SUBLANES + k, 0) * x for k in range(SUBLANES)]
                c = [_fold_pair(p[2 * k], p[2 * k + 1], 4, sub) for k in range(4)]
                d = [_fold_pair(c[0], c[1], 2, sub), _fold_pair(c[2], c[3], 2, sub)]
                groups.append(_fold_pair(d[0], d[1], 1, sub))
            act = jnp.sum(jnp.concatenate(groups, axis=0), axis=-1, keepdims=True)
            w_scr[...] = jnp.broadcast_to(gates[:, t:t + 1] * _gelu(act), (PEER_SLOTS, LANE))
            acc = h_ref[t]
            for g in range(PEER_SLOTS // SUBLANES):
                for k in range(SUBLANES):
                    w_row = w_scr[pl.ds(g * SUBLANES + BITREV8[k], 1), :]
                    acc = acc + slab(g * SUBLANES + k, 1) * w_row
            o_ref[t] = acc

    for k in range(GATHER_DEPTH):
        @pl.when(i % GATHER_DEPTH == k)
        def _(k=k):
            step(k, (k + GATHER_DEPTH - 1) % GATHER_DEPTH)

    @pl.when(i == n - 1)
    def _():
        for k in range(GATHER_DEPTH):
            @pl.when(i % GATHER_DEPTH == k)
            def _(k=k):
                wait_tile((k + 1) % GATHER_DEPTH)
                wait_tile((k + 2) % GATHER_DEPTH)


def peer_gather(idx, xn, h, gate_b, uv):
    t = xn.shape[0]
    nt = t // GATHER_TILE
    assert t % GATHER_TILE == 0 and nt >= GATHER_DEPTH
    n_slabs = GATHER_TILE * PEER_SLOTS
    tok = pl.BlockSpec((GATHER_TILE, SUBLANES, LANE), lambda i: (i, 0, 0))
    idx_spec = lambda k: pl.BlockSpec((1, 1, n_slabs), lambda i: (jnp.minimum(i + k, nt - 1), 0, 0),
                                      memory_space=pltpu.SMEM)
    slab_buf = pltpu.VMEM((n_slabs * SLAB, LANE), F32)
    return pl.pallas_call(
        _peer_gather_kernel,
        grid=(nt,),
        in_specs=[idx_spec(0), idx_spec(1), idx_spec(2), tok, tok,
                  pl.BlockSpec((1, PEER_SLOTS, GATHER_TILE), lambda i: (i, 0, 0)),
                  pl.BlockSpec(memory_space=pl.ANY)],
        out_specs=tok,
        out_shape=jax.ShapeDtypeStruct((t, SUBLANES, LANE), F32),
        scratch_shapes=[slab_buf, slab_buf, slab_buf, pltpu.VMEM((PEER_SLOTS, LANE), F32),
                        pltpu.SemaphoreType.DMA((GATHER_DEPTH,))],
        compiler_params=pltpu.CompilerParams(dimension_semantics=("arbitrary",),
                                             vmem_limit_bytes=VMEM_LIMIT),
        name="peer_gather",
    )(idx, idx, idx, xn, h, gate_b, uv)


def peer_ffn_residual(h, g, w_q, sub_keys, u_tab, v_tab):
    t, d = h.shape
    e = u_tab.shape[0]
    keys = sub_keys.reshape(PEER_HEADS * 2, N_KEYS, PEER_DK // 2).astype(BF16)
    xn, eidx_t, gate_t = peer_route(h, g, w_q.astype(BF16), keys)
    nt = t // GATHER_TILE
    order = np.arange(PEER_SLOTS).reshape(-1, SUBLANES)[:, list(BITREV8)].reshape(-1)
    idx = eidx_t[order].T.reshape(nt, 1, GATHER_TILE * PEER_SLOTS)
    gate_b = gate_t.reshape(PEER_SLOTS, nt, GATHER_TILE).transpose(1, 0, 2)
    uv = jnp.concatenate([u_tab.reshape(e, SUBLANES, LANE), v_tab.reshape(e, SUBLANES, LANE)],
                         axis=1).reshape(e * SLAB, LANE)
    out = peer_gather(idx, xn.reshape(t, SUBLANES, LANE), h.reshape(t, SUBLANES, LANE), gate_b, uv)
    return out.reshape(t, d)


Q_LORA = 256
KV_LORA = 128
QK_NOPE = 64
QK_ROPE = 32
V_HEAD = 64
ROPE_THETA = 10000.0
MLA_TQ = 256
MLA_TK = 1024


def _rms(x, g):
    return x * lax.rsqrt(jnp.mean(x * x, axis=-1, keepdims=True) + NORM_EPS) * g


def _mla_prep_kernel(pos_ref, freq_ref, cq_ref, ckv_ref, kr_ref, gq_ref, gkv_ref, wq_ref, wk_ref, wv_ref,
                     q_out, k_out, v_out):
    rows = cq_ref.shape[0]
    half = QK_ROPE // 2
    cq = _rms(cq_ref[...], gq_ref[...]).astype(BF16)
    ckv = _rms(ckv_ref[...], gkv_ref[...]).astype(BF16)
    qf = _dot(cq, wq_ref[...])
    kf = _dot(ckv, wk_ref[...])
    vf = _dot(ckv, wv_ref[...])
    lane = lax.broadcasted_iota(jnp.int32, (rows, LANE), 1)
    ang = pos_ref[...].astype(F32) * freq_ref[...]
    cos, sin = jnp.cos(ang), jnp.sin(ang)
    in_rope = (lane >= QK_NOPE) & (lane < QK_NOPE + QK_ROPE)
    first = lane < QK_NOPE + half
    c_mul = jnp.where(lane < QK_NOPE, 1.0, jnp.where(in_rope, cos, 0.0))
    s_mul = jnp.where(in_rope, jnp.where(first, -sin, sin), 0.0)

    def rope(x):
        partner = jnp.where(first, pltpu.roll(x, LANE - half, axis=1), pltpu.roll(x, half, axis=1))
        return x * c_mul + partner * s_mul

    scale = (QK_NOPE + QK_ROPE) ** -0.5
    kr = rope(jnp.concatenate([jnp.zeros((rows, QK_NOPE), F32), kr_ref[...],
                               jnp.zeros((rows, LANE - QK_NOPE - QK_ROPE), F32)], axis=1))
    ones_col = jnp.where(lane == V_HEAD, 1.0, 0.0)
    for h in range(GROUP_HEADS):
        cols = slice(h * LANE, (h + 1) * LANE)
        q_out[:, cols] = (rope(qf[:, cols]) * scale).astype(BF16)
        k_out[:, cols] = (kf[:, cols] + kr).astype(BF16)
        v_out[:, cols] = (vf[:, cols] + ones_col).astype(BF16)


def mla_prep(pos, c_q, c_kv, k_r, gq, gkv, w_uq, w_ukv):
    t = c_q.shape[0]
    H = GROUP_HEADS
    wq = jnp.pad(w_uq.reshape(Q_LORA, H, QK_NOPE + QK_ROPE), ((0, 0), (0, 0), (0, LANE - QK_NOPE - QK_ROPE)))
    wkv = w_ukv.reshape(KV_LORA, H, QK_NOPE + V_HEAD)
    wk = jnp.pad(wkv[:, :, :QK_NOPE], ((0, 0), (0, 0), (0, LANE - QK_NOPE)))
    wv = jnp.pad(wkv[:, :, QK_NOPE:], ((0, 0), (0, 0), (0, LANE - V_HEAD)))
    wq, wk, wv = [w.reshape(w.shape[0], H * LANE).astype(BF16) for w in (wq, wk, wv)]
    inv_freq = ROPE_THETA ** (-np.arange(0, QK_ROPE, 2, dtype=np.float32) / QK_ROPE)
    freq = np.zeros((1, LANE), np.float32)
    freq[0, QK_NOPE:QK_NOPE + QK_ROPE] = np.tile(inv_freq, 2)
    row = lambda n: pl.BlockSpec((ROW_TILE, n), lambda i: (i, 0))
    fixed = lambda a: pl.BlockSpec(a.shape, lambda i: (0,) * a.ndim)
    gq2, gkv2, freq = gq.reshape(1, -1), gkv.reshape(1, -1), jnp.asarray(freq)
    out = jax.ShapeDtypeStruct((t, H * LANE), BF16)
    return pl.pallas_call(
        _mla_prep_kernel,
        grid=(t // ROW_TILE,),
        in_specs=[row(1), fixed(freq), row(Q_LORA), row(KV_LORA), row(QK_ROPE), fixed(gq2), fixed(gkv2),
                  fixed(wq), fixed(wk), fixed(wv)],
        out_specs=[row(H * LANE)] * 3,
        out_shape=[out] * 3,
        compiler_params=pltpu.CompilerParams(dimension_semantics=("arbitrary",)),
        name="mla_prep",
    )(pos, freq, c_q, c_kv, k_r, gq2, gkv2, wq, wk, wv)


def _mla_attn_kernel(q_ref, k_ref, v_ref, o_ref, m_scr, acc_scr):
    i = pl.program_id(1)
    tq = q_ref.shape[1]
    row = i * tq + lax.broadcasted_iota(jnp.int32, (tq, MLA_TK), 0)
    col = lax.broadcasted_iota(jnp.int32, (tq, MLA_TK), 1)
    n_full = (i * tq) // MLA_TK
    heads = range(GROUP_HEADS)
    cols = [slice(h * LANE, (h + 1) * LANE) for h in heads]
    q = [q_ref[0, :, cols[h]] for h in heads]
    m_scr[...] = jnp.full(m_scr.shape, NEG, F32)
    acc_scr[...] = jnp.zeros(acc_scr.shape, F32)

    def step(j, masked):
        start = pl.multiple_of(j * MLA_TK, MLA_TK)

        def qk(h):
            s = _dot_nt(q[h], k_ref[0, pl.ds(start, MLA_TK), cols[h]])
            return jnp.where(start + col <= row, s, NEG) if masked else s

        def absorb(h, s):
            m_old = m_scr[h]
            m_new = jnp.maximum(m_old, jnp.max(s, axis=-1, keepdims=True))
            p = jnp.exp(s - m_new).astype(BF16)
            acc_scr[h] = jnp.exp(m_old - m_new) * acc_scr[h] + _dot(p, v_ref[0, pl.ds(start, MLA_TK), cols[h]])
            m_scr[h] = m_new

        s_prev = qk(0)
        for h in range(1, GROUP_HEADS):
            s_next = qk(h)
            absorb(h - 1, s_prev)
            s_prev = s_next
        absorb(GROUP_HEADS - 1, s_prev)

    def body(j, carry):
        step(j, False)
        return carry

    lax.fori_loop(0, n_full, body, 0)
    step(n_full, True)
    outs = [acc_scr[h][:, :V_HEAD] * (1.0 / acc_scr[h][:, V_HEAD:V_HEAD + 1]) for h in heads]
    o_ref[0] = jnp.concatenate(outs, axis=1)


def mla_attention(q, k, v):
    b, s, w = q.shape
    assert s % MLA_TK == 0 and MLA_TK % MLA_TQ == 0
    return pl.pallas_call(
        _mla_attn_kernel,
        grid=(b, s // MLA_TQ),
        in_specs=[pl.BlockSpec((1, MLA_TQ, w), lambda bi, i: (bi, i, 0)),
                  pl.BlockSpec((1, s, w), lambda bi, i: (bi, 0, 0)),
                  pl.BlockSpec((1, s, w), lambda bi, i: (bi, 0, 0))],
        out_specs=pl.BlockSpec((1, MLA_TQ, GROUP_HEADS * V_HEAD), lambda bi, i: (bi, i, 0)),
        out_shape=jax.ShapeDtypeStruct((b, s, GROUP_HEADS * V_HEAD), F32),
        scratch_shapes=[pltpu.VMEM((GROUP_HEADS, MLA_TQ, 1), F32), pltpu.VMEM((GROUP_HEADS, MLA_TQ, LANE), F32)],
        compiler_params=pltpu.CompilerParams(dimension_semantics=("arbitrary", "arbitrary"),
                                             vmem_limit_bytes=VMEM_LIMIT),
        name="mla_attention",
    )(q, k, v)


def mla_group(c_q, c_kv, k_r, positions, q_norm_g, kv_norm_g, w_uq, w_ukv):
    b, s = positions.shape
    t = b * s
    q, k, v = mla_prep(positions.reshape(t, 1), c_q, c_kv, k_r, q_norm_g, kv_norm_g, w_uq, w_ukv)
    w = GROUP_HEADS * LANE
    return mla_attention(q.reshape(b, s, w), k.reshape(b, s, w), v.reshape(b, s, w))


CHUNK = 64
CONV_W = 4


def _split3(a):
    hi = a.astype(BF16)
    r = a - hi.astype(F32)
    mid = r.astype(BF16)
    return hi, mid, (r - mid.astype(F32)).astype(BF16)


def _dot_exact_rhs(sel, a):
    return sum(_dot(sel, p) for p in _split3(a))


def _row_matrix(col, n_rows):
    l = col.shape[0]
    lane = lax.broadcasted_iota(jnp.int32, (l, SUBLANES), 1)
    padded = jnp.where(lane == 0, col, 0.0)
    ones = jnp.ones((n_rows, SUBLANES), BF16)
    return sum(_dot_nt(ones, p) for p in _split3(padded))


def _transpose_bf16(x_bf):
    d = x_bf.shape[1]
    eye = (lax.broadcasted_iota(jnp.int32, (d, d), 0) == lax.broadcasted_iota(jnp.int32, (d, d), 1)).astype(BF16)
    return _dot_nt(eye, x_bf).astype(BF16)


def _log_sigmoid(x):
    return jnp.minimum(x, 0.0) - jnp.log1p(jnp.exp(-jnp.abs(x)))


def _softplus(x):
    return jnp.maximum(x, 0.0) + jnp.log1p(jnp.exp(-jnp.abs(x)))


def _with_ones_col(v):
    l, d = v.shape
    lane = lax.broadcasted_iota(jnp.int32, (l, LANE - d), 1)
    return jnp.concatenate([v, jnp.where(lane == 0, 1.0, 0.0)], axis=1)


def _bf(x):
    return x.astype(BF16)


def _mlstm_kernel(q_ref, k_ref, v_ref, gate_ref, og_ref, bias_ref, ng_ref, o_ref, st_scr, m_scr):
    nb = q_ref.shape[0]
    H, D, L = GROUP_HEADS, HEAD_DIM, CHUNK
    chains = [(b, h) for b in range(nb) for h in range(H)]
    cols = lambda h: slice(h * D, (h + 1) * D)
    each = lambda fn: [fn(c, b, h) for c, (b, h) in enumerate(chains)]

    @pl.when(pl.program_id(0) == 0)
    def _():
        st_scr[...] = jnp.zeros(st_scr.shape, F32)
        m_scr[...] = jnp.zeros(m_scr.shape, F32)

    row = lax.broadcasted_iota(jnp.int32, (L, L), 0)
    col = lax.broadcasted_iota(jnp.int32, (L, L), 1)
    tri = row >= col
    tri_bf = tri.astype(BF16)
    gates = [gate_ref[b] + bias_ref[...] for b in range(nb)]
    bcum = [_dot_exact_rhs(tri_bf, _log_sigmoid(g[:, H:])) for g in gates]
    q = each(lambda c, b, h: _bf(q_ref[b, :, cols(h)]))
    kt = each(lambda c, b, h: _transpose_bf16(_bf(k_ref[b, :, cols(h)] * (D ** -0.5))))
    v_aug = each(lambda c, b, h: _with_ones_col(v_ref[b, :, cols(h)]))
    bc = each(lambda c, b, h: bcum[b][:, h:h + 1])
    ic = each(lambda c, b, h: gates[b][:, h:h + 1])
    m_st = each(lambda c, b, h: m_scr[c, 0:1, 0:1])
    st = each(lambda c, b, h: st_scr[c])
    log_d = each(lambda c, b, h: jnp.where(tri, bc[c] + _row_matrix(ic[c] - bc[c], L), -jnp.inf))
    m_t = each(lambda c, b, h: jnp.maximum(jnp.max(log_d[c], axis=-1, keepdims=True), bc[c] + m_st[c]))
    sqk = each(lambda c, b, h: _dot(q[c], kt[c]) * jnp.exp(log_d[c] - m_t[c]))
    inter = each(lambda c, b, h: jnp.exp(bc[c] + m_st[c] - m_t[c]))
    tot = each(lambda c, b, h: _dot(_bf(sqk[c]), _bf(v_aug[c])) + inter[c] * _dot(q[c], _bf(st[c])))
    h_t = each(lambda c, b, h: tot[c][:, :D] * (1.0 / jnp.maximum(jnp.abs(tot[c][:, D:D + 1]), jnp.exp(-m_t[c]))))
    b_end = each(lambda c, b, h: bc[c][L - 1:L])
    log_w = each(lambda c, b, h: b_end[c] - bc[c] + ic[c])
    m_new = each(lambda c, b, h: jnp.maximum(b_end[c] + m_st[c], jnp.max(log_w[c], axis=0, keepdims=True)))
    upd = each(lambda c, b, h: _dot(kt[c], _bf(jnp.exp(log_w[c] - m_new[c]) * v_aug[c])))
    for c in range(len(chains)):
        st_scr[c] = jnp.exp(b_end[c] + m_st[c] - m_new[c]) * st[c] + upd[c]
        m_scr[c] = jnp.broadcast_to(m_new[c], m_scr.shape[1:])
    for b in range(nb):
        outs = []
        for h in range(H):
            o = jax.nn.sigmoid(og_ref[b, :, cols(h)]) * h_t[b * H + h]
            outs.append(o * lax.rsqrt(jnp.mean(o * o, axis=-1, keepdims=True) + NORM_EPS) * ng_ref[:, cols(h)])
        o_ref[b] = jnp.concatenate(outs, axis=1)


def mlstm_group(q, k, v, gates, og, i_bias, f_bias, norm_g):
    b, s, _ = q.shape
    assert s % CHUNK == 0
    W, H = GROUP_W, GROUP_HEADS
    bias = jnp.concatenate([i_bias, f_bias]).reshape(1, 2 * H)
    tok = lambda n: pl.BlockSpec((b, CHUNK, n), lambda i: (0, i, 0))
    fixed = lambda n: pl.BlockSpec((1, n), lambda i: (0, 0))
    return pl.pallas_call(
        _mlstm_kernel,
        grid=(s // CHUNK,),
        in_specs=[tok(W), tok(W), tok(W), tok(2 * H), tok(W), fixed(2 * H), fixed(W)],
        out_specs=tok(W),
        out_shape=jax.ShapeDtypeStruct((b, s, W), F32),
        scratch_shapes=[pltpu.VMEM((b * H, HEAD_DIM, LANE), F32), pltpu.VMEM((b * H, SUBLANES, LANE), F32)],
        compiler_params=pltpu.CompilerParams(dimension_semantics=("arbitrary",)),
        name="mlstm",
    )(q, k, v, gates, og, bias, norm_g.reshape(1, W))


def _gdn_kernel(x_ref, xprev_ref, cw_ref, gate_ref, z_ref, par_ref, ng_ref, o_ref, st_scr):
    nb = x_ref.shape[0]
    H, D, L = GROUP_HEADS, HEAD_DIM, CHUNK
    W = H * D
    chains = [(b, h) for b in range(nb) for h in range(H)]
    each = lambda fn: [fn(c, b, h) for c, (b, h) in enumerate(chains)]
    first = pl.program_id(0) == 0

    @pl.when(first)
    def _():
        st_scr[...] = jnp.zeros(st_scr.shape, F32)

    row = lax.broadcasted_iota(jnp.int32, (L, L), 0)
    col = lax.broadcasted_iota(jnp.int32, (L, L), 1)
    tri = row >= col
    tri_bf = tri.astype(BF16)
    eye = (row == col).astype(F32)
    par = par_ref[...]
    qkv, beta4, gc4 = [], [], []
    for b in range(nb):
        prev = jnp.where(first, 0.0, xprev_ref[b])
        full = jnp.concatenate([prev, x_ref[b]], axis=0)
        acc = full[SUBLANES:] * cw_ref[CONV_W - 1:CONV_W, :]
        for j in range(CONV_W - 1):
            shifted = pltpu.roll(full, CONV_W - 1 - j, axis=0)[SUBLANES:]
            acc = acc + shifted * cw_ref[j:j + 1, :]
        qkv.append(acc * jax.nn.sigmoid(acc))
        gates = gate_ref[b]
        beta4.append(jax.nn.sigmoid(gates[:, :H]))
        gc4.append(_dot_exact_rhs(tri_bf, -jnp.exp(par[:, :H]) * _softplus(gates[:, H:] + par[:, H:])))

    def l2n(t):
        return t * lax.rsqrt(jnp.sum(t * t, axis=-1, keepdims=True) + NORM_EPS)

    q = each(lambda c, b, h: l2n(qkv[b][:, h * D:(h + 1) * D]) * (D ** -0.5))
    k = each(lambda c, b, h: l2n(qkv[b][:, W + h * D:W + (h + 1) * D]))
    v = each(lambda c, b, h: qkv[b][:, 2 * W + h * D:2 * W + (h + 1) * D])
    beta = each(lambda c, b, h: beta4[b][:, h:h + 1])
    gc = each(lambda c, b, h: gc4[b][:, h:h + 1])
    gc_row = each(lambda c, b, h: _row_matrix(gc[c], L))
    decay = each(lambda c, b, h: jnp.where(tri, jnp.exp(gc[c] - gc_row[c]), 0.0))
    kt = each(lambda c, b, h: _transpose_bf16(_bf(k[c])))
    kb = each(lambda c, b, h: k[c] * beta[c])
    x = each(lambda c, b, h: -jnp.where(row > col, _dot(_bf(kb[c]), kt[c]) * decay[c], 0.0))
    t_inv = [eye + xc for xc in x]
    for _ in range(5):
        x = [_dot(_bf(xc), _bf(xc)) for xc in x]
        t_inv = [tc + _dot(_bf(tc), _bf(xc)) for tc, xc in zip(t_inv, x)]
    u = each(lambda c, b, h: _dot(_bf(t_inv[c]), _bf(v[c] * beta[c])))
    wm = each(lambda c, b, h: _dot(_bf(t_inv[c]), _bf(kb[c] * jnp.exp(gc[c]))))
    attn = each(lambda c, b, h: _dot(_bf(q[c]), kt[c]) * decay[c])
    st = each(lambda c, b, h: st_scr[c])
    v_new = each(lambda c, b, h: u[c] - _dot(_bf(wm[c]), _bf(st[c])))
    o = each(lambda c, b, h: _dot(_bf(q[c] * jnp.exp(gc[c])), _bf(st[c])) + _dot(_bf(attn[c]), _bf(v_new[c])))
    g_end = each(lambda c, b, h: gc[c][L - 1:L])
    upd = each(lambda c, b, h: _dot(_bf(kt[c].astype(F32) * jnp.exp(g_end[c] - gc_row[c])), _bf(v_new[c])))
    for c in range(len(chains)):
        st_scr[c] = st[c] * jnp.exp(g_end[c]) + upd[c]
    for b in range(nb):
        outs = []
        for h in range(H):
            oc = o[b * H + h]
            z = z_ref[b, :, h * D:(h + 1) * D]
            y = oc * lax.rsqrt(jnp.mean(oc * oc, axis=-1, keepdims=True) + NORM_EPS) * ng_ref[...]
            outs.append(y * (z * jax.nn.sigmoid(z)))
        o_ref[b] = jnp.concatenate(outs, axis=1)


def gdn_group(x, gates, z, conv_w, a_log, dt_bias, norm_g):
    b, s, _ = x.shape
    assert s % CHUNK == 0
    W, H = GROUP_W, GROUP_HEADS
    par = jnp.concatenate([a_log, dt_bias]).reshape(1, 2 * H)
    tok = lambda n: pl.BlockSpec((b, CHUNK, n), lambda i: (0, i, 0))
    per_chunk = CHUNK // SUBLANES
    prev = pl.BlockSpec((b, SUBLANES, 3 * W), lambda i: (0, jnp.maximum(i * per_chunk - 1, 0), 0))
    fixed = lambda r, n: pl.BlockSpec((r, n), lambda i: (0, 0))
    return pl.pallas_call(
        _gdn_kernel,
        grid=(s // CHUNK,),
        in_specs=[tok(3 * W), prev, fixed(CONV_W, 3 * W), tok(2 * H), tok(W), fixed(1, 2 * H), fixed(1, HEAD_DIM)],
        out_specs=tok(W),
        out_shape=jax.ShapeDtypeStruct((b, s, W), F32),
        scratch_shapes=[pltpu.VMEM((b * H, HEAD_DIM, HEAD_DIM), F32)],
        compiler_params=pltpu.CompilerParams(dimension_semantics=("arbitrary",)),
        name="gdn",
    )(x, x, conv_w, gates, z, par, norm_g.reshape(1, HEAD_DIM))


NSA_IN = GROUP_W + 6 * HEAD_DIM + 3 * GROUP_HEADS
MLSTM_IN = 4 * GROUP_W + 2 * GROUP_HEADS
GDN_IN = 4 * GROUP_W + 2 * GROUP_HEADS
MLA_IN = Q_LORA + KV_LORA + QK_ROPE


def kernel(x, positions, t5_table, norm1_g, norm2_g, final_norm_g, w_in, w_out, nsa_w_cmp_k, nsa_w_cmp_v, nsa_cmp_pos, mlstm_i_bias, mlstm_f_bias, mlstm_norm_g, gdn_conv_w, gdn_a_log, gdn_dt_bias, gdn_norm_g, mla_q_norm_g, mla_kv_norm_g, mla_w_uq, mla_w_ukv, peer_w_q, peer_sub_keys, peer_u, peer_v):
    b, s, d = x.shape
    t = b * s
    h = x.reshape(t, d)
    depth = w_in.shape[0]
    W, D, H = GROUP_W, HEAD_DIM, GROUP_HEADS
    bias_tiles = nsa_bias_tiles(t5_table)
    c_ml, c_gdn, c_mla = NSA_IN, NSA_IN + MLSTM_IN, NSA_IN + MLSTM_IN + GDN_IN
    small = [(W + 6 * D, 3 * H), (c_ml + 3 * W, 2 * H), (c_gdn + 3 * W, 2 * H),
             (c_mla + Q_LORA + KV_LORA, QK_ROPE)]
    groups = [(0, W, W, F32, None),
              (W, D, D, F32, None), (W + D, D, D, F32, None),
              (W + 2 * D, D, D, BF16, None), (W + 3 * D, D, 2 * D, BF16, D),
              (W + 4 * D, D, D, BF16, None), (W + 5 * D, D, 2 * D, BF16, D),
              (c_ml, W, W, F32, None), (c_ml + W, W, W, F32, None), (c_ml + 2 * W, W, W, F32, None),
              (c_ml + 3 * W + 2 * H, W, W, F32, None),
              (c_gdn, 3 * W, 3 * W, F32, None), (c_gdn + 3 * W + 2 * H, W, W, F32, None),
              (c_mla, Q_LORA, Q_LORA, F32, None), (c_mla + Q_LORA, KV_LORA, KV_LORA, F32, None)]
    for l in range(depth):
        w_l = w_in[l].astype(BF16)
        weights = [jnp.pad(w_l[:, c:c + n], ((0, 0), (0, wide - n))) for c, n, wide, _, _ in groups]
        w_small = jnp.concatenate([w_l[:, c:c + n] for c, n in small], axis=1)
        weights.append(jnp.pad(w_small, ((0, 0), (0, LANE - w_small.shape[1]))))
        outs = norm_matmul_split(h, norm1_g[l], weights, [g[3] for g in groups] + [F32],
                                 [g[4] for g in groups] + [None])
        (nsa_q, kc, vc, ks, vs_aug, kw, vw_aug, ml_q, ml_k, ml_v, ml_og, gdn_x, gdn_z, c_q, c_kv, misc) = outs
        offs = np.cumsum([0] + [n for _, n in small])
        nsa_gates, ml_gates, gdn_gates, k_r = [misc[:, offs[j]:offs[j + 1]] for j in range(len(small))]
        seq = lambda a: a.reshape(b, s, a.shape[-1])
        parts = [
            nsa_group(seq(nsa_q), seq(kc), seq(vc), seq(ks), seq(vs_aug), seq(kw), seq(vw_aug), seq(nsa_gates),
                      nsa_w_cmp_k[l], nsa_w_cmp_v[l], nsa_cmp_pos[l], t5_table, bias_tiles),
            mlstm_group(seq(ml_q), seq(ml_k), seq(ml_v), seq(ml_gates), seq(ml_og),
                        mlstm_i_bias[l], mlstm_f_bias[l], mlstm_norm_g[l]),
            gdn_group(seq(gdn_x), seq(gdn_gates), seq(gdn_z), gdn_conv_w[l], gdn_a_log[l], gdn_dt_bias[l],
                      gdn_norm_g[l]),
            mla_group(c_q, c_kv, k_r, positions, mla_q_norm_g[l], mla_kv_norm_g[l], mla_w_uq[l], mla_w_ukv[l]),
        ]
        w_o = w_out[l].astype(BF16)
        h = matmul_residual([p.reshape(t, W) for p in parts], [w_o[j * W:(j + 1) * W] for j in range(len(parts))], h)
        h = peer_ffn_residual(h, norm2_g[l], peer_w_q[l], peer_sub_keys[l], peer_u[l], peer_v[l])
    return rmsnorm(h, final_norm_g).reshape(b, s, d)
```

```python
import functools
import math

import jax
import jax.numpy as jnp
import numpy as np
from jax import lax
from jax.experimental import pallas as pl
from jax.experimental.pallas import tpu as pltpu

F32 = jnp.float32
BF16 = jnp.bfloat16
NORM_EPS = 1e-6
NEG = -1e30
LANE = 128
ROW_TILE = 512
VMEM_LIMIT = 56 * 1024 * 1024

HEAD_DIM = 64
GROUP_HEADS = 4
GROUP_W = GROUP_HEADS * HEAD_DIM
Q_BLOCK = 128
CMP_LEN = 32
CMP_STRIDE = 16
SEL_LEN = 64
SEL_TOPN = 16
WINDOW = 512
SEL_FORCE = 1e3
N_BUCKETS = 32
MAX_DISTANCE = 128


def _dot_nt(a, b):
    return lax.dot_general(a, b, (((1,), (1,)), ((), ())), preferred_element_type=F32)


def _dot(a, b):
    return jnp.dot(a, b, preferred_element_type=F32)


def _norm_matmul_split_kernel(*refs, ones_cols):
    n = len(ones_cols)
    x_ref, g_ref = refs[:2]
    w_refs, o_refs = refs[2:2 + n], refs[2 + n:]
    x = x_ref[...]
    y = (x * lax.rsqrt(jnp.mean(x * x, axis=-1, keepdims=True) + NORM_EPS) * g_ref[...]).astype(BF16)
    for w_ref, o_ref, col in zip(w_refs, o_refs, ones_cols):
        r = _dot(y, w_ref[...])
        if col is not None:
            r = r + jnp.where(lax.broadcasted_iota(jnp.int32, r.shape, 1) == col, 1.0, 0.0)
        o_ref[...] = r.astype(o_ref.dtype)


def norm_matmul_split(x, g, weights, dtypes, ones_cols):
    t, d = x.shape
    assert t % ROW_TILE == 0
    row = lambda n: pl.BlockSpec((ROW_TILE, n), lambda i: (i, 0))
    return pl.pallas_call(
        functools.partial(_norm_matmul_split_kernel, ones_cols=tuple(ones_cols)),
        grid=(t // ROW_TILE,),
        in_specs=[row(d), pl.BlockSpec((1, d), lambda i: (0, 0))]
                 + [pl.BlockSpec(w.shape, lambda i: (0, 0)) for w in weights],
        out_specs=[row(w.shape[1]) for w in weights],
        out_shape=[jax.ShapeDtypeStruct((t, w.shape[1]), dt) for w, dt in zip(weights, dtypes)],
        compiler_params=pltpu.CompilerParams(dimension_semantics=("arbitrary",),
                                             vmem_limit_bytes=VMEM_LIMIT),
        name="norm_matmul",
    )(x, g.reshape(1, d), *weights)


def _matmul_residual_kernel(*refs):
    n = (len(refs) - 2) // 2
    o_refs, w_refs, r_ref, out_ref = refs[:n], refs[n:2 * n], refs[2 * n], refs[2 * n + 1]
    acc = r_ref[...]
    for o_ref, w_ref in zip(o_refs, w_refs):
        acc = acc + _dot(o_ref[...].astype(BF16), w_ref[...])
    out_ref[...] = acc


def matmul_residual(parts, weights, res):
    t, n = res.shape
    row = lambda w: pl.BlockSpec((ROW_TILE, w), lambda i: (i, 0))
    return pl.pallas_call(
        _matmul_residual_kernel,
        grid=(t // ROW_TILE,),
        in_specs=[row(o.shape[1]) for o in parts]
                 + [pl.BlockSpec(w.shape, lambda i: (0, 0)) for w in weights] + [row(n)],
        out_specs=row(n),
        out_shape=jax.ShapeDtypeStruct((t, n), F32),
        compiler_params=pltpu.CompilerParams(dimension_semantics=("arbitrary",),
                                             vmem_limit_bytes=VMEM_LIMIT),
        name="matmul_residual",
    )(*parts, *weights, res)


def _rmsnorm_kernel(x_ref, g_ref, o_ref):
    x = x_ref[...]
    o_ref[...] = x * lax.rsqrt(jnp.mean(x * x, axis=-1, keepdims=True) + NORM_EPS) * g_ref[...]


def rmsnorm(x, g):
    t, d = x.shape
    return pl.pallas_call(
        _rmsnorm_kernel,
        grid=(t // ROW_TILE,),
        in_specs=[pl.BlockSpec((ROW_TILE, d), lambda i: (i, 0)),
                  pl.BlockSpec((1, d), lambda i: (0, 0))],
        out_specs=pl.BlockSpec((ROW_TILE, d), lambda i: (i, 0)),
        out_shape=jax.ShapeDtypeStruct((t, d), F32),
        compiler_params=pltpu.CompilerParams(dimension_semantics=("arbitrary",)),
        name="rmsnorm",
    )(x, g.reshape(1, d))


def _t5_bucket_starts():
    d = np.arange(0, 2 * MAX_DISTANCE)
    max_exact = N_BUCKETS // 2
    val = (np.log(np.maximum(d, 1).astype(np.float32) / np.float32(max_exact))
           / np.float32(math.log(MAX_DISTANCE / max_exact)) * np.float32(N_BUCKETS - max_exact))
    large = np.minimum(max_exact + val.astype(np.int32), N_BUCKETS - 1)
    bucket = np.where(d < max_exact, d, large)
    assert np.all(np.diff(bucket) >= 0)
    return [(int(k), int(d[bucket == k].min())) for k in range(N_BUCKETS) if np.any(bucket == k)]


T5_STARTS = _t5_bucket_starts()


def _t5_bias_heads(dist, tab_ref):
    accs = [jnp.full(dist.shape, tab_ref[0, h], F32) for h in range(GROUP_HEADS)]
    for bucket, start in T5_STARTS[1:]:
        m = dist >= start
        accs = [jnp.where(m, tab_ref[bucket, h], accs[h]) for h in range(GROUP_HEADS)]
    return jnp.stack(accs)


def _nsa_compress_kernel(x_ref, pos_ref, w_ref, o_ref):
    x = x_ref[0, 0]
    n = x.shape[0]
    lo = _dot((x + pos_ref[0, 0:1, :]).astype(BF16), w_ref[0, 0])
    hi = _dot((x + pos_ref[0, 1:2, :]).astype(BF16), w_ref[0, 1])
    o_ref[0, 0] = (lo + pltpu.roll(hi, n - 1, axis=0)).astype(o_ref.dtype)


def nsa_compress(x, pos, w):
    two, b, n, c = x.shape
    return pl.pallas_call(
        _nsa_compress_kernel,
        grid=(two, b),
        in_specs=[pl.BlockSpec((1, 1, n, c), lambda k, i: (k, i, 0, 0)),
                  pl.BlockSpec((1, 2, c), lambda k, i: (0, 0, 0)),
                  pl.BlockSpec((1, 2, c, HEAD_DIM), lambda k, i: (k, 0, 0, 0))],
        out_specs=pl.BlockSpec((1, 1, n, HEAD_DIM), lambda k, i: (k, i, 0, 0)),
        out_shape=jax.ShapeDtypeStruct((two, b, n, HEAD_DIM), BF16),
        compiler_params=pltpu.CompilerParams(dimension_semantics=("arbitrary", "arbitrary")),
        name="nsa_compress",
    )(x, pos, w)


FAR_TILE = 512
NEAR_KEYS = FAR_TILE + Q_BLOCK
WIN_KEYS = WINDOW + Q_BLOCK
NEAR_PAD = NEAR_KEYS - Q_BLOCK
CMP_TILE = LANE
CMP_TILES_BACK = CMP_TILE * CMP_STRIDE // Q_BLOCK
N_CMP_BIAS = CMP_TILES_BACK + 3


def _masked_bias(dist, valid, tab_ref):
    return jnp.where(valid, _t5_bias_heads(dist, tab_ref), NEG)


def _nsa_bias_kernel(tab_ref, near_ref, win_ref):
    q = lax.broadcasted_iota(jnp.int32, (Q_BLOCK, NEAR_KEYS), 0)
    k = lax.broadcasted_iota(jnp.int32, (Q_BLOCK, NEAR_KEYS), 1)
    dist = NEAR_PAD + q - k
    near_ref[...] = _masked_bias(dist, dist >= 0, tab_ref)
    q = lax.broadcasted_iota(jnp.int32, (Q_BLOCK, WIN_KEYS), 0)
    k = lax.broadcasted_iota(jnp.int32, (Q_BLOCK, WIN_KEYS), 1)
    dist = WINDOW + q - k
    win_ref[...] = _masked_bias(dist, (dist >= 0) & (dist < WINDOW), tab_ref)


def _nsa_cmp_bias_kernel(tab_ref, o_ref):
    e = pl.program_id(0)
    q = lax.broadcasted_iota(jnp.int32, (Q_BLOCK, CMP_TILE), 0)
    n = lax.broadcasted_iota(jnp.int32, (Q_BLOCK, CMP_TILE), 1)
    dist = (e - 1) * Q_BLOCK + q - (n * CMP_STRIDE + CMP_LEN - 1)
    o_ref[0] = _masked_bias(dist, dist >= 0, tab_ref)


def nsa_bias_tiles(t5_table):
    H, Q = GROUP_HEADS, Q_BLOCK
    smem = pl.BlockSpec(memory_space=pltpu.SMEM)
    near, win = pl.pallas_call(
        _nsa_bias_kernel, grid=(1,), in_specs=[smem],
        out_specs=[pl.BlockSpec((H, Q, NEAR_KEYS), lambda i: (0, 0, 0)),
                   pl.BlockSpec((H, Q, WIN_KEYS), lambda i: (0, 0, 0))],
        out_shape=[jax.ShapeDtypeStruct((H, Q, NEAR_KEYS), F32), jax.ShapeDtypeStruct((H, Q, WIN_KEYS), F32)],
        name="nsa_bias",
    )(t5_table)
    cmp = pl.pallas_call(
        _nsa_cmp_bias_kernel, grid=(N_CMP_BIAS,), in_specs=[smem],
        out_specs=pl.BlockSpec((1, H, Q, CMP_TILE), lambda e: (e, 0, 0, 0)),
        out_shape=jax.ShapeDtypeStruct((N_CMP_BIAS, H, Q, CMP_TILE), F32),
        name="nsa_cmp_bias",
    )(t5_table)
    return near, win, cmp


def _nsa_attn_kernel(tab_ref, q_ref, g_ref, kc_ref, vc_ref, ks_ref, vs_ref, kw_ref, vw_ref,
                     bnear_ref, bwin_ref, bcmp_ref, o_ref, m_scr, acc_scr):
    H, Q, D = GROUP_HEADS, Q_BLOCK, HEAD_DIM
    n_cmp = kc_ref.shape[1]
    n_sel = (ks_ref.shape[1] - NEAR_PAD) // SEL_LEN
    i = pl.program_id(1)
    t0 = i * Q

    q = q_ref[0] * (D ** -0.5)
    q4 = jnp.concatenate([q[:, h * D:(h + 1) * D] for h in range(H)], axis=0).astype(BF16)

    s = _dot_nt(q4, kc_ref[0]).reshape(H, Q, n_cmp)
    tiles = []
    for g in range(n_cmp // CMP_TILE):
        e = jnp.clip(i - g * CMP_TILES_BACK, -1, CMP_TILES_BACK + 1) + 1
        tiles.append(s[:, :, g * CMP_TILE:(g + 1) * CMP_TILE] + bcmp_ref[e])
    s = jnp.concatenate(tiles, axis=-1)
    t_c = t0 + lax.broadcasted_iota(jnp.int32, (Q, n_cmp), 0)
    valid_c = t_c >= lax.broadcasted_iota(jnp.int32, (Q, n_cmp), 1) * CMP_STRIDE + CMP_LEN - 1
    p = jnp.where(valid_c, jnp.exp(s - jnp.max(s, axis=-1, keepdims=True)), 0.0)
    l = jnp.sum(p, axis=-1, keepdims=True)
    p = p * (1.0 / jnp.where(l > 0.0, l, 1.0))
    o_c = _dot(p.reshape(H * Q, n_cmp).astype(BF16), vc_ref[0])

    psum = p[0] + p[1] + p[2] + p[3]
    j_idx = lax.broadcasted_iota(jnp.int32, (n_sel, n_cmp), 0)
    c_idx = lax.broadcasted_iota(jnp.int32, (n_sel, n_cmp), 1)
    ratio = SEL_LEN // CMP_STRIDE
    n_in_sel = (SEL_LEN - CMP_LEN) // CMP_STRIDE + 1
    pool_t = ((c_idx // ratio == j_idx) & (c_idx % ratio < n_in_sel)).astype(BF16)
    p_hi = psum.astype(BF16)
    p_mid = (psum - p_hi.astype(F32)).astype(BF16)
    p_lo = (psum - p_hi.astype(F32) - p_mid.astype(F32)).astype(BF16)
    imp_t = _dot_nt(pool_t, p_hi) + _dot_nt(pool_t, p_mid) + _dot_nt(pool_t, p_lo)

    row0 = pl.multiple_of(t0, Q)
    kw = kw_ref[0, pl.ds(row0, WIN_KEYS), :]
    vw = vw_ref[0, pl.ds(row0, WIN_KEYS), :]
    in_seq = lax.broadcasted_iota(jnp.int32, (1, WIN_KEYS), 1) >= WINDOW - t0
    s = _dot_nt(q4, kw).reshape(H, Q, WIN_KEYS) + bwin_ref[...] + jnp.where(in_seq, 0.0, NEG)
    p_w = jnp.exp(s - jnp.max(s, axis=-1, keepdims=True))
    acc_w = _dot(p_w.reshape(H * Q, WIN_KEYS).astype(BF16), vw)
    o_w = acc_w[:, :D] * (1.0 / acc_w[:, D:D + 1])

    jb = lax.broadcasted_iota(jnp.int32, (n_sel, Q), 0)
    cur = (t0 + lax.broadcasted_iota(jnp.int32, (n_sel, Q), 1)) // SEL_LEN
    forced = (jb == 0) | (jb == cur) | (jb == cur - 1)
    work = jnp.where(jb <= cur, imp_t + jnp.where(forced, SEL_FORCE, 0.0), NEG)
    sel_t = jnp.zeros((n_sel, Q), F32)
    jb_f = jb.astype(F32)
    for _ in range(min(SEL_TOPN, n_sel)):
        mx = jnp.max(work, axis=0, keepdims=True)
        first = jnp.min(jnp.where(work == mx, jb_f, jnp.inf), axis=0, keepdims=True)
        pick = jb_f == first
        sel_t = jnp.where(pick & (mx > 0.5 * NEG), 1.0, sel_t)
        work = jnp.where(pick, -jnp.inf, work)
    q_eye = (lax.broadcasted_iota(jnp.int32, (Q, Q), 0) == lax.broadcasted_iota(jnp.int32, (Q, Q), 1)).astype(BF16)
    sel_bf = _dot_nt(q_eye, sel_t.astype(BF16)).astype(BF16)

    far_bias = jnp.stack([jnp.full((Q, 1), tab_ref[N_BUCKETS - 1, h], F32) for h in range(H)])
    m_scr[...] = jnp.full(m_scr.shape, NEG, F32)
    acc_scr[...] = jnp.zeros(acc_scr.shape, F32)

    def sel_scores(row0, n_keys, bias, first_key):
        s = _dot_nt(q4, ks_ref[0, pl.ds(row0, n_keys), :]).reshape(H, Q, n_keys) + bias
        pos = row0 - NEAR_PAD + lax.broadcasted_iota(jnp.int32, (n_sel, n_keys), 1)
        blk = jnp.where(pos >= first_key, pos // SEL_LEN, -1)
        expand = (lax.broadcasted_iota(jnp.int32, (n_sel, n_keys), 0) == blk).astype(BF16)
        mask = _dot(sel_bf, expand) > 0.5
        return jnp.where(mask, s, NEG), mask

    def sel_absorb(row0, n_keys, s, mask):
        m_old = m_scr[...].reshape(H, Q, 1)
        m_new = jnp.maximum(m_old, jnp.max(s, axis=-1, keepdims=True))
        p = jnp.where(mask, jnp.exp(s - m_new), 0.0).reshape(H * Q, n_keys)
        alpha = jnp.exp(m_old - m_new).reshape(H * Q, 1)
        acc_scr[...] = alpha * acc_scr[...] + _dot(p.astype(BF16), vs_ref[0, pl.ds(row0, n_keys), :])
        m_scr[...] = m_new.reshape(H * Q, 1)

    def sel_step(row0, n_keys, bias, first_key):
        sel_absorb(row0, n_keys, *sel_scores(row0, n_keys, bias, first_key))

    n_far = jnp.maximum(i - 1, 0) // (FAR_TILE // Q)

    def far_body(kp, carry):
        row_a = pl.multiple_of(NEAR_PAD + kp * (2 * FAR_TILE), Q)
        row_b = pl.multiple_of(row_a + FAR_TILE, Q)
        sa = sel_scores(row_a, FAR_TILE, far_bias, 0)
        sb = sel_scores(row_b, FAR_TILE, far_bias, 0)
        sel_absorb(row_a, FAR_TILE, *sa)
        sel_absorb(row_b, FAR_TILE, *sb)
        return carry

    lax.fori_loop(0, n_far // 2, far_body, 0)

    @pl.when(n_far % 2 == 1)
    def _():
        sel_step(pl.multiple_of(NEAR_PAD + (n_far - 1) * FAR_TILE, Q), FAR_TILE, far_bias, 0)
    sel_step(row0, NEAR_KEYS, bnear_ref[...], n_far * FAR_TILE)
    acc = acc_scr[...]
    o_s = acc[:, :D] * (1.0 / acc[:, D:D + 1])

    g = jax.nn.sigmoid(g_ref[0])
    outs = []
    for h in range(H):
        rows = slice(h * Q, (h + 1) * Q)
        outs.append(g[:, h:h + 1] * o_c[rows] + g[:, H + h:H + h + 1] * o_s[rows]
                    + g[:, 2 * H + h:2 * H + h + 1] * o_w[rows])
    o_ref[0] = jnp.concatenate(outs, axis=1)


def nsa_attention(t5_table, bias_tiles, q, gates, k_cmp, v_cmp, ks, vs_aug, kw, vw_aug):
    b, s, _ = q.shape
    assert s % (CMP_TILE * CMP_STRIDE) == 0 and s >= NEAR_KEYS
    bnear, bwin, bcmp = bias_tiles
    full = lambda a: pl.BlockSpec((1,) + a.shape[1:], lambda bi, i: (bi,) + (0,) * (a.ndim - 1))
    fixed = lambda a: pl.BlockSpec(a.shape, lambda bi, i: (0,) * a.ndim)
    return pl.pallas_call(
        _nsa_attn_kernel,
        grid=(b, s // Q_BLOCK),
        in_specs=[pl.BlockSpec(memory_space=pltpu.SMEM),
                  pl.BlockSpec((1, Q_BLOCK, GROUP_W), lambda bi, i: (bi, i, 0)),
                  pl.BlockSpec((1, Q_BLOCK, 3 * GROUP_HEADS), lambda bi, i: (bi, i, 0)),
                  full(k_cmp), full(v_cmp), full(ks), full(vs_aug), full(kw), full(vw_aug),
                  fixed(bnear), fixed(bwin), fixed(bcmp)],
        out_specs=pl.BlockSpec((1, Q_BLOCK, GROUP_W), lambda bi, i: (bi, i, 0)),
        out_shape=jax.ShapeDtypeStruct((b, s, GROUP_W), F32),
        scratch_shapes=[pltpu.VMEM((GROUP_HEADS * Q_BLOCK, 1), F32),
                        pltpu.VMEM((GROUP_HEADS * Q_BLOCK, 2 * HEAD_DIM), F32)],
        compiler_params=pltpu.CompilerParams(dimension_semantics=("arbitrary", "arbitrary"),
                                             vmem_limit_bytes=VMEM_LIMIT),
        name="nsa_attention",
    )(t5_table, q, gates, k_cmp, v_cmp, ks, vs_aug, kw, vw_aug, bnear, bwin, bcmp)


def _front_pad(x, rows):
    return jnp.pad(x, ((0, 0), (rows, 0), (0, 0)))


def nsa_group(q, kc, vc, ks, vs_aug, kw, vw_aug, gates, w_cmp_k, w_cmp_v, cmp_pos, t5_table, bias_tiles):
    b, s, _ = q.shape
    d = HEAD_DIM
    n = s // CMP_STRIDE
    x = jnp.stack([kc, vc]).reshape(2, b, n, CMP_STRIDE * d)
    pos = cmp_pos.reshape(1, 2, CMP_STRIDE * d)
    w = jnp.stack([w_cmp_k, w_cmp_v]).reshape(2, 2, CMP_STRIDE * d, d).astype(BF16)
    cmp = nsa_compress(x, pos, w)
    return nsa_attention(t5_table, bias_tiles, q, gates, cmp[0], cmp[1],
                         _front_pad(ks, NEAR_PAD), _front_pad(vs_aug, NEAR_PAD),
                         _front_pad(kw, WINDOW), _front_pad(vw_aug, WINDOW))


PEER_HEADS = 8
PEER_DK = 256
N_KEYS = 128
PEER_TOPK = 16
PEER_SLOTS = PEER_HEADS * PEER_TOPK
ROUTE_TILE = 128
GATHER_TILE = 8


def _top_rows(work, order, n_top, payload=None):
    vals, args = [], []
    order_f = order.astype(F32)
    for _ in range(n_top):
        mx = jnp.max(work, axis=0, keepdims=True)
        first = jnp.min(jnp.where(work == mx, order_f, jnp.inf), axis=0, keepdims=True)
        pick = order_f == first
        vals.append(mx)
        if payload is None:
            args.append(first.astype(jnp.int32))
        else:
            args.append(jnp.sum(jnp.where(pick, payload, 0), axis=0, keepdims=True))
        work = jnp.where(pick, -jnp.inf, work)
    return jnp.concatenate(vals, axis=0), jnp.concatenate(args, axis=0)


def _pair_rows(s1, s2, combine):
    k = PEER_TOPK
    rows = [combine(s1[0:1], s2)]
    for a in range(1, k // 2):
        rows.append(combine(s1[a:a + 1], s2[0:k // 2]))
    rows.append(combine(s1[k // 2:], s2[0:1]))
    return jnp.concatenate(rows, axis=0)


def _peer_route_kernel(h_ref, g_ref, wq_ref, keys_ref, xn_ref, eidx_ref, gate_ref):
    x = h_ref[...]
    xn = x * lax.rsqrt(jnp.mean(x * x, axis=-1, keepdims=True) + NORM_EPS) * g_ref[...]
    xn_ref[...] = xn
    q = _dot(xn.astype(BF16), wq_ref[...]).astype(BF16)
    tq = x.shape[0]
    half = PEER_DK // 2
    n_iota = lax.broadcasted_iota(jnp.int32, (N_KEYS, tq), 0)
    k_iota = lax.broadcasted_iota(jnp.int32, (PEER_TOPK, tq), 0)
    order = _pair_rows(k_iota, k_iota, lambda a, b: a * PEER_TOPK + b)
    reachable = (order // PEER_TOPK + 1) * (order % PEER_TOPK + 1) <= PEER_TOPK
    for p in range(PEER_HEADS):
        tops = []
        for c in range(2):
            g = 2 * p + c
            sc = _dot_nt(keys_ref[g], q[:, g * half:(g + 1) * half])
            tops.append(_top_rows(sc, n_iota, PEER_TOPK))
        (s1, i1), (s2, i2) = tops
        cand = jnp.where(reachable, _pair_rows(s1, s2, lambda a, b: a + b), -jnp.inf)
        cidx = _pair_rows(i1, i2, lambda a, b: a * N_KEYS + b)
        top, eidx = _top_rows(cand, order, PEER_TOPK, payload=cidx)
        e = jnp.exp(top - top[0:1])
        eidx_ref[p * PEER_TOPK:(p + 1) * PEER_TOPK, :] = eidx
        gate_ref[p * PEER_TOPK:(p + 1) * PEER_TOPK, :] = e * (1.0 / jnp.sum(e, axis=0, keepdims=True))


def peer_route(h, g, w_q, keys):
    t, d = h.shape
    assert t % ROUTE_TILE == 0
    nq = w_q.shape[1]
    return pl.pallas_call(
        _peer_route_kernel,
        grid=(t // ROUTE_TILE,),
        in_specs=[pl.BlockSpec((ROUTE_TILE, d), lambda i: (i, 0)),
                  pl.BlockSpec((1, d), lambda i: (0, 0)),
                  pl.BlockSpec((d, nq), lambda i: (0, 0)),
                  pl.BlockSpec(keys.shape, lambda i: (0, 0, 0))],
        out_specs=[pl.BlockSpec((ROUTE_TILE, d), lambda i: (i, 0)),
                   pl.BlockSpec((PEER_SLOTS, ROUTE_TILE), lambda i: (0, i)),
                   pl.BlockSpec((PEER_SLOTS, ROUTE_TILE), lambda i: (0, i))],
        out_shape=[jax.ShapeDtypeStruct((t, d), F32),
                   jax.ShapeDtypeStruct((PEER_SLOTS, t), jnp.int32),
                   jax.ShapeDtypeStruct((PEER_SLOTS, t), F32)],
        compiler_params=pltpu.CompilerParams(dimension_semantics=("arbitrary",),
                                             vmem_limit_bytes=VMEM_LIMIT),
        name="peer_route",
    )(h, g.reshape(1, d), w_q, keys)


SUBLANES = 8
SLAB = 2 * SUBLANES
ISSUE_UNROLL = 8


def _gelu(x):
    return 0.5 * x * (1.0 + lax.erf(x * (2.0 ** -0.5)))


BITREV8 = (0, 4, 2, 6, 1, 5, 3, 7)


def _fold_pair(a, b, half, sub):
    low = (sub % (2 * half)) < half
    partner = jnp.where(low, pltpu.roll(a, SUBLANES - half, axis=0), pltpu.roll(b, half, axis=0))
    return jnp.where(low, a, b) + partner


GATHER_DEPTH = 3
DMA_QUEUES = 2


def _peer_gather_kernel(idx0_ref, idx1_ref, idx2_ref, xn_ref, h_ref, gate_ref, uv_hbm, o_ref,
                        buf_a, buf_b, buf_c, w_scr, sem):
    i = pl.program_id(0)
    n = pl.num_programs(0)
    tt = xn_ref.shape[0]
    n_slabs = tt * PEER_SLOTS
    bufs = (buf_a, buf_b, buf_c)

    def request(ref, k, r, queue):
        src = pl.multiple_of(ref[0, 0, r] * SLAB, SLAB)
        pltpu.make_async_copy(uv_hbm.at[pl.ds(src, SLAB)], bufs[k].at[pl.ds(r * SLAB, SLAB)],
                              sem.at[k]).start(priority=queue)

    def request_tile(ref, k):
        def body(c, carry):
            for j in range(ISSUE_UNROLL):
                request(ref, k, pl.multiple_of(c * ISSUE_UNROLL, ISSUE_UNROLL) + j, j % DMA_QUEUES)
            return carry
        lax.fori_loop(0, n_slabs // ISSUE_UNROLL, body, 0)

    def wait_tile(k):
        pltpu.make_async_copy(uv_hbm.at[pl.ds(0, n_slabs * SLAB)], bufs[k], sem.at[k]).wait()

    @pl.when(i == 0)
    def _():
        request_tile(idx0_ref, 0)
        request_tile(idx1_ref, 1)

    def step(cur, ahead):
        wait_tile(cur)
        gates = gate_ref[0]
        sub = lax.broadcasted_iota(jnp.int32, (SUBLANES, LANE), 0)
        for t in range(tt):
            for j in range(PEER_SLOTS):
                request(idx2_ref, ahead, t * PEER_SLOTS + j, j % DMA_QUEUES)
            x = xn_ref[t]

            def slab(pos, half):
                return bufs[cur][pl.ds((t * PEER_SLOTS + pos) * SLAB + half * SUBLANES, SUBLANES), :]

            groups = []
            for g in range(PEER_SLOTS // SUBLANES):
                p = [slab(g * SUBLANES + k, 0) * x for k in range(SUBLANES)]
                c = [_fold_pair(p[2 * k], p[2 * k + 1], 4, sub) for k in range(4)]
                d = [_fold_pair(c[0], c[1], 2, sub), _fold_pair(c[2], c[3], 2, sub)]
                groups.append(_fold_pair(d[0], d[1], 1, sub))
            act = jnp.sum(jnp.concatenate(groups, axis=0), axis=-1, keepdims=True)
            w_scr[...] = jnp.broadcast_to(gates[:, t:t + 1] * _gelu(act), (PEER_SLOTS, LANE))
            acc = h_ref[t]
            for g in range(PEER_SLOTS // SUBLANES):
                for k in range(SUBLANES):
                    w_row = w_scr[pl.ds(g * SUBLANES + BITREV8[k], 1), :]
                    acc = acc + slab(g * SUBLANES + k, 1) * w_row
            o_ref[t] = acc

    for k in range(GATHER_DEPTH):
        @pl.when(i % GATHER_DEPTH == k)
        def _(k=k):
            step(k, (k + GATHER_DEPTH - 1) % GATHER_DEPTH)

    @pl.when(i == n - 1)
    def _():
        for k in range(GATHER_DEPTH):
            @pl.when(i % GATHER_DEPTH == k)
            def _(k=k):
                wait_tile((k + 1) % GATHER_DEPTH)
                wait_tile((k + 2) % GATHER_DEPTH)


def peer_gather(idx, xn, h, gate_b, uv):
    t = xn.shape[0]
    nt = t // GATHER_TILE
    assert t % GATHER_TILE == 0 and nt >= GATHER_DEPTH
    n_slabs = GATHER_TILE * PEER_SLOTS
    tok = pl.BlockSpec((GATHER_TILE, SUBLANES, LANE), lambda i: (i, 0, 0))
    idx_spec = lambda k: pl.BlockSpec((1, 1, n_slabs), lambda i: (jnp.minimum(i + k, nt - 1), 0, 0),
                                      memory_space=pltpu.SMEM)
    slab_buf = pltpu.VMEM((n_slabs * SLAB, LANE), F32)
    return pl.pallas_call(
        _peer_gather_kernel,
        grid=(nt,),
        in_specs=[idx_spec(0), idx_spec(1), idx_spec(2), tok, tok,
                  pl.BlockSpec((1, PEER_SLOTS, GATHER_TILE), lambda i: (i, 0, 0)),
                  pl.BlockSpec(memory_space=pl.ANY)],
        out_specs=tok,
        out_shape=jax.ShapeDtypeStruct((t, SUBLANES, LANE), F32),
        scratch_shapes=[slab_buf, slab_buf, slab_buf, pltpu.VMEM((PEER_SLOTS, LANE), F32),
                        pltpu.SemaphoreType.DMA((GATHER_DEPTH,))],
        compiler_params=pltpu.CompilerParams(dimension_semantics=("arbitrary",),
                                             vmem_limit_bytes=VMEM_LIMIT),
        name="peer_gather",
    )(idx, idx, idx, xn, h, gate_b, uv)


def peer_ffn_residual(h, g, w_q, sub_keys, u_tab, v_tab):
    t, d = h.shape
    e = u_tab.shape[0]
    keys = sub_keys.reshape(PEER_HEADS * 2, N_KEYS, PEER_DK // 2).astype(BF16)
    xn, eidx_t, gate_t = peer_route(h, g, w_q.astype(BF16), keys)
    nt = t // GATHER_TILE
    order = np.arange(PEER_SLOTS).reshape(-1, SUBLANES)[:, list(BITREV8)].reshape(-1)
    idx = eidx_t[order].T.reshape(nt, 1, GATHER_TILE * PEER_SLOTS)
    gate_b = gate_t.reshape(PEER_SLOTS, nt, GATHER_TILE).transpose(1, 0, 2)
    uv = jnp.concatenate([u_tab.reshape(e, SUBLANES, LANE), v_tab.reshape(e, SUBLANES, LANE)],
                         axis=1).reshape(e * SLAB, LANE)
    out = peer_gather(idx, xn.reshape(t, SUBLANES, LANE), h.reshape(t, SUBLANES, LANE), gate_b, uv)
    return out.reshape(t, d)


Q_LORA = 256
KV_LORA = 128
QK_NOPE = 64
QK_ROPE = 32
V_HEAD = 64
ROPE_THETA = 10000.0
MLA_TQ = 256
MLA_TK = 1024


def _rms(x, g):
    return x * lax.rsqrt(jnp.mean(x * x, axis=-1, keepdims=True) + NORM_EPS) * g


def _mla_prep_kernel(pos_ref, freq_ref, cq_ref, ckv_ref, kr_ref, gq_ref, gkv_ref, wq_ref, wk_ref, wv_ref,
                     q_out, k_out, v_out):
    rows = cq_ref.shape[0]
    half = QK_ROPE // 2
    cq = _rms(cq_ref[...], gq_ref[...]).astype(BF16)
    ckv = _rms(ckv_ref[...], gkv_ref[...]).astype(BF16)
    qf = _dot(cq, wq_ref[...])
    kf = _dot(ckv, wk_ref[...])
    vf = _dot(ckv, wv_ref[...])
    lane = lax.broadcasted_iota(jnp.int32, (rows, LANE), 1)
    ang = pos_ref[...].astype(F32) * freq_ref[...]
    cos, sin = jnp.cos(ang), jnp.sin(ang)
    in_rope = (lane >= QK_NOPE) & (lane < QK_NOPE + QK_ROPE)
    first = lane < QK_NOPE + half
    c_mul = jnp.where(lane < QK_NOPE, 1.0, jnp.where(in_rope, cos, 0.0))
    s_mul = jnp.where(in_rope, jnp.where(first, -sin, sin), 0.0)

    def rope(x):
        partner = jnp.where(first, pltpu.roll(x, LANE - half, axis=1), pltpu.roll(x, half, axis=1))
        return x * c_mul + partner * s_mul

    scale = (QK_NOPE + QK_ROPE) ** -0.5
    kr = rope(jnp.concatenate([jnp.zeros((rows, QK_NOPE), F32), kr_ref[...],
                               jnp.zeros((rows, LANE - QK_NOPE - QK_ROPE), F32)], axis=1))
    ones_col = jnp.where(lane == V_HEAD, 1.0, 0.0)
    for h in range(GROUP_HEADS):
        cols = slice(h * LANE, (h + 1) * LANE)
        q_out[:, cols] = (rope(qf[:, cols]) * scale).astype(BF16)
        k_out[:, cols] = (kf[:, cols] + kr).astype(BF16)
        v_out[:, cols] = (vf[:, cols] + ones_col).astype(BF16)


def mla_prep(pos, c_q, c_kv, k_r, gq, gkv, w_uq, w_ukv):
    t = c_q.shape[0]
    H = GROUP_HEADS
    wq = jnp.pad(w_uq.reshape(Q_LORA, H, QK_NOPE + QK_ROPE), ((0, 0), (0, 0), (0, LANE - QK_NOPE - QK_ROPE)))
    wkv = w_ukv.reshape(KV_LORA, H, QK_NOPE + V_HEAD)
    wk = jnp.pad(wkv[:, :, :QK_NOPE], ((0, 0), (0, 0), (0, LANE - QK_NOPE)))
    wv = jnp.pad(wkv[:, :, QK_NOPE:], ((0, 0), (0, 0), (0, LANE - V_HEAD)))
    wq, wk, wv = [w.reshape(w.shape[0], H * LANE).astype(BF16) for w in (wq, wk, wv)]
    inv_freq = ROPE_THETA ** (-np.arange(0, QK_ROPE, 2, dtype=np.float32) / QK_ROPE)
    freq = np.zeros((1, LANE), np.float32)
    freq[0, QK_NOPE:QK_NOPE + QK_ROPE] = np.tile(inv_freq, 2)
    row = lambda n: pl.BlockSpec((ROW_TILE, n), lambda i: (i, 0))
    fixed = lambda a: pl.BlockSpec(a.shape, lambda i: (0,) * a.ndim)
    gq2, gkv2, freq = gq.reshape(1, -1), gkv.reshape(1, -1), jnp.asarray(freq)
    out = jax.ShapeDtypeStruct((t, H * LANE), BF16)
    return pl.pallas_call(
        _mla_prep_kernel,
        grid=(t // ROW_TILE,),
        in_specs=[row(1), fixed(freq), row(Q_LORA), row(KV_LORA), row(QK_ROPE), fixed(gq2), fixed(gkv2),
                  fixed(wq), fixed(wk), fixed(wv)],
        out_specs=[row(H * LANE)] * 3,
        out_shape=[out] * 3,
        compiler_params=pltpu.CompilerParams(dimension_semantics=("arbitrary",)),
        name="mla_prep",
    )(pos, freq, c_q, c_kv, k_r, gq2, gkv2, wq, wk, wv)


def _mla_attn_kernel(q_ref, k_ref, v_ref, o_ref, m_scr, acc_scr):
    i = pl.program_id(1)
    tq = q_ref.shape[1]
    row = i * tq + lax.broadcasted_iota(jnp.int32, (tq, MLA_TK), 0)
    col = lax.broadcasted_iota(jnp.int32, (tq, MLA_TK), 1)
    n_full = (i * tq) // MLA_TK
    heads = range(GROUP_HEADS)
    cols = [slice(h * LANE, (h + 1) * LANE) for h in heads]
    q = [q_ref[0, :, cols[h]] for h in heads]
    m_scr[...] = jnp.full(m_scr.shape, NEG, F32)
    acc_scr[...] = jnp.zeros(acc_scr.shape, F32)

    def step(j, masked):
        start = pl.multiple_of(j * MLA_TK, MLA_TK)

        def qk(h):
            s = _dot_nt(q[h], k_ref[0, pl.ds(start, MLA_TK), cols[h]])
            return jnp.where(start + col <= row, s, NEG) if masked else s

        def absorb(h, s):
            m_old = m_scr[h]
            m_new = jnp.maximum(m_old, jnp.max(s, axis=-1, keepdims=True))
            p = jnp.exp(s - m_new).astype(BF16)
            acc_scr[h] = jnp.exp(m_old - m_new) * acc_scr[h] + _dot(p, v_ref[0, pl.ds(start, MLA_TK), cols[h]])
            m_scr[h] = m_new

        s_prev = qk(0)
        for h in range(1, GROUP_HEADS):
            s_next = qk(h)
            absorb(h - 1, s_prev)
            s_prev = s_next
        absorb(GROUP_HEADS - 1, s_prev)

    def body(j, carry):
        step(j, False)
        return carry

    lax.fori_loop(0, n_full, body, 0)
    step(n_full, True)
    outs = [acc_scr[h][:, :V_HEAD] * (1.0 / acc_scr[h][:, V_HEAD:V_HEAD + 1]) for h in heads]
    o_ref[0] = jnp.concatenate(outs, axis=1)


def mla_attention(q, k, v):
    b, s, w = q.shape
    assert s % MLA_TK == 0 and MLA_TK % MLA_TQ == 0
    return pl.pallas_call(
        _mla_attn_kernel,
        grid=(b, s // MLA_TQ),
        in_specs=[pl.BlockSpec((1, MLA_TQ, w), lambda bi, i: (bi, i, 0)),
                  pl.BlockSpec((1, s, w), lambda bi, i: (bi, 0, 0)),
                  pl.BlockSpec((1, s, w), lambda bi, i: (bi, 0, 0))],
        out_specs=pl.BlockSpec((1, MLA_TQ, GROUP_HEADS * V_HEAD), lambda bi, i: (bi, i, 0)),
        out_shape=jax.ShapeDtypeStruct((b, s, GROUP_HEADS * V_HEAD), F32),
        scratch_shapes=[pltpu.VMEM((GROUP_HEADS, MLA_TQ, 1), F32), pltpu.VMEM((GROUP_HEADS, MLA_TQ, LANE), F32)],
        compiler_params=pltpu.CompilerParams(dimension_semantics=("arbitrary", "arbitrary"),
                                             vmem_limit_bytes=VMEM_LIMIT),
        name="mla_attention",
    )(q, k, v)


def mla_group(c_q, c_kv, k_r, positions, q_norm_g, kv_norm_g, w_uq, w_ukv):
    b, s = positions.shape
    t = b * s
    q, k, v = mla_prep(positions.reshape(t, 1), c_q, c_kv, k_r, q_norm_g, kv_norm_g, w_uq, w_ukv)
    w = GROUP_HEADS * LANE
    return mla_attention(q.reshape(b, s, w), k.reshape(b, s, w), v.reshape(b, s, w))


CHUNK = 64
CONV_W = 4


def _split3(a):
    hi = a.astype(BF16)
    r = a - hi.astype(F32)
    mid = r.astype(BF16)
    return hi, mid, (r - mid.astype(F32)).astype(BF16)


def _dot_exact_rhs(sel, a):
    return sum(_dot(sel, p) for p in _split3(a))


def _row_matrix(col, n_rows):
    l = col.shape[0]
    lane = lax.broadcasted_iota(jnp.int32, (l, SUBLANES), 1)
    padded = jnp.where(lane == 0, col, 0.0)
    ones = jnp.ones((n_rows, SUBLANES), BF16)
    return sum(_dot_nt(ones, p) for p in _split3(padded))


def _transpose_bf16(x_bf):
    d = x_bf.shape[1]
    eye = (lax.broadcasted_iota(jnp.int32, (d, d), 0) == lax.broadcasted_iota(jnp.int32, (d, d), 1)).astype(BF16)
    return _dot_nt(eye, x_bf).astype(BF16)


def _log_sigmoid(x):
    return jnp.minimum(x, 0.0) - jnp.log1p(jnp.exp(-jnp.abs(x)))


def _softplus(x):
    return jnp.maximum(x, 0.0) + jnp.log1p(jnp.exp(-jnp.abs(x)))


def _with_ones_col(v):
    l, d = v.shape
    lane = lax.broadcasted_iota(jnp.int32, (l, LANE - d), 1)
    return jnp.concatenate([v, jnp.where(lane == 0, 1.0, 0.0)], axis=1)


def _bf(x):
    return x.astype(BF16)


def _mlstm_kernel(q_ref, k_ref, v_ref, gate_ref, og_ref, bias_ref, ng_ref, o_ref, st_scr, m_scr):
    nb = q_ref.shape[0]
    H, D, L = GROUP_HEADS, HEAD_DIM, CHUNK
    chains = [(b, h) for b in range(nb) for h in range(H)]
    cols = lambda h: slice(h * D, (h + 1) * D)
    each = lambda fn: [fn(c, b, h) for c, (b, h) in enumerate(chains)]

    @pl.when(pl.program_id(0) == 0)
    def _():
        st_scr[...] = jnp.zeros(st_scr.shape, F32)
        m_scr[...] = jnp.zeros(m_scr.shape, F32)

    row = lax.broadcasted_iota(jnp.int32, (L, L), 0)
    col = lax.broadcasted_iota(jnp.int32, (L, L), 1)
    tri = row >= col
    tri_bf = tri.astype(BF16)
    gates = [gate_ref[b] + bias_ref[...] for b in range(nb)]
    bcum = [_dot_exact_rhs(tri_bf, _log_sigmoid(g[:, H:])) for g in gates]
    q = each(lambda c, b, h: _bf(q_ref[b, :, cols(h)]))
    kt = each(lambda c, b, h: _transpose_bf16(_bf(k_ref[b, :, cols(h)] * (D ** -0.5))))
    v_aug = each(lambda c, b, h: _with_ones_col(v_ref[b, :, cols(h)]))
    bc = each(lambda c, b, h: bcum[b][:, h:h + 1])
    ic = each(lambda c, b, h: gates[b][:, h:h + 1])
    m_st = each(lambda c, b, h: m_scr[c, 0:1, 0:1])
    st = each(lambda c, b, h: st_scr[c])
    log_d = each(lambda c, b, h: jnp.where(tri, bc[c] + _row_matrix(ic[c] - bc[c], L), -jnp.inf))
    m_t = each(lambda c, b, h: jnp.maximum(jnp.max(log_d[c], axis=-1, keepdims=True), bc[c] + m_st[c]))
    sqk = each(lambda c, b, h: _dot(q[c], kt[c]) * jnp.exp(log_d[c] - m_t[c]))
    inter = each(lambda c, b, h: jnp.exp(bc[c] + m_st[c] - m_t[c]))
    tot = each(lambda c, b, h: _dot(_bf(sqk[c]), _bf(v_aug[c])) + inter[c] * _dot(q[c], _bf(st[c])))
    h_t = each(lambda c, b, h: tot[c][:, :D] * (1.0 / jnp.maximum(jnp.abs(tot[c][:, D:D + 1]), jnp.exp(-m_t[c]))))
    b_end = each(lambda c, b, h: bc[c][L - 1:L])
    log_w = each(lambda c, b, h: b_end[c] - bc[c] + ic[c])
    m_new = each(lambda c, b, h: jnp.maximum(b_end[c] + m_st[c], jnp.max(log_w[c], axis=0, keepdims=True)))
    upd = each(lambda c, b, h: _dot(kt[c], _bf(jnp.exp(log_w[c] - m_new[c]) * v_aug[c])))
    for c in range(len(chains)):
        st_scr[c] = jnp.exp(b_end[c] + m_st[c] - m_new[c]) * st[c] + upd[c]
        m_scr[c] = jnp.broadcast_to(m_new[c], m_scr.shape[1:])
    for b in range(nb):
        outs = []
        for h in range(H):
            o = jax.nn.sigmoid(og_ref[b, :, cols(h)]) * h_t[b * H + h]
            outs.append(o * lax.rsqrt(jnp.mean(o * o, axis=-1, keepdims=True) + NORM_EPS) * ng_ref[:, cols(h)])
        o_ref[b] = jnp.concatenate(outs, axis=1)


def mlstm_group(q, k, v, gates, og, i_bias, f_bias, norm_g):
    b, s, _ = q.shape
    assert s % CHUNK == 0
    W, H = GROUP_W, GROUP_HEADS
    bias = jnp.concatenate([i_bias, f_bias]).reshape(1, 2 * H)
    tok = lambda n: pl.BlockSpec((b, CHUNK, n), lambda i: (0, i, 0))
    fixed = lambda n: pl.BlockSpec((1, n), lambda i: (0, 0))
    return pl.pallas_call(
        _mlstm_kernel,
        grid=(s // CHUNK,),
        in_specs=[tok(W), tok(W), tok(W), tok(2 * H), tok(W), fixed(2 * H), fixed(W)],
        out_specs=tok(W),
        out_shape=jax.ShapeDtypeStruct((b, s, W), F32),
        scratch_shapes=[pltpu.VMEM((b * H, HEAD_DIM, LANE), F32), pltpu.VMEM((b * H, SUBLANES, LANE), F32)],
        compiler_params=pltpu.CompilerParams(dimension_semantics=("arbitrary",)),
        name="mlstm",
    )(q, k, v, gates, og, bias, norm_g.reshape(1, W))


def _gdn_kernel(x_ref, xprev_ref, cw_ref, gate_ref, z_ref, par_ref, ng_ref, o_ref, st_scr):
    nb = x_ref.shape[0]
    H, D, L = GROUP_HEADS, HEAD_DIM, CHUNK
    W = H * D
    chains = [(b, h) for b in range(nb) for h in range(H)]
    each = lambda fn: [fn(c, b, h) for c, (b, h) in enumerate(chains)]
    first = pl.program_id(0) == 0

    @pl.when(first)
    def _():
        st_scr[...] = jnp.zeros(st_scr.shape, F32)

    row = lax.broadcasted_iota(jnp.int32, (L, L), 0)
    col = lax.broadcasted_iota(jnp.int32, (L, L), 1)
    tri = row >= col
    tri_bf = tri.astype(BF16)
    eye = (row == col).astype(F32)
    par = par_ref[...]
    qkv, beta4, gc4 = [], [], []
    for b in range(nb):
        prev = jnp.where(first, 0.0, xprev_ref[b])
        full = jnp.concatenate([prev, x_ref[b]], axis=0)
        acc = full[SUBLANES:] * cw_ref[CONV_W - 1:CONV_W, :]
        for j in range(CONV_W - 1):
            shifted = pltpu.roll(full, CONV_W - 1 - j, axis=0)[SUBLANES:]
            acc = acc + shifted * cw_ref[j:j + 1, :]
        qkv.append(acc * jax.nn.sigmoid(acc))
        gates = gate_ref[b]
        beta4.append(jax.nn.sigmoid(gates[:, :H]))
        gc4.append(_dot_exact_rhs(tri_bf, -jnp.exp(par[:, :H]) * _softplus(gates[:, H:] + par[:, H:])))

    def l2n(t):
        return t * lax.rsqrt(jnp.sum(t * t, axis=-1, keepdims=True) + NORM_EPS)

    q = each(lambda c, b, h: l2n(qkv[b][:, h * D:(h + 1) * D]) * (D ** -0.5))
    k = each(lambda c, b, h: l2n(qkv[b][:, W + h * D:W + (h + 1) * D]))
    v = each(lambda c, b, h: qkv[b][:, 2 * W + h * D:2 * W + (h + 1) * D])
    beta = each(lambda c, b, h: beta4[b][:, h:h + 1])
    gc = each(lambda c, b, h: gc4[b][:, h:h + 1])
    gc_row = each(lambda c, b, h: _row_matrix(gc[c], L))
    decay = each(lambda c, b, h: jnp.where(tri, jnp.exp(gc[c] - gc_row[c]), 0.0))
    kt = each(lambda c, b, h: _transpose_bf16(_bf(k[c])))
    kb = each(lambda c, b, h: k[c] * beta[c])
    x = each(lambda c, b, h: -jnp.where(row > col, _dot(_bf(kb[c]), kt[c]) * decay[c], 0.0))
    t_inv = [eye + xc for xc in x]
    for _ in range(5):
        x = [_dot(_bf(xc), _bf(xc)) for xc in x]
        t_inv = [tc + _dot(_bf(tc), _bf(xc)) for tc, xc in zip(t_inv, x)]
    u = each(lambda c, b, h: _dot(_bf(t_inv[c]), _bf(v[c] * beta[c])))
    wm = each(lambda c, b, h: _dot(_bf(t_inv[c]), _bf(kb[c] * jnp.exp(gc[c]))))
    attn = each(lambda c, b, h: _dot(_bf(q[c]), kt[c]) * decay[c])
    st = each(lambda c, b, h: st_scr[c])
    v_new = each(lambda c, b, h: u[c] - _dot(_bf(wm[c]), _bf(st[c])))
    o = each(lambda c, b, h: _dot(_bf(q[c] * jnp.exp(gc[c])), _bf(st[c])) + _dot(_bf(attn[c]), _bf(v_new[c])))
    g_end = each(lambda c, b, h: gc[c][L - 1:L])
    upd = each(lambda c, b, h: _dot(_bf(kt[c].astype(F32) * jnp.exp(g_end[c] - gc_row[c])), _bf(v_new[c])))
    for c in range(len(chains)):
        st_scr[c] = st[c] * jnp.exp(g_end[c]) + upd[c]
    for b in range(nb):
        outs = []
        for h in range(H):
            oc = o[b * H + h]
            z = z_ref[b, :, h * D:(h + 1) * D]
            y = oc * lax.rsqrt(jnp.mean(oc * oc, axis=-1, keepdims=True) + NORM_EPS) * ng_ref[...]
            outs.append(y * (z * jax.nn.sigmoid(z)))
        o_ref[b] = jnp.concatenate(outs, axis=1)


def gdn_group(x, gates, z, conv_w, a_log, dt_bias, norm_g):
    b, s, _ = x.shape
    assert s % CHUNK == 0
    W, H = GROUP_W, GROUP_HEADS
    par = jnp.concatenate([a_log, dt_bias]).reshape(1, 2 * H)
    tok = lambda n: pl.BlockSpec((b, CHUNK, n), lambda i: (0, i, 0))
    per_chunk = CHUNK // SUBLANES
    prev = pl.BlockSpec((b, SUBLANES, 3 * W), lambda i: (0, jnp.maximum(i * per_chunk - 1, 0), 0))
    fixed = lambda r, n: pl.BlockSpec((r, n), lambda i: (0, 0))
    return pl.pallas_call(
        _gdn_kernel,
        grid=(s // CHUNK,),
        in_specs=[tok(3 * W), prev, fixed(CONV_W, 3 * W), tok(2 * H), tok(W), fixed(1, 2 * H), fixed(1, HEAD_DIM)],
        out_specs=tok(W),
        out_shape=jax.ShapeDtypeStruct((b, s, W), F32),
        scratch_shapes=[pltpu.VMEM((b * H, HEAD_DIM, HEAD_DIM), F32)],
        compiler_params=pltpu.CompilerParams(dimension_semantics=("arbitrary",)),
        name="gdn",
    )(x, x, conv_w, gates, z, par, norm_g.reshape(1, HEAD_DIM))


NSA_IN = GROUP_W + 6 * HEAD_DIM + 3 * GROUP_HEADS
MLSTM_IN = 4 * GROUP_W + 2 * GROUP_HEADS
GDN_IN = 4 * GROUP_W + 2 * GROUP_HEADS
MLA_IN = Q_LORA + KV_LORA + QK_ROPE


def kernel(x, positions, t5_table, norm1_g, norm2_g, final_norm_g, w_in, w_out, nsa_w_cmp_k, nsa_w_cmp_v, nsa_cmp_pos, mlstm_i_bias, mlstm_f_bias, mlstm_norm_g, gdn_conv_w, gdn_a_log, gdn_dt_bias, gdn_norm_g, mla_q_norm_g, mla_kv_norm_g, mla_w_uq, mla_w_ukv, peer_w_q, peer_sub_keys, peer_u, peer_v):
    b, s, d = x.shape
    t = b * s
    h = x.reshape(t, d)
    depth = w_in.shape[0]
    W, D, H = GROUP_W, HEAD_DIM, GROUP_HEADS
    bias_tiles = nsa_bias_tiles(t5_table)
    c_ml, c_gdn, c_mla = NSA_IN, NSA_IN + MLSTM_IN, NSA_IN + MLSTM_IN + GDN_IN
    small = [(W + 6 * D, 3 * H), (c_ml + 3 * W, 2 * H), (c_gdn + 3 * W, 2 * H),
             (c_mla + Q_LORA + KV_LORA, QK_ROPE)]
    groups = [(0, W, W, F32, None),
              (W, D, D, F32, None), (W + D, D, D, F32, None),
              (W + 2 * D, D, D, BF16, None), (W + 3 * D, D, 2 * D, BF16, D),
              (W + 4 * D, D, D, BF16, None), (W + 5 * D, D, 2 * D, BF16, D),
              (c_ml, W, W, F32, None), (c_ml + W, W, W, F32, None), (c_ml + 2 * W, W, W, F32, None),
              (c_ml + 3 * W + 2 * H, W, W, F32, None),
              (c_gdn, 3 * W, 3 * W, F32, None), (c_gdn + 3 * W + 2 * H, W, W, F32, None),
              (c_mla, Q_LORA, Q_LORA, F32, None), (c_mla + Q_LORA, KV_LORA, KV_LORA, F32, None)]
    for l in range(depth):
        w_l = w_in[l].astype(BF16)
        weights = [jnp.pad(w_l[:, c:c + n], ((0, 0), (0, wide - n))) for c, n, wide, _, _ in groups]
        w_small = jnp.concatenate([w_l[:, c:c + n] for c, n in small], axis=1)
        weights.append(jnp.pad(w_small, ((0, 0), (0, LANE - w_small.shape[1]))))
        outs = norm_matmul_split(h, norm1_g[l], weights, [g[3] for g in groups] + [F32],
                                 [g[4] for g in groups] + [None])
        (nsa_q, kc, vc, ks, vs_aug, kw, vw_aug, ml_q, ml_k, ml_v, ml_og, gdn_x, gdn_z, c_q, c_kv, misc) = outs
        offs = np.cumsum([0] + [n for _, n in small])
        nsa_gates, ml_gates, gdn_gates, k_r = [misc[:, offs[j]:offs[j + 1]] for j in range(len(small))]
        seq = lambda a: a.reshape(b, s, a.shape[-1])
        parts = [
            nsa_group(seq(nsa_q), seq(kc), seq(vc), seq(ks), seq(vs_aug), seq(kw), seq(vw_aug), seq(nsa_gates),
                      nsa_w_cmp_k[l], nsa_w_cmp_v[l], nsa_cmp_pos[l], t5_table, bias_tiles),
            mlstm_group(seq(ml_q), seq(ml_k), seq(ml_v), seq(ml_gates), seq(ml_og),
                        mlstm_i_bias[l], mlstm_f_bias[l], mlstm_norm_g[l]),
            gdn_group(seq(gdn_x), seq(gdn_gates), seq(gdn_z), gdn_conv_w[l], gdn_a_log[l], gdn_dt_bias[l],
                      gdn_norm_g[l]),
            mla_group(c_q, c_kv, k_r, positions, mla_q_norm_g[l], mla_kv_norm_g[l], mla_w_uq[l], mla_w_ukv[l]),
        ]
        w_o = w_out[l].astype(BF16)
        h = matmul_residual([p.reshape(t, W) for p in parts], [w_o[j * W:(j + 1) * W] for j in range(len(parts))], h)
        h = peer_ffn_residual(h, norm2_g[l], peer_w_q[l], peer_sub_keys[l], peer_u[l], peer_v[l])
    return rmsnorm(h, final_norm_g).reshape(b, s, d)
```

```python
import functools
import math

import jax
import jax.numpy as jnp
import numpy as np
from jax import lax
from jax.experimental import pallas as pl
from jax.experimental.pallas import tpu as pltpu

F32 = jnp.float32
BF16 = jnp.bfloat16
NORM_EPS = 1e-6
NEG = -1e30
LANE = 128
ROW_TILE = 512
VMEM_LIMIT = 56 * 1024 * 1024

HEAD_DIM = 64
GROUP_HEADS = 4
GROUP_W = GROUP_HEADS * HEAD_DIM
Q_BLOCK = 128
CMP_LEN = 32
CMP_STRIDE = 16
SEL_LEN = 64
SEL_TOPN = 16
WINDOW = 512
SEL_FORCE = 1e3
N_BUCKETS = 32
MAX_DISTANCE = 128


def _dot_nt(a, b):
    return lax.dot_general(a, b, (((1,), (1,)), ((), ())), preferred_element_type=F32)


def _dot(a, b):
    return jnp.dot(a, b, preferred_element_type=F32)


def _norm_matmul_split_kernel(*refs, ones_cols):
    n = len(ones_cols)
    x_ref, g_ref = refs[:2]
    w_refs, o_refs = refs[2:2 + n], refs[2 + n:]
    x = x_ref[...]
    y = (x * lax.rsqrt(jnp.mean(x * x, axis=-1, keepdims=True) + NORM_EPS) * g_ref[...]).astype(BF16)
    for w_ref, o_ref, col in zip(w_refs, o_refs, ones_cols):
        r = _dot(y, w_ref[...])
        if col is not None:
            r = r + jnp.where(lax.broadcasted_iota(jnp.int32, r.shape, 1) == col, 1.0, 0.0)
        o_ref[...] = r.astype(o_ref.dtype)


def norm_matmul_split(x, g, weights, dtypes, ones_cols):
    t, d = x.shape
    assert t % ROW_TILE == 0
    row = lambda n: pl.BlockSpec((ROW_TILE, n), lambda i: (i, 0))
    return pl.pallas_call(
        functools.partial(_norm_matmul_split_kernel, ones_cols=tuple(ones_cols)),
        grid=(t // ROW_TILE,),
        in_specs=[row(d), pl.BlockSpec((1, d), lambda i: (0, 0))]
                 + [pl.BlockSpec(w.shape, lambda i: (0, 0)) for w in weights],
        out_specs=[row(w.shape[1]) for w in weights],
        out_shape=[jax.ShapeDtypeStruct((t, w.shape[1]), dt) for w, dt in zip(weights, dtypes)],
        compiler_params=pltpu.CompilerParams(dimension_semantics=("arbitrary",),
                                             vmem_limit_bytes=VMEM_LIMIT),
        name="norm_matmul",
    )(x, g.reshape(1, d), *weights)


def _matmul_residual_kernel(*refs):
    n = (len(refs) - 2) // 2
    o_refs, w_refs, r_ref, out_ref = refs[:n], refs[n:2 * n], refs[2 * n], refs[2 * n + 1]
    acc = r_ref[...]
    for o_ref, w_ref in zip(o_refs, w_refs):
        acc = acc + _dot(o_ref[...].astype(BF16), w_ref[...])
    out_ref[...] = acc


def matmul_residual(parts, weights, res):
    t, n = res.shape
    row = lambda w: pl.BlockSpec((ROW_TILE, w), lambda i: (i, 0))
    return pl.pallas_call(
        _matmul_residual_kernel,
        grid=(t // ROW_TILE,),
        in_specs=[row(o.shape[1]) for o in parts]
                 + [pl.BlockSpec(w.shape, lambda i: (0, 0)) for w in weights] + [row(n)],
        out_specs=row(n),
        out_shape=jax.ShapeDtypeStruct((t, n), F32),
        compiler_params=pltpu.CompilerParams(dimension_semantics=("arbitrary",),
                                             vmem_limit_bytes=VMEM_LIMIT),
        name="matmul_residual",
    )(*parts, *weights, res)


def _rmsnorm_kernel(x_ref, g_ref, o_ref):
    x = x_ref[...]
    o_ref[...] = x * lax.rsqrt(jnp.mean(x * x, axis=-1, keepdims=True) + NORM_EPS) * g_ref[...]


def rmsnorm(x, g):
    t, d = x.shape
    return pl.pallas_call(
        _rmsnorm_kernel,
        grid=(t // ROW_TILE,),
        in_specs=[pl.BlockSpec((ROW_TILE, d), lambda i: (i, 0)),
                  pl.BlockSpec((1, d), lambda i: (0, 0))],
        out_specs=pl.BlockSpec((ROW_TILE, d), lambda i: (i, 0)),
        out_shape=jax.ShapeDtypeStruct((t, d), F32),
        compiler_params=pltpu.CompilerParams(dimension_semantics=("arbitrary",)),
        name="rmsnorm",
    )(x, g.reshape(1, d))


def _t5_bucket_starts():
    d = np.arange(0, 2 * MAX_DISTANCE)
    max_exact = N_BUCKETS // 2
    val = (np.log(np.maximum(d, 1).astype(np.float32) / np.float32(max_exact))
           / np.float32(math.log(MAX_DISTANCE / max_exact)) * np.float32(N_BUCKETS - max_exact))
    large = np.minimum(max_exact + val.astype(np.int32), N_BUCKETS - 1)
    bucket = np.where(d < max_exact, d, large)
    assert np.all(np.diff(bucket) >= 0)
    return [(int(k), int(d[bucket == k].min())) for k in range(N_BUCKETS) if np.any(bucket == k)]


T5_STARTS = _t5_bucket_starts()


def _t5_bias_heads(dist, tab_ref):
    accs = [jnp.full(dist.shape, tab_ref[0, h], F32) for h in range(GROUP_HEADS)]
    for bucket, start in T5_STARTS[1:]:
        m = dist >= start
        accs = [jnp.where(m, tab_ref[bucket, h], accs[h]) for h in range(GROUP_HEADS)]
    return jnp.stack(accs)


def _nsa_compress_kernel(x_ref, pos_ref, w_ref, o_ref):
    x = x_ref[0, 0]
    n = x.shape[0]
    lo = _dot((x + pos_ref[0, 0:1, :]).astype(BF16), w_ref[0, 0])
    hi = _dot((x + pos_ref[0, 1:2, :]).astype(BF16), w_ref[0, 1])
    o_ref[0, 0] = (lo + pltpu.roll(hi, n - 1, axis=0)).astype(o_ref.dtype)


def nsa_compress(x, pos, w):
    two, b, n, c = x.shape
    return pl.pallas_call(
        _nsa_compress_kernel,
        grid=(two, b),
        in_specs=[pl.BlockSpec((1, 1, n, c), lambda k, i: (k, i, 0, 0)),
                  pl.BlockSpec((1, 2, c), lambda k, i: (0, 0, 0)),
                  pl.BlockSpec((1, 2, c, HEAD_DIM), lambda k, i: (k, 0, 0, 0))],
        out_specs=pl.BlockSpec((1, 1, n, HEAD_DIM), lambda k, i: (k, i, 0, 0)),
        out_shape=jax.ShapeDtypeStruct((two, b, n, HEAD_DIM), BF16),
        compiler_params=pltpu.CompilerParams(dimension_semantics=("arbitrary", "arbitrary")),
        name="nsa_compress",
    )(x, pos, w)


FAR_TILE = 512
NEAR_KEYS = FAR_TILE + Q_BLOCK
WIN_KEYS = WINDOW + Q_BLOCK
NEAR_PAD = NEAR_KEYS - Q_BLOCK
CMP_TILE = LANE
CMP_TILES_BACK = CMP_TILE * CMP_STRIDE // Q_BLOCK
N_CMP_BIAS = CMP_TILES_BACK + 3


def _masked_bias(dist, valid, tab_ref):
    return jnp.where(valid, _t5_bias_heads(dist, tab_ref), NEG)


def _nsa_bias_kernel(tab_ref, near_ref, win_ref):
    q = lax.broadcasted_iota(jnp.int32, (Q_BLOCK, NEAR_KEYS), 0)
    k = lax.broadcasted_iota(jnp.int32, (Q_BLOCK, NEAR_KEYS), 1)
    dist = NEAR_PAD + q - k
    near_ref[...] = _masked_bias(dist, dist >= 0, tab_ref)
    q = lax.broadcasted_iota(jnp.int32, (Q_BLOCK, WIN_KEYS), 0)
    k = lax.broadcasted_iota(jnp.int32, (Q_BLOCK, WIN_KEYS), 1)
    dist = WINDOW + q - k
    win_ref[...] = _masked_bias(dist, (dist >= 0) & (dist < WINDOW), tab_ref)


def _nsa_cmp_bias_kernel(tab_ref, o_ref):
    e = pl.program_id(0)
    q = lax.broadcasted_iota(jnp.int32, (Q_BLOCK, CMP_TILE), 0)
    n = lax.broadcasted_iota(jnp.int32, (Q_BLOCK, CMP_TILE), 1)
    dist = (e - 1) * Q_BLOCK + q - (n * CMP_STRIDE + CMP_LEN - 1)
    o_ref[0] = _masked_bias(dist, dist >= 0, tab_ref)


def nsa_bias_tiles(t5_table):
    H, Q = GROUP_HEADS, Q_BLOCK
    smem = pl.BlockSpec(memory_space=pltpu.SMEM)
    near, win = pl.pallas_call(
        _nsa_bias_kernel, grid=(1,), in_specs=[smem],
        out_specs=[pl.BlockSpec((H, Q, NEAR_KEYS), lambda i: (0, 0, 0)),
                   pl.BlockSpec((H, Q, WIN_KEYS), lambda i: (0, 0, 0))],
        out_shape=[jax.ShapeDtypeStruct((H, Q, NEAR_KEYS), F32), jax.ShapeDtypeStruct((H, Q, WIN_KEYS), F32)],
        name="nsa_bias",
    )(t5_table)
    cmp = pl.pallas_call(
        _nsa_cmp_bias_kernel, grid=(N_CMP_BIAS,), in_specs=[smem],
        out_specs=pl.BlockSpec((1, H, Q, CMP_TILE), lambda e: (e, 0, 0, 0)),
        out_shape=jax.ShapeDtypeStruct((N_CMP_BIAS, H, Q, CMP_TILE), F32),
        name="nsa_cmp_bias",
    )(t5_table)
    return near, win, cmp


def _nsa_attn_kernel(tab_ref, q_ref, g_ref, kc_ref, vc_ref, ks_ref, vs_ref, kw_ref, vw_ref,
                     bnear_ref, bwin_ref, bcmp_ref, o_ref, m_scr, acc_scr):
    H, Q, D = GROUP_HEADS, Q_BLOCK, HEAD_DIM
    n_cmp = kc_ref.shape[1]
    n_sel = (ks_ref.shape[1] - NEAR_PAD) // SEL_LEN
    i = pl.program_id(1)
    t0 = i * Q

    q = q_ref[0] * (D ** -0.5)
    q4 = jnp.concatenate([q[:, h * D:(h + 1) * D] for h in range(H)], axis=0).astype(BF16)

    s = _dot_nt(q4, kc_ref[0]).reshape(H, Q, n_cmp)
    tiles = []
    for g in range(n_cmp // CMP_TILE):
        e = jnp.clip(i - g * CMP_TILES_BACK, -1, CMP_TILES_BACK + 1) + 1
        tiles.append(s[:, :, g * CMP_TILE:(g + 1) * CMP_TILE] + bcmp_ref[e])
    s = jnp.concatenate(tiles, axis=-1)
    t_c = t0 + lax.broadcasted_iota(jnp.int32, (Q, n_cmp), 0)
    valid_c = t_c >= lax.broadcasted_iota(jnp.int32, (Q, n_cmp), 1) * CMP_STRIDE + CMP_LEN - 1
    p = jnp.where(valid_c, jnp.exp(s - jnp.max(s, axis=-1, keepdims=True)), 0.0)
    l = jnp.sum(p, axis=-1, keepdims=True)
    p = p * (1.0 / jnp.where(l > 0.0, l, 1.0))
    o_c = _dot(p.reshape(H * Q, n_cmp).astype(BF16), vc_ref[0])

    psum = p[0] + p[1] + p[2] + p[3]
    j_idx = lax.broadcasted_iota(jnp.int32, (n_sel, n_cmp), 0)
    c_idx = lax.broadcasted_iota(jnp.int32, (n_sel, n_cmp), 1)
    ratio = SEL_LEN // CMP_STRIDE
    n_in_sel = (SEL_LEN - CMP_LEN) // CMP_STRIDE + 1
    pool_t = ((c_idx // ratio == j_idx) & (c_idx % ratio < n_in_sel)).astype(BF16)
    p_hi = psum.astype(BF16)
    p_mid = (psum - p_hi.astype(F32)).astype(BF16)
    p_lo = (psum - p_hi.astype(F32) - p_mid.astype(F32)).astype(BF16)
    imp_t = _dot_nt(pool_t, p_hi) + _dot_nt(pool_t, p_mid) + _dot_nt(pool_t, p_lo)

    row0 = pl.multiple_of(t0, Q)
    kw = kw_ref[0, pl.ds(row0, WIN_KEYS), :]
    vw = vw_ref[0, pl.ds(row0, WIN_KEYS), :]
    in_seq = lax.broadcasted_iota(jnp.int32, (1, WIN_KEYS), 1) >= WINDOW - t0
    s = _dot_nt(q4, kw).reshape(H, Q, WIN_KEYS) + bwin_ref[...] + jnp.where(in_seq, 0.0, NEG)
    p_w = jnp.exp(s - jnp.max(s, axis=-1, keepdims=True))
    acc_w = _dot(p_w.reshape(H * Q, WIN_KEYS).astype(BF16), vw)
    o_w = acc_w[:, :D] * (1.0 / acc_w[:, D:D + 1])

    jb = lax.broadcasted_iota(jnp.int32, (n_sel, Q), 0)
    cur = (t0 + lax.broadcasted_iota(jnp.int32, (n_sel, Q), 1)) // SEL_LEN
    forced = (jb == 0) | (jb == cur) | (jb == cur - 1)
    work = jnp.where(jb <= cur, imp_t + jnp.where(forced, SEL_FORCE, 0.0), NEG)
    sel_t = jnp.zeros((n_sel, Q), F32)
    jb_f = jb.astype(F32)
    for _ in range(min(SEL_TOPN, n_sel)):
        mx = jnp.max(work, axis=0, keepdims=True)
        first = jnp.min(jnp.where(work == mx, jb_f, jnp.inf), axis=0, keepdims=True)
        pick = jb_f == first
        sel_t = jnp.where(pick & (mx > 0.5 * NEG), 1.0, sel_t)
        work = jnp.where(pick, -jnp.inf, work)
    q_eye = (lax.broadcasted_iota(jnp.int32, (Q, Q), 0) == lax.broadcasted_iota(jnp.int32, (Q, Q), 1)).astype(BF16)
    sel_bf = _dot_nt(q_eye, sel_t.astype(BF16)).astype(BF16)

    far_bias = jnp.stack([jnp.full((Q, 1), tab_ref[N_BUCKETS - 1, h], F32) for h in range(H)])
    m_scr[...] = jnp.full(m_scr.shape, NEG, F32)
    acc_scr[...] = jnp.zeros(acc_scr.shape, F32)

    def sel_scores(row0, n_keys, bias, first_key):
        s = _dot_nt(q4, ks_ref[0, pl.ds(row0, n_keys), :]).reshape(H, Q, n_keys) + bias
        pos = row0 - NEAR_PAD + lax.broadcasted_iota(jnp.int32, (n_sel, n_keys), 1)
        blk = jnp.where(pos >= first_key, pos // SEL_LEN, -1)
        expand = (lax.broadcasted_iota(jnp.int32, (n_sel, n_keys), 0) == blk).astype(BF16)
        mask = _dot(sel_bf, expand) > 0.5
        return jnp.where(mask, s, NEG), mask

    def sel_absorb(row0, n_keys, s, mask):
        m_old = m_scr[...].reshape(H, Q, 1)
        m_new = jnp.maximum(m_old, jnp.max(s, axis=-1, keepdims=True))
        p = jnp.where(mask, jnp.exp(s - m_new), 0.0).reshape(H * Q, n_keys)
        alpha = jnp.exp(m_old - m_new).reshape(H * Q, 1)
        acc_scr[...] = alpha * acc_scr[...] + _dot(p.astype(BF16), vs_ref[0, pl.ds(row0, n_keys), :])
        m_scr[...] = m_new.reshape(H * Q, 1)

    def sel_step(row0, n_keys, bias, first_key):
        sel_absorb(row0, n_keys, *sel_scores(row0, n_keys, bias, first_key))

    n_far = jnp.maximum(i - 1, 0) // (FAR_TILE // Q)

    def far_body(kp, carry):
        row_a = pl.multiple_of(NEAR_PAD + kp * (2 * FAR_TILE), Q)
        row_b = pl.multiple_of(row_a + FAR_TILE, Q)
        sa = sel_scores(row_a, FAR_TILE, far_bias, 0)
        sb = sel_scores(row_b, FAR_TILE, far_bias, 0)
        sel_absorb(row_a, FAR_TILE, *sa)
        sel_absorb(row_b, FAR_TILE, *sb)
        return carry

    lax.fori_loop(0, n_far // 2, far_body, 0)

    @pl.when(n_far % 2 == 1)
    def _():
        sel_step(pl.multiple_of(NEAR_PAD + (n_far - 1) * FAR_TILE, Q), FAR_TILE, far_bias, 0)
    sel_step(row0, NEAR_KEYS, bnear_ref[...], n_far * FAR_TILE)
    acc = acc_scr[...]
    o_s = acc[:, :D] * (1.0 / acc[:, D:D + 1])

    g = jax.nn.sigmoid(g_ref[0])
    outs = []
    for h in range(H):
        rows = slice(h * Q, (h + 1) * Q)
        outs.append(g[:, h:h + 1] * o_c[rows] + g[:, H + h:H + h + 1] * o_s[rows]
                    + g[:, 2 * H + h:2 * H + h + 1] * o_w[rows])
    o_ref[0] = jnp.concatenate(outs, axis=1)


def nsa_attention(t5_table, bias_tiles, q, gates, k_cmp, v_cmp, ks, vs_aug, kw, vw_aug):
    b, s, _ = q.shape
    assert s % (CMP_TILE * CMP_STRIDE) == 0 and s >= NEAR_KEYS
    bnear, bwin, bcmp = bias_tiles
    full = lambda a: pl.BlockSpec((1,) + a.shape[1:], lambda bi, i: (bi,) + (0,) * (a.ndim - 1))
    fixed = lambda a: pl.BlockSpec(a.shape, lambda bi, i: (0,) * a.ndim)
    return pl.pallas_call(
        _nsa_attn_kernel,
        grid=(b, s // Q_BLOCK),
        in_specs=[pl.BlockSpec(memory_space=pltpu.SMEM),
                  pl.BlockSpec((1, Q_BLOCK, GROUP_W), lambda bi, i: (bi, i, 0)),
                  pl.BlockSpec((1, Q_BLOCK, 3 * GROUP_HEADS), lambda bi, i: (bi, i, 0)),
                  full(k_cmp), full(v_cmp), full(ks), full(vs_aug), full(kw), full(vw_aug),
                  fixed(bnear), fixed(bwin), fixed(bcmp)],
        out_specs=pl.BlockSpec((1, Q_BLOCK, GROUP_W), lambda bi, i: (bi, i, 0)),
        out_shape=jax.ShapeDtypeStruct((b, s, GROUP_W), F32),
        scratch_shapes=[pltpu.VMEM((GROUP_HEADS * Q_BLOCK, 1), F32),
                        pltpu.VMEM((GROUP_HEADS * Q_BLOCK, 2 * HEAD_DIM), F32)],
        compiler_params=pltpu.CompilerParams(dimension_semantics=("arbitrary", "arbitrary"),
                                             vmem_limit_bytes=VMEM_LIMIT),
        name="nsa_attention",
    )(t5_table, q, gates, k_cmp, v_cmp, ks, vs_aug, kw, vw_aug, bnear, bwin, bcmp)


def _front_pad(x, rows):
    return jnp.pad(x, ((0, 0), (rows, 0), (0, 0)))


def nsa_group(q, kc, vc, ks, vs_aug, kw, vw_aug, gates, w_cmp_k, w_cmp_v, cmp_pos, t5_table, bias_tiles):
    b, s, _ = q.shape
    d = HEAD_DIM
    n = s // CMP_STRIDE
    x = jnp.stack([kc, vc]).reshape(2, b, n, CMP_STRIDE * d)
    pos = cmp_pos.reshape(1, 2, CMP_STRIDE * d)
    w = jnp.stack([w_cmp_k, w_cmp_v]).reshape(2, 2, CMP_STRIDE * d, d).astype(BF16)
    cmp = nsa_compress(x, pos, w)
    return nsa_attention(t5_table, bias_tiles, q, gates, cmp[0], cmp[1],
                         _front_pad(ks, NEAR_PAD), _front_pad(vs_aug, NEAR_PAD),
                         _front_pad(kw, WINDOW), _front_pad(vw_aug, WINDOW))


PEER_HEADS = 8
PEER_DK = 256
N_KEYS = 128
PEER_TOPK = 16
PEER_SLOTS = PEER_HEADS * PEER_TOPK
ROUTE_TILE = 128
GATHER_TILE = 8


def _top_rows(work, order, n_top, payload=None):
    vals, args = [], []
    order_f = order.astype(F32)
    for _ in range(n_top):
        mx = jnp.max(work, axis=0, keepdims=True)
        first = jnp.min(jnp.where(work == mx, order_f, jnp.inf), axis=0, keepdims=True)
        pick = order_f == first
        vals.append(mx)
        if payload is None:
            args.append(first.astype(jnp.int32))
        else:
            args.append(jnp.sum(jnp.where(pick, payload, 0), axis=0, keepdims=True))
        work = jnp.where(pick, -jnp.inf, work)
    return jnp.concatenate(vals, axis=0), jnp.concatenate(args, axis=0)


def _pair_rows(s1, s2, combine):
    k = PEER_TOPK
    rows = [combine(s1[0:1], s2)]
    for a in range(1, k // 2):
        rows.append(combine(s1[a:a + 1], s2[0:k // 2]))
    rows.append(combine(s1[k // 2:], s2[0:1]))
    return jnp.concatenate(rows, axis=0)


def _peer_route_kernel(h_ref, g_ref, wq_ref, keys_ref, xn_ref, eidx_ref, gate_ref):
    x = h_ref[...]
    xn = x * lax.rsqrt(jnp.mean(x * x, axis=-1, keepdims=True) + NORM_EPS) * g_ref[...]
    xn_ref[...] = xn
    q = _dot(xn.astype(BF16), wq_ref[...]).astype(BF16)
    tq = x.shape[0]
    half = PEER_DK // 2
    n_iota = lax.broadcasted_iota(jnp.int32, (N_KEYS, tq), 0)
    k_iota = lax.broadcasted_iota(jnp.int32, (PEER_TOPK, tq), 0)
    order = _pair_rows(k_iota, k_iota, lambda a, b: a * PEER_TOPK + b)
    reachable = (order // PEER_TOPK + 1) * (order % PEER_TOPK + 1) <= PEER_TOPK
    eidx_rows, gate_rows = [], []
    for p in range(PEER_HEADS):
        tops = []
        for c in range(2):
            g = 2 * p + c
            sc = _dot_nt(keys_ref[g], q[:, g * half:(g + 1) * half])
            tops.append(_top_rows(sc, n_iota, PEER_TOPK))
        (s1, i1), (s2, i2) = tops
        cand = jnp.where(reachable, _pair_rows(s1, s2, lambda a, b: a + b), -jnp.inf)
        cidx = _pair_rows(i1, i2, lambda a, b: a * N_KEYS + b)
        top, eidx = _top_rows(cand, order, PEER_TOPK, payload=cidx)
        e = jnp.exp(top - top[0:1])
        eidx_rows += [eidx[g * SUBLANES + k:g * SUBLANES + k + 1] for g in range(PEER_TOPK // SUBLANES)
                      for k in BITREV8]
        gate_rows.append(e * (1.0 / jnp.sum(e, axis=0, keepdims=True)))
    eidx_ref[...] = jnp.concatenate(eidx_rows, axis=0).T
    gate_ref[...] = jnp.concatenate(gate_rows, axis=0).T


def peer_route(h, g, w_q, keys):
    t, d = h.shape
    assert t % ROUTE_TILE == 0
    nq = w_q.shape[1]
    return pl.pallas_call(
        _peer_route_kernel,
        grid=(t // ROUTE_TILE,),
        in_specs=[pl.BlockSpec((ROUTE_TILE, d), lambda i: (i, 0)),
                  pl.BlockSpec((1, d), lambda i: (0, 0)),
                  pl.BlockSpec((d, nq), lambda i: (0, 0)),
                  pl.BlockSpec(keys.shape, lambda i: (0, 0, 0))],
        out_specs=[pl.BlockSpec((ROUTE_TILE, d), lambda i: (i, 0)),
                   pl.BlockSpec((ROUTE_TILE, PEER_SLOTS), lambda i: (i, 0)),
                   pl.BlockSpec((ROUTE_TILE, PEER_SLOTS), lambda i: (i, 0))],
        out_shape=[jax.ShapeDtypeStruct((t, d), F32),
                   jax.ShapeDtypeStruct((t, PEER_SLOTS), jnp.int32),
                   jax.ShapeDtypeStruct((t, PEER_SLOTS), F32)],
        compiler_params=pltpu.CompilerParams(dimension_semantics=("arbitrary",),
                                             vmem_limit_bytes=VMEM_LIMIT),
        name="peer_route",
    )(h, g.reshape(1, d), w_q, keys)


SUBLANES = 8
SLAB = 2 * SUBLANES
ISSUE_UNROLL = 8


def _gelu(x):
    return 0.5 * x * (1.0 + lax.erf(x * (2.0 ** -0.5)))


BITREV8 = (0, 4, 2, 6, 1, 5, 3, 7)


def _fold_pair(a, b, half, sub):
    low = (sub % (2 * half)) < half
    partner = jnp.where(low, pltpu.roll(a, SUBLANES - half, axis=0), pltpu.roll(b, half, axis=0))
    return jnp.where(low, a, b) + partner


GATHER_DEPTH = 3
DMA_QUEUES = 2


def _peer_gather_kernel(idx0_ref, idx1_ref, idx2_ref, xn_ref, h_ref, gate_ref, uv_hbm, o_ref,
                        buf_a, buf_b, buf_c, w_scr, sem):
    i = pl.program_id(0)
    n = pl.num_programs(0)
    tt = xn_ref.shape[0]
    n_slabs = tt * PEER_SLOTS
    bufs = (buf_a, buf_b, buf_c)

    def request(ref, k, r, queue):
        src = pl.multiple_of(ref[0, 0, r] * SLAB, SLAB)
        pltpu.make_async_copy(uv_hbm.at[pl.ds(src, SLAB)], bufs[k].at[pl.ds(r * SLAB, SLAB)],
                              sem.at[k]).start(priority=queue)

    def request_tile(ref, k):
        def body(c, carry):
            for j in range(ISSUE_UNROLL):
                request(ref, k, pl.multiple_of(c * ISSUE_UNROLL, ISSUE_UNROLL) + j, j % DMA_QUEUES)
            return carry
        lax.fori_loop(0, n_slabs // ISSUE_UNROLL, body, 0)

    def wait_tile(k):
        pltpu.make_async_copy(uv_hbm.at[pl.ds(0, n_slabs * SLAB)], bufs[k], sem.at[k]).wait()

    @pl.when(i == 0)
    def _():
        request_tile(idx0_ref, 0)
        request_tile(idx1_ref, 1)

    def step(cur, ahead):
        wait_tile(cur)
        pad = jnp.zeros((PEER_SLOTS - tt, PEER_SLOTS), F32)
        gates = jnp.concatenate([gate_ref[...], pad], axis=0).T[:, :tt]
        sub = lax.broadcasted_iota(jnp.int32, (SUBLANES, LANE), 0)
        for t in range(tt):
            for j in range(PEER_SLOTS):
                request(idx2_ref, ahead, t * PEER_SLOTS + j, j % DMA_QUEUES)
            x = xn_ref[t]

            def slab(pos, half):
                return bufs[cur][pl.ds((t * PEER_SLOTS + pos) * SLAB + half * SUBLANES, SUBLANES), :]

            groups = []
            for g in range(PEER_SLOTS // SUBLANES):
                p = [slab(g * SUBLANES + k, 0) * x for k in range(SUBLANES)]
                c = [_fold_pair(p[2 * k], p[2 * k + 1], 4, sub) for k in range(4)]
                d = [_fold_pair(c[0], c[1], 2, sub), _fold_pair(c[2], c[3], 2, sub)]
                groups.append(_fold_pair(d[0], d[1], 1, sub))
            act = jnp.sum(jnp.concatenate(groups, axis=0), axis=-1, keepdims=True)
            w_scr[...] = jnp.broadcast_to(gates[:, t:t + 1] * _gelu(act), (PEER_SLOTS, LANE))
            acc = h_ref[t]
            for g in range(PEER_SLOTS // SUBLANES):
                for k in range(SUBLANES):
                    w_row = w_scr[pl.ds(g * SUBLANES + BITREV8[k], 1), :]
                    acc = acc + slab(g * SUBLANES + k, 1) * w_row
            o_ref[t] = acc

    for k in range(GATHER_DEPTH):
        @pl.when(i % GATHER_DEPTH == k)
        def _(k=k):
            step(k, (k + GATHER_DEPTH - 1) % GATHER_DEPTH)

    @pl.when(i == n - 1)
    def _():
        for k in range(GATHER_DEPTH):
            @pl.when(i % GATHER_DEPTH == k)
            def _(k=k):
                wait_tile((k + 1) % GATHER_DEPTH)
                wait_tile((k + 2) % GATHER_DEPTH)


def peer_gather(idx, xn, h, gates, uv):
    t = xn.shape[0]
    nt = t // GATHER_TILE
    assert t % GATHER_TILE == 0 and nt >= GATHER_DEPTH
    n_slabs = GATHER_TILE * PEER_SLOTS
    tok = pl.BlockSpec((GATHER_TILE, SUBLANES, LANE), lambda i: (i, 0, 0))
    idx_spec = lambda k: pl.BlockSpec((1, 1, n_slabs), lambda i: (jnp.minimum(i + k, nt - 1), 0, 0),
                                      memory_space=pltpu.SMEM)
    slab_buf = pltpu.VMEM((n_slabs * SLAB, LANE), F32)
    return pl.pallas_call(
        _peer_gather_kernel,
        grid=(nt,),
        in_specs=[idx_spec(0), idx_spec(1), idx_spec(2), tok, tok,
                  pl.BlockSpec((GATHER_TILE, PEER_SLOTS), lambda i: (i, 0)),
                  pl.BlockSpec(memory_space=pl.ANY)],
        out_specs=tok,
        out_shape=jax.ShapeDtypeStruct((t, SUBLANES, LANE), F32),
        scratch_shapes=[slab_buf, slab_buf, slab_buf, pltpu.VMEM((PEER_SLOTS, LANE), F32),
                        pltpu.SemaphoreType.DMA((GATHER_DEPTH,))],
        compiler_params=pltpu.CompilerParams(dimension_semantics=("arbitrary",),
                                             vmem_limit_bytes=VMEM_LIMIT),
        name="peer_gather",
    )(idx, idx, idx, xn, h, gates, uv)


def peer_ffn_residual(h, g, w_q, sub_keys, u_tab, v_tab):
    t, d = h.shape
    e = u_tab.shape[0]
    keys = sub_keys.reshape(PEER_HEADS * 2, N_KEYS, PEER_DK // 2).astype(BF16)
    xn, eidx, gates = peer_route(h, g, w_q.astype(BF16), keys)
    idx = eidx.reshape(t // GATHER_TILE, 1, GATHER_TILE * PEER_SLOTS)
    uv = jnp.concatenate([u_tab.reshape(e, SUBLANES, LANE), v_tab.reshape(e, SUBLANES, LANE)],
                         axis=1).reshape(e * SLAB, LANE)
    out = peer_gather(idx, xn.reshape(t, SUBLANES, LANE), h.reshape(t, SUBLANES, LANE), gates, uv)
    return out.reshape(t, d)


Q_LORA = 256
KV_LORA = 128
QK_NOPE = 64
QK_ROPE = 32
V_HEAD = 64
ROPE_THETA = 10000.0
MLA_TQ = 256
MLA_TK = 1024


def _rms(x, g):
    return x * lax.rsqrt(jnp.mean(x * x, axis=-1, keepdims=True) + NORM_EPS) * g


def _mla_prep_kernel(pos_ref, freq_ref, cq_ref, ckv_ref, kr_ref, gq_ref, gkv_ref, wq_ref, wk_ref, wv_ref,
                     q_out, k_out, v_out):
    rows = cq_ref.shape[0]
    half = QK_ROPE // 2
    cq = _rms(cq_ref[...], gq_ref[...]).astype(BF16)
    ckv = _rms(ckv_ref[...], gkv_ref[...]).astype(BF16)
    qf = _dot(cq, wq_ref[...])
    kf = _dot(ckv, wk_ref[...])
    vf = _dot(ckv, wv_ref[...])
    lane = lax.broadcasted_iota(jnp.int32, (rows, LANE), 1)
    ang = pos_ref[...].astype(F32) * freq_ref[...]
    cos, sin = jnp.cos(ang), jnp.sin(ang)
    in_rope = (lane >= QK_NOPE) & (lane < QK_NOPE + QK_ROPE)
    first = lane < QK_NOPE + half
    c_mul = jnp.where(lane < QK_NOPE, 1.0, jnp.where(in_rope, cos, 0.0))
    s_mul = jnp.where(in_rope, jnp.where(first, -sin, sin), 0.0)

    def rope(x):
        partner = jnp.where(first, pltpu.roll(x, LANE - half, axis=1), pltpu.roll(x, half, axis=1))
        return x * c_mul + partner * s_mul

    scale = (QK_NOPE + QK_ROPE) ** -0.5
    kr = rope(jnp.concatenate([jnp.zeros((rows, QK_NOPE), F32), kr_ref[...],
                               jnp.zeros((rows, LANE - QK_NOPE - QK_ROPE), F32)], axis=1))
    ones_col = jnp.where(lane == V_HEAD, 1.0, 0.0)
    for h in range(GROUP_HEADS):
        cols = slice(h * LANE, (h + 1) * LANE)
        q_out[:, cols] = (rope(qf[:, cols]) * scale).astype(BF16)
        k_out[:, cols] = (kf[:, cols] + kr).astype(BF16)
        v_out[:, cols] = (vf[:, cols] + ones_col).astype(BF16)


def mla_prep(pos, c_q, c_kv, k_r, gq, gkv, w_uq, w_ukv):
    t = c_q.shape[0]
    H = GROUP_HEADS
    wq = jnp.pad(w_uq.reshape(Q_LORA, H, QK_NOPE + QK_ROPE), ((0, 0), (0, 0), (0, LANE - QK_NOPE - QK_ROPE)))
    wkv = w_ukv.reshape(KV_LORA, H, QK_NOPE + V_HEAD)
    wk = jnp.pad(wkv[:, :, :QK_NOPE], ((0, 0), (0, 0), (0, LANE - QK_NOPE)))
    wv = jnp.pad(wkv[:, :, QK_NOPE:], ((0, 0), (0, 0), (0, LANE - V_HEAD)))
    wq, wk, wv = [w.reshape(w.shape[0], H * LANE).astype(BF16) for w in (wq, wk, wv)]
    inv_freq = ROPE_THETA ** (-np.arange(0, QK_ROPE, 2, dtype=np.float32) / QK_ROPE)
    freq = np.zeros((1, LANE), np.float32)
    freq[0, QK_NOPE:QK_NOPE + QK_ROPE] = np.tile(inv_freq, 2)
    row = lambda n: pl.BlockSpec((ROW_TILE, n), lambda i: (i, 0))
    fixed = lambda a: pl.BlockSpec(a.shape, lambda i: (0,) * a.ndim)
    gq2, gkv2, freq = gq.reshape(1, -1), gkv.reshape(1, -1), jnp.asarray(freq)
    out = jax.ShapeDtypeStruct((t, H * LANE), BF16)
    return pl.pallas_call(
        _mla_prep_kernel,
        grid=(t // ROW_TILE,),
        in_specs=[row(1), fixed(freq), row(Q_LORA), row(KV_LORA), row(QK_ROPE), fixed(gq2), fixed(gkv2),
                  fixed(wq), fixed(wk), fixed(wv)],
        out_specs=[row(H * LANE)] * 3,
        out_shape=[out] * 3,
        compiler_params=pltpu.CompilerParams(dimension_semantics=("arbitrary",)),
        name="mla_prep",
    )(pos, freq, c_q, c_kv, k_r, gq2, gkv2, wq, wk, wv)


def _mla_attn_kernel(q_ref, k_ref, v_ref, o_ref, m_scr, acc_scr):
    i = pl.program_id(1)
    tq = q_ref.shape[1]
    row = i * tq + lax.broadcasted_iota(jnp.int32, (tq, MLA_TK), 0)
    col = lax.broadcasted_iota(jnp.int32, (tq, MLA_TK), 1)
    n_full = (i * tq) // MLA_TK
    heads = range(GROUP_HEADS)
    cols = [slice(h * LANE, (h + 1) * LANE) for h in heads]
    q = [q_ref[0, :, cols[h]] for h in heads]
    m_scr[...] = jnp.full(m_scr.shape, NEG, F32)
    acc_scr[...] = jnp.zeros(acc_scr.shape, F32)

    def step(j, masked):
        start = pl.multiple_of(j * MLA_TK, MLA_TK)

        def qk(h):
            s = _dot_nt(q[h], k_ref[0, pl.ds(start, MLA_TK), cols[h]])
            return jnp.where(start + col <= row, s, NEG) if masked else s

        def absorb(h, s):
            m_old = m_scr[h]
            m_new = jnp.maximum(m_old, jnp.max(s, axis=-1, keepdims=True))
            p = jnp.exp(s - m_new).astype(BF16)
            acc_scr[h] = jnp.exp(m_old - m_new) * acc_scr[h] + _dot(p, v_ref[0, pl.ds(start, MLA_TK), cols[h]])
            m_scr[h] = m_new

        s_prev = qk(0)
        for h in range(1, GROUP_HEADS):
            s_next = qk(h)
            absorb(h - 1, s_prev)
            s_prev = s_next
        absorb(GROUP_HEADS - 1, s_prev)

    def body(j, carry):
        step(j, False)
        return carry

    lax.fori_loop(0, n_full, body, 0)
    step(n_full, True)
    outs = [acc_scr[h][:, :V_HEAD] * (1.0 / acc_scr[h][:, V_HEAD:V_HEAD + 1]) for h in heads]
    o_ref[0] = jnp.concatenate(outs, axis=1)


def mla_attention(q, k, v):
    b, s, w = q.shape
    assert s % MLA_TK == 0 and MLA_TK % MLA_TQ == 0
    return pl.pallas_call(
        _mla_attn_kernel,
        grid=(b, s // MLA_TQ),
        in_specs=[pl.BlockSpec((1, MLA_TQ, w), lambda bi, i: (bi, i, 0)),
                  pl.BlockSpec((1, s, w), lambda bi, i: (bi, 0, 0)),
                  pl.BlockSpec((1, s, w), lambda bi, i: (bi, 0, 0))],
        out_specs=pl.BlockSpec((1, MLA_TQ, GROUP_HEADS * V_HEAD), lambda bi, i: (bi, i, 0)),
        out_shape=jax.ShapeDtypeStruct((b, s, GROUP_HEADS * V_HEAD), F32),
        scratch_shapes=[pltpu.VMEM((GROUP_HEADS, MLA_TQ, 1), F32), pltpu.VMEM((GROUP_HEADS, MLA_TQ, LANE), F32)],
        compiler_params=pltpu.CompilerParams(dimension_semantics=("arbitrary", "arbitrary"),
                                             vmem_limit_bytes=VMEM_LIMIT),
        name="mla_attention",
    )(q, k, v)


def mla_group(c_q, c_kv, k_r, positions, q_norm_g, kv_norm_g, w_uq, w_ukv):
    b, s = positions.shape
    t = b * s
    q, k, v = mla_prep(positions.reshape(t, 1), c_q, c_kv, k_r, q_norm_g, kv_norm_g, w_uq, w_ukv)
    w = GROUP_HEADS * LANE
    return mla_attention(q.reshape(b, s, w), k.reshape(b, s, w), v.reshape(b, s, w))


CHUNK = 64
CONV_W = 4


def _split3(a):
    hi = a.astype(BF16)
    r = a - hi.astype(F32)
    mid = r.astype(BF16)
    return hi, mid, (r - mid.astype(F32)).astype(BF16)


def _dot_exact_rhs(sel, a):
    return sum(_dot(sel, p) for p in _split3(a))


def _row_matrix(col, n_rows):
    l = col.shape[0]
    lane = lax.broadcasted_iota(jnp.int32, (l, SUBLANES), 1)
    padded = jnp.where(lane == 0, col, 0.0)
    ones = jnp.ones((n_rows, SUBLANES), BF16)
    return sum(_dot_nt(ones, p) for p in _split3(padded))


def _transpose_bf16(x_bf):
    d = x_bf.shape[1]
    eye = (lax.broadcasted_iota(jnp.int32, (d, d), 0) == lax.broadcasted_iota(jnp.int32, (d, d), 1)).astype(BF16)
    return _dot_nt(eye, x_bf).astype(BF16)


def _log_sigmoid(x):
    return jnp.minimum(x, 0.0) - jnp.log1p(jnp.exp(-jnp.abs(x)))


def _softplus(x):
    return jnp.maximum(x, 0.0) + jnp.log1p(jnp.exp(-jnp.abs(x)))


def _with_ones_col(v):
    l, d = v.shape
    lane = lax.broadcasted_iota(jnp.int32, (l, LANE - d), 1)
    return jnp.concatenate([v, jnp.where(lane == 0, 1.0, 0.0)], axis=1)


def _bf(x):
    return x.astype(BF16)


def _mlstm_kernel(q_ref, k_ref, v_ref, gate_ref, og_ref, bias_ref, ng_ref, o_ref, st_scr, m_scr):
    nb = q_ref.shape[0]
    H, D, L = GROUP_HEADS, HEAD_DIM, CHUNK
    chains = [(b, h) for b in range(nb) for h in range(H)]
    cols = lambda h: slice(h * D, (h + 1) * D)
    each = lambda fn: [fn(c, b, h) for c, (b, h) in enumerate(chains)]

    @pl.when(pl.program_id(0) == 0)
    def _():
        st_scr[...] = jnp.zeros(st_scr.shape, F32)
        m_scr[...] = jnp.zeros(m_scr.shape, F32)

    row = lax.broadcasted_iota(jnp.int32, (L, L), 0)
    col = lax.broadcasted_iota(jnp.int32, (L, L), 1)
    tri = row >= col
    tri_bf = tri.astype(BF16)
    gates = [gate_ref[b] + bias_ref[...] for b in range(nb)]
    bcum = [_dot_exact_rhs(tri_bf, _log_sigmoid(g[:, H:])) for g in gates]
    q = each(lambda c, b, h: _bf(q_ref[b, :, cols(h)]))
    kt = each(lambda c, b, h: _transpose_bf16(_bf(k_ref[b, :, cols(h)] * (D ** -0.5))))
    v_aug = each(lambda c, b, h: _with_ones_col(v_ref[b, :, cols(h)]))
    bc = each(lambda c, b, h: bcum[b][:, h:h + 1])
    ic = each(lambda c, b, h: gates[b][:, h:h + 1])
    m_st = each(lambda c, b, h: m_scr[c, 0:1, 0:1])
    st = each(lambda c, b, h: st_scr[c])
    log_d = each(lambda c, b, h: jnp.where(tri, bc[c] + _row_matrix(ic[c] - bc[c], L), -jnp.inf))
    m_t = each(lambda c, b, h: jnp.maximum(jnp.max(log_d[c], axis=-1, keepdims=True), bc[c] + m_st[c]))
    sqk = each(lambda c, b, h: _dot(q[c], kt[c]) * jnp.exp(log_d[c] - m_t[c]))
    inter = each(lambda c, b, h: jnp.exp(bc[c] + m_st[c] - m_t[c]))
    tot = each(lambda c, b, h: _dot(_bf(sqk[c]), _bf(v_aug[c])) + inter[c] * _dot(q[c], _bf(st[c])))
    h_t = each(lambda c, b, h: tot[c][:, :D] * (1.0 / jnp.maximum(jnp.abs(tot[c][:, D:D + 1]), jnp.exp(-m_t[c]))))
    b_end = each(lambda c, b, h: bc[c][L - 1:L])
    log_w = each(lambda c, b, h: b_end[c] - bc[c] + ic[c])
    m_new = each(lambda c, b, h: jnp.maximum(b_end[c] + m_st[c], jnp.max(log_w[c], axis=0, keepdims=True)))
    upd = each(lambda c, b, h: _dot(kt[c], _bf(jnp.exp(log_w[c] - m_new[c]) * v_aug[c])))
    for c in range(len(chains)):
        st_scr[c] = jnp.exp(b_end[c] + m_st[c] - m_new[c]) * st[c] + upd[c]
        m_scr[c] = jnp.broadcast_to(m_new[c], m_scr.shape[1:])
    for b in range(nb):
        outs = []
        for h in range(H):
            o = jax.nn.sigmoid(og_ref[b, :, cols(h)]) * h_t[b * H + h]
            outs.append(o * lax.rsqrt(jnp.mean(o * o, axis=-1, keepdims=True) + NORM_EPS) * ng_ref[:, cols(h)])
        o_ref[b] = jnp.concatenate(outs, axis=1)


def mlstm_group(q, k, v, gates, og, i_bias, f_bias, norm_g):
    b, s, _ = q.shape
    assert s % CHUNK == 0
    W, H = GROUP_W, GROUP_HEADS
    bias = jnp.concatenate([i_bias, f_bias]).reshape(1, 2 * H)
    tok = lambda n: pl.BlockSpec((b, CHUNK, n), lambda i: (0, i, 0))
    fixed = lambda n: pl.BlockSpec((1, n), lambda i: (0, 0))
    return pl.pallas_call(
        _mlstm_kernel,
        grid=(s // CHUNK,),
        in_specs=[tok(W), tok(W), tok(W), tok(2 * H), tok(W), fixed(2 * H), fixed(W)],
        out_specs=tok(W),
        out_shape=jax.ShapeDtypeStruct((b, s, W), F32),
        scratch_shapes=[pltpu.VMEM((b * H, HEAD_DIM, LANE), F32), pltpu.VMEM((b * H, SUBLANES, LANE), F32)],
        compiler_params=pltpu.CompilerParams(dimension_semantics=("arbitrary",)),
        name="mlstm",
    )(q, k, v, gates, og, bias, norm_g.reshape(1, W))


def _gdn_kernel(x_ref, xprev_ref, cw_ref, gate_ref, z_ref, par_ref, ng_ref, o_ref, st_scr):
    nb = x_ref.shape[0]
    H, D, L = GROUP_HEADS, HEAD_DIM, CHUNK
    W = H * D
    chains = [(b, h) for b in range(nb) for h in range(H)]
    each = lambda fn: [fn(c, b, h) for c, (b, h) in enumerate(chains)]
    first = pl.program_id(0) == 0

    @pl.when(first)
    def _():
        st_scr[...] = jnp.zeros(st_scr.shape, F32)

    row = lax.broadcasted_iota(jnp.int32, (L, L), 0)
    col = lax.broadcasted_iota(jnp.int32, (L, L), 1)
    tri = row >= col
    tri_bf = tri.astype(BF16)
    eye = (row == col).astype(F32)
    par = par_ref[...]
    qkv, beta4, gc4 = [], [], []
    for b in range(nb):
        prev = jnp.where(first, 0.0, xprev_ref[b])
        full = jnp.concatenate([prev, x_ref[b]], axis=0)
        acc = full[SUBLANES:] * cw_ref[CONV_W - 1:CONV_W, :]
        for j in range(CONV_W - 1):
            shifted = pltpu.roll(full, CONV_W - 1 - j, axis=0)[SUBLANES:]
            acc = acc + shifted * cw_ref[j:j + 1, :]
        qkv.append(acc * jax.nn.sigmoid(acc))
        gates = gate_ref[b]
        beta4.append(jax.nn.sigmoid(gates[:, :H]))
        gc4.append(_dot_exact_rhs(tri_bf, -jnp.exp(par[:, :H]) * _softplus(gates[:, H:] + par[:, H:])))

    def l2n(t):
        return t * lax.rsqrt(jnp.sum(t * t, axis=-1, keepdims=True) + NORM_EPS)

    q = each(lambda c, b, h: l2n(qkv[b][:, h * D:(h + 1) * D]) * (D ** -0.5))
    k = each(lambda c, b, h: l2n(qkv[b][:, W + h * D:W + (h + 1) * D]))
    v = each(lambda c, b, h: qkv[b][:, 2 * W + h * D:2 * W + (h + 1) * D])
    beta = each(lambda c, b, h: beta4[b][:, h:h + 1])
    gc = each(lambda c, b, h: gc4[b][:, h:h + 1])
    gc_row = each(lambda c, b, h: _row_matrix(gc[c], L))
    decay = each(lambda c, b, h: jnp.where(tri, jnp.exp(gc[c] - gc_row[c]), 0.0))
    kt = each(lambda c, b, h: _transpose_bf16(_bf(k[c])))
    kb = each(lambda c, b, h: k[c] * beta[c])
    x = each(lambda c, b, h: -jnp.where(row > col, _dot(_bf(kb[c]), kt[c]) * decay[c], 0.0))
    t_inv = [eye + xc for xc in x]
    for _ in range(5):
        x = [_dot(_bf(xc), _bf(xc)) for xc in x]
        t_inv = [tc + _dot(_bf(tc), _bf(xc)) for tc, xc in zip(t_inv, x)]
    u = each(lambda c, b, h: _dot(_bf(t_inv[c]), _bf(v[c] * beta[c])))
    wm = each(lambda c, b, h: _dot(_bf(t_inv[c]), _bf(kb[c] * jnp.exp(gc[c]))))
    attn = each(lambda c, b, h: _dot(_bf(q[c]), kt[c]) * decay[c])
    st = each(lambda c, b, h: st_scr[c])
    v_new = each(lambda c, b, h: u[c] - _dot(_bf(wm[c]), _bf(st[c])))
    o = each(lambda c, b, h: _dot(_bf(q[c] * jnp.exp(gc[c])), _bf(st[c])) + _dot(_bf(attn[c]), _bf(v_new[c])))
    g_end = each(lambda c, b, h: gc[c][L - 1:L])
    upd = each(lambda c, b, h: _dot(_bf(kt[c].astype(F32) * jnp.exp(g_end[c] - gc_row[c])), _bf(v_new[c])))
    for c in range(len(chains)):
        st_scr[c] = st[c] * jnp.exp(g_end[c]) + upd[c]
    for b in range(nb):
        outs = []
        for h in range(H):
            oc = o[b * H + h]
            z = z_ref[b, :, h * D:(h + 1) * D]
            y = oc * lax.rsqrt(jnp.mean(oc * oc, axis=-1, keepdims=True) + NORM_EPS) * ng_ref[...]
            outs.append(y * (z * jax.nn.sigmoid(z)))
        o_ref[b] = jnp.concatenate(outs, axis=1)


def gdn_group(x, gates, z, conv_w, a_log, dt_bias, norm_g):
    b, s, _ = x.shape
    assert s % CHUNK == 0
    W, H = GROUP_W, GROUP_HEADS
    par = jnp.concatenate([a_log, dt_bias]).reshape(1, 2 * H)
    tok = lambda n: pl.BlockSpec((b, CHUNK, n), lambda i: (0, i, 0))
    per_chunk = CHUNK // SUBLANES
    prev = pl.BlockSpec((b, SUBLANES, 3 * W), lambda i: (0, jnp.maximum(i * per_chunk - 1, 0), 0))
    fixed = lambda r, n: pl.BlockSpec((r, n), lambda i: (0, 0))
    return pl.pallas_call(
        _gdn_kernel,
        grid=(s // CHUNK,),
        in_specs=[tok(3 * W), prev, fixed(CONV_W, 3 * W), tok(2 * H), tok(W), fixed(1, 2 * H), fixed(1, HEAD_DIM)],
        out_specs=tok(W),
        out_shape=jax.ShapeDtypeStruct((b, s, W), F32),
        scratch_shapes=[pltpu.VMEM((b * H, HEAD_DIM, HEAD_DIM), F32)],
        compiler_params=pltpu.CompilerParams(dimension_semantics=("arbitrary",)),
        name="gdn",
    )(x, x, conv_w, gates, z, par, norm_g.reshape(1, HEAD_DIM))


NSA_IN = GROUP_W + 6 * HEAD_DIM + 3 * GROUP_HEADS
MLSTM_IN = 4 * GROUP_W + 2 * GROUP_HEADS
GDN_IN = 4 * GROUP_W + 2 * GROUP_HEADS


def kernel(x, positions, t5_table, norm1_g, norm2_g, final_norm_g, w_in, w_out, nsa_w_cmp_k, nsa_w_cmp_v, nsa_cmp_pos, mlstm_i_bias, mlstm_f_bias, mlstm_norm_g, gdn_conv_w, gdn_a_log, gdn_dt_bias, gdn_norm_g, mla_q_norm_g, mla_kv_norm_g, mla_w_uq, mla_w_ukv, peer_w_q, peer_sub_keys, peer_u, peer_v):
    b, s, d = x.shape
    t = b * s
    h = x.reshape(t, d)
    depth = w_in.shape[0]
    W, D, H = GROUP_W, HEAD_DIM, GROUP_HEADS
    bias_tiles = nsa_bias_tiles(t5_table)
    c_ml, c_gdn, c_mla = NSA_IN, NSA_IN + MLSTM_IN, NSA_IN + MLSTM_IN + GDN_IN
    small = [(W + 6 * D, 3 * H), (c_ml + 3 * W, 2 * H), (c_gdn + 3 * W, 2 * H),
             (c_mla + Q_LORA + KV_LORA, QK_ROPE)]
    groups = [(0, W, W, F32, None),
              (W, D, D, F32, None), (W + D, D, D, F32, None),
              (W + 2 * D, D, D, BF16, None), (W + 3 * D, D, 2 * D, BF16, D),
              (W + 4 * D, D, D, BF16, None), (W + 5 * D, D, 2 * D, BF16, D),
              (c_ml, W, W, F32, None), (c_ml + W, W, W, F32, None), (c_ml + 2 * W, W, W, F32, None),
              (c_ml + 3 * W + 2 * H, W, W, F32, None),
              (c_gdn, 3 * W, 3 * W, F32, None), (c_gdn + 3 * W + 2 * H, W, W, F32, None),
              (c_mla, Q_LORA, Q_LORA, F32, None), (c_mla + Q_LORA, KV_LORA, KV_LORA, F32, None)]
    for l in range(depth):
        w_l = w_in[l].astype(BF16)
        weights = [jnp.pad(w_l[:, c:c + n], ((0, 0), (0, wide - n))) for c, n, wide, _, _ in groups]
        w_small = jnp.concatenate([w_l[:, c:c + n] for c, n in small], axis=1)
        weights.append(jnp.pad(w_small, ((0, 0), (0, LANE - w_small.shape[1]))))
        outs = norm_matmul_split(h, norm1_g[l], weights, [g[3] for g in groups] + [F32],
                                 [g[4] for g in groups] + [None])
        (nsa_q, kc, vc, ks, vs_aug, kw, vw_aug, ml_q, ml_k, ml_v, ml_og, gdn_x, gdn_z, c_q, c_kv, misc) = outs
        offs = np.cumsum([0] + [n for _, n in small])
        nsa_gates, ml_gates, gdn_gates, k_r = [misc[:, offs[j]:offs[j + 1]] for j in range(len(small))]
        seq = lambda a: a.reshape(b, s, a.shape[-1])
        parts = [
            nsa_group(seq(nsa_q), seq(kc), seq(vc), seq(ks), seq(vs_aug), seq(kw), seq(vw_aug), seq(nsa_gates),
                      nsa_w_cmp_k[l], nsa_w_cmp_v[l], nsa_cmp_pos[l], t5_table, bias_tiles),
            mlstm_group(seq(ml_q), seq(ml_k), seq(ml_v), seq(ml_gates), seq(ml_og),
                        mlstm_i_bias[l], mlstm_f_bias[l], mlstm_norm_g[l]),
            gdn_group(seq(gdn_x), seq(gdn_gates), seq(gdn_z), gdn_conv_w[l], gdn_a_log[l], gdn_dt_bias[l],
                      gdn_norm_g[l]),
            mla_group(c_q, c_kv, k_r, positions, mla_q_norm_g[l], mla_kv_norm_g[l], mla_w_uq[l], mla_w_ukv[l]),
        ]
        w_o = w_out[l].astype(BF16)
        h = matmul_residual([p.reshape(t, W) for p in parts], [w_o[j * W:(j + 1) * W] for j in range(len(parts))], h)
        h = peer_ffn_residual(h, norm2_g[l], peer_w_q[l], peer_sub_keys[l], peer_u[l], peer_v[l])
    return rmsnorm(h, final_norm_g).reshape(b, s, d)
```
